```python
import math
import jax
import jax.numpy as jnp
from jax import lax
import numpy as np

D_MODEL = 2048
BATCH = 1
SEQ = 8192
DEPTH = 4

GRID_W = 64
CTX_LEN = 256
D_MIX = D_MODEL
LRU_W = D_MIX // 2
LRU_HEADS = 4
LRU_HEAD_DIM = LRU_W // LRU_HEADS
LRU_CONV = 4
LRU_C = 8.0
HY_W = D_MIX - LRU_W
HY_CONV = 3
HY_EMB = 33
HY_BANDS = (HY_EMB - 1) // 2
HY_ORDER_DIM = 64
HY_TARGET = 1e-2
HY_FAST_DECAY_PCT = 0.3
HY_SLOW_DECAY_PCT = 1.5
HY_MAX_DECAY = math.log(HY_TARGET) / HY_FAST_DECAY_PCT
HY_MIN_DECAY = math.log(HY_TARGET) / HY_SLOW_DECAY_PCT
IN_COLS = 2 * LRU_W + 3 * HY_W
D_FF = 5632
N_MOD = 9
EPS = 1e-6

kernel_name = 'hybrid_rglru_hyena_macaron_dit'


def rms_norm(x, g):
    xf = x.astype(jnp.float32)
    y = xf * lax.rsqrt(jnp.mean(xf * xf, axis=-1, keepdims=True) + EPS)
    return (y * g.astype(jnp.float32)).astype(x.dtype)


def modulate(x, g, shift, scale):
    return rms_norm(x, g) * (1 + scale) + shift


def swiglu(u, wg, wu, wd):
    return (jax.nn.silu(u @ wg) * (u @ wu)) @ wd


def ffn_sublayer(h, mod, g, wg, wu, wd):
    shift, scale, gate = mod
    return h + 0.5 * gate * swiglu(modulate(h, g, shift, scale), wg, wu, wd)


def dw_conv(u, w, b):
    K = w.shape[0]
    L = u.shape[1]
    left = (K - 1) // 2
    up = jnp.pad(u, ((0, 0), (left, K - 1 - left), (0, 0)))
    out = b
    for k in range(K):
        out = out + up[:, k:k + L] * w[k]
    return out


def to_col_major(u):
    B, N, C = u.shape
    rows = N // GRID_W
    return u.reshape(B, rows, GRID_W, C).transpose(0, 2, 1, 3).reshape(B, N, C)


def from_col_major(u):
    B, N, C = u.shape
    rows = N // GRID_W
    return u.reshape(B, GRID_W, rows, C).transpose(0, 2, 1, 3).reshape(B, N, C)


def _lin_combine(left, right):
    a1, b1 = left
    a2, b2 = right
    return a1 * a2, a2 * b1 + b2


def rglru_coeffs(xc, wa, ba, wx, bx, lam):
    B, L, _ = xc.shape
    xf = xc.astype(jnp.float32)
    xh = xf.reshape(B, L, LRU_HEADS, LRU_HEAD_DIM)
    r = jax.nn.sigmoid(jnp.einsum('blhd,hde->blhe', xh, wa.astype(jnp.float32)).reshape(B, L, LRU_W) + ba)
    i = jax.nn.sigmoid(jnp.einsum('blhd,hde->blhe', xh, wx.astype(jnp.float32)).reshape(B, L, LRU_W) + bx)
    log_a = -LRU_C * r * jax.nn.softplus(-lam.astype(jnp.float32))
    a = jnp.exp(log_a)
    b = jnp.sqrt(-jnp.expm1(2.0 * log_a)) * (i * xf)
    return a, b


def linear_scan(a, b, h0, reverse):
    A, Bc = lax.associative_scan(_lin_combine, (a, b), axis=1, reverse=reverse)
    if h0 is None:
        return Bc
    return A * h0[:, None, :] + Bc


def bidir_rglru(cv_ctx, cv_lat, wa, ba, wx, bx, lam, need_ctx):
    h_lat = []
    h_ctx = []
    for d in range(2):
        rev = d == 1
        a_c, b_c = rglru_coeffs(cv_ctx, wa[d], ba[d], wx[d], bx[d], lam[d])
        hc = linear_scan(a_c, b_c, None, rev)
        h0 = hc[:, 0] if rev else hc[:, -1]
        a_l, b_l = rglru_coeffs(cv_lat, wa[d], ba[d], wx[d], bx[d], lam[d])
        h_lat.append(linear_scan(a_l, b_l, h0, rev))
        h_ctx.append(hc)
    lat = h_lat[0] + h_lat[1]
    ctx = (h_ctx[0] + h_ctx[1]) if need_ctx else None
    return lat, ctx


def hyena_filters(L, w1, b1, w2, b2, w3, b3, w4, b4, freq):
    f32 = jnp.float32
    t = jnp.linspace(0.0, 1.0, L, dtype=f32)[:, None]
    w = 2.0 * math.pi * jnp.arange(L, dtype=f32) / L
    f = jnp.linspace(1e-4, HY_BANDS - 1, HY_BANDS, dtype=f32)
    ang = w[:, None] * f[None, :]
    z = jnp.concatenate([t, jnp.cos(ang), -jnp.sin(ang)], axis=-1)
    fr = freq.astype(f32)
    h = jnp.sin(fr * (z @ w1.astype(f32) + b1.astype(f32)))
    h = jnp.sin(fr * (h @ w2.astype(f32) + b2.astype(f32)))
    h = jnp.sin(fr * (h @ w3.astype(f32) + b3.astype(f32)))
    k = h @ w4.astype(f32) + b4.astype(f32)
    deltas = jnp.abs(jnp.linspace(HY_MIN_DECAY, HY_MAX_DECAY, HY_W, dtype=f32))
    decay = jnp.exp(-t * deltas[None, :])
    k_fwd = k[:, :HY_W] * decay
    k_bwd = k[:, HY_W:] * decay
    k_full = jnp.concatenate([k_fwd, jnp.zeros((1, HY_W), f32), k_bwd[:0:-1]], axis=0)
    return k_full / jnp.sum(jnp.abs(k_full), axis=0, keepdims=True)


def long_conv(v, k_full, bias):
    L = v.shape[1]
    vf = v.astype(jnp.float32)
    V = jnp.fft.rfft(vf, n=2 * L, axis=1)
    K = jnp.fft.rfft(k_full, n=2 * L, axis=0)
    y = jnp.fft.irfft(V * K[None], n=2 * L, axis=1)[:, :L]
    return (y + vf * bias.astype(jnp.float32)).astype(v.dtype)


def hyena(proj, conv_w, conv_b, k_full, bias):
    z = dw_conv(proj, conv_w, conv_b)
    x0, x1, v = jnp.split(z, 3, axis=-1)
    return long_conv(v * x1, k_full, bias) * x0


def merge_groups(h_lru, yr, y_hy, out_g, w_out):
    lru = h_lru.astype(yr.dtype) * jax.nn.gelu(yr)
    lru = rms_norm(lru, out_g[:LRU_W])
    hy = rms_norm(y_hy, out_g[LRU_W:])
    return jnp.concatenate([lru, hy], axis=-1) @ w_out


def setup_inputs(seed: int = 0) -> dict:
    key = jax.random.key(seed)
    keys = jax.random.split(key, 33)
    f32 = jnp.float32

    def nrm(i, shape, scale):
        return jax.random.normal(keys[i], shape, f32) * scale

    lam_u = jax.random.uniform(keys[19], (DEPTH, 2, LRU_W), f32, 0.9, 0.999)
    lam_s = lam_u ** (1.0 / LRU_C)
    lam = jnp.log(lam_s) - jnp.log1p(-lam_s)
    return {
        'x': nrm(0, (BATCH, SEQ, D_MODEL), 1.0),
        'c': nrm(1, (BATCH, D_MODEL), 1.0),
        'ctx': nrm(2, (BATCH, CTX_LEN, D_MODEL), 1.0),
        'c_ctx': nrm(3, (D_MODEL,), 1.0),
        'ada_w': nrm(4, (DEPTH, D_MODEL, N_MOD * D_MODEL), 0.5 * D_MODEL ** -0.5),
        'ada_b': nrm(5, (DEPTH, N_MOD * D_MODEL), 0.02),
        'norm_g': 1.0 + nrm(6, (DEPTH, 3, D_MODEL), 0.02),
        'ffn_wg': nrm(7, (DEPTH, 2, D_MODEL, D_FF), D_MODEL ** -0.5),
        'ffn_wu': nrm(8, (DEPTH, 2, D_MODEL, D_FF), D_MODEL ** -0.5),
        'ffn_wd': nrm(9, (DEPTH, 2, D_FF, D_MODEL), D_FF ** -0.5),
        'w_in': nrm(10, (DEPTH, D_MODEL, IN_COLS), D_MODEL ** -0.5),
        'w_out': nrm(11, (DEPTH, D_MIX, D_MODEL), D_MIX ** -0.5),
        'out_g': 1.0 + nrm(12, (DEPTH, D_MIX), 0.02),
        'lru_conv_w': nrm(13, (DEPTH, LRU_CONV, LRU_W), LRU_CONV ** -0.5),
        'lru_conv_b': nrm(14, (DEPTH, LRU_W), 0.02),
        'lru_wa': nrm(15, (DEPTH, 2, LRU_HEADS, LRU_HEAD_DIM, LRU_HEAD_DIM), LRU_HEAD_DIM ** -0.5),
        'lru_ba': nrm(16, (DEPTH, 2, LRU_W), 0.02),
        'lru_wx': nrm(17, (DEPTH, 2, LRU_HEADS, LRU_HEAD_DIM, LRU_HEAD_DIM), LRU_HEAD_DIM ** -0.5),
        'lru_bx': nrm(18, (DEPTH, 2, LRU_W), 0.02),
        'lru_lam': lam,
        'hy_conv_w': nrm(20, (DEPTH, HY_CONV, 3 * HY_W), HY_CONV ** -0.5),
        'hy_conv_b': nrm(21, (DEPTH, 3 * HY_W), 0.02),
        'hy_bias': nrm(22, (DEPTH, HY_W), 1.0),
        'filt_w1': nrm(23, (DEPTH, HY_EMB, HY_ORDER_DIM), HY_EMB ** -0.5),
        'filt_b1': nrm(24, (DEPTH, HY_ORDER_DIM), 0.1),
        'filt_w2': nrm(25, (DEPTH, HY_ORDER_DIM, HY_ORDER_DIM), HY_ORDER_DIM ** -0.5),
        'filt_b2': nrm(26, (DEPTH, HY_ORDER_DIM), 0.1),
        'filt_w3': nrm(27, (DEPTH, HY_ORDER_DIM, HY_ORDER_DIM), HY_ORDER_DIM ** -0.5),
        'filt_b3': nrm(28, (DEPTH, HY_ORDER_DIM), 0.1),
        'filt_w4': nrm(29, (DEPTH, HY_ORDER_DIM, 2 * HY_W), HY_ORDER_DIM ** -0.5),
        'filt_b4': nrm(30, (DEPTH, 2 * HY_W), 0.1),
        'filt_freq': 1.0 + nrm(31, (DEPTH, HY_ORDER_DIM), 0.02),
        'final_g': 1.0 + nrm(32, (D_MODEL,), 0.02),
    }


def reference(x, c, ctx, c_ctx, ada_w, ada_b, norm_g, ffn_wg, ffn_wu, ffn_wd, w_in, w_out, out_g,
              lru_conv_w, lru_conv_b, lru_wa, lru_ba, lru_wx, lru_bx, lru_lam,
              hy_conv_w, hy_conv_b, hy_bias, filt_w1, filt_b1, filt_w2, filt_b2, filt_w3, filt_b3,
              filt_w4, filt_b4, filt_freq, final_g):
    n_lat = x.shape[1]
    n_ctx = ctx.shape[1]
    silu_lat = jax.nn.silu(c)[:, None, :]
    silu_ctx = jax.nn.silu(c_ctx)[None, None, :]
    xl = x
    xc = ctx
    for l in range(DEPTH):
        last = l == DEPTH - 1
        col_major = l % 2 == 1
        mod_l = jnp.split(silu_lat @ ada_w[l] + ada_b[l], N_MOD, axis=-1)
        mod_c = jnp.split(silu_ctx @ ada_w[l] + ada_b[l], N_MOD, axis=-1)
        filt = (filt_w1[l], filt_b1[l], filt_w2[l], filt_b2[l], filt_w3[l], filt_b3[l],
                filt_w4[l], filt_b4[l], filt_freq[l])

        xl = ffn_sublayer(xl, mod_l[0:3], norm_g[l, 0], ffn_wg[l, 0], ffn_wu[l, 0], ffn_wd[l, 0])
        xc = ffn_sublayer(xc, mod_c[0:3], norm_g[l, 0], ffn_wg[l, 0], ffn_wu[l, 0], ffn_wd[l, 0])

        ul = modulate(xl, norm_g[l, 1], mod_l[3], mod_l[4])
        uc = modulate(xc, norm_g[l, 1], mod_c[3], mod_c[4])
        if col_major:
            ul = to_col_major(ul)
        pl = ul @ w_in[l]
        pc = uc @ (w_in[l][:, :LRU_W] if last else w_in[l])

        cv_l = dw_conv(pl[..., :LRU_W], lru_conv_w[l], lru_conv_b[l])
        cv_c = dw_conv(pc[..., :LRU_W], lru_conv_w[l], lru_conv_b[l])
        h_lat, h_ctx = bidir_rglru(cv_c, cv_l, lru_wa[l], lru_ba[l], lru_wx[l], lru_bx[l], lru_lam[l], not last)

        k_lat = hyena_filters(n_lat, *filt)
        y_hy_l = hyena(pl[..., 2 * LRU_W:], hy_conv_w[l], hy_conv_b[l], k_lat, hy_bias[l])
        y_lat = merge_groups(h_lat, pl[..., LRU_W:2 * LRU_W], y_hy_l, out_g[l], w_out[l])
        if col_major:
            y_lat = from_col_major(y_lat)
        xl = xl + mod_l[5] * y_lat

        xl = ffn_sublayer(xl, mod_l[6:9], norm_g[l, 2], ffn_wg[l, 1], ffn_wu[l, 1], ffn_wd[l, 1])

        if not last:
            k_ctx = hyena_filters(n_ctx, *filt)
            y_hy_c = hyena(pc[..., 2 * LRU_W:], hy_conv_w[l], hy_conv_b[l], k_ctx, hy_bias[l])
            y_ctx = merge_groups(h_ctx, pc[..., LRU_W:2 * LRU_W], y_hy_c, out_g[l], w_out[l])
            xc = xc + mod_c[5] * y_ctx
            xc = ffn_sublayer(xc, mod_c[6:9], norm_g[l, 2], ffn_wg[l, 1], ffn_wu[l, 1], ffn_wd[l, 1])

    return rms_norm(xl, final_g)
```

```python
import functools
import math

import numpy as np
import jax
import jax.numpy as jnp
from jax import lax
from jax.experimental import pallas as pl
from jax.experimental.pallas import tpu as pltpu

F32 = jnp.float32
BF16 = jnp.bfloat16
EPS = 1e-6
LRU_C = 8.0
GRID_W = 64
HY_BANDS = 16
HY_MAX_DECAY = math.log(1e-2) / 0.3
HY_MIN_DECAY = math.log(1e-2) / 1.5

V7X_VMEM_LIMIT_BYTES = 58 * 1024 * 1024
FFT_B = 128
SUBLANES = 8


def _cparams(sem):
    return pltpu.CompilerParams(dimension_semantics=sem, vmem_limit_bytes=V7X_VMEM_LIMIT_BYTES)


def _rms(x):
    return x * lax.rsqrt(jnp.mean(x * x, axis=-1, keepdims=True) + EPS)


def _modulated(x, mod_ref, g_ref, stream):
    shift = mod_ref[3 * stream:3 * stream + 1, :]
    scale = mod_ref[3 * stream + 1:3 * stream + 2, :]
    return _rms(x) * g_ref[...] * (1.0 + scale) + shift


def _ada_kernel(c_ref, w_ref, b_ref, o_ref):
    c = c_ref[...]
    s = c * jax.nn.sigmoid(c)
    o_ref[...] = jnp.dot(s, w_ref[...], precision=lax.Precision.HIGHEST,
                         preferred_element_type=F32) + b_ref[...]


def _ada_mods(c, c_ctx, ada_w, ada_b):
    depth, d, nm = ada_w.shape
    tn = d
    cc = jnp.zeros((SUBLANES, d), F32).at[0].set(c[0]).at[1].set(c_ctx)
    out = pl.pallas_call(
        _ada_kernel,
        grid=(depth, nm // tn),
        in_specs=[pl.BlockSpec((SUBLANES, d), lambda l, j: (0, 0)),
                  pl.BlockSpec((None, d, tn), lambda l, j: (l, 0, j)),
                  pl.BlockSpec((None, 1, tn), lambda l, j: (l, 0, j))],
        out_specs=pl.BlockSpec((None, SUBLANES, tn), lambda l, j: (l, 0, j)),
        out_shape=jax.ShapeDtypeStruct((depth, SUBLANES, nm), F32),
        compiler_params=_cparams(("arbitrary", "arbitrary")),
        name="ada_mods",
    )(cc, ada_w, ada_b[:, None, :])
    return out


def _sublayer_mods(mods_l, d, k):
    lat = mods_l[0, 3 * k * d:(3 * k + 3) * d].reshape(3, d)
    ctx = mods_l[1, 3 * k * d:(3 * k + 3) * d].reshape(3, d)
    return jnp.concatenate([lat, ctx, jnp.zeros((2, d), F32)], axis=0)


def _ffn_up_kernel(x_ref, mod_ref, g_ref, wg_ref, wu_ref, h_ref, u_scr, *, stream):
    @pl.when(pl.program_id(1) == 0)
    def _():
        u_scr[...] = _modulated(x_ref[...], mod_ref, g_ref, stream).astype(BF16)

    u = u_scr[...]
    g = jnp.dot(u, wg_ref[...].astype(BF16), preferred_element_type=F32)
    up = jnp.dot(u, wu_ref[...].astype(BF16), preferred_element_type=F32)
    h_ref[...] = (g * jax.nn.sigmoid(g) * up).astype(BF16)


def _ffn_down_kernel(h_ref, wd_ref, x_ref, mod_ref, o_ref, *, stream):
    acc = jnp.dot(h_ref[...], wd_ref[...].astype(BF16), preferred_element_type=F32)
    gate = mod_ref[3 * stream + 2:3 * stream + 3, :]
    o_ref[...] = x_ref[...] + (0.5 * gate) * acc


def _ffn(x, mod, g, wg, wu, wd, l, j, stream):
    rows, d = x.shape
    dff = wg.shape[-1]
    tm = min(rows, 1024)
    tf = 512
    tn = 256
    h = pl.pallas_call(
        functools.partial(_ffn_up_kernel, stream=stream),
        grid=(rows // tm, dff // tf),
        in_specs=[pl.BlockSpec((tm, d), lambda i, f: (i, 0)),
                  pl.BlockSpec((SUBLANES, d), lambda i, f: (0, 0)),
                  pl.BlockSpec((1, d), lambda i, f: (0, 0)),
                  pl.BlockSpec((None, None, d, tf), lambda i, f: (l, j, 0, f)),
                  pl.BlockSpec((None, None, d, tf), lambda i, f: (l, j, 0, f))],
        out_specs=pl.BlockSpec((tm, tf), lambda i, f: (i, f)),
        out_shape=jax.ShapeDtypeStruct((rows, dff), BF16),
        scratch_shapes=[pltpu.VMEM((tm, d), BF16)],
        compiler_params=_cparams(("arbitrary", "arbitrary")),
        name="ffn_up",
    )(x, mod, g, wg, wu)
    return pl.pallas_call(
        functools.partial(_ffn_down_kernel, stream=stream),
        grid=(rows // tm, d // tn),
        in_specs=[pl.BlockSpec((tm, dff), lambda i, n: (i, 0)),
                  pl.BlockSpec((None, None, dff, tn), lambda i, n: (l, j, 0, n)),
                  pl.BlockSpec((tm, tn), lambda i, n: (i, n)),
                  pl.BlockSpec((SUBLANES, tn), lambda i, n: (0, n))],
        out_specs=pl.BlockSpec((tm, tn), lambda i, n: (i, n)),
        out_shape=jax.ShapeDtypeStruct((rows, d), F32),
        compiler_params=_cparams(("arbitrary", "arbitrary")),
        name="ffn_down",
    )(h, wd, x, mod)


def _win_kernel(x_ref, mod_ref, g_ref, w_ref, p_ref, *rest, pieces, prow, d, stream, emit_x):
    if emit_x:
        xo_ref, u_scr = rest
    else:
        (u_scr,) = rest

    @pl.when(pl.program_id(1) == 0)
    def _():
        for q in range(pieces):
            xt = x_ref[:, q * d:(q + 1) * d]
            u_scr[q * prow:(q + 1) * prow, :] = _modulated(xt, mod_ref, g_ref, stream).astype(BF16)
            if emit_x:
                xo_ref[q * prow:(q + 1) * prow, :] = xt

    p_ref[...] = jnp.dot(u_scr[...], w_ref[...].astype(BF16), preferred_element_type=F32)


def _win(x, mod, g, w_in, l, ncols, stream, perm):
    rows, d = x.shape
    tm = min(rows, 1024)
    tn = 512
    if perm is None:
        prow, pieces, xv = tm, 1, x
        x_spec = pl.BlockSpec((tm, d), lambda i, n: (i, 0))
    else:
        grid_h = rows // GRID_W
        prow = grid_h if perm == "to_col" else GRID_W
        pieces = tm // prow
        xv = x.reshape(prow, (rows // prow) * d)
        x_spec = pl.BlockSpec((prow, pieces * d), lambda i, n: (0, i))
    emit_x = perm is not None
    out_shape = [jax.ShapeDtypeStruct((rows, ncols), F32)]
    out_specs = [pl.BlockSpec((tm, tn), lambda i, n: (i, n))]
    if emit_x:
        out_shape.append(jax.ShapeDtypeStruct((rows, d), F32))
        out_specs.append(pl.BlockSpec((tm, d), lambda i, n: (i, 0)))
    res = pl.pallas_call(
        functools.partial(_win_kernel, pieces=pieces, prow=prow, d=d, stream=stream, emit_x=emit_x),
        grid=(rows // tm, ncols // tn),
        in_specs=[x_spec,
                  pl.BlockSpec((SUBLANES, d), lambda i, n: (0, 0)),
                  pl.BlockSpec((1, d), lambda i, n: (0, 0)),
                  pl.BlockSpec((None, d, tn), lambda i, n: (l, 0, n))],
        out_specs=out_specs,
        out_shape=out_shape,
        scratch_shapes=[pltpu.VMEM((tm, d), BF16)],
        compiler_params=_cparams(("arbitrary", "arbitrary")),
        name="mixer_in_proj",
    )(xv, mod, g, w_in)
    return (res[0], res[1]) if emit_x else (res[0], x)


def _lru_kernel(x_ref, xp_ref, xn_ref, cw_ref, cb_ref, wax_ref, ba_ref, bx_ref, lam_ref, h0_ref, *rest,
                t_blk, nblk, heads, reverse, add_prev):
    if add_prev:
        prev_ref, o_ref, hf_ref, xs, a_scr, b_scr, carry = rest
    else:
        o_ref, hf_ref, xs, a_scr, b_scr, carry = rest
    j = pl.program_id(0)
    blk = (nblk - 1 - j) if reverse else j
    c = x_ref.shape[1]
    hd = c // heads

    @pl.when(j == 0)
    def _():
        carry[...] = h0_ref[...]

    zero8 = jnp.zeros((SUBLANES, c), F32)
    xs[0:SUBLANES, :] = jnp.where(blk == 0, zero8, xp_ref[...])
    xs[SUBLANES:SUBLANES + t_blk, :] = x_ref[...]
    xs[SUBLANES + t_blk:2 * SUBLANES + t_blk, :] = jnp.where(blk == nblk - 1, zero8, xn_ref[...])
    cv = cb_ref[...]
    for k in range(cw_ref.shape[0]):
        cv = cv + xs[SUBLANES - 1 + k:SUBLANES - 1 + k + t_blk, :] * cw_ref[k:k + 1, :]

    z = -lam_ref[...]
    c_sp = LRU_C * (jnp.maximum(z, 0.0) + jnp.log(1.0 + jnp.exp(-jnp.abs(z))))
    cvb = cv.astype(BF16)
    for h in range(heads):
        sl = slice(h * hd, (h + 1) * hd)
        gx = jnp.dot(cvb[:, sl], wax_ref[h], preferred_element_type=F32)
        r = jax.nn.sigmoid(gx[:, :hd] + ba_ref[:, sl])
        ig = jax.nn.sigmoid(gx[:, hd:] + bx_ref[:, sl])
        log_a = -c_sp[:, sl] * r
        a_scr[:, sl] = jnp.exp(log_a)
        b_scr[:, sl] = jnp.sqrt(1.0 - jnp.exp(2.0 * log_a)) * (ig * cv[:, sl])

    row = lax.broadcasted_iota(jnp.int32, (SUBLANES, c), 0)
    ngrp = t_blk // SUBLANES

    def group(gi, hc):
        g = (ngrp - 1 - gi) if reverse else gi
        r0 = pl.multiple_of(g * SUBLANES, SUBLANES)
        av = a_scr[pl.ds(r0, SUBLANES), :]
        bv = b_scr[pl.ds(r0, SUBLANES), :]
        for s in (1, 2, 4):
            sh = (SUBLANES - s) if reverse else s
            m = (row < SUBLANES - s) if reverse else (row >= s)
            a_s = pltpu.roll(av, sh, axis=0)
            b_s = pltpu.roll(bv, sh, axis=0)
            bv = jnp.where(m, av * b_s + bv, bv)
            av = jnp.where(m, av * a_s, av)
        hv = av * hc + bv
        if add_prev:
            o_ref[pl.ds(r0, SUBLANES), :] = hv + prev_ref[pl.ds(r0, SUBLANES), :]
        else:
            o_ref[pl.ds(r0, SUBLANES), :] = hv
        return hv[0:1, :] if reverse else hv[SUBLANES - 1:SUBLANES, :]

    hc = lax.fori_loop(0, ngrp, group, carry[...])
    carry[...] = hc
    hf_ref[...] = hc


def _lru_scan(p, cw, cb, wax, ba, bx, lam, h0, prev, reverse):
    seq = p.shape[0]
    c = cw.shape[1]
    heads = wax.shape[0]
    t_blk = 256
    nblk = seq // t_blk
    per = t_blk // SUBLANES
    nb8 = seq // SUBLANES

    def bi(j):
        return (nblk - 1 - j) if reverse else j

    in_specs = [pl.BlockSpec((t_blk, c), lambda j: (bi(j), 0)),
                pl.BlockSpec((SUBLANES, c), lambda j: (jnp.maximum(bi(j) * per - 1, 0), 0)),
                pl.BlockSpec((SUBLANES, c), lambda j: (jnp.minimum((bi(j) + 1) * per, nb8 - 1), 0)),
                pl.BlockSpec(cw.shape, lambda j: (0, 0)),
                pl.BlockSpec((1, c), lambda j: (0, 0)),
                pl.BlockSpec(wax.shape, lambda j: (0, 0, 0)),
                pl.BlockSpec((1, c), lambda j: (0, 0)),
                pl.BlockSpec((1, c), lambda j: (0, 0)),
                pl.BlockSpec((1, c), lambda j: (0, 0)),
                pl.BlockSpec((1, c), lambda j: (0, 0))]
    args = [p, p, p, cw, cb, wax, ba, bx, lam, h0]
    if prev is not None:
        in_specs.append(pl.BlockSpec((t_blk, c), lambda j: (bi(j), 0)))
        args.append(prev)
    return pl.pallas_call(
        functools.partial(_lru_kernel, t_blk=t_blk, nblk=nblk, heads=heads, reverse=reverse,
                          add_prev=prev is not None),
        grid=(nblk,),
        in_specs=in_specs,
        out_specs=[pl.BlockSpec((t_blk, c), lambda j: (bi(j), 0)),
                   pl.BlockSpec((1, c), lambda j: (0, 0))],
        out_shape=[jax.ShapeDtypeStruct((seq, c), F32), jax.ShapeDtypeStruct((1, c), F32)],
        scratch_shapes=[pltpu.VMEM((t_blk + 2 * SUBLANES, c), F32),
                        pltpu.VMEM((t_blk, c), F32),
                        pltpu.VMEM((t_blk, c), F32),
                        pltpu.VMEM((1, c), F32)],
        compiler_params=_cparams(("arbitrary",)),
        name="rglru_scan",
    )(*args)


def _hypre_kernel(x0_ref, x0p, x0n, x1_ref, x1p, x1n, v_ref, vp, vn, cw_ref, cb_ref, u_ref, x0c_ref, xs,
                  *, t_blk, nblk, c):
    blk = pl.program_id(0)
    zero8 = jnp.zeros((SUBLANES, c), F32)

    def conv3(cur, prv, nxt, part):
        xs[0:SUBLANES, :] = jnp.where(blk == 0, zero8, prv[...])
        xs[SUBLANES:SUBLANES + t_blk, :] = cur[...]
        xs[SUBLANES + t_blk:2 * SUBLANES + t_blk, :] = jnp.where(blk == nblk - 1, zero8, nxt[...])
        sl = slice(part * c, (part + 1) * c)
        out = cb_ref[:, sl]
        for k in range(cw_ref.shape[0]):
            out = out + xs[SUBLANES - 1 + k:SUBLANES - 1 + k + t_blk, :] * cw_ref[k:k + 1, sl]
        return out

    x0c_ref[...] = conv3(x0_ref, x0p, x0n, 0)
    x1c = conv3(x1_ref, x1p, x1n, 1)
    u_ref[...] = conv3(v_ref, vp, vn, 2) * x1c


def _hypre(p, col0, cw, cb):
    seq = p.shape[0]
    c = cw.shape[1] // 3
    t_blk = min(seq, 512)
    nblk = seq // t_blk
    per = t_blk // SUBLANES
    nb8 = seq // SUBLANES
    cb0 = col0 // c
    in_specs, args = [], []
    for part in range(3):
        in_specs += [pl.BlockSpec((t_blk, c), lambda j, part=part: (j, cb0 + part)),
                     pl.BlockSpec((SUBLANES, c), lambda j, part=part: (jnp.maximum(j * per - 1, 0), cb0 + part)),
                     pl.BlockSpec((SUBLANES, c),
                                  lambda j, part=part: (jnp.minimum((j + 1) * per, nb8 - 1), cb0 + part))]
        args += [p, p, p]
    in_specs += [pl.BlockSpec(cw.shape, lambda j: (0, 0)), pl.BlockSpec((1, 3 * c), lambda j: (0, 0))]
    args += [cw, cb]
    return pl.pallas_call(
        functools.partial(_hypre_kernel, t_blk=t_blk, nblk=nblk, c=c),
        grid=(nblk,),
        in_specs=in_specs,
        out_specs=[pl.BlockSpec((t_blk, c), lambda j: (j, 0)), pl.BlockSpec((t_blk, c), lambda j: (j, 0))],
        out_shape=[jax.ShapeDtypeStruct((seq, c), F32), jax.ShapeDtypeStruct((seq, c), F32)],
        scratch_shapes=[pltpu.VMEM((t_blk + 2 * SUBLANES, c), F32)],
        compiler_params=_cparams(("arbitrary",)),
        name="hyena_short_conv",
    )(*args)


def _filt_kernel(w1_ref, b1_ref, w2_ref, b2_ref, w3_ref, b3_ref, w4_ref, b4_ref, fr_ref, k_ref, nrm_ref,
                 *, seq, t_blk, c):
    i = pl.program_id(0)
    hi = lax.Precision.HIGHEST
    lanes = w1_ref.shape[0]
    pos = i * t_blk + lax.broadcasted_iota(jnp.int32, (t_blk, 1), 0)
    tix = jnp.where(pos < seq, pos, 2 * seq - pos)
    valid = pos != seq
    tf = tix.astype(F32)
    t = tf * (1.0 / (seq - 1))
    w = (2.0 * math.pi / seq) * tf
    lane = lax.broadcasted_iota(jnp.int32, (1, lanes), 1)
    is_cos = (lane >= 1) & (lane <= HY_BANDS)
    is_sin = (lane > HY_BANDS) & (lane <= 2 * HY_BANDS)
    band = jnp.where(is_cos, lane - 1, lane - 1 - HY_BANDS).astype(F32)
    f = 1e-4 + band * ((HY_BANDS - 1 - 1e-4) / (HY_BANDS - 1))
    phase = jnp.where(is_cos, 0.5 * math.pi, 0.0)
    sign = jnp.where(is_cos, 1.0, jnp.where(is_sin, -1.0, 0.0))
    z = jnp.where(lane == 0, t, sign * jnp.sin(w * f + phase))
    fr = fr_ref[...]
    h = jnp.sin(fr * (jnp.dot(z, w1_ref[...], precision=hi, preferred_element_type=F32) + b1_ref[...]))
    h = jnp.sin(fr * (jnp.dot(h, w2_ref[...], precision=hi, preferred_element_type=F32) + b2_ref[...]))
    h = jnp.sin(fr * (jnp.dot(h, w3_ref[...], precision=hi, preferred_element_type=F32) + b3_ref[...]))
    k = jnp.dot(h, w4_ref[...], precision=hi, preferred_element_type=F32) + b4_ref[...]
    ch = lax.broadcasted_iota(jnp.int32, (1, c), 1).astype(F32)
    delta = jnp.abs(HY_MIN_DECAY + ch * ((HY_MAX_DECAY - HY_MIN_DECAY) / (c - 1)))
    k = jnp.where(valid, k * jnp.exp(-t * delta), 0.0)
    k_ref[...] = k
    part = jnp.sum(jnp.abs(k), axis=0, keepdims=True)

    @pl.when(i == 0)
    def _():
        nrm_ref[...] = part

    @pl.when(i > 0)
    def _():
        nrm_ref[...] = nrm_ref[...] + part


def _pad_to(a, shape):
    return jnp.pad(a, [(0, s - n) for n, s in zip(a.shape, shape)])


def _hyena_filter(seq, w1, b1, w2, b2, w3, b3, w4, b4, freq):
    c = w4.shape[1] // 2
    lanes = 128
    t_blk = min(seq, 1024)
    half_blocks = seq // t_blk
    full = lambda shape: pl.BlockSpec(shape, lambda i: (0,) * len(shape))
    sq, row = (lanes, lanes), (1, lanes)
    return pl.pallas_call(
        functools.partial(_filt_kernel, seq=seq, t_blk=t_blk, c=c),
        grid=(2 * seq // t_blk,),
        in_specs=[full(sq), full(row), full(sq), full(row), full(sq), full(row),
                  pl.BlockSpec((lanes, c), lambda i: (0, i // half_blocks)),
                  pl.BlockSpec((1, c), lambda i: (0, i // half_blocks)),
                  full(row)],
        out_specs=[pl.BlockSpec((t_blk, c), lambda i: (i, 0)), pl.BlockSpec((1, c), lambda i: (0, 0))],
        out_shape=[jax.ShapeDtypeStruct((2 * seq, c), F32), jax.ShapeDtypeStruct((1, c), F32)],
        compiler_params=_cparams(("arbitrary",)),
        name="hyena_filter",
    )(_pad_to(w1, sq), _pad_to(b1[None, :], row), _pad_to(w2, sq), _pad_to(b2[None, :], row),
      _pad_to(w3, sq), _pad_to(b3[None, :], row), _pad_to(w4, (lanes, 2 * c)), b4[None, :],
      _pad_to(freq[None, :], row))


def _bitrev(p, bits):
    r = 0
    for k in range(bits):
        r = (r << 1) | ((p >> k) & 1)
    return r


@functools.lru_cache(maxsize=None)
def _fft_tables(s):
    s2 = s // 2
    n = s * FFT_B
    bits = max(s2.bit_length() - 1, 0)
    b = np.arange(FFT_B)
    ang = 2.0 * np.pi * np.outer(b, b) / FFT_B
    fr, fi = np.cos(ang), -np.sin(ang)
    m_fwd = np.block([[fr, -fi], [fi, fr]])
    m_inv = np.block([[fr, fi], [-fi, fr]])
    ka = np.array([[2 * _bitrev(p, bits) + hf for p in range(s2)] for hf in range(2)])
    mid = 2.0 * np.pi * ka[:, :, None] * b[None, None, :] / n
    lvl = 2.0 * np.pi * np.arange(max(s2 // 2, 1)) / s2
    first = 2.0 * np.pi * np.arange(s2) / s
    f32 = lambda x: np.asarray(x, np.float32)
    return dict(m_fwd=f32(m_fwd), m_inv=f32(m_inv), mid_c=f32(np.cos(mid)), mid_s=f32(np.sin(mid)),
                lvl_c=f32(np.cos(lvl)), lvl_s=f32(np.sin(lvl)), first_c=f32(np.cos(first)),
                first_s=f32(np.sin(first)))


def _butterfly_levels(wr, wi, lc_ref, ls_ref, s2, inverse):
    blk = FFT_B
    spans = []
    h = s2 // 2
    while h >= 1:
        spans.append(h)
        h //= 2
    if inverse:
        spans = spans[::-1]
    for h in spans:
        tstep = s2 // (2 * h)
        shift = h.bit_length() - 1

        def pair(jj, carry, h=h, tstep=tstep, shift=shift):
            g = jj >> shift
            i = jj & (h - 1)
            top = pl.multiple_of((g * 2 * h + i) * blk, blk)
            bot = pl.multiple_of((g * 2 * h + i + h) * blk, blk)
            cs = lc_ref[i * tstep]
            sn = ls_ref[i * tstep]
            tr, ti = wr[pl.ds(top, blk), :], wi[pl.ds(top, blk), :]
            br, bi = wr[pl.ds(bot, blk), :], wi[pl.ds(bot, blk), :]
            if inverse:
                xr = br * cs - bi * sn
                xi = bi * cs + br * sn
                wr[pl.ds(top, blk), :] = tr + xr
                wi[pl.ds(top, blk), :] = ti + xi
                wr[pl.ds(bot, blk), :] = tr - xr
                wi[pl.ds(bot, blk), :] = ti - xi
            else:
                dr, di = tr - br, ti - bi
                wr[pl.ds(top, blk), :] = tr + br
                wi[pl.ds(top, blk), :] = ti + bi
                wr[pl.ds(bot, blk), :] = dr * cs + di * sn
                wi[pl.ds(bot, blk), :] = di * cs - dr * sn
            return carry

        lax.fori_loop(0, s2 // 2, pair, 0)


def _col_twiddles(mc_ref, ms_ref, pp):
    out = []
    for q in range(2):
        cr = mc_ref[pl.ds(2 * pp + q, 1), :]
        sr = ms_ref[pl.ds(2 * pp + q, 1), :]
        out.append((jnp.broadcast_to(cr, (FFT_B, FFT_B)).T, jnp.broadcast_to(sr, (FFT_B, FFT_B)).T))
    return out


def _stack_pair(r0, r1, i0, i1):
    return jnp.concatenate([jnp.concatenate([r0, r1], axis=1), jnp.concatenate([i0, i1], axis=1)],
                           axis=0).astype(BF16)


def _fwd_dft_pair(wr, wi, mc_ref, ms_ref, mf_ref, pp):
    blk = FFT_B
    r0 = pl.multiple_of(pp * 2 * blk, 2 * blk)
    gr = wr[pl.ds(r0, 2 * blk), :]
    gi = wi[pl.ds(r0, 2 * blk), :]
    tw = _col_twiddles(mc_ref, ms_ref, pp)
    parts = []
    for q in range(2):
        xr, xi = gr[q * blk:(q + 1) * blk, :], gi[q * blk:(q + 1) * blk, :]
        cq, sq = tw[q]
        parts.append((xr * cq + xi * sq, xi * cq - xr * sq))
    rhs = _stack_pair(parts[0][0], parts[1][0], parts[0][1], parts[1][1])
    out = jnp.dot(mf_ref[...], rhs, preferred_element_type=F32)
    lanes = wr.shape[1]
    return [(out[0:blk, q * lanes:(q + 1) * lanes], out[blk:2 * blk, q * lanes:(q + 1) * lanes])
            for q in range(2)], tw, r0


def _spec_kernel(lo_ref, hi_ref, nrm_ref, fc_ref, fs_ref, lc_ref, ls_ref, mc_ref, ms_ref, mf_ref,
                 kr_ref, ki_ref, *, s2):
    blk = FFT_B
    hf = pl.program_id(1)
    n = 2 * s2 * blk

    @pl.when(hf == 0)
    def _():
        kr_ref[...] = lo_ref[...] + hi_ref[...]
        ki_ref[...] = jnp.zeros_like(ki_ref)

    @pl.when(hf == 1)
    def _():
        def first(a, carry):
            r0 = pl.multiple_of(a * blk, blk)
            dlt = lo_ref[pl.ds(r0, blk), :] - hi_ref[pl.ds(r0, blk), :]
            kr_ref[pl.ds(r0, blk), :] = dlt * fc_ref[a]
            ki_ref[pl.ds(r0, blk), :] = -dlt * fs_ref[a]
            return carry
        lax.fori_loop(0, s2, first, 0)

    _butterfly_levels(kr_ref, ki_ref, lc_ref, ls_ref, s2, inverse=False)
    scale = 1.0 / (nrm_ref[...] * n)

    def mid(pp, carry):
        res, _, r0 = _fwd_dft_pair(kr_ref, ki_ref, mc_ref, ms_ref, mf_ref, pp)
        for q in range(2):
            kr_ref[pl.ds(r0 + q * blk, blk), :] = res[q][0] * scale
            ki_ref[pl.ds(r0 + q * blk, blk), :] = res[q][1] * scale
        return carry

    lax.fori_loop(0, s2 // 2, mid, 0)


def _conv_kernel(u_ref, x0_ref, bias_ref, kr_ref, ki_ref, fc_ref, fs_ref, lc_ref, ls_ref, mc_ref, ms_ref,
                 mf_ref, mi_ref, y_ref, wr, wi, *, s2):
    blk = FFT_B
    hf = pl.program_id(1)

    @pl.when(hf == 0)
    def _():
        wr[...] = u_ref[...]
        wi[...] = jnp.zeros_like(wi)

    @pl.when(hf == 1)
    def _():
        def first(a, carry):
            r0 = pl.multiple_of(a * blk, blk)
            ub = u_ref[pl.ds(r0, blk), :]
            wr[pl.ds(r0, blk), :] = ub * fc_ref[a]
            wi[pl.ds(r0, blk), :] = -ub * fs_ref[a]
            return carry
        lax.fori_loop(0, s2, first, 0)

    _butterfly_levels(wr, wi, lc_ref, ls_ref, s2, inverse=False)

    def mid(pp, carry):
        res, tw, r0 = _fwd_dft_pair(wr, wi, mc_ref, ms_ref, mf_ref, pp)
        prod = []
        for q in range(2):
            kr = kr_ref[pl.ds(r0 + q * blk, blk), :]
            ki = ki_ref[pl.ds(r0 + q * blk, blk), :]
            yr, yi = res[q]
            prod.append((yr * kr - yi * ki, yr * ki + yi * kr))
        rhs = _stack_pair(prod[0][0], prod[1][0], prod[0][1], prod[1][1])
        out = jnp.dot(mi_ref[...], rhs, preferred_element_type=F32)
        lanes = wr.shape[1]
        for q in range(2):
            er = out[0:blk, q * lanes:(q + 1) * lanes]
            ei = out[blk:2 * blk, q * lanes:(q + 1) * lanes]
            cq, sq = tw[q]
            wr[pl.ds(r0 + q * blk, blk), :] = er * cq - ei * sq
            wi[pl.ds(r0 + q * blk, blk), :] = ei * cq + er * sq
        return carry

    lax.fori_loop(0, s2 // 2, mid, 0)
    _butterfly_levels(wr, wi, lc_ref, ls_ref, s2, inverse=True)

    @pl.when(hf == 0)
    def _():
        y_ref[...] = wr[...]

    @pl.when(hf == 1)
    def _():
        def last(a, carry):
            r0 = pl.multiple_of(a * blk, blk)
            ub = u_ref[pl.ds(r0, blk), :]
            conv = (y_ref[pl.ds(r0, blk), :] + wr[pl.ds(r0, blk), :] * fc_ref[a]
                    - wi[pl.ds(r0, blk), :] * fs_ref[a])
            y_ref[pl.ds(r0, blk), :] = (conv + ub * bias_ref[...]) * x0_ref[pl.ds(r0, blk), :]
            return carry
        lax.fori_loop(0, s2, last, 0)


def _smem_spec():
    return pl.BlockSpec(memory_space=pltpu.SMEM)


def _hyena_spectrum(kfull, nrm):
    n, c = kfull.shape
    seq = n // 2
    s2 = seq // FFT_B
    tb = _fft_tables(2 * s2)
    lanes = 128
    return pl.pallas_call(
        functools.partial(_spec_kernel, s2=s2),
        grid=(c // lanes, 2),
        in_specs=[pl.BlockSpec((seq, lanes), lambda ct, hf: (0, ct)),
                  pl.BlockSpec((seq, lanes), lambda ct, hf: (1, ct)),
                  pl.BlockSpec((1, lanes), lambda ct, hf: (0, ct)),
                  _smem_spec(), _smem_spec(), _smem_spec(), _smem_spec(),
                  pl.BlockSpec((None, s2, FFT_B), lambda ct, hf: (hf, 0, 0)),
                  pl.BlockSpec((None, s2, FFT_B), lambda ct, hf: (hf, 0, 0)),
                  pl.BlockSpec((2 * FFT_B, 2 * FFT_B), lambda ct, hf: (0, 0))],
        out_specs=[pl.BlockSpec((None, seq, lanes), lambda ct, hf: (hf, 0, ct)),
                   pl.BlockSpec((None, seq, lanes), lambda ct, hf: (hf, 0, ct))],
        out_shape=[jax.ShapeDtypeStruct((2, seq, c), F32), jax.ShapeDtypeStruct((2, seq, c), F32)],
        compiler_params=_cparams(("arbitrary", "arbitrary")),
        name="hyena_spectrum",
    )(kfull, kfull, nrm, tb["first_c"], tb["first_s"], tb["lvl_c"], tb["lvl_s"],
      tb["mid_c"], tb["mid_s"], jnp.asarray(tb["m_fwd"], BF16))


def _hyena_conv(u, x0c, bias, kr, ki):
    seq, c = u.shape
    s2 = seq // FFT_B
    tb = _fft_tables(2 * s2)
    lanes = 128
    return pl.pallas_call(
        functools.partial(_conv_kernel, s2=s2),
        grid=(c // lanes, 2),
        in_specs=[pl.BlockSpec((seq, lanes), lambda ct, hf: (0, ct)),
                  pl.BlockSpec((seq, lanes), lambda ct, hf: (0, ct)),
                  pl.BlockSpec((1, lanes), lambda ct, hf: (0, ct)),
                  pl.BlockSpec((None, seq, lanes), lambda ct, hf: (hf, 0, ct)),
                  pl.BlockSpec((None, seq, lanes), lambda ct, hf: (hf, 0, ct)),
                  _smem_spec(), _smem_spec(), _smem_spec(), _smem_spec(),
                  pl.BlockSpec((None, s2, FFT_B), lambda ct, hf: (hf, 0, 0)),
                  pl.BlockSpec((None, s2, FFT_B), lambda ct, hf: (hf, 0, 0)),
                  pl.BlockSpec((2 * FFT_B, 2 * FFT_B), lambda ct, hf: (0, 0)),
                  pl.BlockSpec((2 * FFT_B, 2 * FFT_B), lambda ct, hf: (0, 0))],
        out_specs=pl.BlockSpec((seq, lanes), lambda ct, hf: (0, ct)),
        out_shape=jax.ShapeDtypeStruct((seq, c), F32),
        scratch_shapes=[pltpu.VMEM((seq, lanes), F32), pltpu.VMEM((seq, lanes), F32)],
        compiler_params=_cparams(("arbitrary", "arbitrary")),
        name="hyena_fft_conv",
    )(u, x0c, bias, kr, ki, tb["first_c"], tb["first_s"], tb["lvl_c"], tb["lvl_s"],
      tb["mid_c"], tb["mid_s"], jnp.asarray(tb["m_fwd"], BF16), jnp.asarray(tb["m_inv"], BF16))


def _mixout_kernel(h_ref, yr_ref, hy_ref, og_ref, w_ref, x_ref, mod_ref, o_ref, cat_scr, *, stream, lw):
    @pl.when(pl.program_id(1) == 0)
    def _():
        yr = yr_ref[...]
        gelu = 0.5 * yr * (1.0 + jnp.tanh(math.sqrt(2.0 / math.pi) * (yr + 0.044715 * (yr * yr * yr))))
        cat_scr[:, :lw] = (_rms(h_ref[...] * gelu) * og_ref[:, :lw]).astype(BF16)
        cat_scr[:, lw:] = (_rms(hy_ref[...]) * og_ref[:, lw:]).astype(BF16)

    y = jnp.dot(cat_scr[...], w_ref[...].astype(BF16), preferred_element_type=F32)
    gate = mod_ref[3 * stream + 2:3 * stream + 3, :]
    o_ref[...] = x_ref[...] + gate * y


def _mixout(h_lru, p, y_hy, og, w_out, l, x, mod, stream):
    rows, d = x.shape
    lw = h_lru.shape[1]
    dm = w_out.shape[1]
    tm = min(rows, 1024)
    tn = 512
    return pl.pallas_call(
        functools.partial(_mixout_kernel, stream=stream, lw=lw),
        grid=(rows // tm, d // tn),
        in_specs=[pl.BlockSpec((tm, lw), lambda i, n: (i, 0)),
                  pl.BlockSpec((tm, lw), lambda i, n: (i, 1)),
                  pl.BlockSpec((tm, dm - lw), lambda i, n: (i, 0)),
                  pl.BlockSpec((1, dm), lambda i, n: (0, 0)),
                  pl.BlockSpec((None, dm, tn), lambda i, n: (l, 0, n)),
                  pl.BlockSpec((tm, tn), lambda i, n: (i, n)),
                  pl.BlockSpec((SUBLANES, tn), lambda i, n: (0, n))],
        out_specs=pl.BlockSpec((tm, tn), lambda i, n: (i, n)),
        out_shape=jax.ShapeDtypeStruct((rows, d), F32),
        scratch_shapes=[pltpu.VMEM((tm, dm), BF16)],
        compiler_params=_cparams(("arbitrary", "arbitrary")),
        name="mixer_out_proj",
    )(h_lru, p, y_hy, og, w_out, x, mod)


def _final_kernel(x_ref, g_ref, o_ref, *, pieces, prow, d):
    for q in range(pieces):
        o_ref[q * prow:(q + 1) * prow, :] = _rms(x_ref[:, q * d:(q + 1) * d]) * g_ref[...]


def _final_norm(x, g, from_col):
    rows, d = x.shape
    tm = 1024
    if from_col:
        prow, pieces = GRID_W, tm // GRID_W
        xv = x.reshape(prow, (rows // prow) * d)
        x_spec = pl.BlockSpec((prow, pieces * d), lambda i: (0, i))
    else:
        prow, pieces, xv = tm, 1, x
        x_spec = pl.BlockSpec((tm, d), lambda i: (i, 0))
    return pl.pallas_call(
        functools.partial(_final_kernel, pieces=pieces, prow=prow, d=d),
        grid=(rows // tm,),
        in_specs=[x_spec, pl.BlockSpec((1, d), lambda i: (0, 0))],
        out_specs=pl.BlockSpec((tm, d), lambda i: (i, 0)),
        out_shape=jax.ShapeDtypeStruct((rows, d), F32),
        compiler_params=_cparams(("arbitrary",)),
        name="final_norm",
    )(xv, g)


def kernel(x, c, ctx, c_ctx, ada_w, ada_b, norm_g, ffn_wg, ffn_wu, ffn_wd, w_in, w_out, out_g, lru_conv_w, lru_conv_b, lru_wa, lru_ba, lru_wx, lru_bx, lru_lam, hy_conv_w, hy_conv_b, hy_bias, filt_w1, filt_b1, filt_w2, filt_b2, filt_w3, filt_b3, filt_w4, filt_b4, filt_freq, final_g):
    assert x.shape[0] == 1 and ctx.shape[0] == 1
    depth = ada_w.shape[0]
    d = x.shape[-1]
    lw = lru_conv_w.shape[-1]
    hw = hy_bias.shape[-1]
    n_lat, n_ctx = x.shape[1], ctx.shape[1]
    xl, xc = x[0], ctx[0]
    mods = _ada_mods(c, c_ctx, ada_w, ada_b)
    zero_state = jnp.zeros((1, lw), F32)
    lat_is_col = False
    for l in range(depth):
        last = l == depth - 1
        m = [_sublayer_mods(mods[l], d, k) for k in range(3)]
        g = norm_g[l][:, None, :]

        xl = _ffn(xl, m[0], g[0], ffn_wg, ffn_wu, ffn_wd, l, 0, 0)
        xc = _ffn(xc, m[0], g[0], ffn_wg, ffn_wu, ffn_wd, l, 0, 1)

        want_col = l % 2 == 1
        perm = None if want_col == lat_is_col else ("to_col" if want_col else "from_col")
        lat_is_col = want_col
        pl_, xl = _win(xl, m[1], g[1], w_in, l, w_in.shape[-1], 0, perm)
        pc_, _ = _win(xc, m[1], g[1], w_in, l, lw if last else w_in.shape[-1], 1, None)

        cw, cb = lru_conv_w[l], lru_conv_b[l][None, :]
        h_lat = h_ctx = None
        for dr in range(2):
            wax = jnp.concatenate([lru_wa[l, dr], lru_wx[l, dr]], axis=-1).astype(BF16)
            ba, bx, lam = lru_ba[l, dr][None, :], lru_bx[l, dr][None, :], lru_lam[l, dr][None, :]
            h_ctx, state = _lru_scan(pc_, cw, cb, wax, ba, bx, lam, zero_state, h_ctx, dr == 1)
            h_lat, _ = _lru_scan(pl_, cw, cb, wax, ba, bx, lam, state, h_lat, dr == 1)

        filt = (filt_w1[l], filt_b1[l], filt_w2[l], filt_b2[l], filt_w3[l], filt_b3[l], filt_w4[l],
                filt_b4[l], filt_freq[l])
        hcw, hcb, hbias = hy_conv_w[l], hy_conv_b[l][None, :], hy_bias[l][None, :]
        og = out_g[l][None, :]

        u_l, x0_l = _hypre(pl_, 2 * lw, hcw, hcb)
        kr, ki = _hyena_spectrum(*_hyena_filter(n_lat, *filt))
        y_hy = _hyena_conv(u_l, x0_l, hbias, kr, ki)
        xl = _mixout(h_lat, pl_, y_hy, og, w_out, l, xl, m[1], 0)
        xl = _ffn(xl, m[2], g[2], ffn_wg, ffn_wu, ffn_wd, l, 1, 0)

        if not last:
            u_c, x0_c = _hypre(pc_, 2 * lw, hcw, hcb)
            kr, ki = _hyena_spectrum(*_hyena_filter(n_ctx, *filt))
            y_hy = _hyena_conv(u_c, x0_c, hbias, kr, ki)
            xc = _mixout(h_ctx, pc_, y_hy, og, w_out, l, xc, m[1], 1)
            xc = _ffn(xc, m[2], g[2], ffn_wg, ffn_wu, ffn_wd, l, 1, 1)

    return _final_norm(xl, final_g[None, :], lat_is_col)[None]
```

```python
import functools
import math

import numpy as np
import jax
import jax.numpy as jnp
from jax import lax
from jax.experimental import pallas as pl
from jax.experimental.pallas import tpu as pltpu

F32 = jnp.float32
BF16 = jnp.bfloat16
EPS = 1e-6
LRU_C = 8.0
GRID_W = 64
HY_BANDS = 16
HY_MAX_DECAY = math.log(1e-2) / 0.3
HY_MIN_DECAY = math.log(1e-2) / 1.5

V7X_VMEM_LIMIT_BYTES = 58 * 1024 * 1024
FFT_B = 128
SUBLANES = 8


def _cparams(sem):
    return pltpu.CompilerParams(dimension_semantics=sem, vmem_limit_bytes=V7X_VMEM_LIMIT_BYTES)


def _rms(x):
    return x * lax.rsqrt(jnp.mean(x * x, axis=-1, keepdims=True) + EPS)


def _modulated(x, mod_ref, g_ref, stream):
    shift = mod_ref[3 * stream:3 * stream + 1, :]
    scale = mod_ref[3 * stream + 1:3 * stream + 2, :]
    return _rms(x) * g_ref[...] * (1.0 + scale) + shift


def _ada_kernel(c_ref, w_ref, b_ref, o_ref):
    c = c_ref[...]
    s = c * jax.nn.sigmoid(c)
    o_ref[...] = jnp.dot(s, w_ref[...], precision=lax.Precision.HIGHEST,
                         preferred_element_type=F32) + b_ref[...]


def _ada_mods(c, c_ctx, ada_w, ada_b):
    depth, d, nm = ada_w.shape
    tn = d
    cc = jnp.zeros((SUBLANES, d), F32).at[0].set(c[0]).at[1].set(c_ctx)
    out = pl.pallas_call(
        _ada_kernel,
        grid=(depth, nm // tn),
        in_specs=[pl.BlockSpec((SUBLANES, d), lambda l, j: (0, 0)),
                  pl.BlockSpec((None, d, tn), lambda l, j: (l, 0, j)),
                  pl.BlockSpec((None, 1, tn), lambda l, j: (l, 0, j))],
        out_specs=pl.BlockSpec((None, SUBLANES, tn), lambda l, j: (l, 0, j)),
        out_shape=jax.ShapeDtypeStruct((depth, SUBLANES, nm), F32),
        compiler_params=_cparams(("arbitrary", "arbitrary")),
        name="ada_mods",
    )(cc, ada_w, ada_b[:, None, :])
    return out


def _sublayer_mods(mods_l, d, k):
    lat = mods_l[0, 3 * k * d:(3 * k + 3) * d].reshape(3, d)
    ctx = mods_l[1, 3 * k * d:(3 * k + 3) * d].reshape(3, d)
    return jnp.concatenate([lat, ctx, jnp.zeros((2, d), F32)], axis=0)


def _ffn_up_kernel(x_ref, mod_ref, g_ref, wg_ref, wu_ref, h_ref, u_scr, *, stream):
    @pl.when(pl.program_id(1) == 0)
    def _():
        u_scr[...] = _modulated(x_ref[...], mod_ref, g_ref, stream).astype(BF16)

    u = u_scr[...]
    g = jnp.dot(u, wg_ref[...].astype(BF16), preferred_element_type=F32)
    up = jnp.dot(u, wu_ref[...].astype(BF16), preferred_element_type=F32)
    h_ref[...] = (g * jax.nn.sigmoid(g) * up).astype(BF16)


def _ffn_down_kernel(h_ref, wd_ref, x_ref, mod_ref, o_ref, *, stream):
    acc = jnp.dot(h_ref[...], wd_ref[...].astype(BF16), preferred_element_type=F32)
    gate = mod_ref[3 * stream + 2:3 * stream + 3, :]
    o_ref[...] = x_ref[...] + (0.5 * gate) * acc


def _ffn(x, mod, g, wg, wu, wd, l, j, stream):
    rows, d = x.shape
    dff = wg.shape[-1]
    tm = min(rows, 1024)
    tf = 512
    tn = 256
    h = pl.pallas_call(
        functools.partial(_ffn_up_kernel, stream=stream),
        grid=(rows // tm, dff // tf),
        in_specs=[pl.BlockSpec((tm, d), lambda i, f: (i, 0)),
                  pl.BlockSpec((SUBLANES, d), lambda i, f: (0, 0)),
                  pl.BlockSpec((1, d), lambda i, f: (0, 0)),
                  pl.BlockSpec((None, None, d, tf), lambda i, f: (l, j, 0, f)),
                  pl.BlockSpec((None, None, d, tf), lambda i, f: (l, j, 0, f))],
        out_specs=pl.BlockSpec((tm, tf), lambda i, f: (i, f)),
        out_shape=jax.ShapeDtypeStruct((rows, dff), BF16),
        scratch_shapes=[pltpu.VMEM((tm, d), BF16)],
        compiler_params=_cparams(("arbitrary", "arbitrary")),
        name="ffn_up",
    )(x, mod, g, wg, wu)
    return pl.pallas_call(
        functools.partial(_ffn_down_kernel, stream=stream),
        grid=(rows // tm, d // tn),
        in_specs=[pl.BlockSpec((tm, dff), lambda i, n: (i, 0)),
                  pl.BlockSpec((None, None, dff, tn), lambda i, n: (l, j, 0, n)),
                  pl.BlockSpec((tm, tn), lambda i, n: (i, n)),
                  pl.BlockSpec((SUBLANES, tn), lambda i, n: (0, n))],
        out_specs=pl.BlockSpec((tm, tn), lambda i, n: (i, n)),
        out_shape=jax.ShapeDtypeStruct((rows, d), F32),
        compiler_params=_cparams(("arbitrary", "arbitrary")),
        name="ffn_down",
    )(h, wd, x, mod)


def _win_kernel(x_ref, mod_ref, g_ref, w_ref, p_ref, *rest, pieces, prow, d, stream, emit_x):
    if emit_x:
        xo_ref, u_scr = rest
    else:
        (u_scr,) = rest

    @pl.when(pl.program_id(1) == 0)
    def _():
        for q in range(pieces):
            xt = x_ref[:, q * d:(q + 1) * d]
            u_scr[q * prow:(q + 1) * prow, :] = _modulated(xt, mod_ref, g_ref, stream).astype(BF16)
            if emit_x:
                xo_ref[q * prow:(q + 1) * prow, :] = xt

    p_ref[...] = jnp.dot(u_scr[...], w_ref[...].astype(BF16), preferred_element_type=F32)


def _win(x, mod, g, w_in, l, ncols, stream, perm):
    rows, d = x.shape
    tm = min(rows, 1024)
    tn = 512
    if perm is None:
        prow, pieces, xv = tm, 1, x
        x_spec = pl.BlockSpec((tm, d), lambda i, n: (i, 0))
    else:
        grid_h = rows // GRID_W
        prow = grid_h if perm == "to_col" else GRID_W
        pieces = tm // prow
        xv = x.reshape(prow, (rows // prow) * d)
        x_spec = pl.BlockSpec((prow, pieces * d), lambda i, n: (0, i))
    emit_x = perm is not None
    out_shape = [jax.ShapeDtypeStruct((rows, ncols), F32)]
    out_specs = [pl.BlockSpec((tm, tn), lambda i, n: (i, n))]
    if emit_x:
        out_shape.append(jax.ShapeDtypeStruct((rows, d), F32))
        out_specs.append(pl.BlockSpec((tm, d), lambda i, n: (i, 0)))
    res = pl.pallas_call(
        functools.partial(_win_kernel, pieces=pieces, prow=prow, d=d, stream=stream, emit_x=emit_x),
        grid=(rows // tm, ncols // tn),
        in_specs=[x_spec,
                  pl.BlockSpec((SUBLANES, d), lambda i, n: (0, 0)),
                  pl.BlockSpec((1, d), lambda i, n: (0, 0)),
                  pl.BlockSpec((None, d, tn), lambda i, n: (l, 0, n))],
        out_specs=out_specs,
        out_shape=out_shape,
        scratch_shapes=[pltpu.VMEM((tm, d), BF16)],
        compiler_params=_cparams(("arbitrary", "arbitrary")),
        name="mixer_in_proj",
    )(xv, mod, g, w_in)
    return (res[0], res[1]) if emit_x else (res[0], x)


def _lru_kernel(x_ref, xp_ref, xn_ref, cw_ref, cb_ref, wax_ref, ba_ref, bx_ref, lam_ref, h0_ref, *rest,
                t_blk, nblk, heads, reverse, add_prev):
    if add_prev:
        prev_ref, o_ref, hf_ref, xs, a_scr, b_scr, carry = rest
    else:
        o_ref, hf_ref, xs, a_scr, b_scr, carry = rest
    j = pl.program_id(0)
    blk = (nblk - 1 - j) if reverse else j
    c = x_ref.shape[1]
    hd = c // heads

    @pl.when(j == 0)
    def _():
        carry[...] = h0_ref[...]

    zero8 = jnp.zeros((SUBLANES, c), F32)
    xs[0:SUBLANES, :] = jnp.where(blk == 0, zero8, xp_ref[...])
    xs[SUBLANES:SUBLANES + t_blk, :] = x_ref[...]
    xs[SUBLANES + t_blk:2 * SUBLANES + t_blk, :] = jnp.where(blk == nblk - 1, zero8, xn_ref[...])
    cv = cb_ref[...]
    for k in range(cw_ref.shape[0]):
        cv = cv + xs[SUBLANES - 1 + k:SUBLANES - 1 + k + t_blk, :] * cw_ref[k:k + 1, :]

    z = -lam_ref[...]
    c_sp = LRU_C * (jnp.maximum(z, 0.0) + jnp.log(1.0 + jnp.exp(-jnp.abs(z))))
    cvb = cv.astype(BF16)
    for h in range(heads):
        sl = slice(h * hd, (h + 1) * hd)
        gx = jnp.dot(cvb[:, sl], wax_ref[h], preferred_element_type=F32)
        r = jax.nn.sigmoid(gx[:, :hd] + ba_ref[:, sl])
        ig = jax.nn.sigmoid(gx[:, hd:] + bx_ref[:, sl])
        log_a = -c_sp[:, sl] * r
        a_scr[:, sl] = jnp.exp(log_a)
        b_scr[:, sl] = jnp.sqrt(1.0 - jnp.exp(2.0 * log_a)) * (ig * cv[:, sl])

    row = lax.broadcasted_iota(jnp.int32, (SUBLANES, c), 0)
    ngrp = t_blk // SUBLANES

    def group(gi, hc):
        g = (ngrp - 1 - gi) if reverse else gi
        r0 = pl.multiple_of(g * SUBLANES, SUBLANES)
        av = a_scr[pl.ds(r0, SUBLANES), :]
        bv = b_scr[pl.ds(r0, SUBLANES), :]
        for s in (1, 2, 4):
            sh = (SUBLANES - s) if reverse else s
            m = (row < SUBLANES - s) if reverse else (row >= s)
            a_s = pltpu.roll(av, sh, axis=0)
            b_s = pltpu.roll(bv, sh, axis=0)
            bv = jnp.where(m, av * b_s + bv, bv)
            av = jnp.where(m, av * a_s, av)
        hv = av * hc + bv
        if add_prev:
            o_ref[pl.ds(r0, SUBLANES), :] = hv + prev_ref[pl.ds(r0, SUBLANES), :]
        else:
            o_ref[pl.ds(r0, SUBLANES), :] = hv
        return hv[0:1, :] if reverse else hv[SUBLANES - 1:SUBLANES, :]

    hc = lax.fori_loop(0, ngrp, group, carry[...])
    carry[...] = hc
    hf_ref[...] = hc


def _lru_scan(p, cw, cb, wax, ba, bx, lam, h0, prev, reverse):
    seq = p.shape[0]
    c = cw.shape[1]
    heads = wax.shape[0]
    t_blk = 256
    nblk = seq // t_blk
    per = t_blk // SUBLANES
    nb8 = seq // SUBLANES

    def bi(j):
        return (nblk - 1 - j) if reverse else j

    in_specs = [pl.BlockSpec((t_blk, c), lambda j: (bi(j), 0)),
                pl.BlockSpec((SUBLANES, c), lambda j: (jnp.maximum(bi(j) * per - 1, 0), 0)),
                pl.BlockSpec((SUBLANES, c), lambda j: (jnp.minimum((bi(j) + 1) * per, nb8 - 1), 0)),
                pl.BlockSpec(cw.shape, lambda j: (0, 0)),
                pl.BlockSpec((1, c), lambda j: (0, 0)),
                pl.BlockSpec(wax.shape, lambda j: (0, 0, 0)),
                pl.BlockSpec((1, c), lambda j: (0, 0)),
                pl.BlockSpec((1, c), lambda j: (0, 0)),
                pl.BlockSpec((1, c), lambda j: (0, 0)),
                pl.BlockSpec((1, c), lambda j: (0, 0))]
    args = [p, p, p, cw, cb, wax, ba, bx, lam, h0]
    if prev is not None:
        in_specs.append(pl.BlockSpec((t_blk, c), lambda j: (bi(j), 0)))
        args.append(prev)
    return pl.pallas_call(
        functools.partial(_lru_kernel, t_blk=t_blk, nblk=nblk, heads=heads, reverse=reverse,
                          add_prev=prev is not None),
        grid=(nblk,),
        in_specs=in_specs,
        out_specs=[pl.BlockSpec((t_blk, c), lambda j: (bi(j), 0)),
                   pl.BlockSpec((1, c), lambda j: (0, 0))],
        out_shape=[jax.ShapeDtypeStruct((seq, c), F32), jax.ShapeDtypeStruct((1, c), F32)],
        scratch_shapes=[pltpu.VMEM((t_blk + 2 * SUBLANES, c), F32),
                        pltpu.VMEM((t_blk, c), F32),
                        pltpu.VMEM((t_blk, c), F32),
                        pltpu.VMEM((1, c), F32)],
        compiler_params=_cparams(("arbitrary",)),
        name="rglru_scan",
    )(*args)


def _hypre_kernel(x0_ref, x0p, x0n, x1_ref, x1p, x1n, v_ref, vp, vn, cw_ref, cb_ref, u_ref, x0c_ref, xs,
                  *, t_blk, nblk, c):
    blk = pl.program_id(0)
    zero8 = jnp.zeros((SUBLANES, c), F32)

    def conv3(cur, prv, nxt, part):
        xs[0:SUBLANES, :] = jnp.where(blk == 0, zero8, prv[...])
        xs[SUBLANES:SUBLANES + t_blk, :] = cur[...]
        xs[SUBLANES + t_blk:2 * SUBLANES + t_blk, :] = jnp.where(blk == nblk - 1, zero8, nxt[...])
        sl = slice(part * c, (part + 1) * c)
        out = cb_ref[:, sl]
        for k in range(cw_ref.shape[0]):
            out = out + xs[SUBLANES - 1 + k:SUBLANES - 1 + k + t_blk, :] * cw_ref[k:k + 1, sl]
        return out

    x0c_ref[...] = conv3(x0_ref, x0p, x0n, 0)
    x1c = conv3(x1_ref, x1p, x1n, 1)
    u_ref[...] = conv3(v_ref, vp, vn, 2) * x1c


def _hypre(p, col0, cw, cb):
    seq = p.shape[0]
    c = cw.shape[1] // 3
    t_blk = min(seq, 512)
    nblk = seq // t_blk
    per = t_blk // SUBLANES
    nb8 = seq // SUBLANES
    cb0 = col0 // c
    in_specs, args = [], []
    for part in range(3):
        in_specs += [pl.BlockSpec((t_blk, c), lambda j, part=part: (j, cb0 + part)),
                     pl.BlockSpec((SUBLANES, c), lambda j, part=part: (jnp.maximum(j * per - 1, 0), cb0 + part)),
                     pl.BlockSpec((SUBLANES, c),
                                  lambda j, part=part: (jnp.minimum((j + 1) * per, nb8 - 1), cb0 + part))]
        args += [p, p, p]
    in_specs += [pl.BlockSpec(cw.shape, lambda j: (0, 0)), pl.BlockSpec((1, 3 * c), lambda j: (0, 0))]
    args += [cw, cb]
    return pl.pallas_call(
        functools.partial(_hypre_kernel, t_blk=t_blk, nblk=nblk, c=c),
        grid=(nblk,),
        in_specs=in_specs,
        out_specs=[pl.BlockSpec((t_blk, c), lambda j: (j, 0)), pl.BlockSpec((t_blk, c), lambda j: (j, 0))],
        out_shape=[jax.ShapeDtypeStruct((seq, c), F32), jax.ShapeDtypeStruct((seq, c), F32)],
        scratch_shapes=[pltpu.VMEM((t_blk + 2 * SUBLANES, c), F32)],
        compiler_params=_cparams(("arbitrary",)),
        name="hyena_short_conv",
    )(*args)


FILT_SLOT = 64


def _filt_kernel(w1_ref, b1_ref, w2_ref, b2_ref, w3_ref, b3_ref, w4a_ref, w4b_ref, b4_ref, fr_ref,
                 k_ref, nrm_ref, *, seq, t_blk, c):
    i = pl.program_id(0)
    hi = lax.Precision.HIGHEST
    lanes = 2 * FILT_SLOT
    th = t_blk // 2
    lane = lax.broadcasted_iota(jnp.int32, (1, lanes), 1)
    slot = lane & (FILT_SLOT - 1)
    pos_a = i * t_blk + lax.broadcasted_iota(jnp.int32, (th, 1), 0)
    pos = jnp.where(lane < FILT_SLOT, pos_a, pos_a + th)
    tf = jnp.where(pos < seq, pos, 2 * seq - pos).astype(F32)
    t = tf * (1.0 / (seq - 1))
    w = (2.0 * math.pi / seq) * tf
    is_cos = (slot >= 1) & (slot <= HY_BANDS)
    is_sin = (slot > HY_BANDS) & (slot <= 2 * HY_BANDS)
    band = jnp.where(is_cos, slot - 1, slot - 1 - HY_BANDS).astype(F32)
    f = 1e-4 + band * ((HY_BANDS - 1 - 1e-4) / (HY_BANDS - 1))
    phase = jnp.where(is_cos, 0.5 * math.pi, 0.0)
    sign = jnp.where(is_cos, 1.0, jnp.where(is_sin, -1.0, 0.0))
    z = jnp.where(slot == 0, t, sign * jnp.sin(w * f + phase))
    fr = fr_ref[...]
    h = jnp.sin(fr * (jnp.dot(z, w1_ref[...], precision=hi, preferred_element_type=F32) + b1_ref[...]))
    h = jnp.sin(fr * (jnp.dot(h, w2_ref[...], precision=hi, preferred_element_type=F32) + b2_ref[...]))
    h = jnp.sin(fr * (jnp.dot(h, w3_ref[...], precision=hi, preferred_element_type=F32) + b3_ref[...]))
    ch = lax.broadcasted_iota(jnp.int32, (1, c), 1).astype(F32)
    delta = jnp.abs(HY_MIN_DECAY + ch * ((HY_MAX_DECAY - HY_MIN_DECAY) / (c - 1)))
    part = jnp.zeros((1, c), F32)
    for half, w4_ref in enumerate((w4a_ref, w4b_ref)):
        pos_h = pos_a + half * th
        t_h = jnp.where(pos_h < seq, pos_h, 2 * seq - pos_h).astype(F32) * (1.0 / (seq - 1))
        k = jnp.dot(h, w4_ref[...], precision=hi, preferred_element_type=F32) + b4_ref[...]
        k = jnp.where(pos_h != seq, k * jnp.exp(-t_h * delta), 0.0)
        k_ref[half * th:(half + 1) * th, :] = k
        part = part + jnp.sum(jnp.abs(k), axis=0, keepdims=True)

    @pl.when(i == 0)
    def _():
        nrm_ref[...] = part

    @pl.when(i > 0)
    def _():
        nrm_ref[...] = nrm_ref[...] + part


def _slots(a, row_used):
    r, cdim = a.shape
    blk = jnp.pad(a, ((0, (FILT_SLOT if row_used else 1) - r), (0, FILT_SLOT - cdim)))
    if not row_used:
        return jnp.concatenate([blk, blk], axis=1)
    z = jnp.zeros_like(blk)
    return jnp.concatenate([jnp.concatenate([blk, z], axis=1), jnp.concatenate([z, blk], axis=1)], axis=0)


def _hyena_filter(seq, w1, b1, w2, b2, w3, b3, w4, b4, freq):
    c = w4.shape[1] // 2
    od = w2.shape[0]
    assert od <= FILT_SLOT and w1.shape[0] <= FILT_SLOT
    lanes = 2 * FILT_SLOT
    t_blk = min(seq, 1024)
    half_blocks = seq // t_blk
    full = lambda shape: pl.BlockSpec(shape, lambda i: (0,) * len(shape))
    sq, row = (lanes, lanes), (1, lanes)
    w4p = jnp.pad(w4, ((0, FILT_SLOT - od), (0, 0)))
    w4a = jnp.concatenate([w4p, jnp.zeros_like(w4p)], axis=0)
    w4b = jnp.concatenate([jnp.zeros_like(w4p), w4p], axis=0)
    w4_spec = pl.BlockSpec((lanes, c), lambda i: (0, i // half_blocks))
    return pl.pallas_call(
        functools.partial(_filt_kernel, seq=seq, t_blk=t_blk, c=c),
        grid=(2 * seq // t_blk,),
        in_specs=[full(sq), full(row), full(sq), full(row), full(sq), full(row), w4_spec, w4_spec,
                  pl.BlockSpec((1, c), lambda i: (0, i // half_blocks)),
                  full(row)],
        out_specs=[pl.BlockSpec((t_blk, c), lambda i: (i, 0)), pl.BlockSpec((1, c), lambda i: (0, 0))],
        out_shape=[jax.ShapeDtypeStruct((2 * seq, c), F32), jax.ShapeDtypeStruct((1, c), F32)],
        compiler_params=_cparams(("arbitrary",)),
        name="hyena_filter",
    )(_slots(w1, True), _slots(b1[None, :], False), _slots(w2, True), _slots(b2[None, :], False),
      _slots(w3, True), _slots(b3[None, :], False), w4a, w4b, b4[None, :], _slots(freq[None, :], False))


def _bitrev(p, bits):
    r = 0
    for k in range(bits):
        r = (r << 1) | ((p >> k) & 1)
    return r


def _real_block(z):
    return np.block([[z.real, -z.imag], [z.imag, z.real]])


def _fft_split(s2):
    n_lo = max(min(8, s2 // 2), 1)
    return s2 // n_lo, n_lo


@functools.lru_cache(maxsize=None)
def _fft_tables(s):
    s2 = s // 2
    n = s * FFT_B
    n_hi, n_lo = _fft_split(s2)
    bits_hi, bits_lo = n_hi.bit_length() - 1, n_lo.bit_length() - 1
    b = np.arange(FFT_B)
    dft = np.exp(-2j * np.pi * np.outer(b, b) / FFT_B)
    e_lo = [2 * n_hi * _bitrev(p, bits_lo) for p in range(n_lo)]
    g = [dft * np.exp(-2j * np.pi * b * e / n)[None, :] for e in e_lo]
    lhs_f = np.stack([_real_block(x) for x in g])
    lhs_i = np.stack([_real_block(np.conj(x).T) for x in g])
    e_hi = np.array([[2 * _bitrev(p, bits_hi) + hf for p in range(n_hi)] for hf in range(2)])
    ang = 2.0 * np.pi * e_hi[:, :, None, None] * b[None, None, :, None] / n * np.ones((1, 1, 1, FFT_B))
    lvl = 2.0 * np.pi * np.arange(max(s2 // 2, 1)) / s2
    first = 2.0 * np.pi * np.arange(s2) / s
    f32 = lambda x: np.asarray(x, np.float32)
    return dict(lhs_f=f32(lhs_f), lhs_i=f32(lhs_i), col_c=f32(np.cos(ang)), col_s=f32(np.sin(ang)),
                lvl_c=f32(np.cos(lvl)), lvl_s=f32(np.sin(lvl)), first_c=f32(np.cos(first)),
                first_s=f32(np.sin(first)))


def _butterfly_levels(wr, wi, lc_ref, ls_ref, s2, inverse):
    blk = FFT_B
    spans = []
    h = s2 // 2
    while h >= 1:
        spans.append(h)
        h //= 2
    if inverse:
        spans = spans[::-1]
    for h in spans:
        tstep = s2 // (2 * h)
        shift = h.bit_length() - 1

        def pair(jj, carry, h=h, tstep=tstep, shift=shift):
            g = jj >> shift
            i = jj & (h - 1)
            top = pl.multiple_of((g * 2 * h + i) * blk, blk)
            bot = pl.multiple_of((g * 2 * h + i + h) * blk, blk)
            cs = lc_ref[i * tstep]
            sn = ls_ref[i * tstep]
            tr, ti = wr[pl.ds(top, blk), :], wi[pl.ds(top, blk), :]
            br, bi = wr[pl.ds(bot, blk), :], wi[pl.ds(bot, blk), :]
            if inverse:
                xr = br * cs - bi * sn
                xi = bi * cs + br * sn
                wr[pl.ds(top, blk), :] = tr + xr
                wi[pl.ds(top, blk), :] = ti + xi
                wr[pl.ds(bot, blk), :] = tr - xr
                wi[pl.ds(bot, blk), :] = ti - xi
            else:
                dr, di = tr - br, ti - bi
                wr[pl.ds(top, blk), :] = tr + br
                wi[pl.ds(top, blk), :] = ti + bi
                wr[pl.ds(bot, blk), :] = dr * cs + di * sn
                wi[pl.ds(bot, blk), :] = di * cs - dr * sn
            return carry

        lax.fori_loop(0, s2 // 2, pair, 0)


MID_UNROLL = 4


def _pair_blocks(pp, s2):
    n_hi, n_lo = _fft_split(s2)
    half = n_hi // 2
    p_lo = pp >> (half.bit_length() - 1)
    j = pp & (half - 1)
    return p_lo, [(2 * j + q, pl.multiple_of(((2 * j + q) * n_lo + p_lo) * FFT_B, FFT_B)) for q in range(2)]


def _for_pairs(s2, body):
    def step(pp, carry):
        body(pp)
        return carry
    lax.fori_loop(0, s2 // 2, step, 0, unroll=min(MID_UNROLL, s2 // 2))


def _put_pair(panel, pp, q, re, im):
    blk = FFT_B
    panel[pp, 0:blk, q * blk:(q + 1) * blk] = re.astype(BF16)
    panel[pp, blk:2 * blk, q * blk:(q + 1) * blk] = im.astype(BF16)


def _twiddle_pack(wr, wi, cc_ref, cs_ref, panel, s2):
    blk = FFT_B

    def body(pp):
        _, blocks = _pair_blocks(pp, s2)
        for q, (p_hi, r0) in enumerate(blocks):
            xr = wr[pl.ds(r0, blk), :]
            xi = wi[pl.ds(r0, blk), :]
            cq, sq = cc_ref[p_hi], cs_ref[p_hi]
            _put_pair(panel, pp, q, xr * cq + xi * sq, xi * cq - xr * sq)

    _for_pairs(s2, body)


def _dft_pair(m_ref, panel, pp, p_lo):
    blk = FFT_B
    out = jnp.dot(m_ref[p_lo], panel[pp], preferred_element_type=F32)
    return [(out[0:blk, q * blk:(q + 1) * blk], out[blk:2 * blk, q * blk:(q + 1) * blk]) for q in range(2)]


def _spec_kernel(lo_ref, hi_ref, nrm_ref, fc_ref, fs_ref, lc_ref, ls_ref, cc_ref, cs_ref, mf_ref,
                 kr_ref, ki_ref, panel, *, s2):
    blk = FFT_B
    hf = pl.program_id(1)
    n = 2 * s2 * blk

    @pl.when(hf == 0)
    def _():
        kr_ref[...] = lo_ref[...] + hi_ref[...]
        ki_ref[...] = jnp.zeros_like(ki_ref)

    @pl.when(hf == 1)
    def _():
        def first(a, carry):
            r0 = pl.multiple_of(a * blk, blk)
            dlt = lo_ref[pl.ds(r0, blk), :] - hi_ref[pl.ds(r0, blk), :]
            kr_ref[pl.ds(r0, blk), :] = dlt * fc_ref[a]
            ki_ref[pl.ds(r0, blk), :] = -dlt * fs_ref[a]
            return carry
        lax.fori_loop(0, s2, first, 0)

    _butterfly_levels(kr_ref, ki_ref, lc_ref, ls_ref, s2, inverse=False)
    scale = 1.0 / (nrm_ref[...] * n)

    _twiddle_pack(kr_ref, ki_ref, cc_ref, cs_ref, panel, s2)

    def dft(pp):
        p_lo, blocks = _pair_blocks(pp, s2)
        res = _dft_pair(mf_ref, panel, pp, p_lo)
        for q, (_, r0) in enumerate(blocks):
            kr_ref[pl.ds(r0, blk), :] = res[q][0] * scale
            ki_ref[pl.ds(r0, blk), :] = res[q][1] * scale

    _for_pairs(s2, dft)


def _conv_kernel(u_ref, bias_ref, kr_ref, ki_ref, fc_ref, fs_ref, lc_ref, ls_ref, cc_ref, cs_ref,
                 mf_ref, mi_ref, y_ref, wr, wi, panel, *, s2):
    blk = FFT_B
    hf = pl.program_id(1)

    @pl.when(hf == 0)
    def _():
        wr[...] = u_ref[...]
        wi[...] = jnp.zeros_like(wi)

    @pl.when(hf == 1)
    def _():
        def first(a, carry):
            r0 = pl.multiple_of(a * blk, blk)
            ub = u_ref[pl.ds(r0, blk), :]
            wr[pl.ds(r0, blk), :] = ub * fc_ref[a]
            wi[pl.ds(r0, blk), :] = -ub * fs_ref[a]
            return carry
        lax.fori_loop(0, s2, first, 0)

    _butterfly_levels(wr, wi, lc_ref, ls_ref, s2, inverse=False)

    _twiddle_pack(wr, wi, cc_ref, cs_ref, panel, s2)

    def spectrum_product(pp):
        p_lo, blocks = _pair_blocks(pp, s2)
        res = _dft_pair(mf_ref, panel, pp, p_lo)
        for q, (_, r0) in enumerate(blocks):
            kr = kr_ref[pl.ds(r0, blk), :]
            ki = ki_ref[pl.ds(r0, blk), :]
            yr, yi = res[q]
            _put_pair(panel, pp, q, yr * kr - yi * ki, yr * ki + yi * kr)

    _for_pairs(s2, spectrum_product)

    def inverse_dft(pp):
        p_lo, blocks = _pair_blocks(pp, s2)
        res = _dft_pair(mi_ref, panel, pp, p_lo)
        for q, (p_hi, r0) in enumerate(blocks):
            er, ei = res[q]
            cq, sq = cc_ref[p_hi], cs_ref[p_hi]
            wr[pl.ds(r0, blk), :] = er * cq - ei * sq
            wi[pl.ds(r0, blk), :] = ei * cq + er * sq

    _for_pairs(s2, inverse_dft)
    _butterfly_levels(wr, wi, lc_ref, ls_ref, s2, inverse=True)

    @pl.when(hf == 0)
    def _():
        y_ref[...] = wr[...]

    @pl.when(hf == 1)
    def _():
        def last(a, carry):
            r0 = pl.multiple_of(a * blk, blk)
            ub = u_ref[pl.ds(r0, blk), :]
            conv = (y_ref[pl.ds(r0, blk), :] + wr[pl.ds(r0, blk), :] * fc_ref[a]
                    - wi[pl.ds(r0, blk), :] * fs_ref[a])
            y_ref[pl.ds(r0, blk), :] = conv + ub * bias_ref[...]
            return carry
        lax.fori_loop(0, s2, last, 0)


def _smem_spec():
    return pl.BlockSpec(memory_space=pltpu.SMEM)


def _fft_table_specs(s2):
    n_hi, n_lo = _fft_split(s2)
    col = pl.BlockSpec((None, n_hi, FFT_B, FFT_B), lambda ct, hf: (hf, 0, 0, 0))
    lhs = pl.BlockSpec((n_lo, 2 * FFT_B, 2 * FFT_B), lambda ct, hf: (0, 0, 0))
    return [_smem_spec(), _smem_spec(), _smem_spec(), _smem_spec(), col, col], lhs


def _hyena_spectrum(kfull, nrm):
    n, c = kfull.shape
    seq = n // 2
    s2 = seq // FFT_B
    tb = _fft_tables(2 * s2)
    lanes = 128
    tab_specs, lhs_spec = _fft_table_specs(s2)
    return pl.pallas_call(
        functools.partial(_spec_kernel, s2=s2),
        grid=(c // lanes, 2),
        in_specs=[pl.BlockSpec((seq, lanes), lambda ct, hf: (0, ct)),
                  pl.BlockSpec((seq, lanes), lambda ct, hf: (1, ct)),
                  pl.BlockSpec((1, lanes), lambda ct, hf: (0, ct))] + tab_specs + [lhs_spec],
        out_specs=[pl.BlockSpec((None, seq, lanes), lambda ct, hf: (hf, 0, ct)),
                   pl.BlockSpec((None, seq, lanes), lambda ct, hf: (hf, 0, ct))],
        out_shape=[jax.ShapeDtypeStruct((2, seq, c), F32), jax.ShapeDtypeStruct((2, seq, c), F32)],
        scratch_shapes=[pltpu.VMEM((s2 // 2, 2 * FFT_B, 2 * FFT_B), BF16)],
        compiler_params=_cparams(("arbitrary", "arbitrary")),
        name="hyena_spectrum",
    )(kfull, kfull, nrm, tb["first_c"], tb["first_s"], tb["lvl_c"], tb["lvl_s"],
      tb["col_c"], tb["col_s"], jnp.asarray(tb["lhs_f"], BF16))


def _hyena_conv(u, bias, kr, ki):
    seq, c = u.shape
    s2 = seq // FFT_B
    tb = _fft_tables(2 * s2)
    lanes = 128
    tab_specs, lhs_spec = _fft_table_specs(s2)
    return pl.pallas_call(
        functools.partial(_conv_kernel, s2=s2),
        grid=(c // lanes, 2),
        in_specs=[pl.BlockSpec((seq, lanes), lambda ct, hf: (0, ct)),
                  pl.BlockSpec((1, lanes), lambda ct, hf: (0, ct)),
                  pl.BlockSpec((None, seq, lanes), lambda ct, hf: (hf, 0, ct)),
                  pl.BlockSpec((None, seq, lanes), lambda ct, hf: (hf, 0, ct))]
                 + tab_specs + [lhs_spec, lhs_spec],
        out_specs=pl.BlockSpec((seq, lanes), lambda ct, hf: (0, ct)),
        out_shape=jax.ShapeDtypeStruct((seq, c), F32),
        scratch_shapes=[pltpu.VMEM((seq, lanes), F32), pltpu.VMEM((seq, lanes), F32),
                        pltpu.VMEM((s2 // 2, 2 * FFT_B, 2 * FFT_B), BF16)],
        compiler_params=_cparams(("arbitrary", "arbitrary")),
        name="hyena_fft_conv",
    )(u, bias, kr, ki, tb["first_c"], tb["first_s"], tb["lvl_c"], tb["lvl_s"],
      tb["col_c"], tb["col_s"], jnp.asarray(tb["lhs_f"], BF16), jnp.asarray(tb["lhs_i"], BF16))


MIXOUT_CHUNK = 256


def _mixout_kernel(h_ref, yr_ref, hy_ref, x0_ref, og_ref, w_ref, x_ref, mod_ref, o_ref, *, stream, lw, tm):
    gate = mod_ref[3 * stream + 2:3 * stream + 3, :]
    w = w_ref[...]
    for r in range(0, tm, MIXOUT_CHUNK):
        rs = slice(r, r + min(MIXOUT_CHUNK, tm))
        yr = yr_ref[rs, :]
        gelu = 0.5 * yr * (1.0 + jnp.tanh(math.sqrt(2.0 / math.pi) * (yr + 0.044715 * (yr * yr * yr))))
        lru = (_rms(h_ref[rs, :] * gelu) * og_ref[:, :lw]).astype(BF16)
        hy = (_rms(hy_ref[rs, :] * x0_ref[rs, :]) * og_ref[:, lw:]).astype(BF16)
        y = (jnp.dot(lru, w[:lw, :], preferred_element_type=F32)
             + jnp.dot(hy, w[lw:, :], preferred_element_type=F32))
        o_ref[rs, :] = x_ref[rs, :] + gate * y


def _mixout(h_lru, p, y_hy, x0c, og, w_out_bf16, x, mod, stream):
    rows, d = x.shape
    lw = h_lru.shape[1]
    dm = w_out_bf16.shape[0]
    tm = min(rows, 512)
    row_blk = lambda cols, j: pl.BlockSpec((tm, cols), lambda i: (i, j))
    return pl.pallas_call(
        functools.partial(_mixout_kernel, stream=stream, lw=lw, tm=tm),
        grid=(rows // tm,),
        in_specs=[row_blk(lw, 0), row_blk(lw, 1), row_blk(dm - lw, 0), row_blk(dm - lw, 0),
                  pl.BlockSpec((1, dm), lambda i: (0, 0)),
                  pl.BlockSpec((dm, d), lambda i: (0, 0)),
                  row_blk(d, 0),
                  pl.BlockSpec((SUBLANES, d), lambda i: (0, 0))],
        out_specs=row_blk(d, 0),
        out_shape=jax.ShapeDtypeStruct((rows, d), F32),
        compiler_params=_cparams(("arbitrary",)),
        name="mixer_out_proj",
    )(h_lru, p, y_hy, x0c, og, w_out_bf16, x, mod)


def _final_kernel(x_ref, g_ref, o_ref, *, pieces, prow, d):
    for q in range(pieces):
        o_ref[q * prow:(q + 1) * prow, :] = _rms(x_ref[:, q * d:(q + 1) * d]) * g_ref[...]


def _final_norm(x, g, from_col):
    rows, d = x.shape
    tm = 1024
    if from_col:
        prow, pieces = GRID_W, tm // GRID_W
        xv = x.reshape(prow, (rows // prow) * d)
        x_spec = pl.BlockSpec((prow, pieces * d), lambda i: (0, i))
    else:
        prow, pieces, xv = tm, 1, x
        x_spec = pl.BlockSpec((tm, d), lambda i: (i, 0))
    return pl.pallas_call(
        functools.partial(_final_kernel, pieces=pieces, prow=prow, d=d),
        grid=(rows // tm,),
        in_specs=[x_spec, pl.BlockSpec((1, d), lambda i: (0, 0))],
        out_specs=pl.BlockSpec((tm, d), lambda i: (i, 0)),
        out_shape=jax.ShapeDtypeStruct((rows, d), F32),
        compiler_params=_cparams(("arbitrary",)),
        name="final_norm",
    )(xv, g)


def kernel(x, c, ctx, c_ctx, ada_w, ada_b, norm_g, ffn_wg, ffn_wu, ffn_wd, w_in, w_out, out_g, lru_conv_w, lru_conv_b, lru_wa, lru_ba, lru_wx, lru_bx, lru_lam, hy_conv_w, hy_conv_b, hy_bias, filt_w1, filt_b1, filt_w2, filt_b2, filt_w3, filt_b3, filt_w4, filt_b4, filt_freq, final_g):
    assert x.shape[0] == 1 and ctx.shape[0] == 1
    depth = ada_w.shape[0]
    d = x.shape[-1]
    lw = lru_conv_w.shape[-1]
    hw = hy_bias.shape[-1]
    n_lat, n_ctx = x.shape[1], ctx.shape[1]
    xl, xc = x[0], ctx[0]
    mods = _ada_mods(c, c_ctx, ada_w, ada_b)
    zero_state = jnp.zeros((1, lw), F32)
    lat_is_col = False
    for l in range(depth):
        last = l == depth - 1
        m = [_sublayer_mods(mods[l], d, k) for k in range(3)]
        g = norm_g[l][:, None, :]

        xl = _ffn(xl, m[0], g[0], ffn_wg, ffn_wu, ffn_wd, l, 0, 0)
        xc = _ffn(xc, m[0], g[0], ffn_wg, ffn_wu, ffn_wd, l, 0, 1)

        want_col = l % 2 == 1
        perm = None if want_col == lat_is_col else ("to_col" if want_col else "from_col")
        lat_is_col = want_col
        pl_, xl = _win(xl, m[1], g[1], w_in, l, w_in.shape[-1], 0, perm)
        pc_, _ = _win(xc, m[1], g[1], w_in, l, lw if last else w_in.shape[-1], 1, None)

        cw, cb = lru_conv_w[l], lru_conv_b[l][None, :]
        h_lat = h_ctx = None
        for dr in range(2):
            wax = jnp.concatenate([lru_wa[l, dr], lru_wx[l, dr]], axis=-1).astype(BF16)
            ba, bx, lam = lru_ba[l, dr][None, :], lru_bx[l, dr][None, :], lru_lam[l, dr][None, :]
            h_ctx, state = _lru_scan(pc_, cw, cb, wax, ba, bx, lam, zero_state, h_ctx, dr == 1)
            h_lat, _ = _lru_scan(pl_, cw, cb, wax, ba, bx, lam, state, h_lat, dr == 1)

        filt = (filt_w1[l], filt_b1[l], filt_w2[l], filt_b2[l], filt_w3[l], filt_b3[l], filt_w4[l],
                filt_b4[l], filt_freq[l])
        hcw, hcb, hbias = hy_conv_w[l], hy_conv_b[l][None, :], hy_bias[l][None, :]
        og = out_g[l][None, :]
        wo = w_out[l].astype(BF16)

        u_l, x0_l = _hypre(pl_, 2 * lw, hcw, hcb)
        kr, ki = _hyena_spectrum(*_hyena_filter(n_lat, *filt))
        y_hy = _hyena_conv(u_l, hbias, kr, ki)
        xl = _mixout(h_lat, pl_, y_hy, x0_l, og, wo, xl, m[1], 0)
        xl = _ffn(xl, m[2], g[2], ffn_wg, ffn_wu, ffn_wd, l, 1, 0)

        if not last:
            u_c, x0_c = _hypre(pc_, 2 * lw, hcw, hcb)
            kr, ki = _hyena_spectrum(*_hyena_filter(n_ctx, *filt))
            y_hy = _hyena_conv(u_c, hbias, kr, ki)
            xc = _mixout(h_ctx, pc_, y_hy, x0_c, og, wo, xc, m[1], 1)
            xc = _ffn(xc, m[2], g[2], ffn_wg, ffn_wu, ffn_wd, l, 1, 1)

    return _final_norm(xl, final_g[None, :], lat_is_col)[None]
```

```python
import functools
import math

import numpy as np
import jax
import jax.numpy as jnp
from jax import lax
from jax.experimental import pallas as pl
from jax.experimental.pallas import tpu as pltpu

F32 = jnp.float32
BF16 = jnp.bfloat16
EPS = 1e-6
LRU_C = 8.0
GRID_W = 64
HY_BANDS = 16
HY_MAX_DECAY = math.log(1e-2) / 0.3
HY_MIN_DECAY = math.log(1e-2) / 1.5

V7X_VMEM_LIMIT_BYTES = 58 * 1024 * 1024
FFT_B = 128
SUBLANES = 8


def _cparams(sem):
    return pltpu.CompilerParams(dimension_semantics=sem, vmem_limit_bytes=V7X_VMEM_LIMIT_BYTES)


def _rms(x):
    return x * lax.rsqrt(jnp.mean(x * x, axis=-1, keepdims=True) + EPS)


def _modulated(x, mod_ref, g_ref, stream):
    shift = mod_ref[3 * stream:3 * stream + 1, :]
    scale = mod_ref[3 * stream + 1:3 * stream + 2, :]
    return _rms(x) * g_ref[...] * (1.0 + scale) + shift


def _ada_kernel(c_ref, w_ref, b_ref, o_ref):
    c = c_ref[...]
    s = c * jax.nn.sigmoid(c)
    o_ref[...] = jnp.dot(s, w_ref[...], precision=lax.Precision.HIGHEST,
                         preferred_element_type=F32) + b_ref[...]


def _ada_mods(c, c_ctx, ada_w, ada_b):
    depth, d, nm = ada_w.shape
    tn = d
    cc = jnp.zeros((SUBLANES, d), F32).at[0].set(c[0]).at[1].set(c_ctx)
    out = pl.pallas_call(
        _ada_kernel,
        grid=(depth, nm // tn),
        in_specs=[pl.BlockSpec((SUBLANES, d), lambda l, j: (0, 0)),
                  pl.BlockSpec((None, d, tn), lambda l, j: (l, 0, j)),
                  pl.BlockSpec((None, 1, tn), lambda l, j: (l, 0, j))],
        out_specs=pl.BlockSpec((None, SUBLANES, tn), lambda l, j: (l, 0, j)),
        out_shape=jax.ShapeDtypeStruct((depth, SUBLANES, nm), F32),
        compiler_params=_cparams(("arbitrary", "arbitrary")),
        name="ada_mods",
    )(cc, ada_w, ada_b[:, None, :])
    return out


def _sublayer_mods(mods_l, d, k):
    lat = mods_l[0, 3 * k * d:(3 * k + 3) * d].reshape(3, d)
    ctx = mods_l[1, 3 * k * d:(3 * k + 3) * d].reshape(3, d)
    return jnp.concatenate([lat, ctx, jnp.zeros((2, d), F32)], axis=0)


NORM_CHUNK = 256


def _ffn_up_kernel(x_ref, mod_ref, g_ref, wg_ref, wu_ref, h_ref, u_scr, *, stream, tm):
    wg = wg_ref[...].astype(BF16)
    wu = wu_ref[...].astype(BF16)

    def swiglu_rows(u):
        g = jnp.dot(u, wg, preferred_element_type=F32)
        up = jnp.dot(u, wu, preferred_element_type=F32)
        return (g * jax.nn.sigmoid(g) * up).astype(BF16)

    @pl.when(pl.program_id(1) == 0)
    def _():
        for r in range(0, tm, NORM_CHUNK):
            rs = slice(r, min(r + NORM_CHUNK, tm))
            u = _modulated(x_ref[rs, :], mod_ref, g_ref, stream).astype(BF16)
            u_scr[rs, :] = u
            h_ref[rs, :] = swiglu_rows(u)

    @pl.when(pl.program_id(1) > 0)
    def _():
        h_ref[...] = swiglu_rows(u_scr[...])


def _ffn_down_kernel(h_ref, wd_ref, x_ref, mod_ref, o_ref, *, stream):
    acc = jnp.dot(h_ref[...], wd_ref[...].astype(BF16), preferred_element_type=F32)
    gate = mod_ref[3 * stream + 2:3 * stream + 3, :]
    o_ref[...] = x_ref[...] + (0.5 * gate) * acc


def _ffn(x, mod, g, wg, wu, wd, l, j, stream):
    rows, d = x.shape
    dff = wg.shape[-1]
    tm = min(rows, 1024)
    tf = 512
    tn = 256
    h = pl.pallas_call(
        functools.partial(_ffn_up_kernel, stream=stream, tm=tm),
        grid=(rows // tm, dff // tf),
        in_specs=[pl.BlockSpec((tm, d), lambda i, f: (i, 0)),
                  pl.BlockSpec((SUBLANES, d), lambda i, f: (0, 0)),
                  pl.BlockSpec((1, d), lambda i, f: (0, 0)),
                  pl.BlockSpec((None, None, d, tf), lambda i, f: (l, j, 0, f)),
                  pl.BlockSpec((None, None, d, tf), lambda i, f: (l, j, 0, f))],
        out_specs=pl.BlockSpec((tm, tf), lambda i, f: (i, f)),
        out_shape=jax.ShapeDtypeStruct((rows, dff), BF16),
        scratch_shapes=[pltpu.VMEM((tm, d), BF16)],
        compiler_params=_cparams(("arbitrary", "arbitrary")),
        name="ffn_up",
    )(x, mod, g, wg, wu)
    return pl.pallas_call(
        functools.partial(_ffn_down_kernel, stream=stream),
        grid=(rows // tm, d // tn),
        in_specs=[pl.BlockSpec((tm, dff), lambda i, n: (i, 0)),
                  pl.BlockSpec((None, None, dff, tn), lambda i, n: (l, j, 0, n)),
                  pl.BlockSpec((tm, tn), lambda i, n: (i, n)),
                  pl.BlockSpec((SUBLANES, tn), lambda i, n: (0, n))],
        out_specs=pl.BlockSpec((tm, tn), lambda i, n: (i, n)),
        out_shape=jax.ShapeDtypeStruct((rows, d), F32),
        compiler_params=_cparams(("arbitrary", "arbitrary")),
        name="ffn_down",
    )(h, wd, x, mod)


def _win_kernel(x_ref, mod_ref, g_ref, w_ref, p_ref, *rest, pieces, prow, d, stream, emit_x):
    if emit_x:
        xo_ref, u_scr = rest
    else:
        (u_scr,) = rest

    w = w_ref[...]

    @pl.when(pl.program_id(1) == 0)
    def _():
        tm = pieces * prow
        seg = min(prow, NORM_CHUNK)
        for c0 in range(0, tm, NORM_CHUNK):
            c1 = min(c0 + NORM_CHUNK, tm)
            for r0 in range(c0, c1, seg):
                q, r = divmod(r0, prow)
                xt = x_ref[r:r + seg, q * d:(q + 1) * d]
                u_scr[r0:r0 + seg, :] = _modulated(xt, mod_ref, g_ref, stream).astype(BF16)
                if emit_x:
                    xo_ref[r0:r0 + seg, :] = xt
            p_ref[c0:c1, :] = jnp.dot(u_scr[c0:c1, :], w, preferred_element_type=F32)

    @pl.when(pl.program_id(1) > 0)
    def _():
        p_ref[...] = jnp.dot(u_scr[...], w, preferred_element_type=F32)


def _win(x, mod, g, w_in_bf16, ncols, stream, perm):
    rows, d = x.shape
    tm = min(rows, 1024)
    tn = 1024 if perm is None else 512
    if perm is None:
        prow, pieces, xv = tm, 1, x
        x_spec = pl.BlockSpec((tm, d), lambda i, n: (i, 0))
    else:
        grid_h = rows // GRID_W
        prow = grid_h if perm == "to_col" else GRID_W
        pieces = tm // prow
        xv = x.reshape(prow, (rows // prow) * d)
        x_spec = pl.BlockSpec((prow, pieces * d), lambda i, n: (0, i))
    emit_x = perm is not None
    out_shape = [jax.ShapeDtypeStruct((rows, ncols), F32)]
    out_specs = [pl.BlockSpec((tm, tn), lambda i, n: (i, n))]
    if emit_x:
        out_shape.append(jax.ShapeDtypeStruct((rows, d), F32))
        out_specs.append(pl.BlockSpec((tm, d), lambda i, n: (i, 0)))
    res = pl.pallas_call(
        functools.partial(_win_kernel, pieces=pieces, prow=prow, d=d, stream=stream, emit_x=emit_x),
        grid=(rows // tm, ncols // tn),
        in_specs=[x_spec,
                  pl.BlockSpec((SUBLANES, d), lambda i, n: (0, 0)),
                  pl.BlockSpec((1, d), lambda i, n: (0, 0)),
                  pl.BlockSpec((d, tn), lambda i, n: (0, n))],
        out_specs=out_specs,
        out_shape=out_shape,
        scratch_shapes=[pltpu.VMEM((tm, d), BF16)],
        compiler_params=_cparams(("arbitrary", "arbitrary")),
        name="mixer_in_proj",
    )(xv, mod, g, w_in_bf16)
    return (res[0], res[1]) if emit_x else (res[0], x)


def _lru_kernel(x_ref, xp_ref, xn_ref, cw_ref, cb_ref, wax_ref, ba_ref, bx_ref, lam_ref, h0_ref, *rest,
                t_blk, nblk, heads, reverse, add_prev):
    if add_prev:
        prev_ref, o_ref, hf_ref, xs, a_scr, b_scr, carry = rest
    else:
        o_ref, hf_ref, xs, a_scr, b_scr, carry = rest
    j = pl.program_id(0)
    blk = (nblk - 1 - j) if reverse else j
    c = x_ref.shape[1]
    hd = c // heads

    @pl.when(j == 0)
    def _():
        carry[...] = h0_ref[...]

    zero8 = jnp.zeros((SUBLANES, c), F32)
    xs[0:SUBLANES, :] = jnp.where(blk == 0, zero8, xp_ref[...])
    xs[SUBLANES:SUBLANES + t_blk, :] = x_ref[...]
    xs[SUBLANES + t_blk:2 * SUBLANES + t_blk, :] = jnp.where(blk == nblk - 1, zero8, xn_ref[...])
    cv = cb_ref[...]
    for k in range(cw_ref.shape[0]):
        cv = cv + xs[SUBLANES - 1 + k:SUBLANES - 1 + k + t_blk, :] * cw_ref[k:k + 1, :]

    z = -lam_ref[...]
    c_sp = LRU_C * (jnp.maximum(z, 0.0) + jnp.log(1.0 + jnp.exp(-jnp.abs(z))))
    cvb = cv.astype(BF16)
    for h in range(heads):
        sl = slice(h * hd, (h + 1) * hd)
        gx = jnp.dot(cvb[:, sl], wax_ref[h], preferred_element_type=F32)
        r = jax.nn.sigmoid(gx[:, :hd] + ba_ref[:, sl])
        ig = jax.nn.sigmoid(gx[:, hd:] + bx_ref[:, sl])
        log_a = -c_sp[:, sl] * r
        a_scr[:, sl] = jnp.exp(log_a)
        b_scr[:, sl] = jnp.sqrt(1.0 - jnp.exp(2.0 * log_a)) * (ig * cv[:, sl])

    row = lax.broadcasted_iota(jnp.int32, (SUBLANES, c), 0)
    ngrp = t_blk // SUBLANES

    def group(gi, hc):
        g = (ngrp - 1 - gi) if reverse else gi
        r0 = pl.multiple_of(g * SUBLANES, SUBLANES)
        av = a_scr[pl.ds(r0, SUBLANES), :]
        bv = b_scr[pl.ds(r0, SUBLANES), :]
        for s in (1, 2, 4):
            sh = (SUBLANES - s) if reverse else s
            m = (row < SUBLANES - s) if reverse else (row >= s)
            a_s = pltpu.roll(av, sh, axis=0)
            b_s = pltpu.roll(bv, sh, axis=0)
            bv = jnp.where(m, av * b_s + bv, bv)
            av = jnp.where(m, av * a_s, av)
        hv = av * hc + bv
        if add_prev:
            o_ref[pl.ds(r0, SUBLANES), :] = hv + prev_ref[pl.ds(r0, SUBLANES), :]
        else:
            o_ref[pl.ds(r0, SUBLANES), :] = hv
        return hv[0:1, :] if reverse else hv[SUBLANES - 1:SUBLANES, :]

    hc = lax.fori_loop(0, ngrp, group, carry[...])
    carry[...] = hc
    hf_ref[...] = hc


def _lru_scan(p, cw, cb, wax, ba, bx, lam, h0, prev, reverse):
    seq = p.shape[0]
    c = cw.shape[1]
    heads = wax.shape[0]
    t_blk = 256
    nblk = seq // t_blk
    per = t_blk // SUBLANES
    nb8 = seq // SUBLANES

    def bi(j):
        return (nblk - 1 - j) if reverse else j

    in_specs = [pl.BlockSpec((t_blk, c), lambda j: (bi(j), 0)),
                pl.BlockSpec((SUBLANES, c), lambda j: (jnp.maximum(bi(j) * per - 1, 0), 0)),
                pl.BlockSpec((SUBLANES, c), lambda j: (jnp.minimum((bi(j) + 1) * per, nb8 - 1), 0)),
                pl.BlockSpec(cw.shape, lambda j: (0, 0)),
                pl.BlockSpec((1, c), lambda j: (0, 0)),
                pl.BlockSpec(wax.shape, lambda j: (0, 0, 0)),
                pl.BlockSpec((1, c), lambda j: (0, 0)),
                pl.BlockSpec((1, c), lambda j: (0, 0)),
                pl.BlockSpec((1, c), lambda j: (0, 0)),
                pl.BlockSpec((1, c), lambda j: (0, 0))]
    args = [p, p, p, cw, cb, wax, ba, bx, lam, h0]
    if prev is not None:
        in_specs.append(pl.BlockSpec((t_blk, c), lambda j: (bi(j), 0)))
        args.append(prev)
    return pl.pallas_call(
        functools.partial(_lru_kernel, t_blk=t_blk, nblk=nblk, heads=heads, reverse=reverse,
                          add_prev=prev is not None),
        grid=(nblk,),
        in_specs=in_specs,
        out_specs=[pl.BlockSpec((t_blk, c), lambda j: (bi(j), 0)),
                   pl.BlockSpec((1, c), lambda j: (0, 0))],
        out_shape=[jax.ShapeDtypeStruct((seq, c), F32), jax.ShapeDtypeStruct((1, c), F32)],
        scratch_shapes=[pltpu.VMEM((t_blk + 2 * SUBLANES, c), F32),
                        pltpu.VMEM((t_blk, c), F32),
                        pltpu.VMEM((t_blk, c), F32),
                        pltpu.VMEM((1, c), F32)],
        compiler_params=_cparams(("arbitrary",)),
        name="rglru_scan",
    )(*args)


def _hypre_kernel(x0_ref, x0p, x0n, x1_ref, x1p, x1n, v_ref, vp, vn, cw_ref, cb_ref, u_ref, x0c_ref, xs,
                  *, t_blk, nblk, c):
    blk = pl.program_id(0)
    zero8 = jnp.zeros((SUBLANES, c), F32)

    def conv3(cur, prv, nxt, part):
        xs[0:SUBLANES, :] = jnp.where(blk == 0, zero8, prv[...])
        xs[SUBLANES:SUBLANES + t_blk, :] = cur[...]
        xs[SUBLANES + t_blk:2 * SUBLANES + t_blk, :] = jnp.where(blk == nblk - 1, zero8, nxt[...])
        sl = slice(part * c, (part + 1) * c)
        out = cb_ref[:, sl]
        for k in range(cw_ref.shape[0]):
            out = out + xs[SUBLANES - 1 + k:SUBLANES - 1 + k + t_blk, :] * cw_ref[k:k + 1, sl]
        return out

    x0c_ref[...] = conv3(x0_ref, x0p, x0n, 0)
    x1c = conv3(x1_ref, x1p, x1n, 1)
    u_ref[...] = conv3(v_ref, vp, vn, 2) * x1c


def _hypre(p, col0, cw, cb):
    seq = p.shape[0]
    c = cw.shape[1] // 3
    t_blk = min(seq, 512)
    nblk = seq // t_blk
    per = t_blk // SUBLANES
    nb8 = seq // SUBLANES
    cb0 = col0 // c
    in_specs, args = [], []
    for part in range(3):
        in_specs += [pl.BlockSpec((t_blk, c), lambda j, part=part: (j, cb0 + part)),
                     pl.BlockSpec((SUBLANES, c), lambda j, part=part: (jnp.maximum(j * per - 1, 0), cb0 + part)),
                     pl.BlockSpec((SUBLANES, c),
                                  lambda j, part=part: (jnp.minimum((j + 1) * per, nb8 - 1), cb0 + part))]
        args += [p, p, p]
    in_specs += [pl.BlockSpec(cw.shape, lambda j: (0, 0)), pl.BlockSpec((1, 3 * c), lambda j: (0, 0))]
    args += [cw, cb]
    return pl.pallas_call(
        functools.partial(_hypre_kernel, t_blk=t_blk, nblk=nblk, c=c),
        grid=(nblk,),
        in_specs=in_specs,
        out_specs=[pl.BlockSpec((t_blk, c), lambda j: (j, 0)), pl.BlockSpec((t_blk, c), lambda j: (j, 0))],
        out_shape=[jax.ShapeDtypeStruct((seq, c), F32), jax.ShapeDtypeStruct((seq, c), F32)],
        scratch_shapes=[pltpu.VMEM((t_blk + 2 * SUBLANES, c), F32)],
        compiler_params=_cparams(("arbitrary",)),
        name="hyena_short_conv",
    )(*args)


FILT_SLOT = 64


def _filt_kernel(w1_ref, b1_ref, w2_ref, b2_ref, w3_ref, b3_ref, w4a_ref, w4b_ref, b4_ref, fr_ref,
                 k_ref, nrm_ref, *, seq, t_blk, c):
    i = pl.program_id(0)
    hi = lax.Precision.HIGHEST
    lanes = 2 * FILT_SLOT
    th = t_blk // 2
    lane = lax.broadcasted_iota(jnp.int32, (1, lanes), 1)
    slot = lane & (FILT_SLOT - 1)
    pos_a = i * t_blk + lax.broadcasted_iota(jnp.int32, (th, 1), 0)
    pos = jnp.where(lane < FILT_SLOT, pos_a, pos_a + th)
    tf = jnp.where(pos < seq, pos, 2 * seq - pos).astype(F32)
    t = tf * (1.0 / (seq - 1))
    w = (2.0 * math.pi / seq) * tf
    is_cos = (slot >= 1) & (slot <= HY_BANDS)
    is_sin = (slot > HY_BANDS) & (slot <= 2 * HY_BANDS)
    band = jnp.where(is_cos, slot - 1, slot - 1 - HY_BANDS).astype(F32)
    f = 1e-4 + band * ((HY_BANDS - 1 - 1e-4) / (HY_BANDS - 1))
    phase = jnp.where(is_cos, 0.5 * math.pi, 0.0)
    sign = jnp.where(is_cos, 1.0, jnp.where(is_sin, -1.0, 0.0))
    z = jnp.where(slot == 0, t, sign * jnp.sin(w * f + phase))
    fr = fr_ref[...]
    h = jnp.sin(fr * (jnp.dot(z, w1_ref[...], precision=hi, preferred_element_type=F32) + b1_ref[...]))
    h = jnp.sin(fr * (jnp.dot(h, w2_ref[...], precision=hi, preferred_element_type=F32) + b2_ref[...]))
    h = jnp.sin(fr * (jnp.dot(h, w3_ref[...], precision=hi, preferred_element_type=F32) + b3_ref[...]))
    ch = lax.broadcasted_iota(jnp.int32, (1, c), 1).astype(F32)
    delta = jnp.abs(HY_MIN_DECAY + ch * ((HY_MAX_DECAY - HY_MIN_DECAY) / (c - 1)))
    part = jnp.zeros((1, c), F32)
    for half, w4_ref in enumerate((w4a_ref, w4b_ref)):
        pos_h = pos_a + half * th
        t_h = jnp.where(pos_h < seq, pos_h, 2 * seq - pos_h).astype(F32) * (1.0 / (seq - 1))
        k = jnp.dot(h, w4_ref[...], precision=hi, preferred_element_type=F32) + b4_ref[...]
        k = jnp.where(pos_h != seq, k * jnp.exp(-t_h * delta), 0.0)
        k_ref[half * th:(half + 1) * th, :] = k
        part = part + jnp.sum(jnp.abs(k), axis=0, keepdims=True)

    @pl.when(i == 0)
    def _():
        nrm_ref[...] = part

    @pl.when(i > 0)
    def _():
        nrm_ref[...] = nrm_ref[...] + part


def _slots(a, row_used):
    r, cdim = a.shape
    blk = jnp.pad(a, ((0, (FILT_SLOT if row_used else 1) - r), (0, FILT_SLOT - cdim)))
    if not row_used:
        return jnp.concatenate([blk, blk], axis=1)
    z = jnp.zeros_like(blk)
    return jnp.concatenate([jnp.concatenate([blk, z], axis=1), jnp.concatenate([z, blk], axis=1)], axis=0)


def _hyena_filter(seq, w1, b1, w2, b2, w3, b3, w4, b4, freq):
    c = w4.shape[1] // 2
    od = w2.shape[0]
    assert od <= FILT_SLOT and w1.shape[0] <= FILT_SLOT
    lanes = 2 * FILT_SLOT
    t_blk = min(seq, 1024)
    half_blocks = seq // t_blk
    full = lambda shape: pl.BlockSpec(shape, lambda i: (0,) * len(shape))
    sq, row = (lanes, lanes), (1, lanes)
    w4p = jnp.pad(w4, ((0, FILT_SLOT - od), (0, 0)))
    w4a = jnp.concatenate([w4p, jnp.zeros_like(w4p)], axis=0)
    w4b = jnp.concatenate([jnp.zeros_like(w4p), w4p], axis=0)
    w4_spec = pl.BlockSpec((lanes, c), lambda i: (0, i // half_blocks))
    return pl.pallas_call(
        functools.partial(_filt_kernel, seq=seq, t_blk=t_blk, c=c),
        grid=(2 * seq // t_blk,),
        in_specs=[full(sq), full(row), full(sq), full(row), full(sq), full(row), w4_spec, w4_spec,
                  pl.BlockSpec((1, c), lambda i: (0, i // half_blocks)),
                  full(row)],
        out_specs=[pl.BlockSpec((t_blk, c), lambda i: (i, 0)), pl.BlockSpec((1, c), lambda i: (0, 0))],
        out_shape=[jax.ShapeDtypeStruct((2 * seq, c), F32), jax.ShapeDtypeStruct((1, c), F32)],
        compiler_params=_cparams(("arbitrary",)),
        name="hyena_filter",
    )(_slots(w1, True), _slots(b1[None, :], False), _slots(w2, True), _slots(b2[None, :], False),
      _slots(w3, True), _slots(b3[None, :], False), w4a, w4b, b4[None, :], _slots(freq[None, :], False))


def _bitrev(p, bits):
    r = 0
    for k in range(bits):
        r = (r << 1) | ((p >> k) & 1)
    return r


def _real_block(z):
    return np.block([[z.real, -z.imag], [z.imag, z.real]])


def _fft_split(s2):
    n_lo = max(min(8, s2 // 2), 1)
    return s2 // n_lo, n_lo


@functools.lru_cache(maxsize=None)
def _fft_tables(s):
    s2 = s // 2
    n = s * FFT_B
    n_hi, n_lo = _fft_split(s2)
    bits_hi, bits_lo = n_hi.bit_length() - 1, n_lo.bit_length() - 1
    b = np.arange(FFT_B)
    dft = np.exp(-2j * np.pi * np.outer(b, b) / FFT_B)
    e_lo = [2 * n_hi * _bitrev(p, bits_lo) for p in range(n_lo)]
    g = [dft * np.exp(-2j * np.pi * b * e / n)[None, :] for e in e_lo]
    lhs_f = np.stack([_real_block(x) for x in g])
    lhs_i = np.stack([_real_block(np.conj(x).T) for x in g])
    e_hi = np.array([[2 * _bitrev(p, bits_hi) + hf for p in range(n_hi)] for hf in range(2)])
    ang = 2.0 * np.pi * e_hi[:, :, None, None] * b[None, None, :, None] / n * np.ones((1, 1, 1, FFT_B))
    lvl = 2.0 * np.pi * np.arange(max(s2 // 2, 1)) / s2
    first = 2.0 * np.pi * np.arange(s2) / s
    f32 = lambda x: np.asarray(x, np.float32)
    return dict(lhs_f=f32(lhs_f), lhs_i=f32(lhs_i), col_c=f32(np.cos(ang)), col_s=f32(np.sin(ang)),
                lvl_c=f32(np.cos(lvl)), lvl_s=f32(np.sin(lvl)), first_c=f32(np.cos(first)),
                first_s=f32(np.sin(first)))


def _butterfly_levels(wr, wi, lc_ref, ls_ref, s2, inverse):
    blk = FFT_B
    spans = []
    h = s2 // 2
    while h >= 1:
        spans.append(h)
        h //= 2
    if inverse:
        spans = spans[::-1]
    for h in spans:
        tstep = s2 // (2 * h)
        shift = h.bit_length() - 1

        def pair(jj, carry, h=h, tstep=tstep, shift=shift):
            g = jj >> shift
            i = jj & (h - 1)
            top = pl.multiple_of((g * 2 * h + i) * blk, blk)
            bot = pl.multiple_of((g * 2 * h + i + h) * blk, blk)
            cs = lc_ref[i * tstep]
            sn = ls_ref[i * tstep]
            tr, ti = wr[pl.ds(top, blk), :], wi[pl.ds(top, blk), :]
            br, bi = wr[pl.ds(bot, blk), :], wi[pl.ds(bot, blk), :]
            if inverse:
                xr = br * cs - bi * sn
                xi = bi * cs + br * sn
                wr[pl.ds(top, blk), :] = tr + xr
                wi[pl.ds(top, blk), :] = ti + xi
                wr[pl.ds(bot, blk), :] = tr - xr
                wi[pl.ds(bot, blk), :] = ti - xi
            else:
                dr, di = tr - br, ti - bi
                wr[pl.ds(top, blk), :] = tr + br
                wi[pl.ds(top, blk), :] = ti + bi
                wr[pl.ds(bot, blk), :] = dr * cs + di * sn
                wi[pl.ds(bot, blk), :] = di * cs - dr * sn
            return carry

        lax.fori_loop(0, s2 // 2, pair, 0)


MID_UNROLL = 4


def _pair_blocks(pp, s2):
    n_hi, n_lo = _fft_split(s2)
    half = n_hi // 2
    p_lo = pp >> (half.bit_length() - 1)
    j = pp & (half - 1)
    return p_lo, [(2 * j + q, pl.multiple_of(((2 * j + q) * n_lo + p_lo) * FFT_B, FFT_B)) for q in range(2)]


def _for_pairs(s2, body):
    def step(pp, carry):
        body(pp)
        return carry
    lax.fori_loop(0, s2 // 2, step, 0, unroll=min(MID_UNROLL, s2 // 2))


def _put_pair(panel, pp, q, re, im):
    blk = FFT_B
    panel[pp, 0:blk, q * blk:(q + 1) * blk] = re.astype(BF16)
    panel[pp, blk:2 * blk, q * blk:(q + 1) * blk] = im.astype(BF16)


def _twiddle_pack(wr, wi, cc_ref, cs_ref, panel, s2):
    blk = FFT_B

    def body(pp):
        _, blocks = _pair_blocks(pp, s2)
        for q, (p_hi, r0) in enumerate(blocks):
            xr = wr[pl.ds(r0, blk), :]
            xi = wi[pl.ds(r0, blk), :]
            cq, sq = cc_ref[p_hi], cs_ref[p_hi]
            _put_pair(panel, pp, q, xr * cq + xi * sq, xi * cq - xr * sq)

    _for_pairs(s2, body)


def _dft_pair(m_ref, panel, pp, p_lo):
    blk = FFT_B
    out = jnp.dot(m_ref[p_lo], panel[pp], preferred_element_type=F32)
    return [(out[0:blk, q * blk:(q + 1) * blk], out[blk:2 * blk, q * blk:(q + 1) * blk]) for q in range(2)]


def _spec_kernel(lo_ref, hi_ref, nrm_ref, fc_ref, fs_ref, lc_ref, ls_ref, cc_ref, cs_ref, mf_ref,
                 kr_ref, ki_ref, panel, *, s2):
    blk = FFT_B
    hf = pl.program_id(1)
    n = 2 * s2 * blk

    @pl.when(hf == 0)
    def _():
        kr_ref[...] = lo_ref[...] + hi_ref[...]
        ki_ref[...] = jnp.zeros_like(ki_ref)

    @pl.when(hf == 1)
    def _():
        def first(a, carry):
            r0 = pl.multiple_of(a * blk, blk)
            dlt = lo_ref[pl.ds(r0, blk), :] - hi_ref[pl.ds(r0, blk), :]
            kr_ref[pl.ds(r0, blk), :] = dlt * fc_ref[a]
            ki_ref[pl.ds(r0, blk), :] = -dlt * fs_ref[a]
            return carry
        lax.fori_loop(0, s2, first, 0)

    _butterfly_levels(kr_ref, ki_ref, lc_ref, ls_ref, s2, inverse=False)
    scale = 1.0 / (nrm_ref[...] * n)

    _twiddle_pack(kr_ref, ki_ref, cc_ref, cs_ref, panel, s2)

    def dft(pp):
        p_lo, blocks = _pair_blocks(pp, s2)
        res = _dft_pair(mf_ref, panel, pp, p_lo)
        for q, (_, r0) in enumerate(blocks):
            kr_ref[pl.ds(r0, blk), :] = res[q][0] * scale
            ki_ref[pl.ds(r0, blk), :] = res[q][1] * scale

    _for_pairs(s2, dft)


def _conv_kernel(u_ref, bias_ref, kr_ref, ki_ref, fc_ref, fs_ref, lc_ref, ls_ref, cc_ref, cs_ref,
                 mf_ref, mi_ref, y_ref, wr, wi, panel, *, s2):
    blk = FFT_B
    hf = pl.program_id(1)

    @pl.when(hf == 0)
    def _():
        wr[...] = u_ref[...]
        wi[...] = jnp.zeros_like(wi)

    @pl.when(hf == 1)
    def _():
        def first(a, carry):
            r0 = pl.multiple_of(a * blk, blk)
            ub = u_ref[pl.ds(r0, blk), :]
            wr[pl.ds(r0, blk), :] = ub * fc_ref[a]
            wi[pl.ds(r0, blk), :] = -ub * fs_ref[a]
            return carry
        lax.fori_loop(0, s2, first, 0)

    _butterfly_levels(wr, wi, lc_ref, ls_ref, s2, inverse=False)

    _twiddle_pack(wr, wi, cc_ref, cs_ref, panel, s2)

    def spectrum_product(pp):
        p_lo, blocks = _pair_blocks(pp, s2)
        res = _dft_pair(mf_ref, panel, pp, p_lo)
        for q, (_, r0) in enumerate(blocks):
            kr = kr_ref[pl.ds(r0, blk), :]
            ki = ki_ref[pl.ds(r0, blk), :]
            yr, yi = res[q]
            _put_pair(panel, pp, q, yr * kr - yi * ki, yr * ki + yi * kr)

    _for_pairs(s2, spectrum_product)

    def inverse_dft(pp):
        p_lo, blocks = _pair_blocks(pp, s2)
        res = _dft_pair(mi_ref, panel, pp, p_lo)
        for q, (p_hi, r0) in enumerate(blocks):
            er, ei = res[q]
            cq, sq = cc_ref[p_hi], cs_ref[p_hi]
            wr[pl.ds(r0, blk), :] = er * cq - ei * sq
            wi[pl.ds(r0, blk), :] = ei * cq + er * sq

    _for_pairs(s2, inverse_dft)
    _butterfly_levels(wr, wi, lc_ref, ls_ref, s2, inverse=True)

    @pl.when(hf == 0)
    def _():
        y_ref[...] = wr[...]

    @pl.when(hf == 1)
    def _():
        def last(a, carry):
            r0 = pl.multiple_of(a * blk, blk)
            ub = u_ref[pl.ds(r0, blk), :]
            conv = (y_ref[pl.ds(r0, blk), :] + wr[pl.ds(r0, blk), :] * fc_ref[a]
                    - wi[pl.ds(r0, blk), :] * fs_ref[a])
            y_ref[pl.ds(r0, blk), :] = conv + ub * bias_ref[...]
            return carry
        lax.fori_loop(0, s2, last, 0)


def _smem_spec():
    return pl.BlockSpec(memory_space=pltpu.SMEM)


def _fft_table_specs(s2):
    n_hi, n_lo = _fft_split(s2)
    col = pl.BlockSpec((None, n_hi, FFT_B, FFT_B), lambda ct, hf: (hf, 0, 0, 0))
    lhs = pl.BlockSpec((n_lo, 2 * FFT_B, 2 * FFT_B), lambda ct, hf: (0, 0, 0))
    return [_smem_spec(), _smem_spec(), _smem_spec(), _smem_spec(), col, col], lhs


def _hyena_spectrum(kfull, nrm):
    n, c = kfull.shape
    seq = n // 2
    s2 = seq // FFT_B
    tb = _fft_tables(2 * s2)
    lanes = 128
    tab_specs, lhs_spec = _fft_table_specs(s2)
    return pl.pallas_call(
        functools.partial(_spec_kernel, s2=s2),
        grid=(c // lanes, 2),
        in_specs=[pl.BlockSpec((seq, lanes), lambda ct, hf: (0, ct)),
                  pl.BlockSpec((seq, lanes), lambda ct, hf: (1, ct)),
                  pl.BlockSpec((1, lanes), lambda ct, hf: (0, ct))] + tab_specs + [lhs_spec],
        out_specs=[pl.BlockSpec((None, seq, lanes), lambda ct, hf: (hf, 0, ct)),
                   pl.BlockSpec((None, seq, lanes), lambda ct, hf: (hf, 0, ct))],
        out_shape=[jax.ShapeDtypeStruct((2, seq, c), F32), jax.ShapeDtypeStruct((2, seq, c), F32)],
        scratch_shapes=[pltpu.VMEM((s2 // 2, 2 * FFT_B, 2 * FFT_B), BF16)],
        compiler_params=_cparams(("arbitrary", "arbitrary")),
        name="hyena_spectrum",
    )(kfull, kfull, nrm, tb["first_c"], tb["first_s"], tb["lvl_c"], tb["lvl_s"],
      tb["col_c"], tb["col_s"], jnp.asarray(tb["lhs_f"], BF16))


def _hyena_conv(u, bias, kr, ki):
    seq, c = u.shape
    s2 = seq // FFT_B
    tb = _fft_tables(2 * s2)
    lanes = 128
    tab_specs, lhs_spec = _fft_table_specs(s2)
    return pl.pallas_call(
        functools.partial(_conv_kernel, s2=s2),
        grid=(c // lanes, 2),
        in_specs=[pl.BlockSpec((seq, lanes), lambda ct, hf: (0, ct)),
                  pl.BlockSpec((1, lanes), lambda ct, hf: (0, ct)),
                  pl.BlockSpec((None, seq, lanes), lambda ct, hf: (hf, 0, ct)),
                  pl.BlockSpec((None, seq, lanes), lambda ct, hf: (hf, 0, ct))]
                 + tab_specs + [lhs_spec, lhs_spec],
        out_specs=pl.BlockSpec((seq, lanes), lambda ct, hf: (0, ct)),
        out_shape=jax.ShapeDtypeStruct((seq, c), F32),
        scratch_shapes=[pltpu.VMEM((seq, lanes), F32), pltpu.VMEM((seq, lanes), F32),
                        pltpu.VMEM((s2 // 2, 2 * FFT_B, 2 * FFT_B), BF16)],
        compiler_params=_cparams(("arbitrary", "arbitrary")),
        name="hyena_fft_conv",
    )(u, bias, kr, ki, tb["first_c"], tb["first_s"], tb["lvl_c"], tb["lvl_s"],
      tb["col_c"], tb["col_s"], jnp.asarray(tb["lhs_f"], BF16), jnp.asarray(tb["lhs_i"], BF16))


MIXOUT_CHUNK = 256


def _mixout_kernel(h_ref, yr_ref, hy_ref, x0_ref, og_ref, w_ref, x_ref, mod_ref, o_ref, *, stream, lw, tm):
    gate = mod_ref[3 * stream + 2:3 * stream + 3, :]
    w = w_ref[...]
    for r in range(0, tm, MIXOUT_CHUNK):
        rs = slice(r, r + min(MIXOUT_CHUNK, tm))
        yr = yr_ref[rs, :]
        gelu = 0.5 * yr * (1.0 + jnp.tanh(math.sqrt(2.0 / math.pi) * (yr + 0.044715 * (yr * yr * yr))))
        lru = (_rms(h_ref[rs, :] * gelu) * og_ref[:, :lw]).astype(BF16)
        hy = (_rms(hy_ref[rs, :] * x0_ref[rs, :]) * og_ref[:, lw:]).astype(BF16)
        y = (jnp.dot(lru, w[:lw, :], preferred_element_type=F32)
             + jnp.dot(hy, w[lw:, :], preferred_element_type=F32))
        o_ref[rs, :] = x_ref[rs, :] + gate * y


def _mixout(h_lru, p, y_hy, x0c, og, w_out_bf16, x, mod, stream):
    rows, d = x.shape
    lw = h_lru.shape[1]
    dm = w_out_bf16.shape[0]
    tm = min(rows, 512)
    row_blk = lambda cols, j: pl.BlockSpec((tm, cols), lambda i: (i, j))
    return pl.pallas_call(
        functools.partial(_mixout_kernel, stream=stream, lw=lw, tm=tm),
        grid=(rows // tm,),
        in_specs=[row_blk(lw, 0), row_blk(lw, 1), row_blk(dm - lw, 0), row_blk(dm - lw, 0),
                  pl.BlockSpec((1, dm), lambda i: (0, 0)),
                  pl.BlockSpec((dm, d), lambda i: (0, 0)),
                  row_blk(d, 0),
                  pl.BlockSpec((SUBLANES, d), lambda i: (0, 0))],
        out_specs=row_blk(d, 0),
        out_shape=jax.ShapeDtypeStruct((rows, d), F32),
        compiler_params=_cparams(("arbitrary",)),
        name="mixer_out_proj",
    )(h_lru, p, y_hy, x0c, og, w_out_bf16, x, mod)


def _final_kernel(x_ref, g_ref, o_ref, *, pieces, prow, d):
    for q in range(pieces):
        o_ref[q * prow:(q + 1) * prow, :] = _rms(x_ref[:, q * d:(q + 1) * d]) * g_ref[...]


def _final_norm(x, g, from_col):
    rows, d = x.shape
    tm = 1024
    if from_col:
        prow, pieces = GRID_W, tm // GRID_W
        xv = x.reshape(prow, (rows // prow) * d)
        x_spec = pl.BlockSpec((prow, pieces * d), lambda i: (0, i))
    else:
        prow, pieces, xv = tm, 1, x
        x_spec = pl.BlockSpec((tm, d), lambda i: (i, 0))
    return pl.pallas_call(
        functools.partial(_final_kernel, pieces=pieces, prow=prow, d=d),
        grid=(rows // tm,),
        in_specs=[x_spec, pl.BlockSpec((1, d), lambda i: (0, 0))],
        out_specs=pl.BlockSpec((tm, d), lambda i: (i, 0)),
        out_shape=jax.ShapeDtypeStruct((rows, d), F32),
        compiler_params=_cparams(("arbitrary",)),
        name="final_norm",
    )(xv, g)


def kernel(x, c, ctx, c_ctx, ada_w, ada_b, norm_g, ffn_wg, ffn_wu, ffn_wd, w_in, w_out, out_g, lru_conv_w, lru_conv_b, lru_wa, lru_ba, lru_wx, lru_bx, lru_lam, hy_conv_w, hy_conv_b, hy_bias, filt_w1, filt_b1, filt_w2, filt_b2, filt_w3, filt_b3, filt_w4, filt_b4, filt_freq, final_g):
    assert x.shape[0] == 1 and ctx.shape[0] == 1
    depth = ada_w.shape[0]
    d = x.shape[-1]
    lw = lru_conv_w.shape[-1]
    hw = hy_bias.shape[-1]
    n_lat, n_ctx = x.shape[1], ctx.shape[1]
    xl, xc = x[0], ctx[0]
    mods = _ada_mods(c, c_ctx, ada_w, ada_b)
    zero_state = jnp.zeros((1, lw), F32)
    lat_is_col = False
    for l in range(depth):
        last = l == depth - 1
        m = [_sublayer_mods(mods[l], d, k) for k in range(3)]
        g = norm_g[l][:, None, :]

        xl = _ffn(xl, m[0], g[0], ffn_wg, ffn_wu, ffn_wd, l, 0, 0)
        xc = _ffn(xc, m[0], g[0], ffn_wg, ffn_wu, ffn_wd, l, 0, 1)

        want_col = l % 2 == 1
        perm = None if want_col == lat_is_col else ("to_col" if want_col else "from_col")
        lat_is_col = want_col
        wi = w_in[l].astype(BF16)
        pl_, xl = _win(xl, m[1], g[1], wi, w_in.shape[-1], 0, perm)
        pc_, _ = _win(xc, m[1], g[1], wi, lw if last else w_in.shape[-1], 1, None)

        cw, cb = lru_conv_w[l], lru_conv_b[l][None, :]
        h_lat = h_ctx = None
        for dr in range(2):
            wax = jnp.concatenate([lru_wa[l, dr], lru_wx[l, dr]], axis=-1).astype(BF16)
            ba, bx, lam = lru_ba[l, dr][None, :], lru_bx[l, dr][None, :], lru_lam[l, dr][None, :]
            h_ctx, state = _lru_scan(pc_, cw, cb, wax, ba, bx, lam, zero_state, h_ctx, dr == 1)
            h_lat, _ = _lru_scan(pl_, cw, cb, wax, ba, bx, lam, state, h_lat, dr == 1)

        filt = (filt_w1[l], filt_b1[l], filt_w2[l], filt_b2[l], filt_w3[l], filt_b3[l], filt_w4[l],
                filt_b4[l], filt_freq[l])
        hcw, hcb, hbias = hy_conv_w[l], hy_conv_b[l][None, :], hy_bias[l][None, :]
        og = out_g[l][None, :]
        wo = w_out[l].astype(BF16)

        u_l, x0_l = _hypre(pl_, 2 * lw, hcw, hcb)
        kr, ki = _hyena_spectrum(*_hyena_filter(n_lat, *filt))
        y_hy = _hyena_conv(u_l, hbias, kr, ki)
        xl = _mixout(h_lat, pl_, y_hy, x0_l, og, wo, xl, m[1], 0)
        xl = _ffn(xl, m[2], g[2], ffn_wg, ffn_wu, ffn_wd, l, 1, 0)

        if not last:
            u_c, x0_c = _hypre(pc_, 2 * lw, hcw, hcb)
            kr, ki = _hyena_spectrum(*_hyena_filter(n_ctx, *filt))
            y_hy = _hyena_conv(u_c, hbias, kr, ki)
            xc = _mixout(h_ctx, pc_, y_hy, x0_c, og, wo, xc, m[1], 1)
            xc = _ffn(xc, m[2], g[2], ffn_wg, ffn_wu, ffn_wd, l, 1, 1)

    return _final_norm(xl, final_g[None, :], lat_is_col)[None]
```

```python
import functools
import math

import numpy as np
import jax
import jax.numpy as jnp
from jax import lax
from jax.experimental import pallas as pl
from jax.experimental.pallas import tpu as pltpu

F32 = jnp.float32
BF16 = jnp.bfloat16
EPS = 1e-6
LRU_C = 8.0
GRID_W = 64
HY_BANDS = 16
HY_MAX_DECAY = math.log(1e-2) / 0.3
HY_MIN_DECAY = math.log(1e-2) / 1.5

V7X_VMEM_LIMIT_BYTES = 58 * 1024 * 1024
FFT_B = 128
SUBLANES = 8


def _cparams(sem):
    return pltpu.CompilerParams(dimension_semantics=sem, vmem_limit_bytes=V7X_VMEM_LIMIT_BYTES)


def _rms(x):
    return x * lax.rsqrt(jnp.mean(x * x, axis=-1, keepdims=True) + EPS)


def _modulated(x, mod_ref, g_ref, stream):
    shift = mod_ref[3 * stream:3 * stream + 1, :]
    scale = mod_ref[3 * stream + 1:3 * stream + 2, :]
    return _rms(x) * g_ref[...] * (1.0 + scale) + shift


def _ada_kernel(c_ref, w_ref, b_ref, o_ref):
    c = c_ref[...]
    s = c * jax.nn.sigmoid(c)
    o_ref[...] = jnp.dot(s, w_ref[...], precision=lax.Precision.HIGHEST,
                         preferred_element_type=F32) + b_ref[...]


def _ada_mods(c, c_ctx, ada_w, ada_b):
    depth, d, nm = ada_w.shape
    tn = d
    cc = jnp.zeros((SUBLANES, d), F32).at[0].set(c[0]).at[1].set(c_ctx)
    out = pl.pallas_call(
        _ada_kernel,
        grid=(depth, nm // tn),
        in_specs=[pl.BlockSpec((SUBLANES, d), lambda l, j: (0, 0)),
                  pl.BlockSpec((None, d, tn), lambda l, j: (l, 0, j)),
                  pl.BlockSpec((None, 1, tn), lambda l, j: (l, 0, j))],
        out_specs=pl.BlockSpec((None, SUBLANES, tn), lambda l, j: (l, 0, j)),
        out_shape=jax.ShapeDtypeStruct((depth, SUBLANES, nm), F32),
        compiler_params=_cparams(("arbitrary", "arbitrary")),
        name="ada_mods",
    )(cc, ada_w, ada_b[:, None, :])
    return out


def _sublayer_mods(mods_l, d, k):
    lat = mods_l[0, 3 * k * d:(3 * k + 3) * d].reshape(3, d)
    ctx = mods_l[1, 3 * k * d:(3 * k + 3) * d].reshape(3, d)
    return jnp.concatenate([lat, ctx, jnp.zeros((2, d), F32)], axis=0)


NORM_CHUNK = 256


def _ffn_up_kernel(x_ref, mod_ref, g_ref, wg_ref, wu_ref, h_ref, u_scr, *, stream, tm):
    wg = wg_ref[...].astype(BF16)
    wu = wu_ref[...].astype(BF16)

    def swiglu_rows(u):
        g = jnp.dot(u, wg, preferred_element_type=F32)
        up = jnp.dot(u, wu, preferred_element_type=F32)
        return (g * jax.nn.sigmoid(g) * up).astype(BF16)

    @pl.when(pl.program_id(1) == 0)
    def _():
        for r in range(0, tm, NORM_CHUNK):
            rs = slice(r, min(r + NORM_CHUNK, tm))
            u = _modulated(x_ref[rs, :], mod_ref, g_ref, stream).astype(BF16)
            u_scr[rs, :] = u
            h_ref[rs, :] = swiglu_rows(u)

    @pl.when(pl.program_id(1) > 0)
    def _():
        h_ref[...] = swiglu_rows(u_scr[...])


def _ffn_down_kernel(h_ref, wd_ref, x_ref, mod_ref, o_ref, *, stream):
    acc = jnp.dot(h_ref[...], wd_ref[...].astype(BF16), preferred_element_type=F32)
    gate = mod_ref[3 * stream + 2:3 * stream + 3, :]
    o_ref[...] = x_ref[...] + (0.5 * gate) * acc


def _ffn(x, mod, g, wg, wu, wd, l, j, stream):
    rows, d = x.shape
    dff = wg.shape[-1]
    tm = min(rows, 1024)
    tf = 512
    tn = 256
    h = pl.pallas_call(
        functools.partial(_ffn_up_kernel, stream=stream, tm=tm),
        grid=(rows // tm, dff // tf),
        in_specs=[pl.BlockSpec((tm, d), lambda i, f: (i, 0)),
                  pl.BlockSpec((SUBLANES, d), lambda i, f: (0, 0)),
                  pl.BlockSpec((1, d), lambda i, f: (0, 0)),
                  pl.BlockSpec((None, None, d, tf), lambda i, f: (l, j, 0, f)),
                  pl.BlockSpec((None, None, d, tf), lambda i, f: (l, j, 0, f))],
        out_specs=pl.BlockSpec((tm, tf), lambda i, f: (i, f)),
        out_shape=jax.ShapeDtypeStruct((rows, dff), BF16),
        scratch_shapes=[pltpu.VMEM((tm, d), BF16)],
        compiler_params=_cparams(("arbitrary", "arbitrary")),
        name="ffn_up",
    )(x, mod, g, wg, wu)
    return pl.pallas_call(
        functools.partial(_ffn_down_kernel, stream=stream),
        grid=(rows // tm, d // tn),
        in_specs=[pl.BlockSpec((tm, dff), lambda i, n: (i, 0)),
                  pl.BlockSpec((None, None, dff, tn), lambda i, n: (l, j, 0, n)),
                  pl.BlockSpec((tm, tn), lambda i, n: (i, n)),
                  pl.BlockSpec((SUBLANES, tn), lambda i, n: (0, n))],
        out_specs=pl.BlockSpec((tm, tn), lambda i, n: (i, n)),
        out_shape=jax.ShapeDtypeStruct((rows, d), F32),
        compiler_params=_cparams(("arbitrary", "arbitrary")),
        name="ffn_down",
    )(h, wd, x, mod)


def _win_kernel(x_ref, mod_ref, g_ref, w_ref, p_ref, *rest, pieces, prow, d, stream, emit_x):
    if emit_x:
        xo_ref, u_scr = rest
    else:
        (u_scr,) = rest

    w = w_ref[...]

    @pl.when(pl.program_id(1) == 0)
    def _():
        tm = pieces * prow
        seg = min(prow, NORM_CHUNK)
        for c0 in range(0, tm, NORM_CHUNK):
            c1 = min(c0 + NORM_CHUNK, tm)
            for r0 in range(c0, c1, seg):
                q, r = divmod(r0, prow)
                xt = x_ref[r:r + seg, q * d:(q + 1) * d]
                u_scr[r0:r0 + seg, :] = _modulated(xt, mod_ref, g_ref, stream).astype(BF16)
                if emit_x:
                    xo_ref[r0:r0 + seg, :] = xt
            p_ref[c0:c1, :] = jnp.dot(u_scr[c0:c1, :], w, preferred_element_type=F32)

    @pl.when(pl.program_id(1) > 0)
    def _():
        p_ref[...] = jnp.dot(u_scr[...], w, preferred_element_type=F32)


def _win(x, mod, g, w_in_bf16, ncols, stream, perm):
    rows, d = x.shape
    tm = min(rows, 1024)
    tn = 1024 if perm is None else 512
    if perm is None:
        prow, pieces, xv = tm, 1, x
        x_spec = pl.BlockSpec((tm, d), lambda i, n: (i, 0))
    else:
        grid_h = rows // GRID_W
        prow = grid_h if perm == "to_col" else GRID_W
        pieces = tm // prow
        xv = x.reshape(prow, (rows // prow) * d)
        x_spec = pl.BlockSpec((prow, pieces * d), lambda i, n: (0, i))
    emit_x = perm is not None
    out_shape = [jax.ShapeDtypeStruct((rows, ncols), F32)]
    out_specs = [pl.BlockSpec((tm, tn), lambda i, n: (i, n))]
    if emit_x:
        out_shape.append(jax.ShapeDtypeStruct((rows, d), F32))
        out_specs.append(pl.BlockSpec((tm, d), lambda i, n: (i, 0)))
    res = pl.pallas_call(
        functools.partial(_win_kernel, pieces=pieces, prow=prow, d=d, stream=stream, emit_x=emit_x),
        grid=(rows // tm, ncols // tn),
        in_specs=[x_spec,
                  pl.BlockSpec((SUBLANES, d), lambda i, n: (0, 0)),
                  pl.BlockSpec((1, d), lambda i, n: (0, 0)),
                  pl.BlockSpec((d, tn), lambda i, n: (0, n))],
        out_specs=out_specs,
        out_shape=out_shape,
        scratch_shapes=[pltpu.VMEM((tm, d), BF16)],
        compiler_params=_cparams(("arbitrary", "arbitrary")),
        name="mixer_in_proj",
    )(xv, mod, g, w_in_bf16)
    return (res[0], res[1]) if emit_x else (res[0], x)


def _lru_kernel(x_ref, xp_ref, xn_ref, cw_ref, cb_ref, wax_ref, ba_ref, bx_ref, lam_ref, h0_ref, *rest,
                t_blk, nblk, heads, reverse, add_prev):
    if add_prev:
        prev_ref, o_ref, hf_ref, xs, a_scr, b_scr, carry = rest
    else:
        o_ref, hf_ref, xs, a_scr, b_scr, carry = rest
    j = pl.program_id(0)
    blk = (nblk - 1 - j) if reverse else j
    c = x_ref.shape[1]
    hd = c // heads

    @pl.when(j == 0)
    def _():
        carry[...] = h0_ref[...]

    zero8 = jnp.zeros((SUBLANES, c), F32)
    xs[0:SUBLANES, :] = jnp.where(blk == 0, zero8, xp_ref[...])
    xs[SUBLANES:SUBLANES + t_blk, :] = x_ref[...]
    xs[SUBLANES + t_blk:2 * SUBLANES + t_blk, :] = jnp.where(blk == nblk - 1, zero8, xn_ref[...])
    cv = cb_ref[...]
    for k in range(cw_ref.shape[0]):
        cv = cv + xs[SUBLANES - 1 + k:SUBLANES - 1 + k + t_blk, :] * cw_ref[k:k + 1, :]

    z = -lam_ref[...]
    c_sp = LRU_C * (jnp.maximum(z, 0.0) + jnp.log(1.0 + jnp.exp(-jnp.abs(z))))
    cvb = cv.astype(BF16)
    for h in range(heads):
        sl = slice(h * hd, (h + 1) * hd)
        gx = jnp.dot(cvb[:, sl], wax_ref[h], preferred_element_type=F32)
        r = 0.5 + 0.5 * jnp.tanh(0.5 * (gx[:, :hd] + ba_ref[:, sl]))
        ig = 0.5 + 0.5 * jnp.tanh(0.5 * (gx[:, hd:] + bx_ref[:, sl]))
        a = jnp.exp(-c_sp[:, sl] * r)
        a_scr[:, sl] = a
        b_scr[:, sl] = jnp.sqrt(1.0 - a * a) * (ig * cv[:, sl])

    row = lax.broadcasted_iota(jnp.int32, (SUBLANES, c), 0)
    ngrp = t_blk // SUBLANES

    def group(gi, hc):
        g = (ngrp - 1 - gi) if reverse else gi
        r0 = pl.multiple_of(g * SUBLANES, SUBLANES)
        av = a_scr[pl.ds(r0, SUBLANES), :]
        bv = b_scr[pl.ds(r0, SUBLANES), :]
        for s in (1, 2, 4):
            sh = (SUBLANES - s) if reverse else s
            m = (row < SUBLANES - s) if reverse else (row >= s)
            a_s = pltpu.roll(av, sh, axis=0)
            b_s = pltpu.roll(bv, sh, axis=0)
            bv = jnp.where(m, av * b_s + bv, bv)
            av = jnp.where(m, av * a_s, av)
        hv = av * hc + bv
        if add_prev:
            o_ref[pl.ds(r0, SUBLANES), :] = hv + prev_ref[pl.ds(r0, SUBLANES), :]
        else:
            o_ref[pl.ds(r0, SUBLANES), :] = hv
        return hv[0:1, :] if reverse else hv[SUBLANES - 1:SUBLANES, :]

    hc = lax.fori_loop(0, ngrp, group, carry[...])
    carry[...] = hc
    hf_ref[...] = hc


def _lru_scan(p, cw, cb, wax, ba, bx, lam, h0, prev, reverse):
    seq = p.shape[0]
    c = cw.shape[1]
    heads = wax.shape[0]
    t_blk = 256
    nblk = seq // t_blk
    per = t_blk // SUBLANES
    nb8 = seq // SUBLANES

    def bi(j):
        return (nblk - 1 - j) if reverse else j

    in_specs = [pl.BlockSpec((t_blk, c), lambda j: (bi(j), 0)),
                pl.BlockSpec((SUBLANES, c), lambda j: (jnp.maximum(bi(j) * per - 1, 0), 0)),
                pl.BlockSpec((SUBLANES, c), lambda j: (jnp.minimum((bi(j) + 1) * per, nb8 - 1), 0)),
                pl.BlockSpec(cw.shape, lambda j: (0, 0)),
                pl.BlockSpec((1, c), lambda j: (0, 0)),
                pl.BlockSpec(wax.shape, lambda j: (0, 0, 0)),
                pl.BlockSpec((1, c), lambda j: (0, 0)),
                pl.BlockSpec((1, c), lambda j: (0, 0)),
                pl.BlockSpec((1, c), lambda j: (0, 0)),
                pl.BlockSpec((1, c), lambda j: (0, 0))]
    args = [p, p, p, cw, cb, wax, ba, bx, lam, h0]
    if prev is not None:
        in_specs.append(pl.BlockSpec((t_blk, c), lambda j: (bi(j), 0)))
        args.append(prev)
    return pl.pallas_call(
        functools.partial(_lru_kernel, t_blk=t_blk, nblk=nblk, heads=heads, reverse=reverse,
                          add_prev=prev is not None),
        grid=(nblk,),
        in_specs=in_specs,
        out_specs=[pl.BlockSpec((t_blk, c), lambda j: (bi(j), 0)),
                   pl.BlockSpec((1, c), lambda j: (0, 0))],
        out_shape=[jax.ShapeDtypeStruct((seq, c), F32), jax.ShapeDtypeStruct((1, c), F32)],
        scratch_shapes=[pltpu.VMEM((t_blk + 2 * SUBLANES, c), F32),
                        pltpu.VMEM((t_blk, c), F32),
                        pltpu.VMEM((t_blk, c), F32),
                        pltpu.VMEM((1, c), F32)],
        compiler_params=_cparams(("arbitrary",)),
        name="rglru_scan",
    )(*args)


def _hypre_kernel(x0_ref, x0p, x0n, x1_ref, x1p, x1n, v_ref, vp, vn, cw_ref, cb_ref, u_ref, x0c_ref, xs,
                  *, t_blk, nblk, c):
    blk = pl.program_id(0)
    zero8 = jnp.zeros((SUBLANES, c), F32)

    def conv3(cur, prv, nxt, part):
        xs[0:SUBLANES, :] = jnp.where(blk == 0, zero8, prv[...])
        xs[SUBLANES:SUBLANES + t_blk, :] = cur[...]
        xs[SUBLANES + t_blk:2 * SUBLANES + t_blk, :] = jnp.where(blk == nblk - 1, zero8, nxt[...])
        sl = slice(part * c, (part + 1) * c)
        out = cb_ref[:, sl]
        for k in range(cw_ref.shape[0]):
            out = out + xs[SUBLANES - 1 + k:SUBLANES - 1 + k + t_blk, :] * cw_ref[k:k + 1, sl]
        return out

    x0c_ref[...] = conv3(x0_ref, x0p, x0n, 0)
    x1c = conv3(x1_ref, x1p, x1n, 1)
    u_ref[...] = conv3(v_ref, vp, vn, 2) * x1c


def _hypre(p, col0, cw, cb):
    seq = p.shape[0]
    c = cw.shape[1] // 3
    t_blk = min(seq, 512)
    nblk = seq // t_blk
    per = t_blk // SUBLANES
    nb8 = seq // SUBLANES
    cb0 = col0 // c
    in_specs, args = [], []
    for part in range(3):
        in_specs += [pl.BlockSpec((t_blk, c), lambda j, part=part: (j, cb0 + part)),
                     pl.BlockSpec((SUBLANES, c), lambda j, part=part: (jnp.maximum(j * per - 1, 0), cb0 + part)),
                     pl.BlockSpec((SUBLANES, c),
                                  lambda j, part=part: (jnp.minimum((j + 1) * per, nb8 - 1), cb0 + part))]
        args += [p, p, p]
    in_specs += [pl.BlockSpec(cw.shape, lambda j: (0, 0)), pl.BlockSpec((1, 3 * c), lambda j: (0, 0))]
    args += [cw, cb]
    return pl.pallas_call(
        functools.partial(_hypre_kernel, t_blk=t_blk, nblk=nblk, c=c),
        grid=(nblk,),
        in_specs=in_specs,
        out_specs=[pl.BlockSpec((t_blk, c), lambda j: (j, 0)), pl.BlockSpec((t_blk, c), lambda j: (j, 0))],
        out_shape=[jax.ShapeDtypeStruct((seq, c), F32), jax.ShapeDtypeStruct((seq, c), F32)],
        scratch_shapes=[pltpu.VMEM((t_blk + 2 * SUBLANES, c), F32)],
        compiler_params=_cparams(("arbitrary",)),
        name="hyena_short_conv",
    )(*args)


FILT_SLOT = 64


def _filt_kernel(w1_ref, b1_ref, w2_ref, b2_ref, w3_ref, b3_ref, w4a_ref, w4b_ref, b4_ref, fr_ref,
                 k_ref, nrm_ref, *, seq, t_blk, c):
    i = pl.program_id(0)
    hi = lax.Precision.HIGHEST
    lanes = 2 * FILT_SLOT
    th = t_blk // 2
    lane = lax.broadcasted_iota(jnp.int32, (1, lanes), 1)
    slot = lane & (FILT_SLOT - 1)
    pos_a = i * t_blk + lax.broadcasted_iota(jnp.int32, (th, 1), 0)
    pos = jnp.where(lane < FILT_SLOT, pos_a, pos_a + th)
    tf = jnp.where(pos < seq, pos, 2 * seq - pos).astype(F32)
    t = tf * (1.0 / (seq - 1))
    w = (2.0 * math.pi / seq) * tf
    is_cos = (slot >= 1) & (slot <= HY_BANDS)
    is_sin = (slot > HY_BANDS) & (slot <= 2 * HY_BANDS)
    band = jnp.where(is_cos, slot - 1, slot - 1 - HY_BANDS).astype(F32)
    f = 1e-4 + band * ((HY_BANDS - 1 - 1e-4) / (HY_BANDS - 1))
    phase = jnp.where(is_cos, 0.5 * math.pi, 0.0)
    sign = jnp.where(is_cos, 1.0, jnp.where(is_sin, -1.0, 0.0))
    z = jnp.where(slot == 0, t, sign * jnp.sin(w * f + phase))
    fr = fr_ref[...]
    h = jnp.sin(fr * (jnp.dot(z, w1_ref[...], precision=hi, preferred_element_type=F32) + b1_ref[...]))
    h = jnp.sin(fr * (jnp.dot(h, w2_ref[...], precision=hi, preferred_element_type=F32) + b2_ref[...]))
    h = jnp.sin(fr * (jnp.dot(h, w3_ref[...], precision=hi, preferred_element_type=F32) + b3_ref[...]))
    ch = lax.broadcasted_iota(jnp.int32, (1, c), 1).astype(F32)
    delta = jnp.abs(HY_MIN_DECAY + ch * ((HY_MAX_DECAY - HY_MIN_DECAY) / (c - 1)))
    part = jnp.zeros((1, c), F32)
    for half, w4_ref in enumerate((w4a_ref, w4b_ref)):
        pos_h = pos_a + half * th
        t_h = jnp.where(pos_h < seq, pos_h, 2 * seq - pos_h).astype(F32) * (1.0 / (seq - 1))
        k = jnp.dot(h, w4_ref[...], precision=hi, preferred_element_type=F32) + b4_ref[...]
        k = jnp.where(pos_h != seq, k * jnp.exp(-t_h * delta), 0.0)
        k_ref[half * th:(half + 1) * th, :] = k
        part = part + jnp.sum(jnp.abs(k), axis=0, keepdims=True)

    @pl.when(i == 0)
    def _():
        nrm_ref[...] = part

    @pl.when(i > 0)
    def _():
        nrm_ref[...] = nrm_ref[...] + part


def _slots(a, row_used):
    r, cdim = a.shape
    blk = jnp.pad(a, ((0, (FILT_SLOT if row_used else 1) - r), (0, FILT_SLOT - cdim)))
    if not row_used:
        return jnp.concatenate([blk, blk], axis=1)
    z = jnp.zeros_like(blk)
    return jnp.concatenate([jnp.concatenate([blk, z], axis=1), jnp.concatenate([z, blk], axis=1)], axis=0)


def _hyena_filter(seq, w1, b1, w2, b2, w3, b3, w4, b4, freq):
    c = w4.shape[1] // 2
    od = w2.shape[0]
    assert od <= FILT_SLOT and w1.shape[0] <= FILT_SLOT
    lanes = 2 * FILT_SLOT
    t_blk = min(seq, 1024)
    half_blocks = seq // t_blk
    full = lambda shape: pl.BlockSpec(shape, lambda i: (0,) * len(shape))
    sq, row = (lanes, lanes), (1, lanes)
    w4p = jnp.pad(w4, ((0, FILT_SLOT - od), (0, 0)))
    w4a = jnp.concatenate([w4p, jnp.zeros_like(w4p)], axis=0)
    w4b = jnp.concatenate([jnp.zeros_like(w4p), w4p], axis=0)
    w4_spec = pl.BlockSpec((lanes, c), lambda i: (0, i // half_blocks))
    return pl.pallas_call(
        functools.partial(_filt_kernel, seq=seq, t_blk=t_blk, c=c),
        grid=(2 * seq // t_blk,),
        in_specs=[full(sq), full(row), full(sq), full(row), full(sq), full(row), w4_spec, w4_spec,
                  pl.BlockSpec((1, c), lambda i: (0, i // half_blocks)),
                  full(row)],
        out_specs=[pl.BlockSpec((t_blk, c), lambda i: (i, 0)), pl.BlockSpec((1, c), lambda i: (0, 0))],
        out_shape=[jax.ShapeDtypeStruct((2 * seq, c), F32), jax.ShapeDtypeStruct((1, c), F32)],
        compiler_params=_cparams(("arbitrary",)),
        name="hyena_filter",
    )(_slots(w1, True), _slots(b1[None, :], False), _slots(w2, True), _slots(b2[None, :], False),
      _slots(w3, True), _slots(b3[None, :], False), w4a, w4b, b4[None, :], _slots(freq[None, :], False))


def _bitrev(p, bits):
    r = 0
    for k in range(bits):
        r = (r << 1) | ((p >> k) & 1)
    return r


def _real_block(z):
    return np.block([[z.real, -z.imag], [z.imag, z.real]])


def _fft_split(s2):
    n_lo = max(min(8, s2 // 2), 1)
    return s2 // n_lo, n_lo


@functools.lru_cache(maxsize=None)
def _fft_tables(s):
    s2 = s // 2
    n = s * FFT_B
    n_hi, n_lo = _fft_split(s2)
    bits_hi, bits_lo = n_hi.bit_length() - 1, n_lo.bit_length() - 1
    b = np.arange(FFT_B)
    dft = np.exp(-2j * np.pi * np.outer(b, b) / FFT_B)
    e_lo = [2 * n_hi * _bitrev(p, bits_lo) for p in range(n_lo)]
    g = [dft * np.exp(-2j * np.pi * b * e / n)[None, :] for e in e_lo]
    lhs_f = np.stack([_real_block(x) for x in g])
    lhs_i = np.stack([_real_block(np.conj(x).T) for x in g])
    e_hi = np.array([[2 * _bitrev(p, bits_hi) + hf for p in range(n_hi)] for hf in range(2)])
    ang = 2.0 * np.pi * e_hi[:, :, None, None] * b[None, None, :, None] / n * np.ones((1, 1, 1, FFT_B))
    lvl = 2.0 * np.pi * np.arange(max(s2 // 2, 1)) / s2
    first = np.concatenate([np.zeros(s2), 2.0 * np.pi * np.arange(s2) / s])
    f32 = lambda x: np.asarray(x, np.float32)
    return dict(lhs_f=f32(lhs_f), lhs_i=f32(lhs_i), col_c=f32(np.cos(ang)), col_s=f32(np.sin(ang)),
                lvl_c=f32(np.cos(lvl)), lvl_s=f32(np.sin(lvl)), first_c=f32(np.cos(first)),
                first_s=f32(np.sin(first)))


FFT_ROWS = 32


def _cmul(xr, xi, c, s):
    return xr * c + xi * s, xi * c - xr * s


def _cmul_conj(xr, xi, c, s):
    return xr * c - xi * s, xi * c + xr * s


def _pass_plan(s2):
    levels = s2.bit_length() - 1
    plan, h = [], s2 // 2
    if levels % 2 == 1:
        plan.append((h, 2))
        h //= 2
    while h >= 1:
        plan.append((h, 4))
        h //= 4
    return plan


def _butterfly_pass(load, store, lc_ref, ls_ref, s2, h, radix, inverse):
    h2 = h // 2 if radix == 4 else h
    trivial = h2 == 1
    shift = h2.bit_length() - 1

    def group(idx, carry):
        g = idx >> shift
        i = idx & (h2 - 1)
        base = g * 2 * h + i
        blocks = [base + k * h2 for k in range(radix)]
        if not trivial:
            c1, s1 = lc_ref[i * (s2 // (2 * h))], ls_ref[i * (s2 // (2 * h))]
            if radix == 4:
                c2, s2_ = lc_ref[i * (s2 // h)], ls_ref[i * (s2 // h)]
        fwd_mul = (lambda xr, xi, c, s: (xr, xi)) if trivial else _cmul
        inv_mul = (lambda xr, xi, c, s: (xr, xi)) if trivial else _cmul_conj
        if trivial:
            c1 = s1 = c2 = s2_ = None
        for r in range(0, FFT_B, FFT_ROWS):
            x = [load(bk, r) for bk in blocks]
            if radix == 2:
                (ar, ai), (br, bi) = x
                if inverse:
                    br, bi = inv_mul(br, bi, c1, s1)
                    out = [(ar + br, ai + bi), (ar - br, ai - bi)]
                else:
                    out = [(ar + br, ai + bi), fwd_mul(ar - br, ai - bi, c1, s1)]
            elif inverse:
                (x0r, x0i), (x1r, x1i), (x2r, x2i), (x3r, x3i) = x
                x1r, x1i = inv_mul(x1r, x1i, c2, s2_)
                x3r, x3i = inv_mul(x3r, x3i, c2, s2_)
                t0r, t0i, t1r, t1i = x0r + x1r, x0i + x1i, x0r - x1r, x0i - x1i
                t2r, t2i = inv_mul(x2r + x3r, x2i + x3i, c1, s1)
                t3r, t3i = inv_mul(x2r - x3r, x2i - x3i, c1, s1)
                out = [(t0r + t2r, t0i + t2i), (t1r - t3i, t1i + t3r),
                       (t0r - t2r, t0i - t2i), (t1r + t3i, t1i - t3r)]
            else:
                (x0r, x0i), (x1r, x1i), (x2r, x2i), (x3r, x3i) = x
                t0r, t0i, t1r, t1i = x0r + x2r, x0i + x2i, x1r + x3r, x1i + x3i
                t2r, t2i = fwd_mul(x0r - x2r, x0i - x2i, c1, s1)
                t3r, t3i = fwd_mul(x1r - x3r, x1i - x3i, c1, s1)
                out = [(t0r + t1r, t0i + t1i), fwd_mul(t0r - t1r, t0i - t1i, c2, s2_),
                       (t2r + t3i, t2i - t3r), fwd_mul(t2r - t3i, t2i + t3r, c2, s2_)]
            for bk, (o_r, o_i) in zip(blocks, out):
                store(bk, r, o_r, o_i)
        return carry

    lax.fori_loop(0, s2 // radix, group, 0)


def _work_access(wr, wi):
    def rows(bk, r):
        return pl.ds(pl.multiple_of(bk * FFT_B + r, FFT_ROWS), FFT_ROWS)

    def load(bk, r):
        return wr[rows(bk, r), :], wi[rows(bk, r), :]

    def store(bk, r, re, im):
        wr[rows(bk, r), :] = re
        wi[rows(bk, r), :] = im

    return load, store


def _run_passes(plan, first_load, last_store, work, lc_ref, ls_ref, s2, inverse):
    w_load, w_store = work
    for k, (h, radix) in enumerate(plan):
        load = first_load if k == 0 else w_load
        store = last_store if k == len(plan) - 1 else w_store
        _butterfly_pass(load, store, lc_ref, ls_ref, s2, h, radix, inverse)


MID_UNROLL = 4


def _pair_blocks(pp, s2):
    n_hi, n_lo = _fft_split(s2)
    half = n_hi // 2
    p_lo = pp >> (half.bit_length() - 1)
    j = pp & (half - 1)
    return p_lo, [(2 * j + q, pl.multiple_of(((2 * j + q) * n_lo + p_lo) * FFT_B, FFT_B)) for q in range(2)]


def _for_pairs(s2, body):
    def step(pp, carry):
        body(pp)
        return carry
    lax.fori_loop(0, s2 // 2, step, 0, unroll=min(MID_UNROLL, s2 // 2))


def _pack_store(panel, cc_ref, cs_ref, s2):
    n_hi, n_lo = _fft_split(s2)
    lo_bits = n_lo.bit_length() - 1

    def store(bk, r, re, im):
        p_hi = bk >> lo_bits
        p_lo = bk & (n_lo - 1)
        pp = p_lo * (n_hi // 2) + (p_hi >> 1)
        q = p_hi & 1
        tr, ti = _cmul(re, im, cc_ref[p_hi, r:r + FFT_ROWS, :], cs_ref[p_hi, r:r + FFT_ROWS, :])
        panel[pp, q, r:r + FFT_ROWS, :] = tr.astype(BF16)
        panel[pp, q, FFT_B + r:FFT_B + r + FFT_ROWS, :] = ti.astype(BF16)

    return store


def _dft_pair(m_ref, panel, pp, p_lo):
    blk = FFT_B
    rhs = jnp.concatenate([panel[pp, 0], panel[pp, 1]], axis=1)
    out = jnp.dot(m_ref[p_lo], rhs, preferred_element_type=F32)
    for q in range(2):
        for r in range(0, blk, FFT_ROWS):
            yield (q, r, out[r:r + FFT_ROWS, q * blk:(q + 1) * blk],
                   out[blk + r:blk + r + FFT_ROWS, q * blk:(q + 1) * blk])


def _spec_kernel(lo_ref, hi_ref, nrm_ref, fc_ref, fs_ref, lc_ref, ls_ref, cc_ref, cs_ref, mf_ref,
                 kr_ref, ki_ref, panel, *, s2):
    blk = FFT_B
    hf = pl.program_id(1)
    n = 2 * s2 * blk
    sign = (1 - 2 * hf).astype(F32)

    def first_load(bk, r):
        rows = pl.ds(pl.multiple_of(bk * blk + r, FFT_ROWS), FFT_ROWS)
        d = lo_ref[rows, :] + sign * hi_ref[rows, :]
        return d * fc_ref[hf * s2 + bk], -d * fs_ref[hf * s2 + bk]

    _run_passes(_pass_plan(s2), first_load, _pack_store(panel, cc_ref, cs_ref, s2),
                _work_access(kr_ref, ki_ref), lc_ref, ls_ref, s2, inverse=False)
    scale = 1.0 / (nrm_ref[...] * n)

    def dft(pp):
        p_lo, blocks = _pair_blocks(pp, s2)
        for q, r, yr, yi in _dft_pair(mf_ref, panel, pp, p_lo):
            rows = pl.ds(blocks[q][1] + r, FFT_ROWS)
            kr_ref[rows, :] = yr * scale
            ki_ref[rows, :] = yi * scale

    _for_pairs(s2, dft)


def _conv_kernel(u_ref, bias_ref, kr_ref, ki_ref, fc_ref, fs_ref, lc_ref, ls_ref, cc_ref, cs_ref,
                 mf_ref, mi_ref, y_ref, wr, wi, panel, *, s2):
    blk = FFT_B
    hf = pl.program_id(1)
    plan = _pass_plan(s2)
    work = _work_access(wr, wi)

    def first_load(bk, r):
        ub = u_ref[pl.ds(pl.multiple_of(bk * blk + r, FFT_ROWS), FFT_ROWS), :]
        return ub * fc_ref[hf * s2 + bk], -ub * fs_ref[hf * s2 + bk]

    _run_passes(plan, first_load, _pack_store(panel, cc_ref, cs_ref, s2), work, lc_ref, ls_ref, s2,
                inverse=False)

    def spectrum_product(pp):
        p_lo, blocks = _pair_blocks(pp, s2)
        for q, r, yr, yi in _dft_pair(mf_ref, panel, pp, p_lo):
            rows = pl.ds(blocks[q][1] + r, FFT_ROWS)
            kr, ki = kr_ref[rows, :], ki_ref[rows, :]
            panel[pp, q, r:r + FFT_ROWS, :] = (yr * kr - yi * ki).astype(BF16)
            panel[pp, q, blk + r:blk + r + FFT_ROWS, :] = (yr * ki + yi * kr).astype(BF16)

    _for_pairs(s2, spectrum_product)

    def inverse_dft(pp):
        p_lo, blocks = _pair_blocks(pp, s2)
        for q, r, er, ei in _dft_pair(mi_ref, panel, pp, p_lo):
            p_hi, r0 = blocks[q]
            rows = pl.ds(r0 + r, FFT_ROWS)
            wr[rows, :], wi[rows, :] = _cmul_conj(er, ei, cc_ref[p_hi, r:r + FFT_ROWS, :],
                                                  cs_ref[p_hi, r:r + FFT_ROWS, :])

    _for_pairs(s2, inverse_dft)

    def rows_of(bk, r):
        return pl.ds(pl.multiple_of(bk * blk + r, FFT_ROWS), FFT_ROWS)

    def store_even(bk, r, er, ei):
        y_ref[rows_of(bk, r), :] = er

    def store_odd(bk, r, er, ei):
        rows = rows_of(bk, r)
        val = er * fc_ref[s2 + bk] - ei * fs_ref[s2 + bk]
        y_ref[rows, :] = y_ref[rows, :] + u_ref[rows, :] * bias_ref[...] + val

    inv_plan = plan[::-1]
    if len(inv_plan) > 1:
        _run_passes(inv_plan[:-1], work[0], work[1], work, lc_ref, ls_ref, s2, inverse=True)
    h_last, radix_last = inv_plan[-1]
    for half, store in ((0, store_even), (1, store_odd)):
        @pl.when(hf == half)
        def _(store=store):
            _butterfly_pass(work[0], store, lc_ref, ls_ref, s2, h_last, radix_last, True)


def _smem_spec():
    return pl.BlockSpec(memory_space=pltpu.SMEM)


def _fft_table_specs(s2):
    n_hi, n_lo = _fft_split(s2)
    col = pl.BlockSpec((None, n_hi, FFT_B, FFT_B), lambda ct, hf: (hf, 0, 0, 0))
    lhs = pl.BlockSpec((n_lo, 2 * FFT_B, 2 * FFT_B), lambda ct, hf: (0, 0, 0))
    return [_smem_spec(), _smem_spec(), _smem_spec(), _smem_spec(), col, col], lhs


def _hyena_spectrum(kfull, nrm):
    n, c = kfull.shape
    seq = n // 2
    s2 = seq // FFT_B
    tb = _fft_tables(2 * s2)
    lanes = 128
    tab_specs, lhs_spec = _fft_table_specs(s2)
    return pl.pallas_call(
        functools.partial(_spec_kernel, s2=s2),
        grid=(c // lanes, 2),
        in_specs=[pl.BlockSpec((seq, lanes), lambda ct, hf: (0, ct)),
                  pl.BlockSpec((seq, lanes), lambda ct, hf: (1, ct)),
                  pl.BlockSpec((1, lanes), lambda ct, hf: (0, ct))] + tab_specs + [lhs_spec],
        out_specs=[pl.BlockSpec((None, seq, lanes), lambda ct, hf: (hf, 0, ct)),
                   pl.BlockSpec((None, seq, lanes), lambda ct, hf: (hf, 0, ct))],
        out_shape=[jax.ShapeDtypeStruct((2, seq, c), F32), jax.ShapeDtypeStruct((2, seq, c), F32)],
        scratch_shapes=[pltpu.VMEM((s2 // 2, 2, 2 * FFT_B, FFT_B), BF16)],
        compiler_params=_cparams(("arbitrary", "arbitrary")),
        name="hyena_spectrum",
    )(kfull, kfull, nrm, tb["first_c"], tb["first_s"], tb["lvl_c"], tb["lvl_s"],
      tb["col_c"], tb["col_s"], jnp.asarray(tb["lhs_f"], BF16))


def _hyena_conv(u, bias, kr, ki):
    seq, c = u.shape
    s2 = seq // FFT_B
    tb = _fft_tables(2 * s2)
    lanes = 128
    tab_specs, lhs_spec = _fft_table_specs(s2)
    return pl.pallas_call(
        functools.partial(_conv_kernel, s2=s2),
        grid=(c // lanes, 2),
        in_specs=[pl.BlockSpec((seq, lanes), lambda ct, hf: (0, ct)),
                  pl.BlockSpec((1, lanes), lambda ct, hf: (0, ct)),
                  pl.BlockSpec((None, seq, lanes), lambda ct, hf: (hf, 0, ct)),
                  pl.BlockSpec((None, seq, lanes), lambda ct, hf: (hf, 0, ct))]
                 + tab_specs + [lhs_spec, lhs_spec],
        out_specs=pl.BlockSpec((seq, lanes), lambda ct, hf: (0, ct)),
        out_shape=jax.ShapeDtypeStruct((seq, c), F32),
        scratch_shapes=[pltpu.VMEM((seq, lanes), F32), pltpu.VMEM((seq, lanes), F32),
                        pltpu.VMEM((s2 // 2, 2, 2 * FFT_B, FFT_B), BF16)],
        compiler_params=_cparams(("arbitrary", "arbitrary")),
        name="hyena_fft_conv",
    )(u, bias, kr, ki, tb["first_c"], tb["first_s"], tb["lvl_c"], tb["lvl_s"],
      tb["col_c"], tb["col_s"], jnp.asarray(tb["lhs_f"], BF16), jnp.asarray(tb["lhs_i"], BF16))


MIXOUT_CHUNK = 256


def _mixout_kernel(h_ref, yr_ref, hy_ref, x0_ref, og_ref, w_ref, x_ref, mod_ref, o_ref, *, stream, lw, tm):
    gate = mod_ref[3 * stream + 2:3 * stream + 3, :]
    w = w_ref[...]
    for r in range(0, tm, MIXOUT_CHUNK):
        rs = slice(r, r + min(MIXOUT_CHUNK, tm))
        yr = yr_ref[rs, :]
        gelu = 0.5 * yr * (1.0 + jnp.tanh(math.sqrt(2.0 / math.pi) * (yr + 0.044715 * (yr * yr * yr))))
        lru = (_rms(h_ref[rs, :] * gelu) * og_ref[:, :lw]).astype(BF16)
        hy = (_rms(hy_ref[rs, :] * x0_ref[rs, :]) * og_ref[:, lw:]).astype(BF16)
        y = (jnp.dot(lru, w[:lw, :], preferred_element_type=F32)
             + jnp.dot(hy, w[lw:, :], preferred_element_type=F32))
        o_ref[rs, :] = x_ref[rs, :] + gate * y


def _mixout(h_lru, p, y_hy, x0c, og, w_out_bf16, x, mod, stream):
    rows, d = x.shape
    lw = h_lru.shape[1]
    dm = w_out_bf16.shape[0]
    tm = min(rows, 512)
    row_blk = lambda cols, j: pl.BlockSpec((tm, cols), lambda i: (i, j))
    return pl.pallas_call(
        functools.partial(_mixout_kernel, stream=stream, lw=lw, tm=tm),
        grid=(rows // tm,),
        in_specs=[row_blk(lw, 0), row_blk(lw, 1), row_blk(dm - lw, 0), row_blk(dm - lw, 0),
                  pl.BlockSpec((1, dm), lambda i: (0, 0)),
                  pl.BlockSpec((dm, d), lambda i: (0, 0)),
                  row_blk(d, 0),
                  pl.BlockSpec((SUBLANES, d), lambda i: (0, 0))],
        out_specs=row_blk(d, 0),
        out_shape=jax.ShapeDtypeStruct((rows, d), F32),
        compiler_params=_cparams(("arbitrary",)),
        name="mixer_out_proj",
    )(h_lru, p, y_hy, x0c, og, w_out_bf16, x, mod)


def _final_kernel(x_ref, g_ref, o_ref, *, pieces, prow, d):
    for q in range(pieces):
        o_ref[q * prow:(q + 1) * prow, :] = _rms(x_ref[:, q * d:(q + 1) * d]) * g_ref[...]


def _final_norm(x, g, from_col):
    rows, d = x.shape
    tm = 1024
    if from_col:
        prow, pieces = GRID_W, tm // GRID_W
        xv = x.reshape(prow, (rows // prow) * d)
        x_spec = pl.BlockSpec((prow, pieces * d), lambda i: (0, i))
    else:
        prow, pieces, xv = tm, 1, x
        x_spec = pl.BlockSpec((tm, d), lambda i: (i, 0))
    return pl.pallas_call(
        functools.partial(_final_kernel, pieces=pieces, prow=prow, d=d),
        grid=(rows // tm,),
        in_specs=[x_spec, pl.BlockSpec((1, d), lambda i: (0, 0))],
        out_specs=pl.BlockSpec((tm, d), lambda i: (i, 0)),
        out_shape=jax.ShapeDtypeStruct((rows, d), F32),
        compiler_params=_cparams(("arbitrary",)),
        name="final_norm",
    )(xv, g)


def kernel(x, c, ctx, c_ctx, ada_w, ada_b, norm_g, ffn_wg, ffn_wu, ffn_wd, w_in, w_out, out_g, lru_conv_w, lru_conv_b, lru_wa, lru_ba, lru_wx, lru_bx, lru_lam, hy_conv_w, hy_conv_b, hy_bias, filt_w1, filt_b1, filt_w2, filt_b2, filt_w3, filt_b3, filt_w4, filt_b4, filt_freq, final_g):
    assert x.shape[0] == 1 and ctx.shape[0] == 1
    depth = ada_w.shape[0]
    d = x.shape[-1]
    lw = lru_conv_w.shape[-1]
    hw = hy_bias.shape[-1]
    n_lat, n_ctx = x.shape[1], ctx.shape[1]
    xl, xc = x[0], ctx[0]
    mods = _ada_mods(c, c_ctx, ada_w, ada_b)
    zero_state = jnp.zeros((1, lw), F32)
    lat_is_col = False
    for l in range(depth):
        last = l == depth - 1
        m = [_sublayer_mods(mods[l], d, k) for k in range(3)]
        g = norm_g[l][:, None, :]

        xl = _ffn(xl, m[0], g[0], ffn_wg, ffn_wu, ffn_wd, l, 0, 0)
        xc = _ffn(xc, m[0], g[0], ffn_wg, ffn_wu, ffn_wd, l, 0, 1)

        want_col = l % 2 == 1
        perm = None if want_col == lat_is_col else ("to_col" if want_col else "from_col")
        lat_is_col = want_col
        wi = w_in[l].astype(BF16)
        pl_, xl = _win(xl, m[1], g[1], wi, w_in.shape[-1], 0, perm)
        pc_, _ = _win(xc, m[1], g[1], wi, lw if last else w_in.shape[-1], 1, None)

        cw, cb = lru_conv_w[l], lru_conv_b[l][None, :]
        h_lat = h_ctx = None
        for dr in range(2):
            wax = jnp.concatenate([lru_wa[l, dr], lru_wx[l, dr]], axis=-1).astype(BF16)
            ba, bx, lam = lru_ba[l, dr][None, :], lru_bx[l, dr][None, :], lru_lam[l, dr][None, :]
            h_ctx, state = _lru_scan(pc_, cw, cb, wax, ba, bx, lam, zero_state, h_ctx, dr == 1)
            h_lat, _ = _lru_scan(pl_, cw, cb, wax, ba, bx, lam, state, h_lat, dr == 1)

        filt = (filt_w1[l], filt_b1[l], filt_w2[l], filt_b2[l], filt_w3[l], filt_b3[l], filt_w4[l],
                filt_b4[l], filt_freq[l])
        hcw, hcb, hbias = hy_conv_w[l], hy_conv_b[l][None, :], hy_bias[l][None, :]
        og = out_g[l][None, :]
        wo = w_out[l].astype(BF16)

        u_l, x0_l = _hypre(pl_, 2 * lw, hcw, hcb)
        kr, ki = _hyena_spectrum(*_hyena_filter(n_lat, *filt))
        y_hy = _hyena_conv(u_l, hbias, kr, ki)
        xl = _mixout(h_lat, pl_, y_hy, x0_l, og, wo, xl, m[1], 0)
        xl = _ffn(xl, m[2], g[2], ffn_wg, ffn_wu, ffn_wd, l, 1, 0)

        if not last:
            u_c, x0_c = _hypre(pc_, 2 * lw, hcw, hcb)
            kr, ki = _hyena_spectrum(*_hyena_filter(n_ctx, *filt))
            y_hy = _hyena_conv(u_c, hbias, kr, ki)
            xc = _mixout(h_ctx, pc_, y_hy, x0_c, og, wo, xc, m[1], 1)
            xc = _ffn(xc, m[2], g[2], ffn_wg, ffn_wu, ffn_wd, l, 1, 1)

    return _final_norm(xl, final_g[None, :], lat_is_col)[None]
```

```python
import functools
import math

import numpy as np
import jax
import jax.numpy as jnp
from jax import lax
from jax.experimental import pallas as pl
from jax.experimental.pallas import tpu as pltpu

F32 = jnp.float32
BF16 = jnp.bfloat16
EPS = 1e-6
LRU_C = 8.0
GRID_W = 64
HY_BANDS = 16
HY_MAX_DECAY = math.log(1e-2) / 0.3
HY_MIN_DECAY = math.log(1e-2) / 1.5

V7X_VMEM_LIMIT_BYTES = 58 * 1024 * 1024
FFT_B = 128
SUBLANES = 8


def _cparams(sem):
    return pltpu.CompilerParams(dimension_semantics=sem, vmem_limit_bytes=V7X_VMEM_LIMIT_BYTES)


def _rms(x):
    return x * lax.rsqrt(jnp.mean(x * x, axis=-1, keepdims=True) + EPS)


def _modulated(x, mod_ref, g_ref, stream):
    shift = mod_ref[3 * stream:3 * stream + 1, :]
    scale = mod_ref[3 * stream + 1:3 * stream + 2, :]
    return _rms(x) * g_ref[...] * (1.0 + scale) + shift


def _ada_kernel(c_ref, w_ref, b_ref, o_ref):
    c = c_ref[...]
    s = c * jax.nn.sigmoid(c)
    o_ref[...] = jnp.dot(s, w_ref[...], precision=lax.Precision.HIGHEST,
                         preferred_element_type=F32) + b_ref[...]


def _ada_mods(c, c_ctx, ada_w, ada_b):
    depth, d, nm = ada_w.shape
    tn = d
    cc = jnp.zeros((SUBLANES, d), F32).at[0].set(c[0]).at[1].set(c_ctx)
    out = pl.pallas_call(
        _ada_kernel,
        grid=(depth, nm // tn),
        in_specs=[pl.BlockSpec((SUBLANES, d), lambda l, j: (0, 0)),
                  pl.BlockSpec((None, d, tn), lambda l, j: (l, 0, j)),
                  pl.BlockSpec((None, 1, tn), lambda l, j: (l, 0, j))],
        out_specs=pl.BlockSpec((None, SUBLANES, tn), lambda l, j: (l, 0, j)),
        out_shape=jax.ShapeDtypeStruct((depth, SUBLANES, nm), F32),
        compiler_params=_cparams(("arbitrary", "arbitrary")),
        name="ada_mods",
    )(cc, ada_w, ada_b[:, None, :])
    return out


def _sublayer_mods(mods_l, d, k):
    lat = mods_l[0, 3 * k * d:(3 * k + 3) * d].reshape(3, d)
    ctx = mods_l[1, 3 * k * d:(3 * k + 3) * d].reshape(3, d)
    return jnp.concatenate([lat, ctx, jnp.zeros((2, d), F32)], axis=0)


NORM_CHUNK = 256


W_SPLIT = 4


def _split_specs(shape, index_map):
    *lead, rows, cols = shape
    return [pl.BlockSpec((*lead, rows // W_SPLIT, cols), functools.partial(index_map, part=k))
            for k in range(W_SPLIT)]


def _joined_bf16(parts):
    return jnp.concatenate([p[...].astype(BF16) for p in parts], axis=0)


def _ffn_up_kernel(x_ref, mod_ref, g_ref, *rest, stream, tm):
    wg = _joined_bf16(rest[:W_SPLIT])
    wu = _joined_bf16(rest[W_SPLIT:2 * W_SPLIT])
    h_ref, u_scr = rest[2 * W_SPLIT:]

    def swiglu_rows(u):
        g = jnp.dot(u, wg, preferred_element_type=F32)
        up = jnp.dot(u, wu, preferred_element_type=F32)
        return (g * jax.nn.sigmoid(g) * up).astype(BF16)

    @pl.when(pl.program_id(1) == 0)
    def _():
        for r in range(0, tm, NORM_CHUNK):
            rs = slice(r, min(r + NORM_CHUNK, tm))
            u = _modulated(x_ref[rs, :], mod_ref, g_ref, stream).astype(BF16)
            u_scr[rs, :] = u
            h_ref[rs, :] = swiglu_rows(u)

    @pl.when(pl.program_id(1) > 0)
    def _():
        h_ref[...] = swiglu_rows(u_scr[...])


def _ffn_down_kernel(h_ref, *rest, stream):
    x_ref, mod_ref, o_ref = rest[W_SPLIT:]
    acc = jnp.dot(h_ref[...], _joined_bf16(rest[:W_SPLIT]), preferred_element_type=F32)
    gate = mod_ref[3 * stream + 2:3 * stream + 3, :]
    o_ref[...] = x_ref[...] + (0.5 * gate) * acc


def _ffn(x, mod, g, wg, wu, wd, l, j, stream):
    rows, d = x.shape
    dff = wg.shape[-1]
    tm = min(rows, 1024)
    tf = 512
    tn = 256
    w_tile = lambda i, c, part: (l, j, part, c)
    h = pl.pallas_call(
        functools.partial(_ffn_up_kernel, stream=stream, tm=tm),
        grid=(rows // tm, dff // tf),
        in_specs=[pl.BlockSpec((tm, d), lambda i, f: (i, 0)),
                  pl.BlockSpec((SUBLANES, d), lambda i, f: (0, 0)),
                  pl.BlockSpec((1, d), lambda i, f: (0, 0))]
                 + _split_specs((None, None, d, tf), w_tile) + _split_specs((None, None, d, tf), w_tile),
        out_specs=pl.BlockSpec((tm, tf), lambda i, f: (i, f)),
        out_shape=jax.ShapeDtypeStruct((rows, dff), BF16),
        scratch_shapes=[pltpu.VMEM((tm, d), BF16)],
        compiler_params=_cparams(("arbitrary", "arbitrary")),
        name="ffn_up",
    )(x, mod, g, *([wg] * W_SPLIT), *([wu] * W_SPLIT))
    return pl.pallas_call(
        functools.partial(_ffn_down_kernel, stream=stream),
        grid=(rows // tm, d // tn),
        in_specs=[pl.BlockSpec((tm, dff), lambda i, n: (i, 0))]
                 + _split_specs((None, None, dff, tn), w_tile)
                 + [pl.BlockSpec((tm, tn), lambda i, n: (i, n)),
                    pl.BlockSpec((SUBLANES, tn), lambda i, n: (0, n))],
        out_specs=pl.BlockSpec((tm, tn), lambda i, n: (i, n)),
        out_shape=jax.ShapeDtypeStruct((rows, d), F32),
        compiler_params=_cparams(("arbitrary", "arbitrary")),
        name="ffn_down",
    )(h, *([wd] * W_SPLIT), x, mod)


def _win_kernel(x_ref, mod_ref, g_ref, w_ref, p_ref, *rest, pieces, prow, d, stream, emit_x):
    if emit_x:
        xo_ref, u_scr = rest
    else:
        (u_scr,) = rest

    w = w_ref[...]

    @pl.when(pl.program_id(1) == 0)
    def _():
        tm = pieces * prow
        seg = min(prow, NORM_CHUNK)
        for c0 in range(0, tm, NORM_CHUNK):
            c1 = min(c0 + NORM_CHUNK, tm)
            for r0 in range(c0, c1, seg):
                q, r = divmod(r0, prow)
                xt = x_ref[r:r + seg, q * d:(q + 1) * d]
                u_scr[r0:r0 + seg, :] = _modulated(xt, mod_ref, g_ref, stream).astype(BF16)
                if emit_x:
                    xo_ref[r0:r0 + seg, :] = xt
            p_ref[c0:c1, :] = jnp.dot(u_scr[c0:c1, :], w, preferred_element_type=F32)

    @pl.when(pl.program_id(1) > 0)
    def _():
        p_ref[...] = jnp.dot(u_scr[...], w, preferred_element_type=F32)


def _win(x, mod, g, w_in_bf16, ncols, stream, perm):
    rows, d = x.shape
    tm = min(rows, 1024)
    tn = 1024 if perm is None else 512
    if perm is None:
        prow, pieces, xv = tm, 1, x
        x_spec = pl.BlockSpec((tm, d), lambda i, n: (i, 0))
    else:
        grid_h = rows // GRID_W
        prow = grid_h if perm == "to_col" else GRID_W
        pieces = tm // prow
        xv = x.reshape(prow, (rows // prow) * d)
        x_spec = pl.BlockSpec((prow, pieces * d), lambda i, n: (0, i))
    emit_x = perm is not None
    out_shape = [jax.ShapeDtypeStruct((rows, ncols), F32)]
    out_specs = [pl.BlockSpec((tm, tn), lambda i, n: (i, n))]
    if emit_x:
        out_shape.append(jax.ShapeDtypeStruct((rows, d), F32))
        out_specs.append(pl.BlockSpec((tm, d), lambda i, n: (i, 0)))
    res = pl.pallas_call(
        functools.partial(_win_kernel, pieces=pieces, prow=prow, d=d, stream=stream, emit_x=emit_x),
        grid=(rows // tm, ncols // tn),
        in_specs=[x_spec,
                  pl.BlockSpec((SUBLANES, d), lambda i, n: (0, 0)),
                  pl.BlockSpec((1, d), lambda i, n: (0, 0)),
                  pl.BlockSpec((d, tn), lambda i, n: (0, n))],
        out_specs=out_specs,
        out_shape=out_shape,
        scratch_shapes=[pltpu.VMEM((tm, d), BF16)],
        compiler_params=_cparams(("arbitrary", "arbitrary")),
        name="mixer_in_proj",
    )(xv, mod, g, w_in_bf16)
    return (res[0], res[1]) if emit_x else (res[0], x)


def _lru_kernel(x_ref, xp_ref, xn_ref, cw_ref, cb_ref, wax_ref, ba_ref, bx_ref, lam_ref, h0_ref, *rest,
                t_blk, nblk, heads, reverse, add_prev):
    if add_prev:
        prev_ref, o_ref, hf_ref, xs, a_scr, b_scr, carry = rest
    else:
        o_ref, hf_ref, xs, a_scr, b_scr, carry = rest
    j = pl.program_id(0)
    blk = (nblk - 1 - j) if reverse else j
    c = x_ref.shape[1]
    hd = c // heads

    @pl.when(j == 0)
    def _():
        carry[...] = h0_ref[...]

    zero8 = jnp.zeros((SUBLANES, c), F32)
    xs[0:SUBLANES, :] = jnp.where(blk == 0, zero8, xp_ref[...])
    xs[SUBLANES:SUBLANES + t_blk, :] = x_ref[...]
    xs[SUBLANES + t_blk:2 * SUBLANES + t_blk, :] = jnp.where(blk == nblk - 1, zero8, xn_ref[...])
    cv = cb_ref[...]
    for k in range(cw_ref.shape[0]):
        cv = cv + xs[SUBLANES - 1 + k:SUBLANES - 1 + k + t_blk, :] * cw_ref[k:k + 1, :]

    z = -lam_ref[...]
    c_sp = LRU_C * (jnp.maximum(z, 0.0) + jnp.log(1.0 + jnp.exp(-jnp.abs(z))))
    cvb = cv.astype(BF16)
    for h in range(heads):
        sl = slice(h * hd, (h + 1) * hd)
        gx = jnp.dot(cvb[:, sl], wax_ref[h], preferred_element_type=F32)
        r = 0.5 + 0.5 * jnp.tanh(0.5 * (gx[:, :hd] + ba_ref[:, sl]))
        ig = 0.5 + 0.5 * jnp.tanh(0.5 * (gx[:, hd:] + bx_ref[:, sl]))
        a = jnp.exp(-c_sp[:, sl] * r)
        a_scr[:, sl] = a
        b_scr[:, sl] = jnp.sqrt(1.0 - a * a) * (ig * cv[:, sl])

    row = lax.broadcasted_iota(jnp.int32, (SUBLANES, c), 0)
    ngrp = t_blk // SUBLANES

    def group(gi, hc):
        g = (ngrp - 1 - gi) if reverse else gi
        r0 = pl.multiple_of(g * SUBLANES, SUBLANES)
        av = a_scr[pl.ds(r0, SUBLANES), :]
        bv = b_scr[pl.ds(r0, SUBLANES), :]
        for s in (1, 2, 4):
            sh = (SUBLANES - s) if reverse else s
            m = (row < SUBLANES - s) if reverse else (row >= s)
            a_s = pltpu.roll(av, sh, axis=0)
            b_s = pltpu.roll(bv, sh, axis=0)
            bv = jnp.where(m, av * b_s + bv, bv)
            av = jnp.where(m, av * a_s, av)
        hv = av * hc + bv
        if add_prev:
            o_ref[pl.ds(r0, SUBLANES), :] = hv + prev_ref[pl.ds(r0, SUBLANES), :]
        else:
            o_ref[pl.ds(r0, SUBLANES), :] = hv
        return hv[0:1, :] if reverse else hv[SUBLANES - 1:SUBLANES, :]

    hc = lax.fori_loop(0, ngrp, group, carry[...])
    carry[...] = hc
    hf_ref[...] = hc


def _lru_scan(p, cw, cb, wax, ba, bx, lam, h0, prev, reverse):
    seq = p.shape[0]
    c = cw.shape[1]
    heads = wax.shape[0]
    t_blk = 256
    nblk = seq // t_blk
    per = t_blk // SUBLANES
    nb8 = seq // SUBLANES

    def bi(j):
        return (nblk - 1 - j) if reverse else j

    in_specs = [pl.BlockSpec((t_blk, c), lambda j: (bi(j), 0)),
                pl.BlockSpec((SUBLANES, c), lambda j: (jnp.maximum(bi(j) * per - 1, 0), 0)),
                pl.BlockSpec((SUBLANES, c), lambda j: (jnp.minimum((bi(j) + 1) * per, nb8 - 1), 0)),
                pl.BlockSpec(cw.shape, lambda j: (0, 0)),
                pl.BlockSpec((1, c), lambda j: (0, 0)),
                pl.BlockSpec(wax.shape, lambda j: (0, 0, 0)),
                pl.BlockSpec((1, c), lambda j: (0, 0)),
                pl.BlockSpec((1, c), lambda j: (0, 0)),
                pl.BlockSpec((1, c), lambda j: (0, 0)),
                pl.BlockSpec((1, c), lambda j: (0, 0))]
    args = [p, p, p, cw, cb, wax, ba, bx, lam, h0]
    if prev is not None:
        in_specs.append(pl.BlockSpec((t_blk, c), lambda j: (bi(j), 0)))
        args.append(prev)
    return pl.pallas_call(
        functools.partial(_lru_kernel, t_blk=t_blk, nblk=nblk, heads=heads, reverse=reverse,
                          add_prev=prev is not None),
        grid=(nblk,),
        in_specs=in_specs,
        out_specs=[pl.BlockSpec((t_blk, c), lambda j: (bi(j), 0)),
                   pl.BlockSpec((1, c), lambda j: (0, 0))],
        out_shape=[jax.ShapeDtypeStruct((seq, c), F32), jax.ShapeDtypeStruct((1, c), F32)],
        scratch_shapes=[pltpu.VMEM((t_blk + 2 * SUBLANES, c), F32),
                        pltpu.VMEM((t_blk, c), F32),
                        pltpu.VMEM((t_blk, c), F32),
                        pltpu.VMEM((1, c), F32)],
        compiler_params=_cparams(("arbitrary",)),
        name="rglru_scan",
    )(*args)


def _hypre_kernel(x0_ref, x0p, x0n, x1_ref, x1p, x1n, v_ref, vp, vn, cw_ref, cb_ref, u_ref, x0c_ref, xs,
                  *, t_blk, nblk, c):
    blk = pl.program_id(0)
    zero8 = jnp.zeros((SUBLANES, c), F32)

    def conv3(cur, prv, nxt, part):
        xs[0:SUBLANES, :] = jnp.where(blk == 0, zero8, prv[...])
        xs[SUBLANES:SUBLANES + t_blk, :] = cur[...]
        xs[SUBLANES + t_blk:2 * SUBLANES + t_blk, :] = jnp.where(blk == nblk - 1, zero8, nxt[...])
        sl = slice(part * c, (part + 1) * c)
        out = cb_ref[:, sl]
        for k in range(cw_ref.shape[0]):
            out = out + xs[SUBLANES - 1 + k:SUBLANES - 1 + k + t_blk, :] * cw_ref[k:k + 1, sl]
        return out

    x0c_ref[...] = conv3(x0_ref, x0p, x0n, 0)
    x1c = conv3(x1_ref, x1p, x1n, 1)
    u_ref[...] = conv3(v_ref, vp, vn, 2) * x1c


def _hypre(p, col0, cw, cb):
    seq = p.shape[0]
    c = cw.shape[1] // 3
    t_blk = min(seq, 512)
    nblk = seq // t_blk
    per = t_blk // SUBLANES
    nb8 = seq // SUBLANES
    cb0 = col0 // c
    in_specs, args = [], []
    for part in range(3):
        in_specs += [pl.BlockSpec((t_blk, c), lambda j, part=part: (j, cb0 + part)),
                     pl.BlockSpec((SUBLANES, c), lambda j, part=part: (jnp.maximum(j * per - 1, 0), cb0 + part)),
                     pl.BlockSpec((SUBLANES, c),
                                  lambda j, part=part: (jnp.minimum((j + 1) * per, nb8 - 1), cb0 + part))]
        args += [p, p, p]
    in_specs += [pl.BlockSpec(cw.shape, lambda j: (0, 0)), pl.BlockSpec((1, 3 * c), lambda j: (0, 0))]
    args += [cw, cb]
    return pl.pallas_call(
        functools.partial(_hypre_kernel, t_blk=t_blk, nblk=nblk, c=c),
        grid=(nblk,),
        in_specs=in_specs,
        out_specs=[pl.BlockSpec((t_blk, c), lambda j: (j, 0)), pl.BlockSpec((t_blk, c), lambda j: (j, 0))],
        out_shape=[jax.ShapeDtypeStruct((seq, c), F32), jax.ShapeDtypeStruct((seq, c), F32)],
        scratch_shapes=[pltpu.VMEM((t_blk + 2 * SUBLANES, c), F32)],
        compiler_params=_cparams(("arbitrary",)),
        name="hyena_short_conv",
    )(*args)


FILT_SLOT = 64


def _filt_kernel(w1_ref, b1_ref, w2_ref, b2_ref, w3_ref, b3_ref, w4a_ref, w4b_ref, b4_ref, fr_ref,
                 k_ref, nrm_ref, *, seq, t_blk, c):
    i = pl.program_id(0)
    hi = lax.Precision.HIGHEST
    lanes = 2 * FILT_SLOT
    th = t_blk // 2
    lane = lax.broadcasted_iota(jnp.int32, (1, lanes), 1)
    slot = lane & (FILT_SLOT - 1)
    pos_a = i * t_blk + lax.broadcasted_iota(jnp.int32, (th, 1), 0)
    pos = jnp.where(lane < FILT_SLOT, pos_a, pos_a + th)
    tf = jnp.where(pos < seq, pos, 2 * seq - pos).astype(F32)
    t = tf * (1.0 / (seq - 1))
    w = (2.0 * math.pi / seq) * tf
    is_cos = (slot >= 1) & (slot <= HY_BANDS)
    is_sin = (slot > HY_BANDS) & (slot <= 2 * HY_BANDS)
    band = jnp.where(is_cos, slot - 1, slot - 1 - HY_BANDS).astype(F32)
    f = 1e-4 + band * ((HY_BANDS - 1 - 1e-4) / (HY_BANDS - 1))
    phase = jnp.where(is_cos, 0.5 * math.pi, 0.0)
    sign = jnp.where(is_cos, 1.0, jnp.where(is_sin, -1.0, 0.0))
    z = jnp.where(slot == 0, t, sign * jnp.sin(w * f + phase))
    fr = fr_ref[...]
    h = jnp.sin(fr * (jnp.dot(z, w1_ref[...], precision=hi, preferred_element_type=F32) + b1_ref[...]))
    h = jnp.sin(fr * (jnp.dot(h, w2_ref[...], precision=hi, preferred_element_type=F32) + b2_ref[...]))
    h = jnp.sin(fr * (jnp.dot(h, w3_ref[...], precision=hi, preferred_element_type=F32) + b3_ref[...]))
    ch = lax.broadcasted_iota(jnp.int32, (1, c), 1).astype(F32)
    delta = jnp.abs(HY_MIN_DECAY + ch * ((HY_MAX_DECAY - HY_MIN_DECAY) / (c - 1)))
    part = jnp.zeros((1, c), F32)
    for half, w4_ref in enumerate((w4a_ref, w4b_ref)):
        pos_h = pos_a + half * th
        t_h = jnp.where(pos_h < seq, pos_h, 2 * seq - pos_h).astype(F32) * (1.0 / (seq - 1))
        k = jnp.dot(h, w4_ref[...], precision=hi, preferred_element_type=F32) + b4_ref[...]
        k = jnp.where(pos_h != seq, k * jnp.exp(-t_h * delta), 0.0)
        k_ref[half * th:(half + 1) * th, :] = k
        part = part + jnp.sum(jnp.abs(k), axis=0, keepdims=True)

    @pl.when(i == 0)
    def _():
        nrm_ref[...] = part

    @pl.when(i > 0)
    def _():
        nrm_ref[...] = nrm_ref[...] + part


def _slots(a, row_used):
    r, cdim = a.shape
    blk = jnp.pad(a, ((0, (FILT_SLOT if row_used else 1) - r), (0, FILT_SLOT - cdim)))
    if not row_used:
        return jnp.concatenate([blk, blk], axis=1)
    z = jnp.zeros_like(blk)
    return jnp.concatenate([jnp.concatenate([blk, z], axis=1), jnp.concatenate([z, blk], axis=1)], axis=0)


def _hyena_filter(seq, w1, b1, w2, b2, w3, b3, w4, b4, freq):
    c = w4.shape[1] // 2
    od = w2.shape[0]
    assert od <= FILT_SLOT and w1.shape[0] <= FILT_SLOT
    lanes = 2 * FILT_SLOT
    t_blk = min(seq, 1024)
    half_blocks = seq // t_blk
    full = lambda shape: pl.BlockSpec(shape, lambda i: (0,) * len(shape))
    sq, row = (lanes, lanes), (1, lanes)
    w4p = jnp.pad(w4, ((0, FILT_SLOT - od), (0, 0)))
    w4a = jnp.concatenate([w4p, jnp.zeros_like(w4p)], axis=0)
    w4b = jnp.concatenate([jnp.zeros_like(w4p), w4p], axis=0)
    w4_spec = pl.BlockSpec((lanes, c), lambda i: (0, i // half_blocks))
    return pl.pallas_call(
        functools.partial(_filt_kernel, seq=seq, t_blk=t_blk, c=c),
        grid=(2 * seq // t_blk,),
        in_specs=[full(sq), full(row), full(sq), full(row), full(sq), full(row), w4_spec, w4_spec,
                  pl.BlockSpec((1, c), lambda i: (0, i // half_blocks)),
                  full(row)],
        out_specs=[pl.BlockSpec((t_blk, c), lambda i: (i, 0)), pl.BlockSpec((1, c), lambda i: (0, 0))],
        out_shape=[jax.ShapeDtypeStruct((2 * seq, c), F32), jax.ShapeDtypeStruct((1, c), F32)],
        compiler_params=_cparams(("arbitrary",)),
        name="hyena_filter",
    )(_slots(w1, True), _slots(b1[None, :], False), _slots(w2, True), _slots(b2[None, :], False),
      _slots(w3, True), _slots(b3[None, :], False), w4a, w4b, b4[None, :], _slots(freq[None, :], False))


def _bitrev(p, bits):
    r = 0
    for k in range(bits):
        r = (r << 1) | ((p >> k) & 1)
    return r


def _real_block(z):
    return np.block([[z.real, -z.imag], [z.imag, z.real]])


def _fft_split(s2):
    n_lo = max(min(8, s2 // 2), 1)
    return s2 // n_lo, n_lo


@functools.lru_cache(maxsize=None)
def _fft_tables(s):
    s2 = s // 2
    n = s * FFT_B
    n_hi, n_lo = _fft_split(s2)
    bits_hi, bits_lo = n_hi.bit_length() - 1, n_lo.bit_length() - 1
    b = np.arange(FFT_B)
    dft = np.exp(-2j * np.pi * np.outer(b, b) / FFT_B)
    e_lo = [2 * n_hi * _bitrev(p, bits_lo) for p in range(n_lo)]
    g = [dft * np.exp(-2j * np.pi * b * e / n)[None, :] for e in e_lo]
    lhs_f = np.stack([_real_block(x) for x in g])
    lhs_i = np.stack([_real_block(np.conj(x).T) for x in g])
    e_hi = np.array([[2 * _bitrev(p, bits_hi) + hf for p in range(n_hi)] for hf in range(2)])
    ang = 2.0 * np.pi * e_hi[:, :, None, None] * b[None, None, :, None] / n * np.ones((1, 1, 1, FFT_B))
    lvl = 2.0 * np.pi * np.arange(max(s2 // 2, 1)) / s2
    first = np.concatenate([np.zeros(s2), 2.0 * np.pi * np.arange(s2) / s])
    f32 = lambda x: np.asarray(x, np.float32)
    return dict(lhs_f=f32(lhs_f), lhs_i=f32(lhs_i), col_c=f32(np.cos(ang)), col_s=f32(np.sin(ang)),
                lvl_c=f32(np.cos(lvl)), lvl_s=f32(np.sin(lvl)), first_c=f32(np.cos(first)),
                first_s=f32(np.sin(first)))


FFT_ROWS = 32


def _cmul(xr, xi, c, s):
    return xr * c + xi * s, xi * c - xr * s


def _cmul_conj(xr, xi, c, s):
    return xr * c - xi * s, xi * c + xr * s


def _pass_plan(s2):
    levels = s2.bit_length() - 1
    plan, h = [], s2 // 2
    if levels % 2 == 1:
        plan.append((h, 2))
        h //= 2
    while h >= 1:
        plan.append((h, 4))
        h //= 4
    return plan


def _butterfly_pass(load, store, lc_ref, ls_ref, s2, h, radix, inverse):
    h2 = h // 2 if radix == 4 else h
    trivial = h2 == 1
    shift = h2.bit_length() - 1

    def group(idx, carry):
        g = idx >> shift
        i = idx & (h2 - 1)
        base = g * 2 * h + i
        blocks = [base + k * h2 for k in range(radix)]
        if not trivial:
            c1, s1 = lc_ref[i * (s2 // (2 * h))], ls_ref[i * (s2 // (2 * h))]
            if radix == 4:
                c2, s2_ = lc_ref[i * (s2 // h)], ls_ref[i * (s2 // h)]
        fwd_mul = (lambda xr, xi, c, s: (xr, xi)) if trivial else _cmul
        inv_mul = (lambda xr, xi, c, s: (xr, xi)) if trivial else _cmul_conj
        if trivial:
            c1 = s1 = c2 = s2_ = None
        for r in range(0, FFT_B, FFT_ROWS):
            x = [load(bk, r) for bk in blocks]
            if radix == 2:
                (ar, ai), (br, bi) = x
                if inverse:
                    br, bi = inv_mul(br, bi, c1, s1)
                    out = [(ar + br, ai + bi), (ar - br, ai - bi)]
                else:
                    out = [(ar + br, ai + bi), fwd_mul(ar - br, ai - bi, c1, s1)]
            elif inverse:
                (x0r, x0i), (x1r, x1i), (x2r, x2i), (x3r, x3i) = x
                x1r, x1i = inv_mul(x1r, x1i, c2, s2_)
                x3r, x3i = inv_mul(x3r, x3i, c2, s2_)
                t0r, t0i, t1r, t1i = x0r + x1r, x0i + x1i, x0r - x1r, x0i - x1i
                t2r, t2i = inv_mul(x2r + x3r, x2i + x3i, c1, s1)
                t3r, t3i = inv_mul(x2r - x3r, x2i - x3i, c1, s1)
                out = [(t0r + t2r, t0i + t2i), (t1r - t3i, t1i + t3r),
                       (t0r - t2r, t0i - t2i), (t1r + t3i, t1i - t3r)]
            else:
                (x0r, x0i), (x1r, x1i), (x2r, x2i), (x3r, x3i) = x
                t0r, t0i, t1r, t1i = x0r + x2r, x0i + x2i, x1r + x3r, x1i + x3i
                t2r, t2i = fwd_mul(x0r - x2r, x0i - x2i, c1, s1)
                t3r, t3i = fwd_mul(x1r - x3r, x1i - x3i, c1, s1)
                out = [(t0r + t1r, t0i + t1i), fwd_mul(t0r - t1r, t0i - t1i, c2, s2_),
                       (t2r + t3i, t2i - t3r), fwd_mul(t2r - t3i, t2i + t3r, c2, s2_)]
            for bk, (o_r, o_i) in zip(blocks, out):
                store(bk, r, o_r, o_i)
        return carry

    lax.fori_loop(0, s2 // radix, group, 0)


def _work_access(wr, wi):
    def rows(bk, r):
        return pl.ds(pl.multiple_of(bk * FFT_B + r, FFT_ROWS), FFT_ROWS)

    def load(bk, r):
        return wr[rows(bk, r), :], wi[rows(bk, r), :]

    def store(bk, r, re, im):
        wr[rows(bk, r), :] = re
        wi[rows(bk, r), :] = im

    return load, store


def _run_passes(plan, first_load, last_store, work, lc_ref, ls_ref, s2, inverse):
    w_load, w_store = work
    for k, (h, radix) in enumerate(plan):
        load = first_load if k == 0 else w_load
        store = last_store if k == len(plan) - 1 else w_store
        _butterfly_pass(load, store, lc_ref, ls_ref, s2, h, radix, inverse)


MID_UNROLL = 4


def _pair_blocks(pp, s2):
    n_hi, n_lo = _fft_split(s2)
    half = n_hi // 2
    p_lo = pp >> (half.bit_length() - 1)
    j = pp & (half - 1)
    return p_lo, [(2 * j + q, pl.multiple_of(((2 * j + q) * n_lo + p_lo) * FFT_B, FFT_B)) for q in range(2)]


def _for_pairs(s2, body):
    def step(pp, carry):
        body(pp)
        return carry
    lax.fori_loop(0, s2 // 2, step, 0, unroll=min(MID_UNROLL, s2 // 2))


def _pack_store(panel, cc_ref, cs_ref, s2):
    n_hi, n_lo = _fft_split(s2)
    lo_bits = n_lo.bit_length() - 1

    def store(bk, r, re, im):
        p_hi = bk >> lo_bits
        p_lo = bk & (n_lo - 1)
        pp = p_lo * (n_hi // 2) + (p_hi >> 1)
        q = p_hi & 1
        tr, ti = _cmul(re, im, cc_ref[p_hi, r:r + FFT_ROWS, :], cs_ref[p_hi, r:r + FFT_ROWS, :])
        panel[pp, q, r:r + FFT_ROWS, :] = tr.astype(BF16)
        panel[pp, q, FFT_B + r:FFT_B + r + FFT_ROWS, :] = ti.astype(BF16)

    return store


def _dft_pair(m_ref, panel, pp, p_lo):
    blk = FFT_B
    rhs = jnp.concatenate([panel[pp, 0], panel[pp, 1]], axis=1)
    out = jnp.dot(m_ref[p_lo], rhs, preferred_element_type=F32)
    for q in range(2):
        for r in range(0, blk, FFT_ROWS):
            yield (q, r, out[r:r + FFT_ROWS, q * blk:(q + 1) * blk],
                   out[blk + r:blk + r + FFT_ROWS, q * blk:(q + 1) * blk])


def _spec_kernel(lo_ref, hi_ref, nrm_ref, fc_ref, fs_ref, lc_ref, ls_ref, cc_ref, cs_ref, mf_ref,
                 kr_ref, ki_ref, panel, *, s2):
    blk = FFT_B
    hf = pl.program_id(1)
    n = 2 * s2 * blk
    sign = (1 - 2 * hf).astype(F32)

    def first_load(bk, r):
        rows = pl.ds(pl.multiple_of(bk * blk + r, FFT_ROWS), FFT_ROWS)
        d = lo_ref[rows, :] + sign * hi_ref[rows, :]
        return d * fc_ref[hf * s2 + bk], -d * fs_ref[hf * s2 + bk]

    _run_passes(_pass_plan(s2), first_load, _pack_store(panel, cc_ref, cs_ref, s2),
                _work_access(kr_ref, ki_ref), lc_ref, ls_ref, s2, inverse=False)
    scale = 1.0 / (nrm_ref[...] * n)

    def dft(pp):
        p_lo, blocks = _pair_blocks(pp, s2)
        for q, r, yr, yi in _dft_pair(mf_ref, panel, pp, p_lo):
            rows = pl.ds(blocks[q][1] + r, FFT_ROWS)
            kr_ref[rows, :] = yr * scale
            ki_ref[rows, :] = yi * scale

    _for_pairs(s2, dft)


def _conv_kernel(u_ref, bias_ref, kr_ref, ki_ref, fc_ref, fs_ref, lc_ref, ls_ref, cc_ref, cs_ref,
                 mf_ref, mi_ref, y_ref, wr, wi, panel, *, s2):
    blk = FFT_B
    hf = pl.program_id(1)
    plan = _pass_plan(s2)
    work = _work_access(wr, wi)

    def first_load(bk, r):
        ub = u_ref[pl.ds(pl.multiple_of(bk * blk + r, FFT_ROWS), FFT_ROWS), :]
        return ub * fc_ref[hf * s2 + bk], -ub * fs_ref[hf * s2 + bk]

    _run_passes(plan, first_load, _pack_store(panel, cc_ref, cs_ref, s2), work, lc_ref, ls_ref, s2,
                inverse=False)

    def spectrum_product(pp):
        p_lo, blocks = _pair_blocks(pp, s2)
        for q, r, yr, yi in _dft_pair(mf_ref, panel, pp, p_lo):
            rows = pl.ds(blocks[q][1] + r, FFT_ROWS)
            kr, ki = kr_ref[rows, :], ki_ref[rows, :]
            panel[pp, q, r:r + FFT_ROWS, :] = (yr * kr - yi * ki).astype(BF16)
            panel[pp, q, blk + r:blk + r + FFT_ROWS, :] = (yr * ki + yi * kr).astype(BF16)

    _for_pairs(s2, spectrum_product)

    def inverse_dft(pp):
        p_lo, blocks = _pair_blocks(pp, s2)
        for q, r, er, ei in _dft_pair(mi_ref, panel, pp, p_lo):
            p_hi, r0 = blocks[q]
            rows = pl.ds(r0 + r, FFT_ROWS)
            wr[rows, :], wi[rows, :] = _cmul_conj(er, ei, cc_ref[p_hi, r:r + FFT_ROWS, :],
                                                  cs_ref[p_hi, r:r + FFT_ROWS, :])

    _for_pairs(s2, inverse_dft)

    def rows_of(bk, r):
        return pl.ds(pl.multiple_of(bk * blk + r, FFT_ROWS), FFT_ROWS)

    def store_even(bk, r, er, ei):
        y_ref[rows_of(bk, r), :] = er

    def store_odd(bk, r, er, ei):
        rows = rows_of(bk, r)
        val = er * fc_ref[s2 + bk] - ei * fs_ref[s2 + bk]
        y_ref[rows, :] = y_ref[rows, :] + u_ref[rows, :] * bias_ref[...] + val

    inv_plan = plan[::-1]
    if len(inv_plan) > 1:
        _run_passes(inv_plan[:-1], work[0], work[1], work, lc_ref, ls_ref, s2, inverse=True)
    h_last, radix_last = inv_plan[-1]
    for half, store in ((0, store_even), (1, store_odd)):
        @pl.when(hf == half)
        def _(store=store):
            _butterfly_pass(work[0], store, lc_ref, ls_ref, s2, h_last, radix_last, True)


def _smem_spec():
    return pl.BlockSpec(memory_space=pltpu.SMEM)


def _fft_table_specs(s2):
    n_hi, n_lo = _fft_split(s2)
    col = pl.BlockSpec((None, n_hi, FFT_B, FFT_B), lambda ct, hf: (hf, 0, 0, 0))
    lhs = pl.BlockSpec((n_lo, 2 * FFT_B, 2 * FFT_B), lambda ct, hf: (0, 0, 0))
    return [_smem_spec(), _smem_spec(), _smem_spec(), _smem_spec(), col, col], lhs


def _hyena_spectrum(kfull, nrm):
    n, c = kfull.shape
    seq = n // 2
    s2 = seq // FFT_B
    tb = _fft_tables(2 * s2)
    lanes = 128
    tab_specs, lhs_spec = _fft_table_specs(s2)
    return pl.pallas_call(
        functools.partial(_spec_kernel, s2=s2),
        grid=(c // lanes, 2),
        in_specs=[pl.BlockSpec((seq, lanes), lambda ct, hf: (0, ct)),
                  pl.BlockSpec((seq, lanes), lambda ct, hf: (1, ct)),
                  pl.BlockSpec((1, lanes), lambda ct, hf: (0, ct))] + tab_specs + [lhs_spec],
        out_specs=[pl.BlockSpec((None, seq, lanes), lambda ct, hf: (hf, 0, ct)),
                   pl.BlockSpec((None, seq, lanes), lambda ct, hf: (hf, 0, ct))],
        out_shape=[jax.ShapeDtypeStruct((2, seq, c), F32), jax.ShapeDtypeStruct((2, seq, c), F32)],
        scratch_shapes=[pltpu.VMEM((s2 // 2, 2, 2 * FFT_B, FFT_B), BF16)],
        compiler_params=_cparams(("arbitrary", "arbitrary")),
        name="hyena_spectrum",
    )(kfull, kfull, nrm, tb["first_c"], tb["first_s"], tb["lvl_c"], tb["lvl_s"],
      tb["col_c"], tb["col_s"], jnp.asarray(tb["lhs_f"], BF16))


def _hyena_conv(u, bias, kr, ki):
    seq, c = u.shape
    s2 = seq // FFT_B
    tb = _fft_tables(2 * s2)
    lanes = 128
    tab_specs, lhs_spec = _fft_table_specs(s2)
    return pl.pallas_call(
        functools.partial(_conv_kernel, s2=s2),
        grid=(c // lanes, 2),
        in_specs=[pl.BlockSpec((seq, lanes), lambda ct, hf: (0, ct)),
                  pl.BlockSpec((1, lanes), lambda ct, hf: (0, ct)),
                  pl.BlockSpec((None, seq, lanes), lambda ct, hf: (hf, 0, ct)),
                  pl.BlockSpec((None, seq, lanes), lambda ct, hf: (hf, 0, ct))]
                 + tab_specs + [lhs_spec, lhs_spec],
        out_specs=pl.BlockSpec((seq, lanes), lambda ct, hf: (0, ct)),
        out_shape=jax.ShapeDtypeStruct((seq, c), F32),
        scratch_shapes=[pltpu.VMEM((seq, lanes), F32), pltpu.VMEM((seq, lanes), F32),
                        pltpu.VMEM((s2 // 2, 2, 2 * FFT_B, FFT_B), BF16)],
        compiler_params=_cparams(("arbitrary", "arbitrary")),
        name="hyena_fft_conv",
    )(u, bias, kr, ki, tb["first_c"], tb["first_s"], tb["lvl_c"], tb["lvl_s"],
      tb["col_c"], tb["col_s"], jnp.asarray(tb["lhs_f"], BF16), jnp.asarray(tb["lhs_i"], BF16))


MIXOUT_CHUNK = 256


def _mixout_kernel(h_ref, yr_ref, hy_ref, x0_ref, og_ref, w_ref, x_ref, mod_ref, o_ref, *, stream, lw, tm):
    gate = mod_ref[3 * stream + 2:3 * stream + 3, :]
    w = w_ref[...]
    for r in range(0, tm, MIXOUT_CHUNK):
        rs = slice(r, r + min(MIXOUT_CHUNK, tm))
        yr = yr_ref[rs, :]
        gelu = 0.5 * yr * (1.0 + jnp.tanh(math.sqrt(2.0 / math.pi) * (yr + 0.044715 * (yr * yr * yr))))
        lru = (_rms(h_ref[rs, :] * gelu) * og_ref[:, :lw]).astype(BF16)
        hy = (_rms(hy_ref[rs, :] * x0_ref[rs, :]) * og_ref[:, lw:]).astype(BF16)
        y = (jnp.dot(lru, w[:lw, :], preferred_element_type=F32)
             + jnp.dot(hy, w[lw:, :], preferred_element_type=F32))
        o_ref[rs, :] = x_ref[rs, :] + gate * y


def _mixout(h_lru, p, y_hy, x0c, og, w_out_bf16, x, mod, stream):
    rows, d = x.shape
    lw = h_lru.shape[1]
    dm = w_out_bf16.shape[0]
    tm = min(rows, 512)
    row_blk = lambda cols, j: pl.BlockSpec((tm, cols), lambda i: (i, j))
    return pl.pallas_call(
        functools.partial(_mixout_kernel, stream=stream, lw=lw, tm=tm),
        grid=(rows // tm,),
        in_specs=[row_blk(lw, 0), row_blk(lw, 1), row_blk(dm - lw, 0), row_blk(dm - lw, 0),
                  pl.BlockSpec((1, dm), lambda i: (0, 0)),
                  pl.BlockSpec((dm, d), lambda i: (0, 0)),
                  row_blk(d, 0),
                  pl.BlockSpec((SUBLANES, d), lambda i: (0, 0))],
        out_specs=row_blk(d, 0),
        out_shape=jax.ShapeDtypeStruct((rows, d), F32),
        compiler_params=_cparams(("arbitrary",)),
        name="mixer_out_proj",
    )(h_lru, p, y_hy, x0c, og, w_out_bf16, x, mod)


def _final_kernel(x_ref, g_ref, o_ref, *, pieces, prow, d):
    for q in range(pieces):
        o_ref[q * prow:(q + 1) * prow, :] = _rms(x_ref[:, q * d:(q + 1) * d]) * g_ref[...]


def _final_norm(x, g, from_col):
    rows, d = x.shape
    tm = 1024
    if from_col:
        prow, pieces = GRID_W, tm // GRID_W
        xv = x.reshape(prow, (rows // prow) * d)
        x_spec = pl.BlockSpec((prow, pieces * d), lambda i: (0, i))
    else:
        prow, pieces, xv = tm, 1, x
        x_spec = pl.BlockSpec((tm, d), lambda i: (i, 0))
    return pl.pallas_call(
        functools.partial(_final_kernel, pieces=pieces, prow=prow, d=d),
        grid=(rows // tm,),
        in_specs=[x_spec, pl.BlockSpec((1, d), lambda i: (0, 0))],
        out_specs=pl.BlockSpec((tm, d), lambda i: (i, 0)),
        out_shape=jax.ShapeDtypeStruct((rows, d), F32),
        compiler_params=_cparams(("arbitrary",)),
        name="final_norm",
    )(xv, g)


def kernel(x, c, ctx, c_ctx, ada_w, ada_b, norm_g, ffn_wg, ffn_wu, ffn_wd, w_in, w_out, out_g, lru_conv_w, lru_conv_b, lru_wa, lru_ba, lru_wx, lru_bx, lru_lam, hy_conv_w, hy_conv_b, hy_bias, filt_w1, filt_b1, filt_w2, filt_b2, filt_w3, filt_b3, filt_w4, filt_b4, filt_freq, final_g):
    assert x.shape[0] == 1 and ctx.shape[0] == 1
    depth = ada_w.shape[0]
    d = x.shape[-1]
    lw = lru_conv_w.shape[-1]
    hw = hy_bias.shape[-1]
    n_lat, n_ctx = x.shape[1], ctx.shape[1]
    xl, xc = x[0], ctx[0]
    mods = _ada_mods(c, c_ctx, ada_w, ada_b)
    zero_state = jnp.zeros((1, lw), F32)
    lat_is_col = False
    for l in range(depth):
        last = l == depth - 1
        m = [_sublayer_mods(mods[l], d, k) for k in range(3)]
        g = norm_g[l][:, None, :]

        xl = _ffn(xl, m[0], g[0], ffn_wg, ffn_wu, ffn_wd, l, 0, 0)
        xc = _ffn(xc, m[0], g[0], ffn_wg, ffn_wu, ffn_wd, l, 0, 1)

        want_col = l % 2 == 1
        perm = None if want_col == lat_is_col else ("to_col" if want_col else "from_col")
        lat_is_col = want_col
        wi = w_in[l].astype(BF16)
        pl_, xl = _win(xl, m[1], g[1], wi, w_in.shape[-1], 0, perm)
        pc_, _ = _win(xc, m[1], g[1], wi, lw if last else w_in.shape[-1], 1, None)

        cw, cb = lru_conv_w[l], lru_conv_b[l][None, :]
        h_lat = h_ctx = None
        for dr in range(2):
            wax = jnp.concatenate([lru_wa[l, dr], lru_wx[l, dr]], axis=-1).astype(BF16)
            ba, bx, lam = lru_ba[l, dr][None, :], lru_bx[l, dr][None, :], lru_lam[l, dr][None, :]
            h_ctx, state = _lru_scan(pc_, cw, cb, wax, ba, bx, lam, zero_state, h_ctx, dr == 1)
            h_lat, _ = _lru_scan(pl_, cw, cb, wax, ba, bx, lam, state, h_lat, dr == 1)

        filt = (filt_w1[l], filt_b1[l], filt_w2[l], filt_b2[l], filt_w3[l], filt_b3[l], filt_w4[l],
                filt_b4[l], filt_freq[l])
        hcw, hcb, hbias = hy_conv_w[l], hy_conv_b[l][None, :], hy_bias[l][None, :]
        og = out_g[l][None, :]
        wo = w_out[l].astype(BF16)

        u_l, x0_l = _hypre(pl_, 2 * lw, hcw, hcb)
        kr, ki = _hyena_spectrum(*_hyena_filter(n_lat, *filt))
        y_hy = _hyena_conv(u_l, hbias, kr, ki)
        xl = _mixout(h_lat, pl_, y_hy, x0_l, og, wo, xl, m[1], 0)
        xl = _ffn(xl, m[2], g[2], ffn_wg, ffn_wu, ffn_wd, l, 1, 0)

        if not last:
            u_c, x0_c = _hypre(pc_, 2 * lw, hcw, hcb)
            kr, ki = _hyena_spectrum(*_hyena_filter(n_ctx, *filt))
            y_hy = _hyena_conv(u_c, hbias, kr, ki)
            xc = _mixout(h_ctx, pc_, y_hy, x0_c, og, wo, xc, m[1], 1)
            xc = _ffn(xc, m[2], g[2], ffn_wg, ffn_wu, ffn_wd, l, 1, 1)

    return _final_norm(xl, final_g[None, :], lat_is_col)[None]
```

```python
import functools
import math

import numpy as np
import jax
import jax.numpy as jnp
from jax import lax
from jax.experimental import pallas as pl
from jax.experimental.pallas import tpu as pltpu

F32 = jnp.float32
BF16 = jnp.bfloat16
EPS = 1e-6
LRU_C = 8.0
GRID_W = 64
HY_BANDS = 16
HY_MAX_DECAY = math.log(1e-2) / 0.3
HY_MIN_DECAY = math.log(1e-2) / 1.5

V7X_VMEM_LIMIT_BYTES = 58 * 1024 * 1024
FFT_B = 128
SUBLANES = 8


def _cparams(sem):
    return pltpu.CompilerParams(dimension_semantics=sem, vmem_limit_bytes=V7X_VMEM_LIMIT_BYTES)


def _rms(x):
    return x * lax.rsqrt(jnp.mean(x * x, axis=-1, keepdims=True) + EPS)


def _modulated(x, shift_ref, scale_ref, g_ref, stream):
    shift = shift_ref[stream:stream + 1, :]
    scale = scale_ref[stream:stream + 1, :]
    return _rms(x) * g_ref[...] * (1.0 + scale) + shift


LANES = 128
N_STREAMS = 2


def _ada_kernel(c_ref, w_ref, b_ref, o_ref):
    o_ref[...] = jnp.zeros_like(o_ref)
    for k in range(N_STREAMS):
        c = c_ref[k]
        s = c * jax.nn.sigmoid(c)
        for j in range(0, w_ref.shape[1], LANES):
            cols = slice(j, j + LANES)
            o_ref[k:k + 1, cols] = jnp.sum(w_ref[:, cols] * s, axis=0, keepdims=True) + b_ref[:, cols]


def _ada_mods(c, c_ctx, ada_w, ada_b):
    depth, d, nm = ada_w.shape
    tn = d
    cc = jnp.broadcast_to(jnp.stack([c[0], c_ctx])[:, :, None], (N_STREAMS, d, LANES))
    return pl.pallas_call(
        _ada_kernel,
        grid=(depth, nm // tn),
        in_specs=[pl.BlockSpec((N_STREAMS, d, LANES), lambda l, j: (0, 0, 0)),
                  pl.BlockSpec((None, d, tn), lambda l, j: (l, 0, j)),
                  pl.BlockSpec((None, 1, tn), lambda l, j: (l, 0, j))],
        out_specs=pl.BlockSpec((None, SUBLANES, tn), lambda l, j: (l, 0, j)),
        out_shape=jax.ShapeDtypeStruct((depth, SUBLANES, nm), F32),
        compiler_params=_cparams(("arbitrary", "arbitrary")),
        name="ada_mods",
    )(cc, ada_w, ada_b[:, None, :])


def _mod_spec(l, k, which, d, tn=None):
    chunk = 3 * k + which
    if tn is None:
        return pl.BlockSpec((None, SUBLANES, d), lambda *g: (l, 0, chunk))
    return pl.BlockSpec((None, SUBLANES, tn), lambda *g: (l, 0, chunk * (d // tn) + g[-1]))


NORM_CHUNK = 256


def _ffn_up_kernel(x_ref, shift_ref, scale_ref, g_ref, wg_ref, wu_ref, h_ref, u_scr, *, stream, tm):
    wg = wg_ref[...].astype(BF16)
    wu = wu_ref[...].astype(BF16)

    def swiglu_rows(u):
        g = jnp.dot(u, wg, preferred_element_type=F32)
        up = jnp.dot(u, wu, preferred_element_type=F32)
        return (g * jax.nn.sigmoid(g) * up).astype(BF16)

    @pl.when(pl.program_id(1) == 0)
    def _():
        for r in range(0, tm, NORM_CHUNK):
            rs = slice(r, min(r + NORM_CHUNK, tm))
            u = _modulated(x_ref[rs, :], shift_ref, scale_ref, g_ref, stream).astype(BF16)
            u_scr[rs, :] = u
            h_ref[rs, :] = swiglu_rows(u)

    @pl.when(pl.program_id(1) > 0)
    def _():
        h_ref[...] = swiglu_rows(u_scr[...])


def _ffn_down_kernel(h_ref, wd_ref, x_ref, gate_ref, o_ref, *, stream):
    acc = jnp.dot(h_ref[...], wd_ref[...].astype(BF16), preferred_element_type=F32)
    gate = gate_ref[stream:stream + 1, :]
    o_ref[...] = x_ref[...] + (0.5 * gate) * acc


def _ffn(x, mods, k, g, wg, wu, wd, l, j, stream):
    rows, d = x.shape
    dff = wg.shape[-1]
    tm = min(rows, 1024)
    tf = 512
    tn = 256
    h = pl.pallas_call(
        functools.partial(_ffn_up_kernel, stream=stream, tm=tm),
        grid=(rows // tm, dff // tf),
        in_specs=[pl.BlockSpec((tm, d), lambda i, f: (i, 0)),
                  _mod_spec(l, k, 0, d), _mod_spec(l, k, 1, d),
                  pl.BlockSpec((1, d), lambda i, f: (0, 0)),
                  pl.BlockSpec((None, None, d, tf), lambda i, f: (l, j, 0, f)),
                  pl.BlockSpec((None, None, d, tf), lambda i, f: (l, j, 0, f))],
        out_specs=pl.BlockSpec((tm, tf), lambda i, f: (i, f)),
        out_shape=jax.ShapeDtypeStruct((rows, dff), BF16),
        scratch_shapes=[pltpu.VMEM((tm, d), BF16)],
        compiler_params=_cparams(("arbitrary", "arbitrary")),
        name="ffn_up",
    )(x, mods, mods, g, wg, wu)
    return pl.pallas_call(
        functools.partial(_ffn_down_kernel, stream=stream),
        grid=(rows // tm, d // tn),
        in_specs=[pl.BlockSpec((tm, dff), lambda i, n: (i, 0)),
                  pl.BlockSpec((None, None, dff, tn), lambda i, n: (l, j, 0, n)),
                  pl.BlockSpec((tm, tn), lambda i, n: (i, n)),
                  _mod_spec(l, k, 2, d, tn)],
        out_specs=pl.BlockSpec((tm, tn), lambda i, n: (i, n)),
        out_shape=jax.ShapeDtypeStruct((rows, d), F32),
        compiler_params=_cparams(("arbitrary", "arbitrary")),
        name="ffn_down",
    )(h, wd, x, mods)


def _win_kernel(x_ref, shift_ref, scale_ref, g_ref, w_ref, p_ref, *rest, pieces, prow, d, stream, emit_x):
    if emit_x:
        xo_ref, u_scr = rest
    else:
        (u_scr,) = rest

    w = w_ref[...]

    @pl.when(pl.program_id(1) == 0)
    def _():
        tm = pieces * prow
        seg = min(prow, NORM_CHUNK)
        for c0 in range(0, tm, NORM_CHUNK):
            c1 = min(c0 + NORM_CHUNK, tm)
            for r0 in range(c0, c1, seg):
                q, r = divmod(r0, prow)
                xt = x_ref[r:r + seg, q * d:(q + 1) * d]
                u_scr[r0:r0 + seg, :] = _modulated(xt, shift_ref, scale_ref, g_ref, stream).astype(BF16)
                if emit_x:
                    xo_ref[r0:r0 + seg, :] = xt
            p_ref[c0:c1, :] = jnp.dot(u_scr[c0:c1, :], w, preferred_element_type=F32)

    @pl.when(pl.program_id(1) > 0)
    def _():
        p_ref[...] = jnp.dot(u_scr[...], w, preferred_element_type=F32)


def _win(x, mods, l, g, w_in_bf16, ncols, stream, perm):
    rows, d = x.shape
    tm = min(rows, 1024)
    tn = 1024 if perm is None else 512
    if perm is None:
        prow, pieces, xv = tm, 1, x
        x_spec = pl.BlockSpec((tm, d), lambda i, n: (i, 0))
    else:
        grid_h = rows // GRID_W
        prow = grid_h if perm == "to_col" else GRID_W
        pieces = tm // prow
        xv = x.reshape(prow, (rows // prow) * d)
        x_spec = pl.BlockSpec((prow, pieces * d), lambda i, n: (0, i))
    emit_x = perm is not None
    out_shape = [jax.ShapeDtypeStruct((rows, ncols), F32)]
    out_specs = [pl.BlockSpec((tm, tn), lambda i, n: (i, n))]
    if emit_x:
        out_shape.append(jax.ShapeDtypeStruct((rows, d), F32))
        out_specs.append(pl.BlockSpec((tm, d), lambda i, n: (i, 0)))
    res = pl.pallas_call(
        functools.partial(_win_kernel, pieces=pieces, prow=prow, d=d, stream=stream, emit_x=emit_x),
        grid=(rows // tm, ncols // tn),
        in_specs=[x_spec,
                  _mod_spec(l, 1, 0, d), _mod_spec(l, 1, 1, d),
                  pl.BlockSpec((1, d), lambda i, n: (0, 0)),
                  pl.BlockSpec((d, tn), lambda i, n: (0, n))],
        out_specs=out_specs,
        out_shape=out_shape,
        scratch_shapes=[pltpu.VMEM((tm, d), BF16)],
        compiler_params=_cparams(("arbitrary", "arbitrary")),
        name="mixer_in_proj",
    )(xv, mods, mods, g, w_in_bf16)
    return (res[0], res[1]) if emit_x else (res[0], x)


def _lru_kernel(x_ref, xp_ref, xn_ref, cw_ref, cb_ref, wax_ref, ba_ref, bx_ref, lam_ref, h0_ref, *rest,
                t_blk, nblk, heads, reverse, add_prev):
    if add_prev:
        prev_ref, o_ref, hf_ref, xs, a_scr, b_scr, carry = rest
    else:
        o_ref, hf_ref, xs, a_scr, b_scr, carry = rest
    j = pl.program_id(0)
    blk = (nblk - 1 - j) if reverse else j
    c = x_ref.shape[1]
    hd = c // heads

    @pl.when(j == 0)
    def _():
        carry[...] = h0_ref[...]

    zero8 = jnp.zeros((SUBLANES, c), F32)
    xs[0:SUBLANES, :] = jnp.where(blk == 0, zero8, xp_ref[...])
    xs[SUBLANES:SUBLANES + t_blk, :] = x_ref[...]
    xs[SUBLANES + t_blk:2 * SUBLANES + t_blk, :] = jnp.where(blk == nblk - 1, zero8, xn_ref[...])
    cv = cb_ref[...]
    for k in range(cw_ref.shape[0]):
        cv = cv + xs[SUBLANES - 1 + k:SUBLANES - 1 + k + t_blk, :] * cw_ref[k:k + 1, :]

    z = -lam_ref[...]
    c_sp = LRU_C * (jnp.maximum(z, 0.0) + jnp.log(1.0 + jnp.exp(-jnp.abs(z))))
    cvb = cv.astype(BF16)
    for h in range(heads):
        sl = slice(h * hd, (h + 1) * hd)
        gx = jnp.dot(cvb[:, sl], wax_ref[h], preferred_element_type=F32)
        r = 0.5 + 0.5 * jnp.tanh(0.5 * (gx[:, :hd] + ba_ref[:, sl]))
        ig = 0.5 + 0.5 * jnp.tanh(0.5 * (gx[:, hd:] + bx_ref[:, sl]))
        a = jnp.exp(-c_sp[:, sl] * r)
        a_scr[:, sl] = a
        b_scr[:, sl] = jnp.sqrt(1.0 - a * a) * (ig * cv[:, sl])

    row = lax.broadcasted_iota(jnp.int32, (SUBLANES, c), 0)
    ngrp = t_blk // SUBLANES

    def group(gi, hc):
        g = (ngrp - 1 - gi) if reverse else gi
        r0 = pl.multiple_of(g * SUBLANES, SUBLANES)
        av = a_scr[pl.ds(r0, SUBLANES), :]
        bv = b_scr[pl.ds(r0, SUBLANES), :]
        for s in (1, 2, 4):
            sh = (SUBLANES - s) if reverse else s
            m = (row < SUBLANES - s) if reverse else (row >= s)
            a_s = pltpu.roll(av, sh, axis=0)
            b_s = pltpu.roll(bv, sh, axis=0)
            bv = jnp.where(m, av * b_s + bv, bv)
            av = jnp.where(m, av * a_s, av)
        hv = av * hc + bv
        if add_prev:
            o_ref[pl.ds(r0, SUBLANES), :] = hv + prev_ref[pl.ds(r0, SUBLANES), :]
        else:
            o_ref[pl.ds(r0, SUBLANES), :] = hv
        return hv[0:1, :] if reverse else hv[SUBLANES - 1:SUBLANES, :]

    hc = lax.fori_loop(0, ngrp, group, carry[...])
    carry[...] = hc
    hf_ref[...] = hc


def _lru_scan(p, cw, cb, wax, ba, bx, lam, h0, prev, reverse):
    seq = p.shape[0]
    c = cw.shape[1]
    heads = wax.shape[0]
    t_blk = 256
    nblk = seq // t_blk
    per = t_blk // SUBLANES
    nb8 = seq // SUBLANES

    def bi(j):
        return (nblk - 1 - j) if reverse else j

    in_specs = [pl.BlockSpec((t_blk, c), lambda j: (bi(j), 0)),
                pl.BlockSpec((SUBLANES, c), lambda j: (jnp.maximum(bi(j) * per - 1, 0), 0)),
                pl.BlockSpec((SUBLANES, c), lambda j: (jnp.minimum((bi(j) + 1) * per, nb8 - 1), 0)),
                pl.BlockSpec(cw.shape, lambda j: (0, 0)),
                pl.BlockSpec((1, c), lambda j: (0, 0)),
                pl.BlockSpec(wax.shape, lambda j: (0, 0, 0)),
                pl.BlockSpec((1, c), lambda j: (0, 0)),
                pl.BlockSpec((1, c), lambda j: (0, 0)),
                pl.BlockSpec((1, c), lambda j: (0, 0)),
                pl.BlockSpec((1, c), lambda j: (0, 0))]
    args = [p, p, p, cw, cb, wax, ba, bx, lam, h0]
    if prev is not None:
        in_specs.append(pl.BlockSpec((t_blk, c), lambda j: (bi(j), 0)))
        args.append(prev)
    return pl.pallas_call(
        functools.partial(_lru_kernel, t_blk=t_blk, nblk=nblk, heads=heads, reverse=reverse,
                          add_prev=prev is not None),
        grid=(nblk,),
        in_specs=in_specs,
        out_specs=[pl.BlockSpec((t_blk, c), lambda j: (bi(j), 0)),
                   pl.BlockSpec((1, c), lambda j: (0, 0))],
        out_shape=[jax.ShapeDtypeStruct((seq, c), F32), jax.ShapeDtypeStruct((1, c), F32)],
        scratch_shapes=[pltpu.VMEM((t_blk + 2 * SUBLANES, c), F32),
                        pltpu.VMEM((t_blk, c), F32),
                        pltpu.VMEM((t_blk, c), F32),
                        pltpu.VMEM((1, c), F32)],
        compiler_params=_cparams(("arbitrary",)),
        name="rglru_scan",
    )(*args)


def _hypre_kernel(x0_ref, x0p, x0n, x1_ref, x1p, x1n, v_ref, vp, vn, cw_ref, cb_ref, u_ref, x0c_ref, xs,
                  *, t_blk, nblk, c):
    blk = pl.program_id(0)
    zero8 = jnp.zeros((SUBLANES, c), F32)

    def conv3(cur, prv, nxt, part):
        xs[0:SUBLANES, :] = jnp.where(blk == 0, zero8, prv[...])
        xs[SUBLANES:SUBLANES + t_blk, :] = cur[...]
        xs[SUBLANES + t_blk:2 * SUBLANES + t_blk, :] = jnp.where(blk == nblk - 1, zero8, nxt[...])
        sl = slice(part * c, (part + 1) * c)
        out = cb_ref[:, sl]
        for k in range(cw_ref.shape[0]):
            out = out + xs[SUBLANES - 1 + k:SUBLANES - 1 + k + t_blk, :] * cw_ref[k:k + 1, sl]
        return out

    x0c_ref[...] = conv3(x0_ref, x0p, x0n, 0)
    x1c = conv3(x1_ref, x1p, x1n, 1)
    u_ref[...] = conv3(v_ref, vp, vn, 2) * x1c


def _hypre(p, col0, cw, cb):
    seq = p.shape[0]
    c = cw.shape[1] // 3
    t_blk = min(seq, 512)
    nblk = seq // t_blk
    per = t_blk // SUBLANES
    nb8 = seq // SUBLANES
    cb0 = col0 // c
    in_specs, args = [], []
    for part in range(3):
        in_specs += [pl.BlockSpec((t_blk, c), lambda j, part=part: (j, cb0 + part)),
                     pl.BlockSpec((SUBLANES, c), lambda j, part=part: (jnp.maximum(j * per - 1, 0), cb0 + part)),
                     pl.BlockSpec((SUBLANES, c),
                                  lambda j, part=part: (jnp.minimum((j + 1) * per, nb8 - 1), cb0 + part))]
        args += [p, p, p]
    in_specs += [pl.BlockSpec(cw.shape, lambda j: (0, 0)), pl.BlockSpec((1, 3 * c), lambda j: (0, 0))]
    args += [cw, cb]
    return pl.pallas_call(
        functools.partial(_hypre_kernel, t_blk=t_blk, nblk=nblk, c=c),
        grid=(nblk,),
        in_specs=in_specs,
        out_specs=[pl.BlockSpec((t_blk, c), lambda j: (j, 0)), pl.BlockSpec((t_blk, c), lambda j: (j, 0))],
        out_shape=[jax.ShapeDtypeStruct((seq, c), F32), jax.ShapeDtypeStruct((seq, c), F32)],
        scratch_shapes=[pltpu.VMEM((t_blk + 2 * SUBLANES, c), F32)],
        compiler_params=_cparams(("arbitrary",)),
        name="hyena_short_conv",
    )(*args)


FILT_SLOT = 64


def _filt_kernel(w1_ref, b1_ref, w2_ref, b2_ref, w3_ref, b3_ref, w4a_ref, w4b_ref, b4_ref, fr_ref,
                 k_ref, nrm_ref, *, seq, t_blk, c):
    i = pl.program_id(0)
    hi = lax.Precision.HIGHEST
    lanes = 2 * FILT_SLOT
    th = t_blk // 2
    lane = lax.broadcasted_iota(jnp.int32, (1, lanes), 1)
    slot = lane & (FILT_SLOT - 1)
    pos_a = i * t_blk + lax.broadcasted_iota(jnp.int32, (th, 1), 0)
    pos = jnp.where(lane < FILT_SLOT, pos_a, pos_a + th)
    tf = jnp.where(pos < seq, pos, 2 * seq - pos).astype(F32)
    t = tf * (1.0 / (seq - 1))
    w = (2.0 * math.pi / seq) * tf
    is_cos = (slot >= 1) & (slot <= HY_BANDS)
    is_sin = (slot > HY_BANDS) & (slot <= 2 * HY_BANDS)
    band = jnp.where(is_cos, slot - 1, slot - 1 - HY_BANDS).astype(F32)
    f = 1e-4 + band * ((HY_BANDS - 1 - 1e-4) / (HY_BANDS - 1))
    phase = jnp.where(is_cos, 0.5 * math.pi, 0.0)
    sign = jnp.where(is_cos, 1.0, jnp.where(is_sin, -1.0, 0.0))
    z = jnp.where(slot == 0, t, sign * jnp.sin(w * f + phase))
    fr = fr_ref[...]
    h = jnp.sin(fr * (jnp.dot(z, w1_ref[...], precision=hi, preferred_element_type=F32) + b1_ref[...]))
    h = jnp.sin(fr * (jnp.dot(h, w2_ref[...], precision=hi, preferred_element_type=F32) + b2_ref[...]))
    h = jnp.sin(fr * (jnp.dot(h, w3_ref[...], precision=hi, preferred_element_type=F32) + b3_ref[...]))
    ch = lax.broadcasted_iota(jnp.int32, (1, c), 1).astype(F32)
    delta = jnp.abs(HY_MIN_DECAY + ch * ((HY_MAX_DECAY - HY_MIN_DECAY) / (c - 1)))
    part = jnp.zeros((1, c), F32)
    for half, w4_ref in enumerate((w4a_ref, w4b_ref)):
        pos_h = pos_a + half * th
        t_h = jnp.where(pos_h < seq, pos_h, 2 * seq - pos_h).astype(F32) * (1.0 / (seq - 1))
        k = jnp.dot(h, w4_ref[...], precision=hi, preferred_element_type=F32) + b4_ref[...]
        k = jnp.where(pos_h != seq, k * jnp.exp(-t_h * delta), 0.0)
        k_ref[half * th:(half + 1) * th, :] = k
        part = part + jnp.sum(jnp.abs(k), axis=0, keepdims=True)

    @pl.when(i == 0)
    def _():
        nrm_ref[...] = part

    @pl.when(i > 0)
    def _():
        nrm_ref[...] = nrm_ref[...] + part


def _slots(a, row_used):
    r, cdim = a.shape
    blk = jnp.pad(a, ((0, (FILT_SLOT if row_used else 1) - r), (0, FILT_SLOT - cdim)))
    if not row_used:
        return jnp.concatenate([blk, blk], axis=1)
    z = jnp.zeros_like(blk)
    return jnp.concatenate([jnp.concatenate([blk, z], axis=1), jnp.concatenate([z, blk], axis=1)], axis=0)


def _hyena_filter(seq, w1, b1, w2, b2, w3, b3, w4, b4, freq):
    c = w4.shape[1] // 2
    od = w2.shape[0]
    assert od <= FILT_SLOT and w1.shape[0] <= FILT_SLOT
    lanes = 2 * FILT_SLOT
    t_blk = min(seq, 1024)
    half_blocks = seq // t_blk
    full = lambda shape: pl.BlockSpec(shape, lambda i: (0,) * len(shape))
    sq, row = (lanes, lanes), (1, lanes)
    w4p = jnp.pad(w4, ((0, FILT_SLOT - od), (0, 0)))
    w4a = jnp.concatenate([w4p, jnp.zeros_like(w4p)], axis=0)
    w4b = jnp.concatenate([jnp.zeros_like(w4p), w4p], axis=0)
    w4_spec = pl.BlockSpec((lanes, c), lambda i: (0, i // half_blocks))
    return pl.pallas_call(
        functools.partial(_filt_kernel, seq=seq, t_blk=t_blk, c=c),
        grid=(2 * seq // t_blk,),
        in_specs=[full(sq), full(row), full(sq), full(row), full(sq), full(row), w4_spec, w4_spec,
                  pl.BlockSpec((1, c), lambda i: (0, i // half_blocks)),
                  full(row)],
        out_specs=[pl.BlockSpec((t_blk, c), lambda i: (i, 0)), pl.BlockSpec((1, c), lambda i: (0, 0))],
        out_shape=[jax.ShapeDtypeStruct((2 * seq, c), F32), jax.ShapeDtypeStruct((1, c), F32)],
        compiler_params=_cparams(("arbitrary",)),
        name="hyena_filter",
    )(_slots(w1, True), _slots(b1[None, :], False), _slots(w2, True), _slots(b2[None, :], False),
      _slots(w3, True), _slots(b3[None, :], False), w4a, w4b, b4[None, :], _slots(freq[None, :], False))


def _bitrev(p, bits):
    r = 0
    for k in range(bits):
        r = (r << 1) | ((p >> k) & 1)
    return r


def _real_block(z):
    return np.block([[z.real, -z.imag], [z.imag, z.real]])


def _fft_split(s2):
    n_lo = max(min(8, s2 // 2), 1)
    return s2 // n_lo, n_lo


@functools.lru_cache(maxsize=None)
def _fft_tables(s):
    s2 = s // 2
    n = s * FFT_B
    n_hi, n_lo = _fft_split(s2)
    bits_hi, bits_lo = n_hi.bit_length() - 1, n_lo.bit_length() - 1
    b = np.arange(FFT_B)
    dft = np.exp(-2j * np.pi * np.outer(b, b) / FFT_B)
    e_lo = [2 * n_hi * _bitrev(p, bits_lo) for p in range(n_lo)]
    g = [dft * np.exp(-2j * np.pi * b * e / n)[None, :] for e in e_lo]
    chunks = np.arange(FFT_B).reshape(-1, FFT_ROWS)
    order = np.concatenate([np.concatenate([c, c + FFT_B]) for c in chunks])
    lhs_f = np.stack([_real_block(x)[order] for x in g])
    lhs_i = np.stack([_real_block(np.conj(x).T)[order] for x in g])
    e_hi = np.array([[2 * _bitrev(p, bits_hi) + hf for p in range(n_hi)] for hf in range(2)])
    ang = 2.0 * np.pi * e_hi[:, :, None, None] * b[None, None, :, None] / n * np.ones((1, 1, 1, FFT_B))
    lvl = 2.0 * np.pi * np.arange(max(s2 // 2, 1)) / s2
    first = np.concatenate([np.zeros(s2), 2.0 * np.pi * np.arange(s2) / s])
    f32 = lambda x: np.asarray(x, np.float32)
    return dict(lhs_f=f32(lhs_f), lhs_i=f32(lhs_i), col_c=f32(np.cos(ang)), col_s=f32(np.sin(ang)),
                lvl_c=f32(np.cos(lvl)), lvl_s=f32(np.sin(lvl)), first_c=f32(np.cos(first)),
                first_s=f32(np.sin(first)))


FFT_ROWS = 32


def _cmul(xr, xi, c, s):
    return xr * c + xi * s, xi * c - xr * s


def _cmul_conj(xr, xi, c, s):
    return xr * c - xi * s, xi * c + xr * s


def _pass_plan(s2):
    levels = s2.bit_length() - 1
    plan, h = [], s2 // 2
    if levels % 2 == 1:
        plan.append((h, 2))
        h //= 2
    while h >= 1:
        plan.append((h, 4))
        h //= 4
    return plan


def _butterfly_pass(load, store, lc_ref, ls_ref, s2, h, radix, inverse):
    h2 = h // 2 if radix == 4 else h
    trivial = h2 == 1
    shift = h2.bit_length() - 1

    def group(idx, carry):
        g = idx >> shift
        i = idx & (h2 - 1)
        base = g * 2 * h + i
        blocks = [base + k * h2 for k in range(radix)]
        if not trivial:
            c1, s1 = lc_ref[i * (s2 // (2 * h))], ls_ref[i * (s2 // (2 * h))]
            if radix == 4:
                c2, s2_ = lc_ref[i * (s2 // h)], ls_ref[i * (s2 // h)]
        fwd_mul = (lambda xr, xi, c, s: (xr, xi)) if trivial else _cmul
        inv_mul = (lambda xr, xi, c, s: (xr, xi)) if trivial else _cmul_conj
        if trivial:
            c1 = s1 = c2 = s2_ = None
        for r in range(0, FFT_B, FFT_ROWS):
            x = [load(bk, r) for bk in blocks]
            if radix == 2:
                (ar, ai), (br, bi) = x
                if inverse:
                    br, bi = inv_mul(br, bi, c1, s1)
                    out = [(ar + br, ai + bi), (ar - br, ai - bi)]
                else:
                    out = [(ar + br, ai + bi), fwd_mul(ar - br, ai - bi, c1, s1)]
            elif inverse:
                (x0r, x0i), (x1r, x1i), (x2r, x2i), (x3r, x3i) = x
                x1r, x1i = inv_mul(x1r, x1i, c2, s2_)
                x3r, x3i = inv_mul(x3r, x3i, c2, s2_)
                t0r, t0i, t1r, t1i = x0r + x1r, x0i + x1i, x0r - x1r, x0i - x1i
                t2r, t2i = inv_mul(x2r + x3r, x2i + x3i, c1, s1)
                t3r, t3i = inv_mul(x2r - x3r, x2i - x3i, c1, s1)
                out = [(t0r + t2r, t0i + t2i), (t1r - t3i, t1i + t3r),
                       (t0r - t2r, t0i - t2i), (t1r + t3i, t1i - t3r)]
            else:
                (x0r, x0i), (x1r, x1i), (x2r, x2i), (x3r, x3i) = x
                t0r, t0i, t1r, t1i = x0r + x2r, x0i + x2i, x1r + x3r, x1i + x3i
                t2r, t2i = fwd_mul(x0r - x2r, x0i - x2i, c1, s1)
                t3r, t3i = fwd_mul(x1r - x3r, x1i - x3i, c1, s1)
                out = [(t0r + t1r, t0i + t1i), fwd_mul(t0r - t1r, t0i - t1i, c2, s2_),
                       (t2r + t3i, t2i - t3r), fwd_mul(t2r - t3i, t2i + t3r, c2, s2_)]
            for bk, (o_r, o_i) in zip(blocks, out):
                store(bk, r, o_r, o_i)
        return carry

    lax.fori_loop(0, s2 // radix, group, 0)


def _work_access(wr, wi):
    def rows(bk, r):
        return pl.ds(pl.multiple_of(bk * FFT_B + r, FFT_ROWS), FFT_ROWS)

    def load(bk, r):
        return wr[rows(bk, r), :], wi[rows(bk, r), :]

    def store(bk, r, re, im):
        wr[rows(bk, r), :] = re
        wi[rows(bk, r), :] = im

    return load, store


def _run_passes(plan, first_load, last_store, work, lc_ref, ls_ref, s2, inverse):
    w_load, w_store = work
    for k, (h, radix) in enumerate(plan):
        load = first_load if k == 0 else w_load
        store = last_store if k == len(plan) - 1 else w_store
        _butterfly_pass(load, store, lc_ref, ls_ref, s2, h, radix, inverse)


MID_UNROLL = 4


def _pair_blocks(pp, s2):
    n_hi, n_lo = _fft_split(s2)
    half = n_hi // 2
    p_lo = pp >> (half.bit_length() - 1)
    j = pp & (half - 1)
    return p_lo, [(2 * j + q, pl.multiple_of(((2 * j + q) * n_lo + p_lo) * FFT_B, FFT_B)) for q in range(2)]


def _for_pairs(s2, body):
    def step(pp, carry):
        body(pp)
        return carry
    lax.fori_loop(0, s2 // 2, step, 0, unroll=min(MID_UNROLL, s2 // 2))


def _pack_store(panel, cc_ref, cs_ref, s2):
    n_hi, n_lo = _fft_split(s2)
    lo_bits = n_lo.bit_length() - 1

    def store(bk, r, re, im):
        p_hi = bk >> lo_bits
        p_lo = bk & (n_lo - 1)
        pp = p_lo * (n_hi // 2) + (p_hi >> 1)
        q = p_hi & 1
        tr, ti = _cmul(re, im, cc_ref[p_hi, r:r + FFT_ROWS, :], cs_ref[p_hi, r:r + FFT_ROWS, :])
        panel[pp, q, r:r + FFT_ROWS, :] = tr.astype(BF16)
        panel[pp, q, FFT_B + r:FFT_B + r + FFT_ROWS, :] = ti.astype(BF16)

    return store


def _dft_pair(m_ref, panel, pp, p_lo):
    blk = FFT_B
    rhs = jnp.concatenate([panel[pp, 0], panel[pp, 1]], axis=1)
    out = jnp.dot(m_ref[p_lo], rhs, preferred_element_type=F32)
    for r in range(0, blk, FFT_ROWS):
        for q in range(2):
            yield (q, r, out[2 * r:2 * r + FFT_ROWS, q * blk:(q + 1) * blk],
                   out[2 * r + FFT_ROWS:2 * r + 2 * FFT_ROWS, q * blk:(q + 1) * blk])


def _spec_kernel(lo_ref, hi_ref, nrm_ref, fc_ref, fs_ref, lc_ref, ls_ref, cc_ref, cs_ref, mf_ref,
                 kr_ref, ki_ref, panel, *, s2):
    blk = FFT_B
    hf = pl.program_id(1)
    n = 2 * s2 * blk
    sign = (1 - 2 * hf).astype(F32)

    def first_load(bk, r):
        rows = pl.ds(pl.multiple_of(bk * blk + r, FFT_ROWS), FFT_ROWS)
        d = lo_ref[rows, :] + sign * hi_ref[rows, :]
        return d * fc_ref[hf * s2 + bk], -d * fs_ref[hf * s2 + bk]

    _run_passes(_pass_plan(s2), first_load, _pack_store(panel, cc_ref, cs_ref, s2),
                _work_access(kr_ref, ki_ref), lc_ref, ls_ref, s2, inverse=False)
    scale = 1.0 / (nrm_ref[...] * n)

    def dft(pp):
        p_lo, blocks = _pair_blocks(pp, s2)
        for q, r, yr, yi in _dft_pair(mf_ref, panel, pp, p_lo):
            rows = pl.ds(blocks[q][1] + r, FFT_ROWS)
            kr_ref[rows, :] = yr * scale
            ki_ref[rows, :] = yi * scale

    _for_pairs(s2, dft)


def _conv_kernel(u_ref, bias_ref, kr_ref, ki_ref, fc_ref, fs_ref, lc_ref, ls_ref, cc_ref, cs_ref,
                 mf_ref, mi_ref, y_ref, wr, wi, panel, *, s2):
    blk = FFT_B
    hf = pl.program_id(1)
    plan = _pass_plan(s2)
    work = _work_access(wr, wi)

    def first_load(bk, r):
        ub = u_ref[pl.ds(pl.multiple_of(bk * blk + r, FFT_ROWS), FFT_ROWS), :]
        return ub * fc_ref[hf * s2 + bk], -ub * fs_ref[hf * s2 + bk]

    _run_passes(plan, first_load, _pack_store(panel, cc_ref, cs_ref, s2), work, lc_ref, ls_ref, s2,
                inverse=False)

    def spectrum_product(pp):
        p_lo, blocks = _pair_blocks(pp, s2)
        for q, r, yr, yi in _dft_pair(mf_ref, panel, pp, p_lo):
            rows = pl.ds(blocks[q][1] + r, FFT_ROWS)
            kr, ki = kr_ref[rows, :], ki_ref[rows, :]
            panel[pp, q, r:r + FFT_ROWS, :] = (yr * kr - yi * ki).astype(BF16)
            panel[pp, q, blk + r:blk + r + FFT_ROWS, :] = (yr * ki + yi * kr).astype(BF16)

    _for_pairs(s2, spectrum_product)

    def inverse_dft(pp):
        p_lo, blocks = _pair_blocks(pp, s2)
        for q, r, er, ei in _dft_pair(mi_ref, panel, pp, p_lo):
            p_hi, r0 = blocks[q]
            rows = pl.ds(r0 + r, FFT_ROWS)
            wr[rows, :], wi[rows, :] = _cmul_conj(er, ei, cc_ref[p_hi, r:r + FFT_ROWS, :],
                                                  cs_ref[p_hi, r:r + FFT_ROWS, :])

    _for_pairs(s2, inverse_dft)

    def rows_of(bk, r):
        return pl.ds(pl.multiple_of(bk * blk + r, FFT_ROWS), FFT_ROWS)

    def store_even(bk, r, er, ei):
        y_ref[rows_of(bk, r), :] = er

    def store_odd(bk, r, er, ei):
        rows = rows_of(bk, r)
        val = er * fc_ref[s2 + bk] - ei * fs_ref[s2 + bk]
        y_ref[rows, :] = y_ref[rows, :] + u_ref[rows, :] * bias_ref[...] + val

    inv_plan = plan[::-1]
    if len(inv_plan) > 1:
        _run_passes(inv_plan[:-1], work[0], work[1], work, lc_ref, ls_ref, s2, inverse=True)
    h_last, radix_last = inv_plan[-1]
    for half, store in ((0, store_even), (1, store_odd)):
        @pl.when(hf == half)
        def _(store=store):
            _butterfly_pass(work[0], store, lc_ref, ls_ref, s2, h_last, radix_last, True)


def _smem_spec():
    return pl.BlockSpec(memory_space=pltpu.SMEM)


def _fft_table_specs(s2):
    n_hi, n_lo = _fft_split(s2)
    col = pl.BlockSpec((None, n_hi, FFT_B, FFT_B), lambda ct, hf: (hf, 0, 0, 0))
    lhs = pl.BlockSpec((n_lo, 2 * FFT_B, 2 * FFT_B), lambda ct, hf: (0, 0, 0))
    return [_smem_spec(), _smem_spec(), _smem_spec(), _smem_spec(), col, col], lhs


def _hyena_spectrum(kfull, nrm):
    n, c = kfull.shape
    seq = n // 2
    s2 = seq // FFT_B
    tb = _fft_tables(2 * s2)
    lanes = 128
    tab_specs, lhs_spec = _fft_table_specs(s2)
    return pl.pallas_call(
        functools.partial(_spec_kernel, s2=s2),
        grid=(c // lanes, 2),
        in_specs=[pl.BlockSpec((seq, lanes), lambda ct, hf: (0, ct)),
                  pl.BlockSpec((seq, lanes), lambda ct, hf: (1, ct)),
                  pl.BlockSpec((1, lanes), lambda ct, hf: (0, ct))] + tab_specs + [lhs_spec],
        out_specs=[pl.BlockSpec((None, seq, lanes), lambda ct, hf: (hf, 0, ct)),
                   pl.BlockSpec((None, seq, lanes), lambda ct, hf: (hf, 0, ct))],
        out_shape=[jax.ShapeDtypeStruct((2, seq, c), F32), jax.ShapeDtypeStruct((2, seq, c), F32)],
        scratch_shapes=[pltpu.VMEM((s2 // 2, 2, 2 * FFT_B, FFT_B), BF16)],
        compiler_params=_cparams(("arbitrary", "arbitrary")),
        name="hyena_spectrum",
    )(kfull, kfull, nrm, tb["first_c"], tb["first_s"], tb["lvl_c"], tb["lvl_s"],
      tb["col_c"], tb["col_s"], jnp.asarray(tb["lhs_f"], BF16))


def _hyena_conv(u, bias, kr, ki):
    seq, c = u.shape
    s2 = seq // FFT_B
    tb = _fft_tables(2 * s2)
    lanes = 128
    tab_specs, lhs_spec = _fft_table_specs(s2)
    return pl.pallas_call(
        functools.partial(_conv_kernel, s2=s2),
        grid=(c // lanes, 2),
        in_specs=[pl.BlockSpec((seq, lanes), lambda ct, hf: (0, ct)),
                  pl.BlockSpec((1, lanes), lambda ct, hf: (0, ct)),
                  pl.BlockSpec((None, seq, lanes), lambda ct, hf: (hf, 0, ct)),
                  pl.BlockSpec((None, seq, lanes), lambda ct, hf: (hf, 0, ct))]
                 + tab_specs + [lhs_spec, lhs_spec],
        out_specs=pl.BlockSpec((seq, lanes), lambda ct, hf: (0, ct)),
        out_shape=jax.ShapeDtypeStruct((seq, c), F32),
        scratch_shapes=[pltpu.VMEM((seq, lanes), F32), pltpu.VMEM((seq, lanes), F32),
                        pltpu.VMEM((s2 // 2, 2, 2 * FFT_B, FFT_B), BF16)],
        compiler_params=_cparams(("arbitrary", "arbitrary")),
        name="hyena_fft_conv",
    )(u, bias, kr, ki, tb["first_c"], tb["first_s"], tb["lvl_c"], tb["lvl_s"],
      tb["col_c"], tb["col_s"], jnp.asarray(tb["lhs_f"], BF16), jnp.asarray(tb["lhs_i"], BF16))


MIXOUT_CHUNK = 256


def _mixout_kernel(h_ref, yr_ref, hy_ref, x0_ref, og_ref, w_ref, x_ref, gate_ref, o_ref, *, stream, lw, tm):
    gate = gate_ref[stream:stream + 1, :]
    w = w_ref[...]
    for r in range(0, tm, MIXOUT_CHUNK):
        rs = slice(r, r + min(MIXOUT_CHUNK, tm))
        yr = yr_ref[rs, :]
        gelu = 0.5 * yr * (1.0 + jnp.tanh(math.sqrt(2.0 / math.pi) * (yr + 0.044715 * (yr * yr * yr))))
        lru = (_rms(h_ref[rs, :] * gelu) * og_ref[:, :lw]).astype(BF16)
        hy = (_rms(hy_ref[rs, :] * x0_ref[rs, :]) * og_ref[:, lw:]).astype(BF16)
        y = (jnp.dot(lru, w[:lw, :], preferred_element_type=F32)
             + jnp.dot(hy, w[lw:, :], preferred_element_type=F32))
        o_ref[rs, :] = x_ref[rs, :] + gate * y


def _mixout(h_lru, p, y_hy, x0c, og, w_out_bf16, x, mods, l, stream):
    rows, d = x.shape
    lw = h_lru.shape[1]
    dm = w_out_bf16.shape[0]
    tm = min(rows, 512)
    row_blk = lambda cols, j: pl.BlockSpec((tm, cols), lambda i: (i, j))
    return pl.pallas_call(
        functools.partial(_mixout_kernel, stream=stream, lw=lw, tm=tm),
        grid=(rows // tm,),
        in_specs=[row_blk(lw, 0), row_blk(lw, 1), row_blk(dm - lw, 0), row_blk(dm - lw, 0),
                  pl.BlockSpec((1, dm), lambda i: (0, 0)),
                  pl.BlockSpec((dm, d), lambda i: (0, 0)),
                  row_blk(d, 0),
                  _mod_spec(l, 1, 2, d)],
        out_specs=row_blk(d, 0),
        out_shape=jax.ShapeDtypeStruct((rows, d), F32),
        compiler_params=_cparams(("arbitrary",)),
        name="mixer_out_proj",
    )(h_lru, p, y_hy, x0c, og, w_out_bf16, x, mods)


def _final_kernel(x_ref, g_ref, o_ref, *, pieces, prow, d):
    for q in range(pieces):
        o_ref[q * prow:(q + 1) * prow, :] = _rms(x_ref[:, q * d:(q + 1) * d]) * g_ref[...]


def _final_norm(x, g, from_col):
    rows, d = x.shape
    tm = 1024
    if from_col:
        prow, pieces = GRID_W, tm // GRID_W
        xv = x.reshape(prow, (rows // prow) * d)
        x_spec = pl.BlockSpec((prow, pieces * d), lambda i: (0, i))
    else:
        prow, pieces, xv = tm, 1, x
        x_spec = pl.BlockSpec((tm, d), lambda i: (i, 0))
    return pl.pallas_call(
        functools.partial(_final_kernel, pieces=pieces, prow=prow, d=d),
        grid=(rows // tm,),
        in_specs=[x_spec, pl.BlockSpec((1, d), lambda i: (0, 0))],
        out_specs=pl.BlockSpec((tm, d), lambda i: (i, 0)),
        out_shape=jax.ShapeDtypeStruct((rows, d), F32),
        compiler_params=_cparams(("arbitrary",)),
        name="final_norm",
    )(xv, g)


def kernel(x, c, ctx, c_ctx, ada_w, ada_b, norm_g, ffn_wg, ffn_wu, ffn_wd, w_in, w_out, out_g, lru_conv_w, lru_conv_b, lru_wa, lru_ba, lru_wx, lru_bx, lru_lam, hy_conv_w, hy_conv_b, hy_bias, filt_w1, filt_b1, filt_w2, filt_b2, filt_w3, filt_b3, filt_w4, filt_b4, filt_freq, final_g):
    assert x.shape[0] == 1 and ctx.shape[0] == 1
    depth = ada_w.shape[0]
    d = x.shape[-1]
    lw = lru_conv_w.shape[-1]
    hw = hy_bias.shape[-1]
    n_lat, n_ctx = x.shape[1], ctx.shape[1]
    xl, xc = x[0], ctx[0]
    mods = _ada_mods(c, c_ctx, ada_w, ada_b)
    zero_state = jnp.zeros((1, lw), F32)
    lat_is_col = False
    for l in range(depth):
        last = l == depth - 1
        g = norm_g[l][:, None, :]

        xl = _ffn(xl, mods, 0, g[0], ffn_wg, ffn_wu, ffn_wd, l, 0, 0)
        xc = _ffn(xc, mods, 0, g[0], ffn_wg, ffn_wu, ffn_wd, l, 0, 1)

        want_col = l % 2 == 1
        perm = None if want_col == lat_is_col else ("to_col" if want_col else "from_col")
        lat_is_col = want_col
        wi = w_in[l].astype(BF16)
        pl_, xl = _win(xl, mods, l, g[1], wi, w_in.shape[-1], 0, perm)
        pc_, _ = _win(xc, mods, l, g[1], wi, lw if last else w_in.shape[-1], 1, None)

        cw, cb = lru_conv_w[l], lru_conv_b[l][None, :]
        h_lat = h_ctx = None
        for dr in range(2):
            wax = jnp.concatenate([lru_wa[l, dr], lru_wx[l, dr]], axis=-1).astype(BF16)
            ba, bx, lam = lru_ba[l, dr][None, :], lru_bx[l, dr][None, :], lru_lam[l, dr][None, :]
            h_ctx, state = _lru_scan(pc_, cw, cb, wax, ba, bx, lam, zero_state, h_ctx, dr == 1)
            h_lat, _ = _lru_scan(pl_, cw, cb, wax, ba, bx, lam, state, h_lat, dr == 1)

        filt = (filt_w1[l], filt_b1[l], filt_w2[l], filt_b2[l], filt_w3[l], filt_b3[l], filt_w4[l],
                filt_b4[l], filt_freq[l])
        hcw, hcb, hbias = hy_conv_w[l], hy_conv_b[l][None, :], hy_bias[l][None, :]
        og = out_g[l][None, :]
        wo = w_out[l].astype(BF16)

        u_l, x0_l = _hypre(pl_, 2 * lw, hcw, hcb)
        kr, ki = _hyena_spectrum(*_hyena_filter(n_lat, *filt))
        y_hy = _hyena_conv(u_l, hbias, kr, ki)
        xl = _mixout(h_lat, pl_, y_hy, x0_l, og, wo, xl, mods, l, 0)
        xl = _ffn(xl, mods, 2, g[2], ffn_wg, ffn_wu, ffn_wd, l, 1, 0)

        if not last:
            u_c, x0_c = _hypre(pc_, 2 * lw, hcw, hcb)
            kr, ki = _hyena_spectrum(*_hyena_filter(n_ctx, *filt))
            y_hy = _hyena_conv(u_c, hbias, kr, ki)
            xc = _mixout(h_ctx, pc_, y_hy, x0_c, og, wo, xc, mods, l, 1)
            xc = _ffn(xc, mods, 2, g[2], ffn_wg, ffn_wu, ffn_wd, l, 1, 1)

    return _final_norm(xl, final_g[None, :], lat_is_col)[None]
```

```python
import functools
import math

import numpy as np
import jax
import jax.numpy as jnp
from jax import lax
from jax.experimental import pallas as pl
from jax.experimental.pallas import tpu as pltpu

F32 = jnp.float32
BF16 = jnp.bfloat16
EPS = 1e-6
LRU_C = 8.0
GRID_W = 64
HY_BANDS = 16
HY_MAX_DECAY = math.log(1e-2) / 0.3
HY_MIN_DECAY = math.log(1e-2) / 1.5

V7X_VMEM_LIMIT_BYTES = 58 * 1024 * 1024
FFT_B = 128
SUBLANES = 8


def _cparams(sem):
    return pltpu.CompilerParams(dimension_semantics=sem, vmem_limit_bytes=V7X_VMEM_LIMIT_BYTES)


def _rms(x):
    return x * lax.rsqrt(jnp.mean(x * x, axis=-1, keepdims=True) + EPS)


def _modulated(x, shift_ref, scale_ref, g_ref, stream):
    shift = shift_ref[stream:stream + 1, :]
    scale = scale_ref[stream:stream + 1, :]
    return _rms(x) * g_ref[...] * (1.0 + scale) + shift


LANES = 128
N_STREAMS = 2


def _ada_kernel(c_ref, w_ref, b_ref, o_ref):
    o_ref[...] = jnp.zeros_like(o_ref)
    for k in range(N_STREAMS):
        c = c_ref[k]
        s = c * jax.nn.sigmoid(c)
        for j in range(0, w_ref.shape[1], LANES):
            cols = slice(j, j + LANES)
            o_ref[k:k + 1, cols] = jnp.sum(w_ref[:, cols] * s, axis=0, keepdims=True) + b_ref[:, cols]


def _ada_mods(c, c_ctx, ada_w, ada_b):
    depth, d, nm = ada_w.shape
    tn = d
    cc = jnp.broadcast_to(jnp.stack([c[0], c_ctx])[:, :, None], (N_STREAMS, d, LANES))
    return pl.pallas_call(
        _ada_kernel,
        grid=(depth, nm // tn),
        in_specs=[pl.BlockSpec((N_STREAMS, d, LANES), lambda l, j: (0, 0, 0)),
                  pl.BlockSpec((None, d, tn), lambda l, j: (l, 0, j)),
                  pl.BlockSpec((None, 1, tn), lambda l, j: (l, 0, j))],
        out_specs=pl.BlockSpec((None, SUBLANES, tn), lambda l, j: (l, 0, j)),
        out_shape=jax.ShapeDtypeStruct((depth, SUBLANES, nm), F32),
        compiler_params=_cparams(("arbitrary", "arbitrary")),
        name="ada_mods",
    )(cc, ada_w, ada_b[:, None, :])


def _mod_spec(l, k, which, d, tn=None):
    chunk = 3 * k + which
    if tn is None:
        return pl.BlockSpec((None, SUBLANES, d), lambda *g: (l, 0, chunk))
    return pl.BlockSpec((None, SUBLANES, tn), lambda *g: (l, 0, chunk * (d // tn) + g[-1]))


NORM_CHUNK = 256


def _ffn_up_kernel(x_ref, shift_ref, scale_ref, g_ref, wg_ref, wu_ref, h_ref, u_scr, *, stream, tm):
    wg = wg_ref[...].astype(BF16)
    wu = wu_ref[...].astype(BF16)

    def swiglu_rows(u):
        g = jnp.dot(u, wg, preferred_element_type=F32)
        up = jnp.dot(u, wu, preferred_element_type=F32)
        return (g * jax.nn.sigmoid(g) * up).astype(BF16)

    @pl.when(pl.program_id(1) == 0)
    def _():
        for r in range(0, tm, NORM_CHUNK):
            rs = slice(r, min(r + NORM_CHUNK, tm))
            u = _modulated(x_ref[rs, :], shift_ref, scale_ref, g_ref, stream).astype(BF16)
            u_scr[rs, :] = u
            h_ref[rs, :] = swiglu_rows(u)

    @pl.when(pl.program_id(1) > 0)
    def _():
        h_ref[...] = swiglu_rows(u_scr[...])


def _ffn_down_kernel(h_ref, wd_ref, x_ref, gate_ref, o_ref, *, stream):
    acc = jnp.dot(h_ref[...], wd_ref[...].astype(BF16), preferred_element_type=F32)
    gate = gate_ref[stream:stream + 1, :]
    o_ref[...] = x_ref[...] + (0.5 * gate) * acc


def _ffn(x, mods, k, g, wg, wu, wd, l, j, stream):
    rows, d = x.shape
    dff = wg.shape[-1]
    tm = min(rows, 1024)
    tf = 512
    tn = 256
    h = pl.pallas_call(
        functools.partial(_ffn_up_kernel, stream=stream, tm=tm),
        grid=(rows // tm, dff // tf),
        in_specs=[pl.BlockSpec((tm, d), lambda i, f: (i, 0)),
                  _mod_spec(l, k, 0, d), _mod_spec(l, k, 1, d),
                  pl.BlockSpec((1, d), lambda i, f: (0, 0)),
                  pl.BlockSpec((None, None, d, tf), lambda i, f: (l, j, 0, f)),
                  pl.BlockSpec((None, None, d, tf), lambda i, f: (l, j, 0, f))],
        out_specs=pl.BlockSpec((tm, tf), lambda i, f: (i, f)),
        out_shape=jax.ShapeDtypeStruct((rows, dff), BF16),
        scratch_shapes=[pltpu.VMEM((tm, d), BF16)],
        compiler_params=_cparams(("arbitrary", "arbitrary")),
        name="ffn_up",
    )(x, mods, mods, g, wg, wu)
    return pl.pallas_call(
        functools.partial(_ffn_down_kernel, stream=stream),
        grid=(rows // tm, d // tn),
        in_specs=[pl.BlockSpec((tm, dff), lambda i, n: (i, 0)),
                  pl.BlockSpec((None, None, dff, tn), lambda i, n: (l, j, 0, n)),
                  pl.BlockSpec((tm, tn), lambda i, n: (i, n)),
                  _mod_spec(l, k, 2, d, tn)],
        out_specs=pl.BlockSpec((tm, tn), lambda i, n: (i, n)),
        out_shape=jax.ShapeDtypeStruct((rows, d), F32),
        compiler_params=_cparams(("arbitrary", "arbitrary")),
        name="ffn_down",
    )(h, wd, x, mods)


def _win_kernel(x_ref, shift_ref, scale_ref, g_ref, w_ref, p_ref, *rest, pieces, prow, d, stream, emit_x):
    if emit_x:
        xo_ref, u_scr = rest
    else:
        (u_scr,) = rest

    w = w_ref[...]

    @pl.when(pl.program_id(1) == 0)
    def _():
        tm = pieces * prow
        seg = min(prow, NORM_CHUNK)
        for c0 in range(0, tm, NORM_CHUNK):
            c1 = min(c0 + NORM_CHUNK, tm)
            for r0 in range(c0, c1, seg):
                q, r = divmod(r0, prow)
                xt = x_ref[r:r + seg, q * d:(q + 1) * d]
                u_scr[r0:r0 + seg, :] = _modulated(xt, shift_ref, scale_ref, g_ref, stream).astype(BF16)
                if emit_x:
                    xo_ref[r0:r0 + seg, :] = xt
            p_ref[c0:c1, :] = jnp.dot(u_scr[c0:c1, :], w, preferred_element_type=F32)

    @pl.when(pl.program_id(1) > 0)
    def _():
        p_ref[...] = jnp.dot(u_scr[...], w, preferred_element_type=F32)


def _win(x, mods, l, g, w_in_bf16, ncols, stream, perm):
    rows, d = x.shape
    tm = min(rows, 1024)
    tn = 1024 if perm is None else 512
    if perm is None:
        prow, pieces, xv = tm, 1, x
        x_spec = pl.BlockSpec((tm, d), lambda i, n: (i, 0))
    else:
        grid_h = rows // GRID_W
        prow = grid_h if perm == "to_col" else GRID_W
        pieces = tm // prow
        xv = x.reshape(prow, (rows // prow) * d)
        x_spec = pl.BlockSpec((prow, pieces * d), lambda i, n: (0, i))
    emit_x = perm is not None
    out_shape = [jax.ShapeDtypeStruct((rows, ncols), F32)]
    out_specs = [pl.BlockSpec((tm, tn), lambda i, n: (i, n))]
    if emit_x:
        out_shape.append(jax.ShapeDtypeStruct((rows, d), F32))
        out_specs.append(pl.BlockSpec((tm, d), lambda i, n: (i, 0)))
    res = pl.pallas_call(
        functools.partial(_win_kernel, pieces=pieces, prow=prow, d=d, stream=stream, emit_x=emit_x),
        grid=(rows // tm, ncols // tn),
        in_specs=[x_spec,
                  _mod_spec(l, 1, 0, d), _mod_spec(l, 1, 1, d),
                  pl.BlockSpec((1, d), lambda i, n: (0, 0)),
                  pl.BlockSpec((d, tn), lambda i, n: (0, n))],
        out_specs=out_specs,
        out_shape=out_shape,
        scratch_shapes=[pltpu.VMEM((tm, d), BF16)],
        compiler_params=_cparams(("arbitrary", "arbitrary")),
        name="mixer_in_proj",
    )(xv, mods, mods, g, w_in_bf16)
    return (res[0], res[1]) if emit_x else (res[0], x)


def _lru_kernel(cv_ref, wax_ref, ba_ref, bx_ref, lam_ref, h0_ref, *rest, t_blk, heads, reverse, add_prev):
    if add_prev:
        prev_ref, o_ref, hf_ref, a_scr, b_scr, carry = rest
    else:
        o_ref, hf_ref, a_scr, b_scr, carry = rest
    c = cv_ref.shape[1]
    hd = c // heads

    @pl.when(pl.program_id(0) == 0)
    def _():
        carry[...] = h0_ref[...]

    cv = cv_ref[...]
    z = -lam_ref[...]
    c_sp = LRU_C * (jnp.maximum(z, 0.0) + jnp.log(1.0 + jnp.exp(-jnp.abs(z))))
    cvb = cv.astype(BF16)
    for h in range(heads):
        sl = slice(h * hd, (h + 1) * hd)
        gx = jnp.dot(cvb[:, sl], wax_ref[h], preferred_element_type=F32)
        r = 0.5 + 0.5 * jnp.tanh(0.5 * (gx[:, :hd] + ba_ref[:, sl]))
        ig = 0.5 + 0.5 * jnp.tanh(0.5 * (gx[:, hd:] + bx_ref[:, sl]))
        a = jnp.exp(-c_sp[:, sl] * r)
        a_scr[:, sl] = a
        b_scr[:, sl] = jnp.sqrt(1.0 - a * a) * (ig * cv[:, sl])

    row = lax.broadcasted_iota(jnp.int32, (SUBLANES, c), 0)
    ngrp = t_blk // SUBLANES

    def group(gi, hc):
        g = (ngrp - 1 - gi) if reverse else gi
        r0 = pl.multiple_of(g * SUBLANES, SUBLANES)
        av = a_scr[pl.ds(r0, SUBLANES), :]
        bv = b_scr[pl.ds(r0, SUBLANES), :]
        for s in (1, 2, 4):
            sh = (SUBLANES - s) if reverse else s
            m = (row < SUBLANES - s) if reverse else (row >= s)
            a_s = pltpu.roll(av, sh, axis=0)
            b_s = pltpu.roll(bv, sh, axis=0)
            bv = jnp.where(m, av * b_s + bv, bv)
            av = jnp.where(m, av * a_s, av)
        hv = av * hc + bv
        if add_prev:
            o_ref[pl.ds(r0, SUBLANES), :] = hv + prev_ref[pl.ds(r0, SUBLANES), :]
        else:
            o_ref[pl.ds(r0, SUBLANES), :] = hv
        return hv[0:1, :] if reverse else hv[SUBLANES - 1:SUBLANES, :]

    hc = lax.fori_loop(0, ngrp, group, carry[...])
    carry[...] = hc
    hf_ref[...] = hc


def _lru_scan(cv, wax, ba, bx, lam, h0, prev, reverse):
    seq, c = cv.shape
    heads = wax.shape[0]
    t_blk = min(seq, 512)
    nblk = seq // t_blk

    def bi(j):
        return (nblk - 1 - j) if reverse else j

    row = pl.BlockSpec((1, c), lambda j: (0, 0))
    in_specs = [pl.BlockSpec((t_blk, c), lambda j: (bi(j), 0)),
                pl.BlockSpec(wax.shape, lambda j: (0, 0, 0)), row, row, row, row]
    args = [cv, wax, ba, bx, lam, h0]
    if prev is not None:
        in_specs.append(pl.BlockSpec((t_blk, c), lambda j: (bi(j), 0)))
        args.append(prev)
    return pl.pallas_call(
        functools.partial(_lru_kernel, t_blk=t_blk, heads=heads, reverse=reverse, add_prev=prev is not None),
        grid=(nblk,),
        in_specs=in_specs,
        out_specs=[pl.BlockSpec((t_blk, c), lambda j: (bi(j), 0)), row],
        out_shape=[jax.ShapeDtypeStruct((seq, c), F32), jax.ShapeDtypeStruct((1, c), F32)],
        scratch_shapes=[pltpu.VMEM((t_blk, c), F32), pltpu.VMEM((t_blk, c), F32), pltpu.VMEM((1, c), F32)],
        compiler_params=_cparams(("arbitrary",)),
        name="rglru_scan",
    )(*args)


def _short_conv_kernel(*refs, t_blk, nblk, c, with_hyena):
    n_in = 4 if with_hyena else 1
    ins = [refs[3 * k:3 * k + 3] for k in range(n_in)]
    lw_ref, lb_ref, hw_ref, hb_ref = refs[3 * n_in:3 * n_in + 4]
    outs = refs[3 * n_in + 4:-1]
    xs = refs[-1]
    blk = pl.program_id(0)
    zero8 = jnp.zeros((SUBLANES, c), F32)

    def conv(src, w_ref, b_ref, part):
        cur, prv, nxt = src
        xs[0:SUBLANES, :] = jnp.where(blk == 0, zero8, prv[...])
        xs[SUBLANES:SUBLANES + t_blk, :] = cur[...]
        xs[SUBLANES + t_blk:2 * SUBLANES + t_blk, :] = jnp.where(blk == nblk - 1, zero8, nxt[...])
        taps = w_ref.shape[0]
        left = (taps - 1) // 2
        sl = slice(part * c, (part + 1) * c)
        out = b_ref[:, sl]
        for k in range(taps):
            r0 = SUBLANES - left + k
            out = out + xs[r0:r0 + t_blk, :] * w_ref[k:k + 1, sl]
        return out

    outs[0][...] = conv(ins[0], lw_ref, lb_ref, 0)
    if with_hyena:
        outs[2][...] = conv(ins[1], hw_ref, hb_ref, 0)
        x1c = conv(ins[2], hw_ref, hb_ref, 1)
        outs[1][...] = conv(ins[3], hw_ref, hb_ref, 2) * x1c


def _short_convs(p, lru_w, lru_b, hy_w, hy_b, with_hyena):
    seq = p.shape[0]
    c = lru_w.shape[1]
    t_blk = min(seq, 512)
    nblk = seq // t_blk
    per = t_blk // SUBLANES
    nb8 = seq // SUBLANES
    in_specs, args = [], []
    for col in ([0, 2, 3, 4] if with_hyena else [0]):
        in_specs += [pl.BlockSpec((t_blk, c), lambda j, col=col: (j, col)),
                     pl.BlockSpec((SUBLANES, c), lambda j, col=col: (jnp.maximum(j * per - 1, 0), col)),
                     pl.BlockSpec((SUBLANES, c), lambda j, col=col: (jnp.minimum((j + 1) * per, nb8 - 1), col))]
        args += [p, p, p]
    full = lambda a: pl.BlockSpec(a.shape, lambda j: (0, 0))
    in_specs += [full(lru_w), full(lru_b), full(hy_w), full(hy_b)]
    args += [lru_w, lru_b, hy_w, hy_b]
    n_out = 3 if with_hyena else 1
    return pl.pallas_call(
        functools.partial(_short_conv_kernel, t_blk=t_blk, nblk=nblk, c=c, with_hyena=with_hyena),
        grid=(nblk,),
        in_specs=in_specs,
        out_specs=[pl.BlockSpec((t_blk, c), lambda j: (j, 0))] * n_out,
        out_shape=[jax.ShapeDtypeStruct((seq, c), F32)] * n_out,
        scratch_shapes=[pltpu.VMEM((t_blk + 2 * SUBLANES, c), F32)],
        compiler_params=_cparams(("arbitrary",)),
        name="mixer_short_convs",
    )(*args)


FILT_SLOT = 64


def _filt_kernel(w1_ref, b1_ref, w2_ref, b2_ref, w3_ref, b3_ref, w4a_ref, w4b_ref, b4_ref, fr_ref,
                 k_ref, nrm_ref, *, seq, t_blk, c):
    i = pl.program_id(0)
    hi = lax.Precision.HIGHEST
    lanes = 2 * FILT_SLOT
    th = t_blk // 2
    lane = lax.broadcasted_iota(jnp.int32, (1, lanes), 1)
    slot = lane & (FILT_SLOT - 1)
    pos_a = i * t_blk + lax.broadcasted_iota(jnp.int32, (th, 1), 0)
    pos = jnp.where(lane < FILT_SLOT, pos_a, pos_a + th)
    tf = jnp.where(pos < seq, pos, 2 * seq - pos).astype(F32)
    t = tf * (1.0 / (seq - 1))
    w = (2.0 * math.pi / seq) * tf
    is_cos = (slot >= 1) & (slot <= HY_BANDS)
    is_sin = (slot > HY_BANDS) & (slot <= 2 * HY_BANDS)
    band = jnp.where(is_cos, slot - 1, slot - 1 - HY_BANDS).astype(F32)
    f = 1e-4 + band * ((HY_BANDS - 1 - 1e-4) / (HY_BANDS - 1))
    phase = jnp.where(is_cos, 0.5 * math.pi, 0.0)
    sign = jnp.where(is_cos, 1.0, jnp.where(is_sin, -1.0, 0.0))
    z = jnp.where(slot == 0, t, sign * jnp.sin(w * f + phase))
    fr = fr_ref[...]
    h = jnp.sin(fr * (jnp.dot(z, w1_ref[...], precision=hi, preferred_element_type=F32) + b1_ref[...]))
    h = jnp.sin(fr * (jnp.dot(h, w2_ref[...], precision=hi, preferred_element_type=F32) + b2_ref[...]))
    h = jnp.sin(fr * (jnp.dot(h, w3_ref[...], precision=hi, preferred_element_type=F32) + b3_ref[...]))
    ch = lax.broadcasted_iota(jnp.int32, (1, c), 1).astype(F32)
    delta = jnp.abs(HY_MIN_DECAY + ch * ((HY_MAX_DECAY - HY_MIN_DECAY) / (c - 1)))
    part = jnp.zeros((1, c), F32)
    for half, w4_ref in enumerate((w4a_ref, w4b_ref)):
        pos_h = pos_a + half * th
        t_h = jnp.where(pos_h < seq, pos_h, 2 * seq - pos_h).astype(F32) * (1.0 / (seq - 1))
        k = jnp.dot(h, w4_ref[...], precision=hi, preferred_element_type=F32) + b4_ref[...]
        k = jnp.where(pos_h != seq, k * jnp.exp(-t_h * delta), 0.0)
        k_ref[half * th:(half + 1) * th, :] = k
        part = part + jnp.sum(jnp.abs(k), axis=0, keepdims=True)

    @pl.when(i == 0)
    def _():
        nrm_ref[...] = part

    @pl.when(i > 0)
    def _():
        nrm_ref[...] = nrm_ref[...] + part


def _slots(a, row_used):
    r, cdim = a.shape
    blk = jnp.pad(a, ((0, (FILT_SLOT if row_used else 1) - r), (0, FILT_SLOT - cdim)))
    if not row_used:
        return jnp.concatenate([blk, blk], axis=1)
    z = jnp.zeros_like(blk)
    return jnp.concatenate([jnp.concatenate([blk, z], axis=1), jnp.concatenate([z, blk], axis=1)], axis=0)


def _hyena_filter(seq, w1, b1, w2, b2, w3, b3, w4, b4, freq):
    c = w4.shape[1] // 2
    od = w2.shape[0]
    assert od <= FILT_SLOT and w1.shape[0] <= FILT_SLOT
    lanes = 2 * FILT_SLOT
    t_blk = min(seq, 1024)
    half_blocks = seq // t_blk
    full = lambda shape: pl.BlockSpec(shape, lambda i: (0,) * len(shape))
    sq, row = (lanes, lanes), (1, lanes)
    w4p = jnp.pad(w4, ((0, FILT_SLOT - od), (0, 0)))
    w4a = jnp.concatenate([w4p, jnp.zeros_like(w4p)], axis=0)
    w4b = jnp.concatenate([jnp.zeros_like(w4p), w4p], axis=0)
    w4_spec = pl.BlockSpec((lanes, c), lambda i: (0, i // half_blocks))
    return pl.pallas_call(
        functools.partial(_filt_kernel, seq=seq, t_blk=t_blk, c=c),
        grid=(2 * seq // t_blk,),
        in_specs=[full(sq), full(row), full(sq), full(row), full(sq), full(row), w4_spec, w4_spec,
                  pl.BlockSpec((1, c), lambda i: (0, i // half_blocks)),
                  full(row)],
        out_specs=[pl.BlockSpec((t_blk, c), lambda i: (i, 0)), pl.BlockSpec((1, c), lambda i: (0, 0))],
        out_shape=[jax.ShapeDtypeStruct((2 * seq, c), F32), jax.ShapeDtypeStruct((1, c), F32)],
        compiler_params=_cparams(("arbitrary",)),
        name="hyena_filter",
    )(_slots(w1, True), _slots(b1[None, :], False), _slots(w2, True), _slots(b2[None, :], False),
      _slots(w3, True), _slots(b3[None, :], False), w4a, w4b, b4[None, :], _slots(freq[None, :], False))


def _bitrev(p, bits):
    r = 0
    for k in range(bits):
        r = (r << 1) | ((p >> k) & 1)
    return r


def _real_block(z):
    return np.block([[z.real, -z.imag], [z.imag, z.real]])


def _fft_split(s2):
    n_lo = max(min(8, s2 // 2), 1)
    return s2 // n_lo, n_lo


@functools.lru_cache(maxsize=None)
def _fft_tables(s):
    s2 = s // 2
    n = s * FFT_B
    n_hi, n_lo = _fft_split(s2)
    bits_hi, bits_lo = n_hi.bit_length() - 1, n_lo.bit_length() - 1
    b = np.arange(FFT_B)
    dft = np.exp(-2j * np.pi * np.outer(b, b) / FFT_B)
    e_lo = [2 * n_hi * _bitrev(p, bits_lo) for p in range(n_lo)]
    g = [dft * np.exp(-2j * np.pi * b * e / n)[None, :] for e in e_lo]
    chunks = np.arange(FFT_B).reshape(-1, FFT_ROWS)
    order = np.concatenate([np.concatenate([c, c + FFT_B]) for c in chunks])
    lhs_f = np.stack([_real_block(x)[order] for x in g])
    lhs_i = np.stack([_real_block(np.conj(x).T)[order] for x in g])
    e_hi = np.array([[2 * _bitrev(p, bits_hi) + hf for p in range(n_hi)] for hf in range(2)])
    ang = 2.0 * np.pi * e_hi[:, :, None, None] * b[None, None, :, None] / n * np.ones((1, 1, 1, FFT_B))
    lvl = 2.0 * np.pi * np.arange(max(s2 // 2, 1)) / s2
    first = np.concatenate([np.zeros(s2), 2.0 * np.pi * np.arange(s2) / s])
    f32 = lambda x: np.asarray(x, np.float32)
    return dict(lhs_f=f32(lhs_f), lhs_i=f32(lhs_i), col_c=f32(np.cos(ang)), col_s=f32(np.sin(ang)),
                lvl_c=f32(np.cos(lvl)), lvl_s=f32(np.sin(lvl)), first_c=f32(np.cos(first)),
                first_s=f32(np.sin(first)))


FFT_ROWS = 32


def _cmul(xr, xi, c, s):
    return xr * c + xi * s, xi * c - xr * s


def _cmul_conj(xr, xi, c, s):
    return xr * c - xi * s, xi * c + xr * s


def _pass_plan(s2):
    levels = s2.bit_length() - 1
    plan, h = [], s2 // 2
    if levels % 2 == 1:
        plan.append((h, 2))
        h //= 2
    while h >= 1:
        plan.append((h, 4))
        h //= 4
    return plan


def _butterfly_pass(load, store, lc_ref, ls_ref, s2, h, radix, inverse):
    h2 = h // 2 if radix == 4 else h
    trivial = h2 == 1
    shift = h2.bit_length() - 1

    def group(idx, carry):
        g = idx >> shift
        i = idx & (h2 - 1)
        base = g * 2 * h + i
        blocks = [base + k * h2 for k in range(radix)]
        if not trivial:
            c1, s1 = lc_ref[i * (s2 // (2 * h))], ls_ref[i * (s2 // (2 * h))]
            if radix == 4:
                c2, s2_ = lc_ref[i * (s2 // h)], ls_ref[i * (s2 // h)]
        fwd_mul = (lambda xr, xi, c, s: (xr, xi)) if trivial else _cmul
        inv_mul = (lambda xr, xi, c, s: (xr, xi)) if trivial else _cmul_conj
        if trivial:
            c1 = s1 = c2 = s2_ = None
        for r in range(0, FFT_B, FFT_ROWS):
            x = [load(bk, r) for bk in blocks]
            if radix == 2:
                (ar, ai), (br, bi) = x
                if inverse:
                    br, bi = inv_mul(br, bi, c1, s1)
                    out = [(ar + br, ai + bi), (ar - br, ai - bi)]
                else:
                    out = [(ar + br, ai + bi), fwd_mul(ar - br, ai - bi, c1, s1)]
            elif inverse:
                (x0r, x0i), (x1r, x1i), (x2r, x2i), (x3r, x3i) = x
                x1r, x1i = inv_mul(x1r, x1i, c2, s2_)
                x3r, x3i = inv_mul(x3r, x3i, c2, s2_)
                t0r, t0i, t1r, t1i = x0r + x1r, x0i + x1i, x0r - x1r, x0i - x1i
                t2r, t2i = inv_mul(x2r + x3r, x2i + x3i, c1, s1)
                t3r, t3i = inv_mul(x2r - x3r, x2i - x3i, c1, s1)
                out = [(t0r + t2r, t0i + t2i), (t1r - t3i, t1i + t3r),
                       (t0r - t2r, t0i - t2i), (t1r + t3i, t1i - t3r)]
            else:
                (x0r, x0i), (x1r, x1i), (x2r, x2i), (x3r, x3i) = x
                t0r, t0i, t1r, t1i = x0r + x2r, x0i + x2i, x1r + x3r, x1i + x3i
                t2r, t2i = fwd_mul(x0r - x2r, x0i - x2i, c1, s1)
                t3r, t3i = fwd_mul(x1r - x3r, x1i - x3i, c1, s1)
                out = [(t0r + t1r, t0i + t1i), fwd_mul(t0r - t1r, t0i - t1i, c2, s2_),
                       (t2r + t3i, t2i - t3r), fwd_mul(t2r - t3i, t2i + t3r, c2, s2_)]
            for bk, (o_r, o_i) in zip(blocks, out):
                store(bk, r, o_r, o_i)
        return carry

    lax.fori_loop(0, s2 // radix, group, 0)


def _work_access(wr, wi):
    def rows(bk, r):
        return pl.ds(pl.multiple_of(bk * FFT_B + r, FFT_ROWS), FFT_ROWS)

    def load(bk, r):
        return wr[rows(bk, r), :], wi[rows(bk, r), :]

    def store(bk, r, re, im):
        wr[rows(bk, r), :] = re
        wi[rows(bk, r), :] = im

    return load, store


def _run_passes(plan, first_load, last_store, work, lc_ref, ls_ref, s2, inverse):
    w_load, w_store = work
    for k, (h, radix) in enumerate(plan):
        load = first_load if k == 0 else w_load
        store = last_store if k == len(plan) - 1 else w_store
        _butterfly_pass(load, store, lc_ref, ls_ref, s2, h, radix, inverse)


MID_UNROLL = 4


def _pair_blocks(pp, s2):
    n_hi, n_lo = _fft_split(s2)
    half = n_hi // 2
    p_lo = pp >> (half.bit_length() - 1)
    j = pp & (half - 1)
    return p_lo, [(2 * j + q, pl.multiple_of(((2 * j + q) * n_lo + p_lo) * FFT_B, FFT_B)) for q in range(2)]


def _for_pairs(s2, body):
    def step(pp, carry):
        body(pp)
        return carry
    lax.fori_loop(0, s2 // 2, step, 0, unroll=min(MID_UNROLL, s2 // 2))


def _pack_store(panel, cc_ref, cs_ref, s2):
    n_hi, n_lo = _fft_split(s2)
    lo_bits = n_lo.bit_length() - 1

    def store(bk, r, re, im):
        p_hi = bk >> lo_bits
        p_lo = bk & (n_lo - 1)
        pp = p_lo * (n_hi // 2) + (p_hi >> 1)
        q = p_hi & 1
        tr, ti = _cmul(re, im, cc_ref[p_hi, r:r + FFT_ROWS, :], cs_ref[p_hi, r:r + FFT_ROWS, :])
        panel[pp, q, r:r + FFT_ROWS, :] = tr.astype(BF16)
        panel[pp, q, FFT_B + r:FFT_B + r + FFT_ROWS, :] = ti.astype(BF16)

    return store


def _dft_pair(m_ref, panel, pp, p_lo):
    blk = FFT_B
    rhs = jnp.concatenate([panel[pp, 0], panel[pp, 1]], axis=1)
    out = jnp.dot(m_ref[p_lo], rhs, preferred_element_type=F32)
    for r in range(0, blk, FFT_ROWS):
        for q in range(2):
            yield (q, r, out[2 * r:2 * r + FFT_ROWS, q * blk:(q + 1) * blk],
                   out[2 * r + FFT_ROWS:2 * r + 2 * FFT_ROWS, q * blk:(q + 1) * blk])


def _spec_kernel(lo_ref, hi_ref, nrm_ref, fc_ref, fs_ref, lc_ref, ls_ref, cc_ref, cs_ref, mf_ref,
                 kr_ref, ki_ref, panel, *, s2):
    blk = FFT_B
    hf = pl.program_id(1)
    n = 2 * s2 * blk
    sign = (1 - 2 * hf).astype(F32)

    def first_load(bk, r):
        rows = pl.ds(pl.multiple_of(bk * blk + r, FFT_ROWS), FFT_ROWS)
        d = lo_ref[rows, :] + sign * hi_ref[rows, :]
        return d * fc_ref[hf * s2 + bk], -d * fs_ref[hf * s2 + bk]

    _run_passes(_pass_plan(s2), first_load, _pack_store(panel, cc_ref, cs_ref, s2),
                _work_access(kr_ref, ki_ref), lc_ref, ls_ref, s2, inverse=False)
    scale = 1.0 / (nrm_ref[...] * n)

    def dft(pp):
        p_lo, blocks = _pair_blocks(pp, s2)
        for q, r, yr, yi in _dft_pair(mf_ref, panel, pp, p_lo):
            rows = pl.ds(blocks[q][1] + r, FFT_ROWS)
            kr_ref[rows, :] = yr * scale
            ki_ref[rows, :] = yi * scale

    _for_pairs(s2, dft)


def _conv_kernel(u_ref, bias_ref, kr_ref, ki_ref, fc_ref, fs_ref, lc_ref, ls_ref, cc_ref, cs_ref,
                 mf_ref, mi_ref, y_ref, wr, wi, panel, *, s2):
    blk = FFT_B
    hf = pl.program_id(1)
    plan = _pass_plan(s2)
    work = _work_access(wr, wi)

    def first_load(bk, r):
        ub = u_ref[pl.ds(pl.multiple_of(bk * blk + r, FFT_ROWS), FFT_ROWS), :]
        return ub * fc_ref[hf * s2 + bk], -ub * fs_ref[hf * s2 + bk]

    _run_passes(plan, first_load, _pack_store(panel, cc_ref, cs_ref, s2), work, lc_ref, ls_ref, s2,
                inverse=False)

    def spectrum_product(pp):
        p_lo, blocks = _pair_blocks(pp, s2)
        for q, r, yr, yi in _dft_pair(mf_ref, panel, pp, p_lo):
            rows = pl.ds(blocks[q][1] + r, FFT_ROWS)
            kr, ki = kr_ref[rows, :], ki_ref[rows, :]
            panel[pp, q, r:r + FFT_ROWS, :] = (yr * kr - yi * ki).astype(BF16)
            panel[pp, q, blk + r:blk + r + FFT_ROWS, :] = (yr * ki + yi * kr).astype(BF16)

    _for_pairs(s2, spectrum_product)

    def inverse_dft(pp):
        p_lo, blocks = _pair_blocks(pp, s2)
        for q, r, er, ei in _dft_pair(mi_ref, panel, pp, p_lo):
            p_hi, r0 = blocks[q]
            rows = pl.ds(r0 + r, FFT_ROWS)
            wr[rows, :], wi[rows, :] = _cmul_conj(er, ei, cc_ref[p_hi, r:r + FFT_ROWS, :],
                                                  cs_ref[p_hi, r:r + FFT_ROWS, :])

    _for_pairs(s2, inverse_dft)

    def rows_of(bk, r):
        return pl.ds(pl.multiple_of(bk * blk + r, FFT_ROWS), FFT_ROWS)

    def store_even(bk, r, er, ei):
        y_ref[rows_of(bk, r), :] = er

    def store_odd(bk, r, er, ei):
        rows = rows_of(bk, r)
        val = er * fc_ref[s2 + bk] - ei * fs_ref[s2 + bk]
        y_ref[rows, :] = y_ref[rows, :] + u_ref[rows, :] * bias_ref[...] + val

    inv_plan = plan[::-1]
    if len(inv_plan) > 1:
        _run_passes(inv_plan[:-1], work[0], work[1], work, lc_ref, ls_ref, s2, inverse=True)
    h_last, radix_last = inv_plan[-1]
    for half, store in ((0, store_even), (1, store_odd)):
        @pl.when(hf == half)
        def _(store=store):
            _butterfly_pass(work[0], store, lc_ref, ls_ref, s2, h_last, radix_last, True)


def _smem_spec():
    return pl.BlockSpec(memory_space=pltpu.SMEM)


def _fft_table_specs(s2):
    n_hi, n_lo = _fft_split(s2)
    col = pl.BlockSpec((None, n_hi, FFT_B, FFT_B), lambda ct, hf: (hf, 0, 0, 0))
    lhs = pl.BlockSpec((n_lo, 2 * FFT_B, 2 * FFT_B), lambda ct, hf: (0, 0, 0))
    return [_smem_spec(), _smem_spec(), _smem_spec(), _smem_spec(), col, col], lhs


def _hyena_spectrum(kfull, nrm):
    n, c = kfull.shape
    seq = n // 2
    s2 = seq // FFT_B
    tb = _fft_tables(2 * s2)
    lanes = 128
    tab_specs, lhs_spec = _fft_table_specs(s2)
    return pl.pallas_call(
        functools.partial(_spec_kernel, s2=s2),
        grid=(c // lanes, 2),
        in_specs=[pl.BlockSpec((seq, lanes), lambda ct, hf: (0, ct)),
                  pl.BlockSpec((seq, lanes), lambda ct, hf: (1, ct)),
                  pl.BlockSpec((1, lanes), lambda ct, hf: (0, ct))] + tab_specs + [lhs_spec],
        out_specs=[pl.BlockSpec((None, seq, lanes), lambda ct, hf: (hf, 0, ct)),
                   pl.BlockSpec((None, seq, lanes), lambda ct, hf: (hf, 0, ct))],
        out_shape=[jax.ShapeDtypeStruct((2, seq, c), F32), jax.ShapeDtypeStruct((2, seq, c), F32)],
        scratch_shapes=[pltpu.VMEM((s2 // 2, 2, 2 * FFT_B, FFT_B), BF16)],
        compiler_params=_cparams(("arbitrary", "arbitrary")),
        name="hyena_spectrum",
    )(kfull, kfull, nrm, tb["first_c"], tb["first_s"], tb["lvl_c"], tb["lvl_s"],
      tb["col_c"], tb["col_s"], jnp.asarray(tb["lhs_f"], BF16))


def _hyena_conv(u, bias, kr, ki):
    seq, c = u.shape
    s2 = seq // FFT_B
    tb = _fft_tables(2 * s2)
    lanes = 128
    tab_specs, lhs_spec = _fft_table_specs(s2)
    return pl.pallas_call(
        functools.partial(_conv_kernel, s2=s2),
        grid=(c // lanes, 2),
        in_specs=[pl.BlockSpec((seq, lanes), lambda ct, hf: (0, ct)),
                  pl.BlockSpec((1, lanes), lambda ct, hf: (0, ct)),
                  pl.BlockSpec((None, seq, lanes), lambda ct, hf: (hf, 0, ct)),
                  pl.BlockSpec((None, seq, lanes), lambda ct, hf: (hf, 0, ct))]
                 + tab_specs + [lhs_spec, lhs_spec],
        out_specs=pl.BlockSpec((seq, lanes), lambda ct, hf: (0, ct)),
        out_shape=jax.ShapeDtypeStruct((seq, c), F32),
        scratch_shapes=[pltpu.VMEM((seq, lanes), F32), pltpu.VMEM((seq, lanes), F32),
                        pltpu.VMEM((s2 // 2, 2, 2 * FFT_B, FFT_B), BF16)],
        compiler_params=_cparams(("arbitrary", "arbitrary")),
        name="hyena_fft_conv",
    )(u, bias, kr, ki, tb["first_c"], tb["first_s"], tb["lvl_c"], tb["lvl_s"],
      tb["col_c"], tb["col_s"], jnp.asarray(tb["lhs_f"], BF16), jnp.asarray(tb["lhs_i"], BF16))


MIXOUT_CHUNK = 256


def _mixout_kernel(h_ref, yr_ref, hy_ref, x0_ref, og_ref, w_ref, x_ref, gate_ref, o_ref, *, stream, lw, tm):
    gate = gate_ref[stream:stream + 1, :]
    w = w_ref[...]
    for r in range(0, tm, MIXOUT_CHUNK):
        rs = slice(r, r + min(MIXOUT_CHUNK, tm))
        yr = yr_ref[rs, :]
        gelu = 0.5 * yr * (1.0 + jnp.tanh(math.sqrt(2.0 / math.pi) * (yr + 0.044715 * (yr * yr * yr))))
        lru = (_rms(h_ref[rs, :] * gelu) * og_ref[:, :lw]).astype(BF16)
        hy = (_rms(hy_ref[rs, :] * x0_ref[rs, :]) * og_ref[:, lw:]).astype(BF16)
        y = (jnp.dot(lru, w[:lw, :], preferred_element_type=F32)
             + jnp.dot(hy, w[lw:, :], preferred_element_type=F32))
        o_ref[rs, :] = x_ref[rs, :] + gate * y


def _mixout(h_lru, p, y_hy, x0c, og, w_out_bf16, x, mods, l, stream):
    rows, d = x.shape
    lw = h_lru.shape[1]
    dm = w_out_bf16.shape[0]
    tm = min(rows, 512)
    row_blk = lambda cols, j: pl.BlockSpec((tm, cols), lambda i: (i, j))
    return pl.pallas_call(
        functools.partial(_mixout_kernel, stream=stream, lw=lw, tm=tm),
        grid=(rows // tm,),
        in_specs=[row_blk(lw, 0), row_blk(lw, 1), row_blk(dm - lw, 0), row_blk(dm - lw, 0),
                  pl.BlockSpec((1, dm), lambda i: (0, 0)),
                  pl.BlockSpec((dm, d), lambda i: (0, 0)),
                  row_blk(d, 0),
                  _mod_spec(l, 1, 2, d)],
        out_specs=row_blk(d, 0),
        out_shape=jax.ShapeDtypeStruct((rows, d), F32),
        compiler_params=_cparams(("arbitrary",)),
        name="mixer_out_proj",
    )(h_lru, p, y_hy, x0c, og, w_out_bf16, x, mods)


def _final_kernel(x_ref, g_ref, o_ref, *, pieces, prow, d):
    for q in range(pieces):
        o_ref[q * prow:(q + 1) * prow, :] = _rms(x_ref[:, q * d:(q + 1) * d]) * g_ref[...]


def _final_norm(x, g, from_col):
    rows, d = x.shape
    tm = 1024
    if from_col:
        prow, pieces = GRID_W, tm // GRID_W
        xv = x.reshape(prow, (rows // prow) * d)
        x_spec = pl.BlockSpec((prow, pieces * d), lambda i: (0, i))
    else:
        prow, pieces, xv = tm, 1, x
        x_spec = pl.BlockSpec((tm, d), lambda i: (i, 0))
    return pl.pallas_call(
        functools.partial(_final_kernel, pieces=pieces, prow=prow, d=d),
        grid=(rows // tm,),
        in_specs=[x_spec, pl.BlockSpec((1, d), lambda i: (0, 0))],
        out_specs=pl.BlockSpec((tm, d), lambda i: (i, 0)),
        out_shape=jax.ShapeDtypeStruct((rows, d), F32),
        compiler_params=_cparams(("arbitrary",)),
        name="final_norm",
    )(xv, g)


def kernel(x, c, ctx, c_ctx, ada_w, ada_b, norm_g, ffn_wg, ffn_wu, ffn_wd, w_in, w_out, out_g, lru_conv_w, lru_conv_b, lru_wa, lru_ba, lru_wx, lru_bx, lru_lam, hy_conv_w, hy_conv_b, hy_bias, filt_w1, filt_b1, filt_w2, filt_b2, filt_w3, filt_b3, filt_w4, filt_b4, filt_freq, final_g):
    assert x.shape[0] == 1 and ctx.shape[0] == 1
    depth = ada_w.shape[0]
    d = x.shape[-1]
    lw = lru_conv_w.shape[-1]
    n_lat, n_ctx = x.shape[1], ctx.shape[1]
    xl, xc = x.reshape(n_lat, d), ctx.reshape(n_ctx, d)
    mods = _ada_mods(c, c_ctx, ada_w, ada_b)
    zero_state = jnp.zeros((1, lw), F32)
    lat_is_col = False
    for l in range(depth):
        last = l == depth - 1
        g = norm_g[l][:, None, :]

        xl = _ffn(xl, mods, 0, g[0], ffn_wg, ffn_wu, ffn_wd, l, 0, 0)
        xc = _ffn(xc, mods, 0, g[0], ffn_wg, ffn_wu, ffn_wd, l, 0, 1)

        want_col = l % 2 == 1
        perm = None if want_col == lat_is_col else ("to_col" if want_col else "from_col")
        lat_is_col = want_col
        wi = w_in[l].astype(BF16)
        pl_, xl = _win(xl, mods, l, g[1], wi, w_in.shape[-1], 0, perm)
        pc_, _ = _win(xc, mods, l, g[1], wi, lw if last else w_in.shape[-1], 1, None)

        convs = (lru_conv_w[l], lru_conv_b[l][None, :], hy_conv_w[l], hy_conv_b[l][None, :])
        cv_l, u_l, x0_l = _short_convs(pl_, *convs, True)
        cv_c, *hy_c = _short_convs(pc_, *convs, not last)

        h_lat = h_ctx = None
        for dr in range(2):
            wax = jnp.concatenate([lru_wa[l, dr], lru_wx[l, dr]], axis=-1).astype(BF16)
            ba, bx, lam = lru_ba[l, dr][None, :], lru_bx[l, dr][None, :], lru_lam[l, dr][None, :]
            h_ctx, state = _lru_scan(cv_c, wax, ba, bx, lam, zero_state, h_ctx, dr == 1)
            h_lat, _ = _lru_scan(cv_l, wax, ba, bx, lam, state, h_lat, dr == 1)

        filt = (filt_w1[l], filt_b1[l], filt_w2[l], filt_b2[l], filt_w3[l], filt_b3[l], filt_w4[l],
                filt_b4[l], filt_freq[l])
        hbias = hy_bias[l][None, :]
        og = out_g[l][None, :]
        wo = w_out[l].astype(BF16)

        kr, ki = _hyena_spectrum(*_hyena_filter(n_lat, *filt))
        y_hy = _hyena_conv(u_l, hbias, kr, ki)
        xl = _mixout(h_lat, pl_, y_hy, x0_l, og, wo, xl, mods, l, 0)
        xl = _ffn(xl, mods, 2, g[2], ffn_wg, ffn_wu, ffn_wd, l, 1, 0)

        if not last:
            u_c, x0_c = hy_c
            kr, ki = _hyena_spectrum(*_hyena_filter(n_ctx, *filt))
            y_hy = _hyena_conv(u_c, hbias, kr, ki)
            xc = _mixout(h_ctx, pc_, y_hy, x0_c, og, wo, xc, mods, l, 1)
            xc = _ffn(xc, mods, 2, g[2], ffn_wg, ffn_wu, ffn_wd, l, 1, 1)

    return _final_norm(xl, final_g[None, :], lat_is_col).reshape(x.shape)
```

```python
import functools
import math

import numpy as np
import jax
import jax.numpy as jnp
from jax import lax
from jax.experimental import pallas as pl
from jax.experimental.pallas import tpu as pltpu

F32 = jnp.float32
BF16 = jnp.bfloat16
EPS = 1e-6
LRU_C = 8.0
GRID_W = 64
HY_BANDS = 16
HY_MAX_DECAY = math.log(1e-2) / 0.3
HY_MIN_DECAY = math.log(1e-2) / 1.5

V7X_VMEM_LIMIT_BYTES = 58 * 1024 * 1024
FFT_B = 128
SUBLANES = 8


def _cparams(sem):
    return pltpu.CompilerParams(dimension_semantics=sem, vmem_limit_bytes=V7X_VMEM_LIMIT_BYTES)


def _rms(x):
    return x * lax.rsqrt(jnp.mean(x * x, axis=-1, keepdims=True) + EPS)


def _modulated(x, shift_ref, scale_ref, g_ref, stream):
    shift = shift_ref[stream:stream + 1, :]
    scale = scale_ref[stream:stream + 1, :]
    return _rms(x) * g_ref[...] * (1.0 + scale) + shift


LANES = 128
N_STREAMS = 2


def _ada_kernel(c_ref, w_ref, b_ref, o_ref):
    o_ref[...] = jnp.zeros_like(o_ref)
    for k in range(N_STREAMS):
        c = c_ref[k]
        s = c * jax.nn.sigmoid(c)
        for j in range(0, w_ref.shape[1], LANES):
            cols = slice(j, j + LANES)
            o_ref[k:k + 1, cols] = jnp.sum(w_ref[:, cols] * s, axis=0, keepdims=True) + b_ref[:, cols]


def _ada_mods(c, c_ctx, ada_w, ada_b):
    depth, d, nm = ada_w.shape
    tn = d
    cc = jnp.broadcast_to(jnp.stack([c[0], c_ctx])[:, :, None], (N_STREAMS, d, LANES))
    return pl.pallas_call(
        _ada_kernel,
        grid=(depth, nm // tn),
        in_specs=[pl.BlockSpec((N_STREAMS, d, LANES), lambda l, j: (0, 0, 0)),
                  pl.BlockSpec((None, d, tn), lambda l, j: (l, 0, j)),
                  pl.BlockSpec((None, 1, tn), lambda l, j: (l, 0, j))],
        out_specs=pl.BlockSpec((None, SUBLANES, tn), lambda l, j: (l, 0, j)),
        out_shape=jax.ShapeDtypeStruct((depth, SUBLANES, nm), F32),
        compiler_params=_cparams(("arbitrary", "arbitrary")),
        name="ada_mods",
    )(cc, ada_w, ada_b[:, None, :])


def _mod_spec(l, k, which, d, tn=None):
    chunk = 3 * k + which
    if tn is None:
        return pl.BlockSpec((None, SUBLANES, d), lambda *g: (l, 0, chunk))
    return pl.BlockSpec((None, SUBLANES, tn), lambda *g: (l, 0, chunk * (d // tn) + g[-1]))


NORM_CHUNK = 256


def _ffn_up_kernel(x_ref, shift_ref, scale_ref, g_ref, wg_ref, wu_ref, h_ref, u_scr, *, stream, tm):
    wg = wg_ref[...].astype(BF16)
    wu = wu_ref[...].astype(BF16)

    def swiglu_rows(u):
        g = jnp.dot(u, wg, preferred_element_type=F32)
        up = jnp.dot(u, wu, preferred_element_type=F32)
        return (g * jax.nn.sigmoid(g) * up).astype(BF16)

    @pl.when(pl.program_id(1) == 0)
    def _():
        for r in range(0, tm, NORM_CHUNK):
            rs = slice(r, min(r + NORM_CHUNK, tm))
            u = _modulated(x_ref[rs, :], shift_ref, scale_ref, g_ref, stream).astype(BF16)
            u_scr[rs, :] = u
            h_ref[rs, :] = swiglu_rows(u)

    @pl.when(pl.program_id(1) > 0)
    def _():
        h_ref[...] = swiglu_rows(u_scr[...])


def _ffn_down_kernel(h_ref, wd_ref, x_ref, gate_ref, o_ref, *, stream):
    acc = jnp.dot(h_ref[...], wd_ref[...].astype(BF16), preferred_element_type=F32)
    gate = gate_ref[stream:stream + 1, :]
    o_ref[...] = x_ref[...] + (0.5 * gate) * acc


def _ffn(x, mods, k, g, wg, wu, wd, l, j, stream):
    rows, d = x.shape
    dff = wg.shape[-1]
    tm = min(rows, 1024)
    tf = 512
    tn = 256
    h = pl.pallas_call(
        functools.partial(_ffn_up_kernel, stream=stream, tm=tm),
        grid=(rows // tm, dff // tf),
        in_specs=[pl.BlockSpec((tm, d), lambda i, f: (i, 0)),
                  _mod_spec(l, k, 0, d), _mod_spec(l, k, 1, d),
                  pl.BlockSpec((1, d), lambda i, f: (0, 0)),
                  pl.BlockSpec((None, None, d, tf), lambda i, f: (l, j, 0, f)),
                  pl.BlockSpec((None, None, d, tf), lambda i, f: (l, j, 0, f))],
        out_specs=pl.BlockSpec((tm, tf), lambda i, f: (i, f)),
        out_shape=jax.ShapeDtypeStruct((rows, dff), BF16),
        scratch_shapes=[pltpu.VMEM((tm, d), BF16)],
        compiler_params=_cparams(("arbitrary", "arbitrary")),
        name="ffn_up",
    )(x, mods, mods, g, wg, wu)
    return pl.pallas_call(
        functools.partial(_ffn_down_kernel, stream=stream),
        grid=(rows // tm, d // tn),
        in_specs=[pl.BlockSpec((tm, dff), lambda i, n: (i, 0)),
                  pl.BlockSpec((None, None, dff, tn), lambda i, n: (l, j, 0, n)),
                  pl.BlockSpec((tm, tn), lambda i, n: (i, n)),
                  _mod_spec(l, k, 2, d, tn)],
        out_specs=pl.BlockSpec((tm, tn), lambda i, n: (i, n)),
        out_shape=jax.ShapeDtypeStruct((rows, d), F32),
        compiler_params=_cparams(("arbitrary", "arbitrary")),
        name="ffn_down",
    )(h, wd, x, mods)


def _win_kernel(x_ref, shift_ref, scale_ref, g_ref, w_ref, p_ref, u_scr, *, stream, tm):
    w = w_ref[...]

    @pl.when(pl.program_id(1) == 0)
    def _():
        for r in range(0, tm, NORM_CHUNK):
            rs = slice(r, min(r + NORM_CHUNK, tm))
            u = _modulated(x_ref[rs, :], shift_ref, scale_ref, g_ref, stream).astype(BF16)
            u_scr[rs, :] = u
            p_ref[rs, :] = jnp.dot(u, w, preferred_element_type=F32)

    @pl.when(pl.program_id(1) > 0)
    def _():
        p_ref[...] = jnp.dot(u_scr[...], w, preferred_element_type=F32)


def _win(x, mods, l, g, w_in_bf16, ncols, stream):
    rows, d = x.shape
    tm = min(rows, 1024)
    tn = 1024
    return pl.pallas_call(
        functools.partial(_win_kernel, stream=stream, tm=tm),
        grid=(rows // tm, ncols // tn),
        in_specs=[pl.BlockSpec((tm, d), lambda i, n: (i, 0)),
                  _mod_spec(l, 1, 0, d), _mod_spec(l, 1, 1, d),
                  pl.BlockSpec((1, d), lambda i, n: (0, 0)),
                  pl.BlockSpec((None, d, tn), lambda i, n: (l, 0, n))],
        out_specs=pl.BlockSpec((tm, tn), lambda i, n: (i, n)),
        out_shape=jax.ShapeDtypeStruct((rows, ncols), F32),
        scratch_shapes=[pltpu.VMEM((tm, d), BF16)],
        compiler_params=_cparams(("arbitrary", "arbitrary")),
        name="mixer_in_proj",
    )(x, mods, mods, g, w_in_bf16)


def _to_col_major(x):
    rows, d = x.shape
    return x.reshape(rows // GRID_W, GRID_W, d).transpose(1, 0, 2).reshape(rows, d)


def _from_col_major(x):
    rows, d = x.shape
    return x.reshape(GRID_W, rows // GRID_W, d).transpose(1, 0, 2).reshape(rows, d)


def _lru_kernel(cv_ref, wax_ref, ba_ref, bx_ref, lam_ref, h0_ref, *rest, t_blk, heads, reverse, add_prev):
    if add_prev:
        prev_ref, o_ref, hf_ref, a_scr, b_scr, carry = rest
    else:
        o_ref, hf_ref, a_scr, b_scr, carry = rest
    c = cv_ref.shape[1]
    hd = c // heads

    @pl.when(pl.program_id(0) == 0)
    def _():
        carry[...] = h0_ref[...]

    cv = cv_ref[...]
    z = -lam_ref[...]
    c_sp = LRU_C * (jnp.maximum(z, 0.0) + jnp.log(1.0 + jnp.exp(-jnp.abs(z))))
    cvb = cv.astype(BF16)
    for h in range(heads):
        sl = slice(h * hd, (h + 1) * hd)
        gx = jnp.dot(cvb[:, sl], wax_ref[h], preferred_element_type=F32)
        r = 0.5 + 0.5 * jnp.tanh(0.5 * (gx[:, :hd] + ba_ref[:, sl]))
        ig = 0.5 + 0.5 * jnp.tanh(0.5 * (gx[:, hd:] + bx_ref[:, sl]))
        a = jnp.exp(-c_sp[:, sl] * r)
        a_scr[:, sl] = a
        b_scr[:, sl] = jnp.sqrt(1.0 - a * a) * (ig * cv[:, sl])

    row = lax.broadcasted_iota(jnp.int32, (SUBLANES, c), 0)
    ngrp = t_blk // SUBLANES

    def group(gi, hc):
        g = (ngrp - 1 - gi) if reverse else gi
        r0 = pl.multiple_of(g * SUBLANES, SUBLANES)
        av = a_scr[pl.ds(r0, SUBLANES), :]
        bv = b_scr[pl.ds(r0, SUBLANES), :]
        for s in (1, 2, 4):
            sh = (SUBLANES - s) if reverse else s
            m = (row < SUBLANES - s) if reverse else (row >= s)
            a_s = pltpu.roll(av, sh, axis=0)
            b_s = pltpu.roll(bv, sh, axis=0)
            bv = jnp.where(m, av * b_s + bv, bv)
            av = jnp.where(m, av * a_s, av)
        hv = av * hc + bv
        if add_prev:
            o_ref[pl.ds(r0, SUBLANES), :] = hv + prev_ref[pl.ds(r0, SUBLANES), :]
        else:
            o_ref[pl.ds(r0, SUBLANES), :] = hv
        return hv[0:1, :] if reverse else hv[SUBLANES - 1:SUBLANES, :]

    hc = lax.fori_loop(0, ngrp, group, carry[...])
    carry[...] = hc
    hf_ref[...] = hc


def _lru_scan(cv, wax, ba, bx, lam, h0, prev, reverse):
    seq, c = cv.shape
    heads = wax.shape[0]
    t_blk = min(seq, 512)
    nblk = seq // t_blk

    def bi(j):
        return (nblk - 1 - j) if reverse else j

    row = pl.BlockSpec((1, c), lambda j: (0, 0))
    in_specs = [pl.BlockSpec((t_blk, c), lambda j: (bi(j), 0)),
                pl.BlockSpec(wax.shape, lambda j: (0, 0, 0)), row, row, row, row]
    args = [cv, wax, ba, bx, lam, h0]
    if prev is not None:
        in_specs.append(pl.BlockSpec((t_blk, c), lambda j: (bi(j), 0)))
        args.append(prev)
    return pl.pallas_call(
        functools.partial(_lru_kernel, t_blk=t_blk, heads=heads, reverse=reverse, add_prev=prev is not None),
        grid=(nblk,),
        in_specs=in_specs,
        out_specs=[pl.BlockSpec((t_blk, c), lambda j: (bi(j), 0)), row],
        out_shape=[jax.ShapeDtypeStruct((seq, c), F32), jax.ShapeDtypeStruct((1, c), F32)],
        scratch_shapes=[pltpu.VMEM((t_blk, c), F32), pltpu.VMEM((t_blk, c), F32), pltpu.VMEM((1, c), F32)],
        compiler_params=_cparams(("arbitrary",)),
        name="rglru_scan",
    )(*args)


def _short_conv_kernel(*refs, t_blk, nblk, c, with_hyena):
    n_in = 4 if with_hyena else 1
    ins = [refs[3 * k:3 * k + 3] for k in range(n_in)]
    lw_ref, lb_ref, hw_ref, hb_ref = refs[3 * n_in:3 * n_in + 4]
    outs = refs[3 * n_in + 4:-1]
    xs = refs[-1]
    blk = pl.program_id(0)
    zero8 = jnp.zeros((SUBLANES, c), F32)

    def conv(src, w_ref, b_ref, part):
        cur, prv, nxt = src
        xs[0:SUBLANES, :] = jnp.where(blk == 0, zero8, prv[...])
        xs[SUBLANES:SUBLANES + t_blk, :] = cur[...]
        xs[SUBLANES + t_blk:2 * SUBLANES + t_blk, :] = jnp.where(blk == nblk - 1, zero8, nxt[...])
        taps = w_ref.shape[0]
        left = (taps - 1) // 2
        sl = slice(part * c, (part + 1) * c)
        out = b_ref[:, sl]
        for k in range(taps):
            r0 = SUBLANES - left + k
            out = out + xs[r0:r0 + t_blk, :] * w_ref[k:k + 1, sl]
        return out

    outs[0][...] = conv(ins[0], lw_ref, lb_ref, 0)
    if with_hyena:
        outs[2][...] = conv(ins[1], hw_ref, hb_ref, 0)
        x1c = conv(ins[2], hw_ref, hb_ref, 1)
        outs[1][...] = conv(ins[3], hw_ref, hb_ref, 2) * x1c


def _short_convs(p, lru_w, lru_b, hy_w, hy_b, with_hyena):
    seq = p.shape[0]
    c = lru_w.shape[1]
    t_blk = min(seq, 512)
    nblk = seq // t_blk
    per = t_blk // SUBLANES
    nb8 = seq // SUBLANES
    in_specs, args = [], []
    for col in ([0, 2, 3, 4] if with_hyena else [0]):
        in_specs += [pl.BlockSpec((t_blk, c), lambda j, col=col: (j, col)),
                     pl.BlockSpec((SUBLANES, c), lambda j, col=col: (jnp.maximum(j * per - 1, 0), col)),
                     pl.BlockSpec((SUBLANES, c), lambda j, col=col: (jnp.minimum((j + 1) * per, nb8 - 1), col))]
        args += [p, p, p]
    full = lambda a: pl.BlockSpec(a.shape, lambda j: (0, 0))
    in_specs += [full(lru_w), full(lru_b), full(hy_w), full(hy_b)]
    args += [lru_w, lru_b, hy_w, hy_b]
    n_out = 3 if with_hyena else 1
    return pl.pallas_call(
        functools.partial(_short_conv_kernel, t_blk=t_blk, nblk=nblk, c=c, with_hyena=with_hyena),
        grid=(nblk,),
        in_specs=in_specs,
        out_specs=[pl.BlockSpec((t_blk, c), lambda j: (j, 0))] * n_out,
        out_shape=[jax.ShapeDtypeStruct((seq, c), F32)] * n_out,
        scratch_shapes=[pltpu.VMEM((t_blk + 2 * SUBLANES, c), F32)],
        compiler_params=_cparams(("arbitrary",)),
        name="mixer_short_convs",
    )(*args)


FILT_SLOT = 64


def _filt_kernel(w1_ref, b1_ref, w2_ref, b2_ref, w3_ref, b3_ref, w4a_ref, w4b_ref, b4_ref, fr_ref,
                 k_ref, nrm_ref, *, seq, t_blk, c):
    i = pl.program_id(0)
    hi = lax.Precision.HIGHEST
    lanes = 2 * FILT_SLOT
    th = t_blk // 2
    lane = lax.broadcasted_iota(jnp.int32, (1, lanes), 1)
    slot = lane & (FILT_SLOT - 1)
    pos_a = i * t_blk + lax.broadcasted_iota(jnp.int32, (th, 1), 0)
    pos = jnp.where(lane < FILT_SLOT, pos_a, pos_a + th)
    tf = jnp.where(pos < seq, pos, 2 * seq - pos).astype(F32)
    t = tf * (1.0 / (seq - 1))
    w = (2.0 * math.pi / seq) * tf
    is_cos = (slot >= 1) & (slot <= HY_BANDS)
    is_sin = (slot > HY_BANDS) & (slot <= 2 * HY_BANDS)
    band = jnp.where(is_cos, slot - 1, slot - 1 - HY_BANDS).astype(F32)
    f = 1e-4 + band * ((HY_BANDS - 1 - 1e-4) / (HY_BANDS - 1))
    phase = jnp.where(is_cos, 0.5 * math.pi, 0.0)
    sign = jnp.where(is_cos, 1.0, jnp.where(is_sin, -1.0, 0.0))
    z = jnp.where(slot == 0, t, sign * jnp.sin(w * f + phase))
    fr = fr_ref[...]
    h = jnp.sin(fr * (jnp.dot(z, w1_ref[...], precision=hi, preferred_element_type=F32) + b1_ref[...]))
    h = jnp.sin(fr * (jnp.dot(h, w2_ref[...], precision=hi, preferred_element_type=F32) + b2_ref[...]))
    h = jnp.sin(fr * (jnp.dot(h, w3_ref[...], precision=hi, preferred_element_type=F32) + b3_ref[...]))
    ch = lax.broadcasted_iota(jnp.int32, (1, c), 1).astype(F32)
    delta = jnp.abs(HY_MIN_DECAY + ch * ((HY_MAX_DECAY - HY_MIN_DECAY) / (c - 1)))
    part = jnp.zeros((1, c), F32)
    for half, w4_ref in enumerate((w4a_ref, w4b_ref)):
        pos_h = pos_a + half * th
        t_h = jnp.where(pos_h < seq, pos_h, 2 * seq - pos_h).astype(F32) * (1.0 / (seq - 1))
        k = jnp.dot(h, w4_ref[...], precision=hi, preferred_element_type=F32) + b4_ref[...]
        k = jnp.where(pos_h != seq, k * jnp.exp(-t_h * delta), 0.0)
        k_ref[half * th:(half + 1) * th, :] = k
        part = part + jnp.sum(jnp.abs(k), axis=0, keepdims=True)

    @pl.when(i == 0)
    def _():
        nrm_ref[...] = part

    @pl.when(i > 0)
    def _():
        nrm_ref[...] = nrm_ref[...] + part


def _slots(a, row_used):
    r, cdim = a.shape
    blk = jnp.pad(a, ((0, (FILT_SLOT if row_used else 1) - r), (0, FILT_SLOT - cdim)))
    if not row_used:
        return jnp.concatenate([blk, blk], axis=1)
    z = jnp.zeros_like(blk)
    return jnp.concatenate([jnp.concatenate([blk, z], axis=1), jnp.concatenate([z, blk], axis=1)], axis=0)


def _hyena_filter(seq, w1, b1, w2, b2, w3, b3, w4, b4, freq):
    c = w4.shape[1] // 2
    od = w2.shape[0]
    assert od <= FILT_SLOT and w1.shape[0] <= FILT_SLOT
    lanes = 2 * FILT_SLOT
    t_blk = min(seq, 1024)
    half_blocks = seq // t_blk
    full = lambda shape: pl.BlockSpec(shape, lambda i: (0,) * len(shape))
    sq, row = (lanes, lanes), (1, lanes)
    w4p = jnp.pad(w4, ((0, FILT_SLOT - od), (0, 0)))
    w4a = jnp.concatenate([w4p, jnp.zeros_like(w4p)], axis=0)
    w4b = jnp.concatenate([jnp.zeros_like(w4p), w4p], axis=0)
    w4_spec = pl.BlockSpec((lanes, c), lambda i: (0, i // half_blocks))
    return pl.pallas_call(
        functools.partial(_filt_kernel, seq=seq, t_blk=t_blk, c=c),
        grid=(2 * seq // t_blk,),
        in_specs=[full(sq), full(row), full(sq), full(row), full(sq), full(row), w4_spec, w4_spec,
                  pl.BlockSpec((1, c), lambda i: (0, i // half_blocks)),
                  full(row)],
        out_specs=[pl.BlockSpec((t_blk, c), lambda i: (i, 0)), pl.BlockSpec((1, c), lambda i: (0, 0))],
        out_shape=[jax.ShapeDtypeStruct((2 * seq, c), F32), jax.ShapeDtypeStruct((1, c), F32)],
        compiler_params=_cparams(("arbitrary",)),
        name="hyena_filter",
    )(_slots(w1, True), _slots(b1[None, :], False), _slots(w2, True), _slots(b2[None, :], False),
      _slots(w3, True), _slots(b3[None, :], False), w4a, w4b, b4[None, :], _slots(freq[None, :], False))


def _bitrev(p, bits):
    r = 0
    for k in range(bits):
        r = (r << 1) | ((p >> k) & 1)
    return r


def _real_block(z):
    return np.block([[z.real, -z.imag], [z.imag, z.real]])


def _fft_split(s2):
    n_lo = max(min(8, s2 // 2), 1)
    return s2 // n_lo, n_lo


@functools.lru_cache(maxsize=None)
def _fft_tables(s):
    s2 = s // 2
    n = s * FFT_B
    n_hi, n_lo = _fft_split(s2)
    bits_hi, bits_lo = n_hi.bit_length() - 1, n_lo.bit_length() - 1
    b = np.arange(FFT_B)
    dft = np.exp(-2j * np.pi * np.outer(b, b) / FFT_B)
    e_lo = [2 * n_hi * _bitrev(p, bits_lo) for p in range(n_lo)]
    g = [dft * np.exp(-2j * np.pi * b * e / n)[None, :] for e in e_lo]
    chunks = np.arange(FFT_B).reshape(-1, FFT_ROWS)
    order = np.concatenate([np.concatenate([c, c + FFT_B]) for c in chunks])
    lhs_f = np.stack([_real_block(x)[order] for x in g])
    lhs_i = np.stack([_real_block(np.conj(x).T)[order] for x in g])
    e_hi = np.array([[2 * _bitrev(p, bits_hi) + hf for p in range(n_hi)] for hf in range(2)])
    ang = 2.0 * np.pi * e_hi[:, :, None, None] * b[None, None, :, None] / n * np.ones((1, 1, 1, FFT_B))
    lvl = 2.0 * np.pi * np.arange(max(s2 // 2, 1)) / s2
    first = np.concatenate([np.zeros(s2), 2.0 * np.pi * np.arange(s2) / s])
    f32 = lambda x: np.asarray(x, np.float32)
    return dict(lhs_f=f32(lhs_f), lhs_i=f32(lhs_i), col_c=f32(np.cos(ang)), col_s=f32(np.sin(ang)),
                lvl_c=f32(np.cos(lvl)), lvl_s=f32(np.sin(lvl)), first_c=f32(np.cos(first)),
                first_s=f32(np.sin(first)))


FFT_ROWS = 32


def _cmul(xr, xi, c, s):
    return xr * c + xi * s, xi * c - xr * s


def _cmul_conj(xr, xi, c, s):
    return xr * c - xi * s, xi * c + xr * s


def _pass_plan(s2):
    levels = s2.bit_length() - 1
    plan, h = [], s2 // 2
    if levels % 2 == 1:
        plan.append((h, 2))
        h //= 2
    while h >= 1:
        plan.append((h, 4))
        h //= 4
    return plan


def _butterfly_pass(load, store, lc_ref, ls_ref, s2, h, radix, inverse):
    h2 = h // 2 if radix == 4 else h
    trivial = h2 == 1
    shift = h2.bit_length() - 1

    def group(idx, carry):
        g = idx >> shift
        i = idx & (h2 - 1)
        base = g * 2 * h + i
        blocks = [base + k * h2 for k in range(radix)]
        if not trivial:
            c1, s1 = lc_ref[i * (s2 // (2 * h))], ls_ref[i * (s2 // (2 * h))]
            if radix == 4:
                c2, s2_ = lc_ref[i * (s2 // h)], ls_ref[i * (s2 // h)]
        fwd_mul = (lambda xr, xi, c, s: (xr, xi)) if trivial else _cmul
        inv_mul = (lambda xr, xi, c, s: (xr, xi)) if trivial else _cmul_conj
        if trivial:
            c1 = s1 = c2 = s2_ = None
        for r in range(0, FFT_B, FFT_ROWS):
            x = [load(bk, r) for bk in blocks]
            if radix == 2:
                (ar, ai), (br, bi) = x
                if inverse:
                    br, bi = inv_mul(br, bi, c1, s1)
                    out = [(ar + br, ai + bi), (ar - br, ai - bi)]
                else:
                    out = [(ar + br, ai + bi), fwd_mul(ar - br, ai - bi, c1, s1)]
            elif inverse:
                (x0r, x0i), (x1r, x1i), (x2r, x2i), (x3r, x3i) = x
                x1r, x1i = inv_mul(x1r, x1i, c2, s2_)
                x3r, x3i = inv_mul(x3r, x3i, c2, s2_)
                t0r, t0i, t1r, t1i = x0r + x1r, x0i + x1i, x0r - x1r, x0i - x1i
                t2r, t2i = inv_mul(x2r + x3r, x2i + x3i, c1, s1)
                t3r, t3i = inv_mul(x2r - x3r, x2i - x3i, c1, s1)
                out = [(t0r + t2r, t0i + t2i), (t1r - t3i, t1i + t3r),
                       (t0r - t2r, t0i - t2i), (t1r + t3i, t1i - t3r)]
            else:
                (x0r, x0i), (x1r, x1i), (x2r, x2i), (x3r, x3i) = x
                t0r, t0i, t1r, t1i = x0r + x2r, x0i + x2i, x1r + x3r, x1i + x3i
                t2r, t2i = fwd_mul(x0r - x2r, x0i - x2i, c1, s1)
                t3r, t3i = fwd_mul(x1r - x3r, x1i - x3i, c1, s1)
                out = [(t0r + t1r, t0i + t1i), fwd_mul(t0r - t1r, t0i - t1i, c2, s2_),
                       (t2r + t3i, t2i - t3r), fwd_mul(t2r - t3i, t2i + t3r, c2, s2_)]
            for bk, (o_r, o_i) in zip(blocks, out):
                store(bk, r, o_r, o_i)
        return carry

    lax.fori_loop(0, s2 // radix, group, 0)


def _work_access(wr, wi):
    def rows(bk, r):
        return pl.ds(pl.multiple_of(bk * FFT_B + r, FFT_ROWS), FFT_ROWS)

    def load(bk, r):
        return wr[rows(bk, r), :], wi[rows(bk, r), :]

    def store(bk, r, re, im):
        wr[rows(bk, r), :] = re
        wi[rows(bk, r), :] = im

    return load, store


def _run_passes(plan, first_load, last_store, work, lc_ref, ls_ref, s2, inverse):
    w_load, w_store = work
    for k, (h, radix) in enumerate(plan):
        load = first_load if k == 0 else w_load
        store = last_store if k == len(plan) - 1 else w_store
        _butterfly_pass(load, store, lc_ref, ls_ref, s2, h, radix, inverse)


MID_UNROLL = 4


def _pair_blocks(pp, s2):
    n_hi, n_lo = _fft_split(s2)
    half = n_hi // 2
    p_lo = pp >> (half.bit_length() - 1)
    j = pp & (half - 1)
    return p_lo, [(2 * j + q, pl.multiple_of(((2 * j + q) * n_lo + p_lo) * FFT_B, FFT_B)) for q in range(2)]


def _for_pairs(s2, body):
    def step(pp, carry):
        body(pp)
        return carry
    lax.fori_loop(0, s2 // 2, step, 0, unroll=min(MID_UNROLL, s2 // 2))


def _pack_store(panel, cc_ref, cs_ref, s2):
    n_hi, n_lo = _fft_split(s2)
    lo_bits = n_lo.bit_length() - 1

    def store(bk, r, re, im):
        p_hi = bk >> lo_bits
        p_lo = bk & (n_lo - 1)
        pp = p_lo * (n_hi // 2) + (p_hi >> 1)
        q = p_hi & 1
        tr, ti = _cmul(re, im, cc_ref[p_hi, r:r + FFT_ROWS, :], cs_ref[p_hi, r:r + FFT_ROWS, :])
        panel[pp, q, r:r + FFT_ROWS, :] = tr.astype(BF16)
        panel[pp, q, FFT_B + r:FFT_B + r + FFT_ROWS, :] = ti.astype(BF16)

    return store


def _dft_pair(m_ref, panel, pp, p_lo):
    blk = FFT_B
    rhs = jnp.concatenate([panel[pp, 0], panel[pp, 1]], axis=1)
    out = jnp.dot(m_ref[p_lo], rhs, preferred_element_type=F32)
    for r in range(0, blk, FFT_ROWS):
        for q in range(2):
            yield (q, r, out[2 * r:2 * r + FFT_ROWS, q * blk:(q + 1) * blk],
                   out[2 * r + FFT_ROWS:2 * r + 2 * FFT_ROWS, q * blk:(q + 1) * blk])


def _spec_kernel(lo_ref, hi_ref, nrm_ref, fc_ref, fs_ref, lc_ref, ls_ref, cc_ref, cs_ref, mf_ref,
                 kr_ref, ki_ref, panel, *, s2):
    blk = FFT_B
    hf = pl.program_id(1)
    n = 2 * s2 * blk
    sign = (1 - 2 * hf).astype(F32)

    def first_load(bk, r):
        rows = pl.ds(pl.multiple_of(bk * blk + r, FFT_ROWS), FFT_ROWS)
        d = lo_ref[rows, :] + sign * hi_ref[rows, :]
        return d * fc_ref[hf * s2 + bk], -d * fs_ref[hf * s2 + bk]

    _run_passes(_pass_plan(s2), first_load, _pack_store(panel, cc_ref, cs_ref, s2),
                _work_access(kr_ref, ki_ref), lc_ref, ls_ref, s2, inverse=False)
    scale = 1.0 / (nrm_ref[...] * n)

    def dft(pp):
        p_lo, blocks = _pair_blocks(pp, s2)
        for q, r, yr, yi in _dft_pair(mf_ref, panel, pp, p_lo):
            rows = pl.ds(blocks[q][1] + r, FFT_ROWS)
            kr_ref[rows, :] = yr * scale
            ki_ref[rows, :] = yi * scale

    _for_pairs(s2, dft)


def _conv_kernel(u_ref, bias_ref, kr_ref, ki_ref, fc_ref, fs_ref, lc_ref, ls_ref, cc_ref, cs_ref,
                 mf_ref, mi_ref, y_ref, wr, wi, panel, *, s2):
    blk = FFT_B
    hf = pl.program_id(1)
    plan = _pass_plan(s2)
    work = _work_access(wr, wi)

    def first_load(bk, r):
        ub = u_ref[pl.ds(pl.multiple_of(bk * blk + r, FFT_ROWS), FFT_ROWS), :]
        return ub * fc_ref[hf * s2 + bk], -ub * fs_ref[hf * s2 + bk]

    _run_passes(plan, first_load, _pack_store(panel, cc_ref, cs_ref, s2), work, lc_ref, ls_ref, s2,
                inverse=False)

    def spectrum_product(pp):
        p_lo, blocks = _pair_blocks(pp, s2)
        for q, r, yr, yi in _dft_pair(mf_ref, panel, pp, p_lo):
            rows = pl.ds(blocks[q][1] + r, FFT_ROWS)
            kr, ki = kr_ref[rows, :], ki_ref[rows, :]
            panel[pp, q, r:r + FFT_ROWS, :] = (yr * kr - yi * ki).astype(BF16)
            panel[pp, q, blk + r:blk + r + FFT_ROWS, :] = (yr * ki + yi * kr).astype(BF16)

    _for_pairs(s2, spectrum_product)

    def inverse_dft(pp):
        p_lo, blocks = _pair_blocks(pp, s2)
        for q, r, er, ei in _dft_pair(mi_ref, panel, pp, p_lo):
            p_hi, r0 = blocks[q]
            rows = pl.ds(r0 + r, FFT_ROWS)
            wr[rows, :], wi[rows, :] = _cmul_conj(er, ei, cc_ref[p_hi, r:r + FFT_ROWS, :],
                                                  cs_ref[p_hi, r:r + FFT_ROWS, :])

    _for_pairs(s2, inverse_dft)

    def rows_of(bk, r):
        return pl.ds(pl.multiple_of(bk * blk + r, FFT_ROWS), FFT_ROWS)

    def store_even(bk, r, er, ei):
        y_ref[rows_of(bk, r), :] = er

    def store_odd(bk, r, er, ei):
        rows = rows_of(bk, r)
        val = er * fc_ref[s2 + bk] - ei * fs_ref[s2 + bk]
        y_ref[rows, :] = y_ref[rows, :] + u_ref[rows, :] * bias_ref[...] + val

    inv_plan = plan[::-1]
    if len(inv_plan) > 1:
        _run_passes(inv_plan[:-1], work[0], work[1], work, lc_ref, ls_ref, s2, inverse=True)
    h_last, radix_last = inv_plan[-1]
    for half, store in ((0, store_even), (1, store_odd)):
        @pl.when(hf == half)
        def _(store=store):
            _butterfly_pass(work[0], store, lc_ref, ls_ref, s2, h_last, radix_last, True)


def _smem_spec():
    return pl.BlockSpec(memory_space=pltpu.SMEM)


def _fft_table_specs(s2):
    n_hi, n_lo = _fft_split(s2)
    col = pl.BlockSpec((None, n_hi, FFT_B, FFT_B), lambda ct, hf: (hf, 0, 0, 0))
    lhs = pl.BlockSpec((n_lo, 2 * FFT_B, 2 * FFT_B), lambda ct, hf: (0, 0, 0))
    return [_smem_spec(), _smem_spec(), _smem_spec(), _smem_spec(), col, col], lhs


def _hyena_spectrum(kfull, nrm):
    n, c = kfull.shape
    seq = n // 2
    s2 = seq // FFT_B
    tb = _fft_tables(2 * s2)
    lanes = 128
    tab_specs, lhs_spec = _fft_table_specs(s2)
    return pl.pallas_call(
        functools.partial(_spec_kernel, s2=s2),
        grid=(c // lanes, 2),
        in_specs=[pl.BlockSpec((seq, lanes), lambda ct, hf: (0, ct)),
                  pl.BlockSpec((seq, lanes), lambda ct, hf: (1, ct)),
                  pl.BlockSpec((1, lanes), lambda ct, hf: (0, ct))] + tab_specs + [lhs_spec],
        out_specs=[pl.BlockSpec((None, seq, lanes), lambda ct, hf: (hf, 0, ct)),
                   pl.BlockSpec((None, seq, lanes), lambda ct, hf: (hf, 0, ct))],
        out_shape=[jax.ShapeDtypeStruct((2, seq, c), F32), jax.ShapeDtypeStruct((2, seq, c), F32)],
        scratch_shapes=[pltpu.VMEM((s2 // 2, 2, 2 * FFT_B, FFT_B), BF16)],
        compiler_params=_cparams(("arbitrary", "arbitrary")),
        name="hyena_spectrum",
    )(kfull, kfull, nrm, tb["first_c"], tb["first_s"], tb["lvl_c"], tb["lvl_s"],
      tb["col_c"], tb["col_s"], jnp.asarray(tb["lhs_f"], BF16))


def _hyena_conv(u, bias, kr, ki):
    seq, c = u.shape
    s2 = seq // FFT_B
    tb = _fft_tables(2 * s2)
    lanes = 128
    tab_specs, lhs_spec = _fft_table_specs(s2)
    return pl.pallas_call(
        functools.partial(_conv_kernel, s2=s2),
        grid=(c // lanes, 2),
        in_specs=[pl.BlockSpec((seq, lanes), lambda ct, hf: (0, ct)),
                  pl.BlockSpec((1, lanes), lambda ct, hf: (0, ct)),
                  pl.BlockSpec((None, seq, lanes), lambda ct, hf: (hf, 0, ct)),
                  pl.BlockSpec((None, seq, lanes), lambda ct, hf: (hf, 0, ct))]
                 + tab_specs + [lhs_spec, lhs_spec],
        out_specs=pl.BlockSpec((seq, lanes), lambda ct, hf: (0, ct)),
        out_shape=jax.ShapeDtypeStruct((seq, c), F32),
        scratch_shapes=[pltpu.VMEM((seq, lanes), F32), pltpu.VMEM((seq, lanes), F32),
                        pltpu.VMEM((s2 // 2, 2, 2 * FFT_B, FFT_B), BF16)],
        compiler_params=_cparams(("arbitrary", "arbitrary")),
        name="hyena_fft_conv",
    )(u, bias, kr, ki, tb["first_c"], tb["first_s"], tb["lvl_c"], tb["lvl_s"],
      tb["col_c"], tb["col_s"], jnp.asarray(tb["lhs_f"], BF16), jnp.asarray(tb["lhs_i"], BF16))


MIXOUT_CHUNK = 256


def _mixout_kernel(h_ref, yr_ref, hy_ref, x0_ref, og_ref, w_ref, x_ref, gate_ref, o_ref, *, stream, lw, tm):
    gate = gate_ref[stream:stream + 1, :]
    w = w_ref[...]
    for r in range(0, tm, MIXOUT_CHUNK):
        rs = slice(r, r + min(MIXOUT_CHUNK, tm))
        yr = yr_ref[rs, :]
        gelu = 0.5 * yr * (1.0 + jnp.tanh(math.sqrt(2.0 / math.pi) * (yr + 0.044715 * (yr * yr * yr))))
        lru = (_rms(h_ref[rs, :] * gelu) * og_ref[:, :lw]).astype(BF16)
        hy = (_rms(hy_ref[rs, :] * x0_ref[rs, :]) * og_ref[:, lw:]).astype(BF16)
        y = (jnp.dot(lru, w[:lw, :], preferred_element_type=F32)
             + jnp.dot(hy, w[lw:, :], preferred_element_type=F32))
        o_ref[rs, :] = x_ref[rs, :] + gate * y


def _mixout(h_lru, p, y_hy, x0c, og, w_out_bf16, x, mods, l, stream):
    rows, d = x.shape
    lw = h_lru.shape[1]
    dm = w_out_bf16.shape[1]
    tm = min(rows, 512)
    row_blk = lambda cols, j: pl.BlockSpec((tm, cols), lambda i: (i, j))
    return pl.pallas_call(
        functools.partial(_mixout_kernel, stream=stream, lw=lw, tm=tm),
        grid=(rows // tm,),
        in_specs=[row_blk(lw, 0), row_blk(lw, 1), row_blk(dm - lw, 0), row_blk(dm - lw, 0),
                  pl.BlockSpec((1, dm), lambda i: (0, 0)),
                  pl.BlockSpec((None, dm, d), lambda i: (l, 0, 0)),
                  row_blk(d, 0),
                  _mod_spec(l, 1, 2, d)],
        out_specs=row_blk(d, 0),
        out_shape=jax.ShapeDtypeStruct((rows, d), F32),
        compiler_params=_cparams(("arbitrary",)),
        name="mixer_out_proj",
    )(h_lru, p, y_hy, x0c, og, w_out_bf16, x, mods)


def _final_kernel(x_ref, g_ref, o_ref):
    o_ref[...] = _rms(x_ref[...]) * g_ref[...]


def _final_norm(x, g):
    rows, d = x.shape
    tm = 1024
    return pl.pallas_call(
        _final_kernel,
        grid=(rows // tm,),
        in_specs=[pl.BlockSpec((tm, d), lambda i: (i, 0)), pl.BlockSpec((1, d), lambda i: (0, 0))],
        out_specs=pl.BlockSpec((tm, d), lambda i: (i, 0)),
        out_shape=jax.ShapeDtypeStruct((rows, d), F32),
        compiler_params=_cparams(("arbitrary",)),
        name="final_norm",
    )(x, g)


def kernel(x, c, ctx, c_ctx, ada_w, ada_b, norm_g, ffn_wg, ffn_wu, ffn_wd, w_in, w_out, out_g, lru_conv_w, lru_conv_b, lru_wa, lru_ba, lru_wx, lru_bx, lru_lam, hy_conv_w, hy_conv_b, hy_bias, filt_w1, filt_b1, filt_w2, filt_b2, filt_w3, filt_b3, filt_w4, filt_b4, filt_freq, final_g):
    assert x.shape[0] == 1 and ctx.shape[0] == 1
    depth = ada_w.shape[0]
    d = x.shape[-1]
    lw = lru_conv_w.shape[-1]
    n_lat, n_ctx = x.shape[1], ctx.shape[1]
    xl, xc = x.reshape(n_lat, d), ctx.reshape(n_ctx, d)
    mods = _ada_mods(c, c_ctx, ada_w, ada_b)
    w_in_bf16, w_out_bf16 = w_in.astype(BF16), w_out.astype(BF16)
    zero_state = jnp.zeros((1, lw), F32)
    lat_is_col = False
    for l in range(depth):
        last = l == depth - 1
        g = norm_g[l][:, None, :]

        xl = _ffn(xl, mods, 0, g[0], ffn_wg, ffn_wu, ffn_wd, l, 0, 0)
        xc = _ffn(xc, mods, 0, g[0], ffn_wg, ffn_wu, ffn_wd, l, 0, 1)

        want_col = l % 2 == 1
        if want_col != lat_is_col:
            xl = _to_col_major(xl) if want_col else _from_col_major(xl)
            lat_is_col = want_col
        pl_ = _win(xl, mods, l, g[1], w_in_bf16, w_in.shape[-1], 0)
        pc_ = _win(xc, mods, l, g[1], w_in_bf16, lw if last else w_in.shape[-1], 1)

        convs = (lru_conv_w[l], lru_conv_b[l][None, :], hy_conv_w[l], hy_conv_b[l][None, :])
        cv_l, u_l, x0_l = _short_convs(pl_, *convs, True)
        cv_c, *hy_c = _short_convs(pc_, *convs, not last)

        h_lat = h_ctx = None
        for dr in range(2):
            wax = jnp.concatenate([lru_wa[l, dr], lru_wx[l, dr]], axis=-1).astype(BF16)
            ba, bx, lam = lru_ba[l, dr][None, :], lru_bx[l, dr][None, :], lru_lam[l, dr][None, :]
            h_ctx, state = _lru_scan(cv_c, wax, ba, bx, lam, zero_state, h_ctx, dr == 1)
            h_lat, _ = _lru_scan(cv_l, wax, ba, bx, lam, state, h_lat, dr == 1)

        filt = (filt_w1[l], filt_b1[l], filt_w2[l], filt_b2[l], filt_w3[l], filt_b3[l], filt_w4[l],
                filt_b4[l], filt_freq[l])
        hbias = hy_bias[l][None, :]
        og = out_g[l][None, :]

        kr, ki = _hyena_spectrum(*_hyena_filter(n_lat, *filt))
        y_hy = _hyena_conv(u_l, hbias, kr, ki)
        xl = _mixout(h_lat, pl_, y_hy, x0_l, og, w_out_bf16, xl, mods, l, 0)
        xl = _ffn(xl, mods, 2, g[2], ffn_wg, ffn_wu, ffn_wd, l, 1, 0)

        if not last:
            u_c, x0_c = hy_c
            kr, ki = _hyena_spectrum(*_hyena_filter(n_ctx, *filt))
            y_hy = _hyena_conv(u_c, hbias, kr, ki)
            xc = _mixout(h_ctx, pc_, y_hy, x0_c, og, w_out_bf16, xc, mods, l, 1)
            xc = _ffn(xc, mods, 2, g[2], ffn_wg, ffn_wu, ffn_wd, l, 1, 1)

    if lat_is_col:
        xl = _from_col_major(xl)
    return _final_norm(xl, final_g[None, :]).reshape(x.shape)
```

```python
import functools
import math

import numpy as np
import jax
import jax.numpy as jnp
from jax import lax
from jax.experimental import pallas as pl
from jax.experimental.pallas import tpu as pltpu

F32 = jnp.float32
BF16 = jnp.bfloat16
EPS = 1e-6
LRU_C = 8.0
GRID_W = 64
HY_BANDS = 16
HY_MAX_DECAY = math.log(1e-2) / 0.3
HY_MIN_DECAY = math.log(1e-2) / 1.5

V7X_VMEM_LIMIT_BYTES = 58 * 1024 * 1024
FFT_B = 128
SUBLANES = 8


def _cparams(sem):
    return pltpu.CompilerParams(dimension_semantics=sem, vmem_limit_bytes=V7X_VMEM_LIMIT_BYTES)


def _rms(x):
    return x * lax.rsqrt(jnp.mean(x * x, axis=-1, keepdims=True) + EPS)


def _modulated(x, shift_ref, scale_ref, g_ref, stream):
    shift = shift_ref[stream:stream + 1, :]
    scale = scale_ref[stream:stream + 1, :]
    return _rms(x) * g_ref[...] * (1.0 + scale) + shift


LANES = 128
N_STREAMS = 2


def _ada_kernel(c_ref, w_ref, b_ref, o_ref):
    o_ref[...] = jnp.zeros_like(o_ref)
    for k in range(N_STREAMS):
        c = c_ref[k]
        s = c * jax.nn.sigmoid(c)
        for j in range(0, w_ref.shape[1], LANES):
            cols = slice(j, j + LANES)
            o_ref[k:k + 1, cols] = jnp.sum(w_ref[:, cols] * s, axis=0, keepdims=True) + b_ref[:, cols]


def _ada_mods(c, c_ctx, ada_w, ada_b):
    depth, d, nm = ada_w.shape
    tn = d
    cc = jnp.broadcast_to(jnp.stack([c[0], c_ctx])[:, :, None], (N_STREAMS, d, LANES))
    return pl.pallas_call(
        _ada_kernel,
        grid=(depth, nm // tn),
        in_specs=[pl.BlockSpec((N_STREAMS, d, LANES), lambda l, j: (0, 0, 0)),
                  pl.BlockSpec((None, d, tn), lambda l, j: (l, 0, j)),
                  pl.BlockSpec((None, 1, tn), lambda l, j: (l, 0, j))],
        out_specs=pl.BlockSpec((None, SUBLANES, tn), lambda l, j: (l, 0, j)),
        out_shape=jax.ShapeDtypeStruct((depth, SUBLANES, nm), F32),
        compiler_params=_cparams(("arbitrary", "arbitrary")),
        name="ada_mods",
    )(cc, ada_w, ada_b[:, None, :])


def _mod_spec(l, k, which, d, tn=None):
    chunk = 3 * k + which
    if tn is None:
        return pl.BlockSpec((None, SUBLANES, d), lambda *g: (l, 0, chunk))
    return pl.BlockSpec((None, SUBLANES, tn), lambda *g: (l, 0, chunk * (d // tn) + g[-1]))


NORM_CHUNK = 256
LAT, CTX = 0, 1


def _rider_spec(rows, tcol, ncol_steps):
    return pl.BlockSpec((rows, tcol), lambda i, c: (0, jnp.where(i == 0, c, ncol_steps - 1)))


def _ffn_up_kernel(*refs, tm, with_ctx):
    if with_ctx:
        x_ref, xc_ref, shift_ref, scale_ref, g_ref, wg_ref, wu_ref, h_ref, hc_ref, u_scr, uc_scr = refs
    else:
        x_ref, shift_ref, scale_ref, g_ref, wg_ref, wu_ref, h_ref, u_scr = refs
    i, f = pl.program_id(0), pl.program_id(1)
    wg = wg_ref[...].astype(BF16)
    wu = wu_ref[...].astype(BF16)

    def swiglu_rows(u):
        g = jnp.dot(u, wg, preferred_element_type=F32)
        up = jnp.dot(u, wu, preferred_element_type=F32)
        return (g * jax.nn.sigmoid(g) * up).astype(BF16)

    @pl.when(f == 0)
    def _():
        for r in range(0, tm, NORM_CHUNK):
            rs = slice(r, min(r + NORM_CHUNK, tm))
            u = _modulated(x_ref[rs, :], shift_ref, scale_ref, g_ref, LAT).astype(BF16)
            u_scr[rs, :] = u
            h_ref[rs, :] = swiglu_rows(u)

    @pl.when(f > 0)
    def _():
        h_ref[...] = swiglu_rows(u_scr[...])

    if with_ctx:
        @pl.when(i == 0)
        def _():
            @pl.when(f == 0)
            def _():
                uc_scr[...] = _modulated(xc_ref[...], shift_ref, scale_ref, g_ref, CTX).astype(BF16)

            hc_ref[...] = swiglu_rows(uc_scr[...])


def _ffn_down_kernel(*refs, with_ctx):
    if with_ctx:
        h_ref, hc_ref, wd_ref, x_ref, xc_ref, gate_ref, o_ref, oc_ref = refs
    else:
        h_ref, wd_ref, x_ref, gate_ref, o_ref = refs
    wd = wd_ref[...].astype(BF16)
    o_ref[...] = x_ref[...] + (0.5 * gate_ref[LAT:LAT + 1, :]) * jnp.dot(h_ref[...], wd,
                                                                          preferred_element_type=F32)
    if with_ctx:
        @pl.when(pl.program_id(0) == 0)
        def _():
            oc_ref[...] = xc_ref[...] + (0.5 * gate_ref[CTX:CTX + 1, :]) * jnp.dot(
                hc_ref[...], wd, preferred_element_type=F32)


def _ffn(x, xc, mods, k, g, wg, wu, wd, l, j):
    rows, d = x.shape
    dff = wg.shape[-1]
    tm, tf, tn = 1024, 512, 256
    nf, nn = dff // tf, d // tn
    with_ctx = xc is not None
    rc = xc.shape[0] if with_ctx else 0
    ctx_in = [xc] if with_ctx else []
    res = pl.pallas_call(
        functools.partial(_ffn_up_kernel, tm=tm, with_ctx=with_ctx),
        grid=(rows // tm, nf),
        in_specs=[pl.BlockSpec((tm, d), lambda i, f: (i, 0))]
                 + ([pl.BlockSpec((rc, d), lambda i, f: (0, 0))] if with_ctx else [])
                 + [_mod_spec(l, k, 0, d), _mod_spec(l, k, 1, d),
                    pl.BlockSpec((1, d), lambda i, f: (0, 0)),
                    pl.BlockSpec((None, None, d, tf), lambda i, f: (l, j, 0, f)),
                    pl.BlockSpec((None, None, d, tf), lambda i, f: (l, j, 0, f))],
        out_specs=[pl.BlockSpec((tm, tf), lambda i, f: (i, f))]
                  + ([_rider_spec(rc, tf, nf)] if with_ctx else []),
        out_shape=[jax.ShapeDtypeStruct((rows, dff), BF16)]
                  + ([jax.ShapeDtypeStruct((rc, dff), BF16)] if with_ctx else []),
        scratch_shapes=[pltpu.VMEM((tm, d), BF16)] + ([pltpu.VMEM((rc, d), BF16)] if with_ctx else []),
        compiler_params=_cparams(("arbitrary", "arbitrary")),
        name="ffn_up",
    )(x, *ctx_in, mods, mods, g, wg, wu)
    h, hc = (res[0], res[1]) if with_ctx else (res[0], None)
    res = pl.pallas_call(
        functools.partial(_ffn_down_kernel, with_ctx=with_ctx),
        grid=(rows // tm, nn),
        in_specs=[pl.BlockSpec((tm, dff), lambda i, n: (i, 0))]
                 + ([pl.BlockSpec((rc, dff), lambda i, n: (0, 0))] if with_ctx else [])
                 + [pl.BlockSpec((None, None, dff, tn), lambda i, n: (l, j, 0, n)),
                    pl.BlockSpec((tm, tn), lambda i, n: (i, n))]
                 + ([_rider_spec(rc, tn, nn)] if with_ctx else [])
                 + [_mod_spec(l, k, 2, d, tn)],
        out_specs=[pl.BlockSpec((tm, tn), lambda i, n: (i, n))]
                  + ([_rider_spec(rc, tn, nn)] if with_ctx else []),
        out_shape=[jax.ShapeDtypeStruct((rows, d), F32)]
                  + ([jax.ShapeDtypeStruct((rc, d), F32)] if with_ctx else []),
        compiler_params=_cparams(("arbitrary", "arbitrary")),
        name="ffn_down",
    )(h, *([hc] if with_ctx else []), wd, x, *ctx_in, mods)
    return (res[0], res[1]) if with_ctx else (res[0], None)


def _win_kernel(x_ref, xc_ref, shift_ref, scale_ref, g_ref, w_ref, p_ref, pc_ref, u_scr, uc_scr, *, tm):
    i, n = pl.program_id(0), pl.program_id(1)
    w = w_ref[...]

    @pl.when(n == 0)
    def _():
        for r in range(0, tm, NORM_CHUNK):
            rs = slice(r, min(r + NORM_CHUNK, tm))
            u = _modulated(x_ref[rs, :], shift_ref, scale_ref, g_ref, LAT).astype(BF16)
            u_scr[rs, :] = u
            p_ref[rs, :] = jnp.dot(u, w, preferred_element_type=F32)

    @pl.when(n > 0)
    def _():
        p_ref[...] = jnp.dot(u_scr[...], w, preferred_element_type=F32)

    @pl.when(i == 0)
    def _():
        @pl.when(n == 0)
        def _():
            uc_scr[...] = _modulated(xc_ref[...], shift_ref, scale_ref, g_ref, CTX).astype(BF16)

        pc_ref[...] = jnp.dot(uc_scr[...], w, preferred_element_type=F32)


def _win(x, xc, mods, l, g, w_in_bf16):
    rows, d = x.shape
    rc = xc.shape[0]
    ncols = w_in_bf16.shape[-1]
    tm, tn = 1024, 1024
    nn = ncols // tn
    return pl.pallas_call(
        functools.partial(_win_kernel, tm=tm),
        grid=(rows // tm, nn),
        in_specs=[pl.BlockSpec((tm, d), lambda i, n: (i, 0)),
                  pl.BlockSpec((rc, d), lambda i, n: (0, 0)),
                  _mod_spec(l, 1, 0, d), _mod_spec(l, 1, 1, d),
                  pl.BlockSpec((1, d), lambda i, n: (0, 0)),
                  pl.BlockSpec((None, d, tn), lambda i, n: (l, 0, n))],
        out_specs=[pl.BlockSpec((tm, tn), lambda i, n: (i, n)), _rider_spec(rc, tn, nn)],
        out_shape=[jax.ShapeDtypeStruct((rows, ncols), F32), jax.ShapeDtypeStruct((rc, ncols), F32)],
        scratch_shapes=[pltpu.VMEM((tm, d), BF16), pltpu.VMEM((rc, d), BF16)],
        compiler_params=_cparams(("arbitrary", "arbitrary")),
        name="mixer_in_proj",
    )(x, xc, mods, mods, g, w_in_bf16)


def _to_col_major(x):
    rows, d = x.shape
    return x.reshape(rows // GRID_W, GRID_W, d).transpose(1, 0, 2).reshape(rows, d)


def _from_col_major(x):
    rows, d = x.shape
    return x.reshape(GRID_W, rows // GRID_W, d).transpose(1, 0, 2).reshape(rows, d)


def _lru_kernel(cv_ref, wax_ref, ba_ref, bx_ref, lam_ref, h0_ref, *rest, t_blk, heads, reverse, add_prev):
    if add_prev:
        prev_ref, o_ref, hf_ref, a_scr, b_scr, carry = rest
    else:
        o_ref, hf_ref, a_scr, b_scr, carry = rest
    c = cv_ref.shape[1]
    hd = c // heads

    @pl.when(pl.program_id(0) == 0)
    def _():
        carry[...] = h0_ref[...]

    cv = cv_ref[...]
    z = -lam_ref[...]
    c_sp = LRU_C * (jnp.maximum(z, 0.0) + jnp.log(1.0 + jnp.exp(-jnp.abs(z))))
    cvb = cv.astype(BF16)
    for h in range(heads):
        sl = slice(h * hd, (h + 1) * hd)
        gx = jnp.dot(cvb[:, sl], wax_ref[h], preferred_element_type=F32)
        r = 0.5 + 0.5 * jnp.tanh(0.5 * (gx[:, :hd] + ba_ref[:, sl]))
        ig = 0.5 + 0.5 * jnp.tanh(0.5 * (gx[:, hd:] + bx_ref[:, sl]))
        a = jnp.exp(-c_sp[:, sl] * r)
        a_scr[:, sl] = a
        b_scr[:, sl] = jnp.sqrt(1.0 - a * a) * (ig * cv[:, sl])

    row = lax.broadcasted_iota(jnp.int32, (SUBLANES, c), 0)
    ngrp = t_blk // SUBLANES

    def group(gi, hc):
        g = (ngrp - 1 - gi) if reverse else gi
        r0 = pl.multiple_of(g * SUBLANES, SUBLANES)
        av = a_scr[pl.ds(r0, SUBLANES), :]
        bv = b_scr[pl.ds(r0, SUBLANES), :]
        for s in (1, 2, 4):
            sh = (SUBLANES - s) if reverse else s
            m = (row < SUBLANES - s) if reverse else (row >= s)
            a_s = pltpu.roll(av, sh, axis=0)
            b_s = pltpu.roll(bv, sh, axis=0)
            bv = jnp.where(m, av * b_s + bv, bv)
            av = jnp.where(m, av * a_s, av)
        hv = av * hc + bv
        if add_prev:
            o_ref[pl.ds(r0, SUBLANES), :] = hv + prev_ref[pl.ds(r0, SUBLANES), :]
        else:
            o_ref[pl.ds(r0, SUBLANES), :] = hv
        return hv[0:1, :] if reverse else hv[SUBLANES - 1:SUBLANES, :]

    hc = lax.fori_loop(0, ngrp, group, carry[...])
    carry[...] = hc
    hf_ref[...] = hc


def _lru_scan(cv, wax, ba, bx, lam, h0, prev, reverse):
    seq, c = cv.shape
    heads = wax.shape[0]
    t_blk = min(seq, 512)
    nblk = seq // t_blk

    def bi(j):
        return (nblk - 1 - j) if reverse else j

    row = pl.BlockSpec((1, c), lambda j: (0, 0))
    in_specs = [pl.BlockSpec((t_blk, c), lambda j: (bi(j), 0)),
                pl.BlockSpec(wax.shape, lambda j: (0, 0, 0)), row, row, row, row]
    args = [cv, wax, ba, bx, lam, h0]
    if prev is not None:
        in_specs.append(pl.BlockSpec((t_blk, c), lambda j: (bi(j), 0)))
        args.append(prev)
    return pl.pallas_call(
        functools.partial(_lru_kernel, t_blk=t_blk, heads=heads, reverse=reverse, add_prev=prev is not None),
        grid=(nblk,),
        in_specs=in_specs,
        out_specs=[pl.BlockSpec((t_blk, c), lambda j: (bi(j), 0)), row],
        out_shape=[jax.ShapeDtypeStruct((seq, c), F32), jax.ShapeDtypeStruct((1, c), F32)],
        scratch_shapes=[pltpu.VMEM((t_blk, c), F32), pltpu.VMEM((t_blk, c), F32), pltpu.VMEM((1, c), F32)],
        compiler_params=_cparams(("arbitrary",)),
        name="rglru_scan",
    )(*args)


def _short_conv_kernel(*refs, t_blk, nblk, c, with_hyena):
    n_in = 4 if with_hyena else 1
    ins = [refs[3 * k:3 * k + 3] for k in range(n_in)]
    lw_ref, lb_ref, hw_ref, hb_ref = refs[3 * n_in:3 * n_in + 4]
    outs = refs[3 * n_in + 4:-1]
    xs = refs[-1]
    blk = pl.program_id(0)
    zero8 = jnp.zeros((SUBLANES, c), F32)

    def conv(src, w_ref, b_ref, part):
        cur, prv, nxt = src
        xs[0:SUBLANES, :] = jnp.where(blk == 0, zero8, prv[...])
        xs[SUBLANES:SUBLANES + t_blk, :] = cur[...]
        xs[SUBLANES + t_blk:2 * SUBLANES + t_blk, :] = jnp.where(blk == nblk - 1, zero8, nxt[...])
        taps = w_ref.shape[0]
        left = (taps - 1) // 2
        sl = slice(part * c, (part + 1) * c)
        out = b_ref[:, sl]
        for k in range(taps):
            r0 = SUBLANES - left + k
            out = out + xs[r0:r0 + t_blk, :] * w_ref[k:k + 1, sl]
        return out

    outs[0][...] = conv(ins[0], lw_ref, lb_ref, 0)
    if with_hyena:
        outs[2][...] = conv(ins[1], hw_ref, hb_ref, 0)
        x1c = conv(ins[2], hw_ref, hb_ref, 1)
        outs[1][...] = conv(ins[3], hw_ref, hb_ref, 2) * x1c


def _short_convs(p, lru_w, lru_b, hy_w, hy_b, with_hyena):
    seq = p.shape[0]
    c = lru_w.shape[1]
    t_blk = min(seq, 512)
    nblk = seq // t_blk
    per = t_blk // SUBLANES
    nb8 = seq // SUBLANES
    in_specs, args = [], []
    for col in ([0, 2, 3, 4] if with_hyena else [0]):
        in_specs += [pl.BlockSpec((t_blk, c), lambda j, col=col: (j, col)),
                     pl.BlockSpec((SUBLANES, c), lambda j, col=col: (jnp.maximum(j * per - 1, 0), col)),
                     pl.BlockSpec((SUBLANES, c), lambda j, col=col: (jnp.minimum((j + 1) * per, nb8 - 1), col))]
        args += [p, p, p]
    full = lambda a: pl.BlockSpec(a.shape, lambda j: (0, 0))
    in_specs += [full(lru_w), full(lru_b), full(hy_w), full(hy_b)]
    args += [lru_w, lru_b, hy_w, hy_b]
    n_out = 3 if with_hyena else 1
    return pl.pallas_call(
        functools.partial(_short_conv_kernel, t_blk=t_blk, nblk=nblk, c=c, with_hyena=with_hyena),
        grid=(nblk,),
        in_specs=in_specs,
        out_specs=[pl.BlockSpec((t_blk, c), lambda j: (j, 0))] * n_out,
        out_shape=[jax.ShapeDtypeStruct((seq, c), F32)] * n_out,
        scratch_shapes=[pltpu.VMEM((t_blk + 2 * SUBLANES, c), F32)],
        compiler_params=_cparams(("arbitrary",)),
        name="mixer_short_convs",
    )(*args)


FILT_SLOT = 64


def _filt_kernel(w1_ref, b1_ref, w2_ref, b2_ref, w3_ref, b3_ref, w4a_ref, w4b_ref, b4_ref, fr_ref,
                 k_ref, nrm_ref, *, seq, t_blk, c):
    i = pl.program_id(0)
    hi = lax.Precision.HIGHEST
    lanes = 2 * FILT_SLOT
    th = t_blk // 2
    lane = lax.broadcasted_iota(jnp.int32, (1, lanes), 1)
    slot = lane & (FILT_SLOT - 1)
    pos_a = i * t_blk + lax.broadcasted_iota(jnp.int32, (th, 1), 0)
    pos = jnp.where(lane < FILT_SLOT, pos_a, pos_a + th)
    tf = jnp.where(pos < seq, pos, 2 * seq - pos).astype(F32)
    t = tf * (1.0 / (seq - 1))
    w = (2.0 * math.pi / seq) * tf
    is_cos = (slot >= 1) & (slot <= HY_BANDS)
    is_sin = (slot > HY_BANDS) & (slot <= 2 * HY_BANDS)
    band = jnp.where(is_cos, slot - 1, slot - 1 - HY_BANDS).astype(F32)
    f = 1e-4 + band * ((HY_BANDS - 1 - 1e-4) / (HY_BANDS - 1))
    phase = jnp.where(is_cos, 0.5 * math.pi, 0.0)
    sign = jnp.where(is_cos, 1.0, jnp.where(is_sin, -1.0, 0.0))
    z = jnp.where(slot == 0, t, sign * jnp.sin(w * f + phase))
    fr = fr_ref[...]
    h = jnp.sin(fr * (jnp.dot(z, w1_ref[...], precision=hi, preferred_element_type=F32) + b1_ref[...]))
    h = jnp.sin(fr * (jnp.dot(h, w2_ref[...], precision=hi, preferred_element_type=F32) + b2_ref[...]))
    h = jnp.sin(fr * (jnp.dot(h, w3_ref[...], precision=hi, preferred_element_type=F32) + b3_ref[...]))
    ch = lax.broadcasted_iota(jnp.int32, (1, c), 1).astype(F32)
    delta = jnp.abs(HY_MIN_DECAY + ch * ((HY_MAX_DECAY - HY_MIN_DECAY) / (c - 1)))
    part = jnp.zeros((1, c), F32)
    h_hi = h.astype(BF16)
    h_lo = (h - h_hi.astype(F32)).astype(BF16)
    for half, w4_ref in enumerate((w4a_ref, w4b_ref)):
        pos_h = pos_a + half * th
        t_h = jnp.where(pos_h < seq, pos_h, 2 * seq - pos_h).astype(F32) * (1.0 / (seq - 1))
        w_hi, w_lo = w4_ref[0], w4_ref[1]
        k = (jnp.dot(h_hi, w_hi, preferred_element_type=F32) + jnp.dot(h_lo, w_hi, preferred_element_type=F32)
             + jnp.dot(h_hi, w_lo, preferred_element_type=F32)) + b4_ref[...]
        k = jnp.where(pos_h != seq, k * jnp.exp(-t_h * delta), 0.0)
        k_ref[half * th:(half + 1) * th, :] = k
        part = part + jnp.sum(jnp.abs(k), axis=0, keepdims=True)

    @pl.when(i == 0)
    def _():
        nrm_ref[...] = part

    @pl.when(i > 0)
    def _():
        nrm_ref[...] = nrm_ref[...] + part


def _slots(a, row_used):
    r, cdim = a.shape
    blk = jnp.pad(a, ((0, (FILT_SLOT if row_used else 1) - r), (0, FILT_SLOT - cdim)))
    if not row_used:
        return jnp.concatenate([blk, blk], axis=1)
    z = jnp.zeros_like(blk)
    return jnp.concatenate([jnp.concatenate([blk, z], axis=1), jnp.concatenate([z, blk], axis=1)], axis=0)


def _hyena_filter(seq, w1, b1, w2, b2, w3, b3, w4, b4, freq):
    c = w4.shape[1] // 2
    od = w2.shape[0]
    assert od <= FILT_SLOT and w1.shape[0] <= FILT_SLOT
    lanes = 2 * FILT_SLOT
    t_blk = min(seq, 1024)
    half_blocks = seq // t_blk
    full = lambda shape: pl.BlockSpec(shape, lambda i: (0,) * len(shape))
    sq, row = (lanes, lanes), (1, lanes)
    w4p = jnp.pad(w4, ((0, FILT_SLOT - od), (0, 0)))
    w4_hi = w4p.astype(BF16)
    w4_lo = (w4p - w4_hi.astype(F32)).astype(BF16)
    w4s = jnp.stack([w4_hi, w4_lo])
    w4a = jnp.concatenate([w4s, jnp.zeros_like(w4s)], axis=1)
    w4b = jnp.concatenate([jnp.zeros_like(w4s), w4s], axis=1)
    w4_spec = pl.BlockSpec((2, lanes, c), lambda i: (0, 0, i // half_blocks))
    return pl.pallas_call(
        functools.partial(_filt_kernel, seq=seq, t_blk=t_blk, c=c),
        grid=(2 * seq // t_blk,),
        in_specs=[full(sq), full(row), full(sq), full(row), full(sq), full(row), w4_spec, w4_spec,
                  pl.BlockSpec((1, c), lambda i: (0, i // half_blocks)),
                  full(row)],
        out_specs=[pl.BlockSpec((t_blk, c), lambda i: (i, 0)), pl.BlockSpec((1, c), lambda i: (0, 0))],
        out_shape=[jax.ShapeDtypeStruct((2 * seq, c), F32), jax.ShapeDtypeStruct((1, c), F32)],
        compiler_params=_cparams(("arbitrary",)),
        name="hyena_filter",
    )(_slots(w1, True), _slots(b1[None, :], False), _slots(w2, True), _slots(b2[None, :], False),
      _slots(w3, True), _slots(b3[None, :], False), w4a, w4b, b4[None, :], _slots(freq[None, :], False))


def _bitrev(p, bits):
    r = 0
    for k in range(bits):
        r = (r << 1) | ((p >> k) & 1)
    return r


def _real_block(z):
    return np.block([[z.real, -z.imag], [z.imag, z.real]])


def _fft_split(s2):
    n_lo = max(min(8, s2 // 2), 1)
    return s2 // n_lo, n_lo


@functools.lru_cache(maxsize=None)
def _fft_tables(s):
    s2 = s // 2
    n = s * FFT_B
    n_hi, n_lo = _fft_split(s2)
    bits_hi, bits_lo = n_hi.bit_length() - 1, n_lo.bit_length() - 1
    b = np.arange(FFT_B)
    dft = np.exp(-2j * np.pi * np.outer(b, b) / FFT_B)
    e_lo = [2 * n_hi * _bitrev(p, bits_lo) for p in range(n_lo)]
    g = [dft * np.exp(-2j * np.pi * b * e / n)[None, :] for e in e_lo]
    chunks = np.arange(FFT_B).reshape(-1, FFT_ROWS)
    order = np.concatenate([np.concatenate([c, c + FFT_B]) for c in chunks])
    lhs_f = np.stack([_real_block(x)[order] for x in g])
    lhs_i = np.stack([_real_block(np.conj(x).T)[order] for x in g])
    e_hi = np.array([[2 * _bitrev(p, bits_hi) + hf for p in range(n_hi)] for hf in range(2)])
    ang = 2.0 * np.pi * e_hi[:, :, None, None] * b[None, None, :, None] / n * np.ones((1, 1, 1, FFT_B))
    lvl = 2.0 * np.pi * np.arange(s2) / s2
    first = np.concatenate([np.zeros(s2), 2.0 * np.pi * np.arange(s2) / s])
    f32 = lambda x: np.asarray(x, np.float32)
    return dict(lhs_f=f32(lhs_f), lhs_i=f32(lhs_i), col_c=f32(np.cos(ang)), col_s=f32(np.sin(ang)),
                lvl_c=f32(np.cos(lvl)), lvl_s=f32(np.sin(lvl)), first_c=f32(np.cos(first)),
                first_s=f32(np.sin(first)))


FFT_ROWS = 32


def _cmul(xr, xi, c, s):
    return xr * c + xi * s, xi * c - xr * s


def _cmul_conj(xr, xi, c, s):
    return xr * c - xi * s, xi * c + xr * s


def _pass_plan(s2):
    levels = s2.bit_length() - 1
    plan, h = [], s2 // 2
    if levels % 2 == 1:
        plan.append((h, 2))
        h //= 2
    while h >= 1:
        plan.append((h, 4))
        h //= 4
    return plan


def _butterfly_pass(load, store, lc_ref, ls_ref, s2, h, radix, inverse):
    h2 = h // 2 if radix == 4 else h
    trivial = h2 == 1
    shift = h2.bit_length() - 1

    def group(idx, carry):
        g = idx >> shift
        i = idx & (h2 - 1)
        base = g * 2 * h + i
        blocks = [base + k * h2 for k in range(radix)]
        c1 = s1 = c2 = s2_ = c3 = s3 = None
        if not trivial:
            t1 = i * (s2 // (2 * h))
            c1, s1 = lc_ref[t1], ls_ref[t1]
            if radix == 4:
                c2, s2_, c3, s3 = lc_ref[2 * t1], ls_ref[2 * t1], lc_ref[3 * t1], ls_ref[3 * t1]
        fwd_mul = (lambda xr, xi, c, s: (xr, xi)) if trivial else _cmul
        inv_mul = (lambda xr, xi, c, s: (xr, xi)) if trivial else _cmul_conj
        for r in range(0, FFT_B, FFT_ROWS):
            x = [load(bk, r) for bk in blocks]
            if radix == 2:
                (ar, ai), (br, bi) = x
                if inverse:
                    br, bi = inv_mul(br, bi, c1, s1)
                    out = [(ar + br, ai + bi), (ar - br, ai - bi)]
                else:
                    out = [(ar + br, ai + bi), fwd_mul(ar - br, ai - bi, c1, s1)]
            elif inverse:
                (x0r, x0i), (x1r, x1i), (x2r, x2i), (x3r, x3i) = x
                x1r, x1i = inv_mul(x1r, x1i, c2, s2_)
                x2r, x2i = inv_mul(x2r, x2i, c1, s1)
                x3r, x3i = inv_mul(x3r, x3i, c3, s3)
                t0r, t0i, t1r, t1i = x0r + x1r, x0i + x1i, x0r - x1r, x0i - x1i
                t2r, t2i, t3r, t3i = x2r + x3r, x2i + x3i, x2r - x3r, x2i - x3i
                out = [(t0r + t2r, t0i + t2i), (t1r - t3i, t1i + t3r),
                       (t0r - t2r, t0i - t2i), (t1r + t3i, t1i - t3r)]
            else:
                (x0r, x0i), (x1r, x1i), (x2r, x2i), (x3r, x3i) = x
                t0r, t0i, t1r, t1i = x0r + x2r, x0i + x2i, x1r + x3r, x1i + x3i
                t2r, t2i, t3r, t3i = x0r - x2r, x0i - x2i, x1r - x3r, x1i - x3i
                out = [(t0r + t1r, t0i + t1i), fwd_mul(t0r - t1r, t0i - t1i, c2, s2_),
                       fwd_mul(t2r + t3i, t2i - t3r, c1, s1), fwd_mul(t2r - t3i, t2i + t3r, c3, s3)]
            for bk, (o_r, o_i) in zip(blocks, out):
                store(bk, r, o_r, o_i)
        return carry

    lax.fori_loop(0, s2 // radix, group, 0)


def _work_access(wr, wi):
    def rows(bk, r):
        return pl.ds(pl.multiple_of(bk * FFT_B + r, FFT_ROWS), FFT_ROWS)

    def load(bk, r):
        return wr[rows(bk, r), :], wi[rows(bk, r), :]

    def store(bk, r, re, im):
        wr[rows(bk, r), :] = re
        wi[rows(bk, r), :] = im

    return load, store


def _run_passes(plan, first_load, last_store, work, lc_ref, ls_ref, s2, inverse):
    w_load, w_store = work
    for k, (h, radix) in enumerate(plan):
        load = first_load if k == 0 else w_load
        store = last_store if k == len(plan) - 1 else w_store
        _butterfly_pass(load, store, lc_ref, ls_ref, s2, h, radix, inverse)


MID_UNROLL = 4


def _pair_blocks(pp, s2):
    n_hi, n_lo = _fft_split(s2)
    half = n_hi // 2
    p_lo = pp >> (half.bit_length() - 1)
    j = pp & (half - 1)
    return p_lo, [(2 * j + q, pl.multiple_of(((2 * j + q) * n_lo + p_lo) * FFT_B, FFT_B)) for q in range(2)]


def _for_pairs(s2, body):
    def step(pp, carry):
        body(pp)
        return carry
    lax.fori_loop(0, s2 // 2, step, 0, unroll=min(MID_UNROLL, s2 // 2))


def _pack_store(panel, cc_ref, cs_ref, s2):
    n_hi, n_lo = _fft_split(s2)
    lo_bits = n_lo.bit_length() - 1

    def store(bk, r, re, im):
        p_hi = bk >> lo_bits
        p_lo = bk & (n_lo - 1)
        pp = p_lo * (n_hi // 2) + (p_hi >> 1)
        q = p_hi & 1
        tr, ti = _cmul(re, im, cc_ref[p_hi, r:r + FFT_ROWS, :], cs_ref[p_hi, r:r + FFT_ROWS, :])
        panel[pp, q, r:r + FFT_ROWS, :] = tr.astype(BF16)
        panel[pp, q, FFT_B + r:FFT_B + r + FFT_ROWS, :] = ti.astype(BF16)

    return store


def _dft_pair(m_ref, panel, pp, p_lo):
    blk = FFT_B
    rhs = jnp.concatenate([panel[pp, 0], panel[pp, 1]], axis=1)
    out = jnp.dot(m_ref[p_lo], rhs, preferred_element_type=F32)
    for r in range(0, blk, FFT_ROWS):
        for q in range(2):
            yield (q, r, out[2 * r:2 * r + FFT_ROWS, q * blk:(q + 1) * blk],
                   out[2 * r + FFT_ROWS:2 * r + 2 * FFT_ROWS, q * blk:(q + 1) * blk])


def _spec_kernel(lo_ref, hi_ref, nrm_ref, fc_ref, fs_ref, lc_ref, ls_ref, cc_ref, cs_ref, mf_ref,
                 kr_ref, ki_ref, panel, *, s2):
    blk = FFT_B
    hf = pl.program_id(1)
    n = 2 * s2 * blk
    sign = (1 - 2 * hf).astype(F32)

    def first_load(bk, r):
        rows = pl.ds(pl.multiple_of(bk * blk + r, FFT_ROWS), FFT_ROWS)
        d = lo_ref[rows, :] + sign * hi_ref[rows, :]
        return d * fc_ref[hf * s2 + bk], -d * fs_ref[hf * s2 + bk]

    _run_passes(_pass_plan(s2), first_load, _pack_store(panel, cc_ref, cs_ref, s2),
                _work_access(kr_ref, ki_ref), lc_ref, ls_ref, s2, inverse=False)
    scale = 1.0 / (nrm_ref[...] * n)

    def dft(pp):
        p_lo, blocks = _pair_blocks(pp, s2)
        for q, r, yr, yi in _dft_pair(mf_ref, panel, pp, p_lo):
            rows = pl.ds(blocks[q][1] + r, FFT_ROWS)
            kr_ref[rows, :] = yr * scale
            ki_ref[rows, :] = yi * scale

    _for_pairs(s2, dft)


def _conv_kernel(u_ref, bias_ref, kr_ref, ki_ref, fc_ref, fs_ref, lc_ref, ls_ref, cc_ref, cs_ref,
                 mf_ref, mi_ref, y_ref, wr, wi, panel, *, s2):
    blk = FFT_B
    hf = pl.program_id(1)
    plan = _pass_plan(s2)
    work = _work_access(wr, wi)

    def first_load(bk, r):
        ub = u_ref[pl.ds(pl.multiple_of(bk * blk + r, FFT_ROWS), FFT_ROWS), :]
        return ub * fc_ref[hf * s2 + bk], -ub * fs_ref[hf * s2 + bk]

    _run_passes(plan, first_load, _pack_store(panel, cc_ref, cs_ref, s2), work, lc_ref, ls_ref, s2,
                inverse=False)

    def spectrum_product(pp):
        p_lo, blocks = _pair_blocks(pp, s2)
        for q, r, yr, yi in _dft_pair(mf_ref, panel, pp, p_lo):
            rows = pl.ds(blocks[q][1] + r, FFT_ROWS)
            kr, ki = kr_ref[rows, :], ki_ref[rows, :]
            panel[pp, q, r:r + FFT_ROWS, :] = (yr * kr - yi * ki).astype(BF16)
            panel[pp, q, blk + r:blk + r + FFT_ROWS, :] = (yr * ki + yi * kr).astype(BF16)

    _for_pairs(s2, spectrum_product)

    def inverse_dft(pp):
        p_lo, blocks = _pair_blocks(pp, s2)
        for q, r, er, ei in _dft_pair(mi_ref, panel, pp, p_lo):
            p_hi, r0 = blocks[q]
            rows = pl.ds(r0 + r, FFT_ROWS)
            wr[rows, :], wi[rows, :] = _cmul_conj(er, ei, cc_ref[p_hi, r:r + FFT_ROWS, :],
                                                  cs_ref[p_hi, r:r + FFT_ROWS, :])

    _for_pairs(s2, inverse_dft)

    def rows_of(bk, r):
        return pl.ds(pl.multiple_of(bk * blk + r, FFT_ROWS), FFT_ROWS)

    def store_even(bk, r, er, ei):
        y_ref[rows_of(bk, r), :] = er

    def store_odd(bk, r, er, ei):
        rows = rows_of(bk, r)
        val = er * fc_ref[s2 + bk] - ei * fs_ref[s2 + bk]
        y_ref[rows, :] = y_ref[rows, :] + u_ref[rows, :] * bias_ref[...] + val

    inv_plan = plan[::-1]
    if len(inv_plan) > 1:
        _run_passes(inv_plan[:-1], work[0], work[1], work, lc_ref, ls_ref, s2, inverse=True)
    h_last, radix_last = inv_plan[-1]
    for half, store in ((0, store_even), (1, store_odd)):
        @pl.when(hf == half)
        def _(store=store):
            _butterfly_pass(work[0], store, lc_ref, ls_ref, s2, h_last, radix_last, True)


def _smem_spec():
    return pl.BlockSpec(memory_space=pltpu.SMEM)


def _fft_table_specs(s2):
    n_hi, n_lo = _fft_split(s2)
    col = pl.BlockSpec((None, n_hi, FFT_B, FFT_B), lambda ct, hf: (hf, 0, 0, 0))
    lhs = pl.BlockSpec((n_lo, 2 * FFT_B, 2 * FFT_B), lambda ct, hf: (0, 0, 0))
    return [_smem_spec(), _smem_spec(), _smem_spec(), _smem_spec(), col, col], lhs


def _hyena_spectrum(kfull, nrm):
    n, c = kfull.shape
    seq = n // 2
    s2 = seq // FFT_B
    tb = _fft_tables(2 * s2)
    lanes = 128
    tab_specs, lhs_spec = _fft_table_specs(s2)
    return pl.pallas_call(
        functools.partial(_spec_kernel, s2=s2),
        grid=(c // lanes, 2),
        in_specs=[pl.BlockSpec((seq, lanes), lambda ct, hf: (0, ct)),
                  pl.BlockSpec((seq, lanes), lambda ct, hf: (1, ct)),
                  pl.BlockSpec((1, lanes), lambda ct, hf: (0, ct))] + tab_specs + [lhs_spec],
        out_specs=[pl.BlockSpec((None, seq, lanes), lambda ct, hf: (hf, 0, ct)),
                   pl.BlockSpec((None, seq, lanes), lambda ct, hf: (hf, 0, ct))],
        out_shape=[jax.ShapeDtypeStruct((2, seq, c), F32), jax.ShapeDtypeStruct((2, seq, c), F32)],
        scratch_shapes=[pltpu.VMEM((s2 // 2, 2, 2 * FFT_B, FFT_B), BF16)],
        compiler_params=_cparams(("arbitrary", "arbitrary")),
        name="hyena_spectrum",
    )(kfull, kfull, nrm, tb["first_c"], tb["first_s"], tb["lvl_c"], tb["lvl_s"],
      tb["col_c"], tb["col_s"], jnp.asarray(tb["lhs_f"], BF16))


def _hyena_conv(u, bias, kr, ki):
    seq, c = u.shape
    s2 = seq // FFT_B
    tb = _fft_tables(2 * s2)
    lanes = 128
    tab_specs, lhs_spec = _fft_table_specs(s2)
    return pl.pallas_call(
        functools.partial(_conv_kernel, s2=s2),
        grid=(c // lanes, 2),
        in_specs=[pl.BlockSpec((seq, lanes), lambda ct, hf: (0, ct)),
                  pl.BlockSpec((1, lanes), lambda ct, hf: (0, ct)),
                  pl.BlockSpec((None, seq, lanes), lambda ct, hf: (hf, 0, ct)),
                  pl.BlockSpec((None, seq, lanes), lambda ct, hf: (hf, 0, ct))]
                 + tab_specs + [lhs_spec, lhs_spec],
        out_specs=pl.BlockSpec((seq, lanes), lambda ct, hf: (0, ct)),
        out_shape=jax.ShapeDtypeStruct((seq, c), F32),
        scratch_shapes=[pltpu.VMEM((seq, lanes), F32), pltpu.VMEM((seq, lanes), F32),
                        pltpu.VMEM((s2 // 2, 2, 2 * FFT_B, FFT_B), BF16)],
        compiler_params=_cparams(("arbitrary", "arbitrary")),
        name="hyena_fft_conv",
    )(u, bias, kr, ki, tb["first_c"], tb["first_s"], tb["lvl_c"], tb["lvl_s"],
      tb["col_c"], tb["col_s"], jnp.asarray(tb["lhs_f"], BF16), jnp.asarray(tb["lhs_i"], BF16))


MIXOUT_CHUNK = 256


def _mixout_kernel(h_ref, yr_ref, hy_ref, x0_ref, og_ref, w_ref, x_ref, gate_ref, o_ref, *, stream, lw, tm):
    gate = gate_ref[stream:stream + 1, :]
    w = w_ref[...]
    for r in range(0, tm, MIXOUT_CHUNK):
        rs = slice(r, r + min(MIXOUT_CHUNK, tm))
        yr = yr_ref[rs, :]
        gelu = 0.5 * yr * (1.0 + jnp.tanh(math.sqrt(2.0 / math.pi) * (yr + 0.044715 * (yr * yr * yr))))
        lru = (_rms(h_ref[rs, :] * gelu) * og_ref[:, :lw]).astype(BF16)
        hy = (_rms(hy_ref[rs, :] * x0_ref[rs, :]) * og_ref[:, lw:]).astype(BF16)
        y = (jnp.dot(lru, w[:lw, :], preferred_element_type=F32)
             + jnp.dot(hy, w[lw:, :], preferred_element_type=F32))
        o_ref[rs, :] = x_ref[rs, :] + gate * y


def _mixout(h_lru, p, y_hy, x0c, og, w_out_bf16, x, mods, l, stream):
    rows, d = x.shape
    lw = h_lru.shape[1]
    dm = w_out_bf16.shape[1]
    tm = min(rows, 512)
    row_blk = lambda cols, j: pl.BlockSpec((tm, cols), lambda i: (i, j))
    return pl.pallas_call(
        functools.partial(_mixout_kernel, stream=stream, lw=lw, tm=tm),
        grid=(rows // tm,),
        in_specs=[row_blk(lw, 0), row_blk(lw, 1), row_blk(dm - lw, 0), row_blk(dm - lw, 0),
                  pl.BlockSpec((1, dm), lambda i: (0, 0)),
                  pl.BlockSpec((None, dm, d), lambda i: (l, 0, 0)),
                  row_blk(d, 0),
                  _mod_spec(l, 1, 2, d)],
        out_specs=row_blk(d, 0),
        out_shape=jax.ShapeDtypeStruct((rows, d), F32),
        compiler_params=_cparams(("arbitrary",)),
        name="mixer_out_proj",
    )(h_lru, p, y_hy, x0c, og, w_out_bf16, x, mods)


def _final_kernel(x_ref, g_ref, o_ref):
    o_ref[...] = _rms(x_ref[...]) * g_ref[...]


def _final_norm(x, g):
    rows, d = x.shape
    tm = 1024
    return pl.pallas_call(
        _final_kernel,
        grid=(rows // tm,),
        in_specs=[pl.BlockSpec((tm, d), lambda i: (i, 0)), pl.BlockSpec((1, d), lambda i: (0, 0))],
        out_specs=pl.BlockSpec((tm, d), lambda i: (i, 0)),
        out_shape=jax.ShapeDtypeStruct((rows, d), F32),
        compiler_params=_cparams(("arbitrary",)),
        name="final_norm",
    )(x, g)


def kernel(x, c, ctx, c_ctx, ada_w, ada_b, norm_g, ffn_wg, ffn_wu, ffn_wd, w_in, w_out, out_g, lru_conv_w, lru_conv_b, lru_wa, lru_ba, lru_wx, lru_bx, lru_lam, hy_conv_w, hy_conv_b, hy_bias, filt_w1, filt_b1, filt_w2, filt_b2, filt_w3, filt_b3, filt_w4, filt_b4, filt_freq, final_g):
    assert x.shape[0] == 1 and ctx.shape[0] == 1
    depth = ada_w.shape[0]
    d = x.shape[-1]
    lw = lru_conv_w.shape[-1]
    n_lat, n_ctx = x.shape[1], ctx.shape[1]
    xl, xc = x.reshape(n_lat, d), ctx.reshape(n_ctx, d)
    mods = _ada_mods(c, c_ctx, ada_w, ada_b)
    w_in_bf16, w_out_bf16 = w_in.astype(BF16), w_out.astype(BF16)
    zero_state = jnp.zeros((1, lw), F32)
    lat_is_col = False
    for l in range(depth):
        last = l == depth - 1
        g = norm_g[l][:, None, :]

        xl, xc = _ffn(xl, xc, mods, 0, g[0], ffn_wg, ffn_wu, ffn_wd, l, 0)

        want_col = l % 2 == 1
        if want_col != lat_is_col:
            xl = _to_col_major(xl) if want_col else _from_col_major(xl)
            lat_is_col = want_col
        pl_, pc_ = _win(xl, xc, mods, l, g[1], w_in_bf16)

        convs = (lru_conv_w[l], lru_conv_b[l][None, :], hy_conv_w[l], hy_conv_b[l][None, :])
        cv_l, u_l, x0_l = _short_convs(pl_, *convs, True)
        cv_c, *hy_c = _short_convs(pc_, *convs, not last)

        h_lat = h_ctx = None
        for dr in range(2):
            wax = jnp.concatenate([lru_wa[l, dr], lru_wx[l, dr]], axis=-1).astype(BF16)
            ba, bx, lam = lru_ba[l, dr][None, :], lru_bx[l, dr][None, :], lru_lam[l, dr][None, :]
            h_ctx, state = _lru_scan(cv_c, wax, ba, bx, lam, zero_state, h_ctx, dr == 1)
            h_lat, _ = _lru_scan(cv_l, wax, ba, bx, lam, state, h_lat, dr == 1)

        filt = (filt_w1[l], filt_b1[l], filt_w2[l], filt_b2[l], filt_w3[l], filt_b3[l], filt_w4[l],
                filt_b4[l], filt_freq[l])
        hbias = hy_bias[l][None, :]
        og = out_g[l][None, :]

        kr, ki = _hyena_spectrum(*_hyena_filter(n_lat, *filt))
        y_hy = _hyena_conv(u_l, hbias, kr, ki)
        xl = _mixout(h_lat, pl_, y_hy, x0_l, og, w_out_bf16, xl, mods, l, LAT)

        if last:
            xc = None
        else:
            u_c, x0_c = hy_c
            kr, ki = _hyena_spectrum(*_hyena_filter(n_ctx, *filt))
            y_hy = _hyena_conv(u_c, hbias, kr, ki)
            xc = _mixout(h_ctx, pc_, y_hy, x0_c, og, w_out_bf16, xc, mods, l, CTX)
        xl, xc = _ffn(xl, xc, mods, 2, g[2], ffn_wg, ffn_wu, ffn_wd, l, 1)

    if lat_is_col:
        xl = _from_col_major(xl)
    return _final_norm(xl, final_g[None, :]).reshape(x.shape)
```

```python
import functools
import math

import numpy as np
import jax
import jax.numpy as jnp
from jax import lax
from jax.experimental import pallas as pl
from jax.experimental.pallas import tpu as pltpu

F32 = jnp.float32
BF16 = jnp.bfloat16
EPS = 1e-6
LRU_C = 8.0
GRID_W = 64
HY_BANDS = 16
HY_MAX_DECAY = math.log(1e-2) / 0.3
HY_MIN_DECAY = math.log(1e-2) / 1.5

V7X_VMEM_LIMIT_BYTES = 58 * 1024 * 1024
FFT_B = 128
SUBLANES = 8


def _cparams(sem):
    return pltpu.CompilerParams(dimension_semantics=sem, vmem_limit_bytes=V7X_VMEM_LIMIT_BYTES)


def _rms(x):
    return x * lax.rsqrt(jnp.mean(x * x, axis=-1, keepdims=True) + EPS)


def _modulated(x, shift_ref, scale_ref, g_ref, stream):
    shift = shift_ref[stream:stream + 1, :]
    scale = scale_ref[stream:stream + 1, :]
    return _rms(x) * g_ref[...] * (1.0 + scale) + shift


LANES = 128
N_STREAMS = 2


def _ada_kernel(c_ref, w_ref, b_ref, o_ref):
    o_ref[...] = jnp.zeros_like(o_ref)
    for k in range(N_STREAMS):
        c = c_ref[k]
        s = c * jax.nn.sigmoid(c)
        for j in range(0, w_ref.shape[1], LANES):
            cols = slice(j, j + LANES)
            o_ref[k:k + 1, cols] = jnp.sum(w_ref[:, cols] * s, axis=0, keepdims=True) + b_ref[:, cols]


def _ada_mods(c, c_ctx, ada_w, ada_b):
    depth, d, nm = ada_w.shape
    tn = d
    cc = jnp.broadcast_to(jnp.stack([c[0], c_ctx])[:, :, None], (N_STREAMS, d, LANES))
    return pl.pallas_call(
        _ada_kernel,
        grid=(depth, nm // tn),
        in_specs=[pl.BlockSpec((N_STREAMS, d, LANES), lambda l, j: (0, 0, 0)),
                  pl.BlockSpec((None, d, tn), lambda l, j: (l, 0, j)),
                  pl.BlockSpec((None, 1, tn), lambda l, j: (l, 0, j))],
        out_specs=pl.BlockSpec((None, SUBLANES, tn), lambda l, j: (l, 0, j)),
        out_shape=jax.ShapeDtypeStruct((depth, SUBLANES, nm), F32),
        compiler_params=_cparams(("arbitrary", "arbitrary")),
        name="ada_mods",
    )(cc, ada_w, ada_b[:, None, :])


def _mod_spec(l, k, which, d, tn=None):
    chunk = 3 * k + which
    if tn is None:
        return pl.BlockSpec((None, SUBLANES, d), lambda *g: (l, 0, chunk))
    return pl.BlockSpec((None, SUBLANES, tn), lambda *g: (l, 0, chunk * (d // tn) + g[-1]))


NORM_CHUNK = 256
LAT, CTX = 0, 1


def _rider_spec(rows, tcol, ncol_steps):
    return pl.BlockSpec((rows, tcol), lambda i, c: (0, jnp.where(i == 0, c, ncol_steps - 1)))


def _ffn_up_kernel(*refs, tm, with_ctx):
    if with_ctx:
        x_ref, xc_ref, shift_ref, scale_ref, g_ref, wg_ref, wu_ref, h_ref, hc_ref, u_scr, uc_scr = refs
    else:
        x_ref, shift_ref, scale_ref, g_ref, wg_ref, wu_ref, h_ref, u_scr = refs
    i, f = pl.program_id(0), pl.program_id(1)
    wg = wg_ref[...].astype(BF16)
    wu = wu_ref[...].astype(BF16)

    def swiglu_rows(u):
        g = jnp.dot(u, wg, preferred_element_type=F32)
        up = jnp.dot(u, wu, preferred_element_type=F32)
        return (g * jax.nn.sigmoid(g) * up).astype(BF16)

    @pl.when(f == 0)
    def _():
        for r in range(0, tm, NORM_CHUNK):
            rs = slice(r, min(r + NORM_CHUNK, tm))
            u = _modulated(x_ref[rs, :], shift_ref, scale_ref, g_ref, LAT).astype(BF16)
            u_scr[rs, :] = u
            h_ref[rs, :] = swiglu_rows(u)

    @pl.when(f > 0)
    def _():
        h_ref[...] = swiglu_rows(u_scr[...])

    if with_ctx:
        @pl.when(i == 0)
        def _():
            @pl.when(f == 0)
            def _():
                uc_scr[...] = _modulated(xc_ref[...], shift_ref, scale_ref, g_ref, CTX).astype(BF16)

            hc_ref[...] = swiglu_rows(uc_scr[...])


def _ffn_down_kernel(*refs, with_ctx):
    if with_ctx:
        h_ref, hc_ref, wd_ref, x_ref, xc_ref, gate_ref, o_ref, oc_ref = refs
    else:
        h_ref, wd_ref, x_ref, gate_ref, o_ref = refs
    wd = wd_ref[...].astype(BF16)
    o_ref[...] = x_ref[...] + (0.5 * gate_ref[LAT:LAT + 1, :]) * jnp.dot(h_ref[...], wd,
                                                                          preferred_element_type=F32)
    if with_ctx:
        @pl.when(pl.program_id(0) == 0)
        def _():
            oc_ref[...] = xc_ref[...] + (0.5 * gate_ref[CTX:CTX + 1, :]) * jnp.dot(
                hc_ref[...], wd, preferred_element_type=F32)


def _ffn(x, xc, mods, k, g, wg, wu, wd, l, j):
    rows, d = x.shape
    dff = wg.shape[-1]
    tm, tf, tn = 1024, 512, 256
    nf, nn = dff // tf, d // tn
    with_ctx = xc is not None
    rc = xc.shape[0] if with_ctx else 0
    ctx_in = [xc] if with_ctx else []
    res = pl.pallas_call(
        functools.partial(_ffn_up_kernel, tm=tm, with_ctx=with_ctx),
        grid=(rows // tm, nf),
        in_specs=[pl.BlockSpec((tm, d), lambda i, f: (i, 0))]
                 + ([pl.BlockSpec((rc, d), lambda i, f: (0, 0))] if with_ctx else [])
                 + [_mod_spec(l, k, 0, d), _mod_spec(l, k, 1, d),
                    pl.BlockSpec((1, d), lambda i, f: (0, 0)),
                    pl.BlockSpec((None, None, d, tf), lambda i, f: (l, j, 0, f)),
                    pl.BlockSpec((None, None, d, tf), lambda i, f: (l, j, 0, f))],
        out_specs=[pl.BlockSpec((tm, tf), lambda i, f: (i, f))]
                  + ([_rider_spec(rc, tf, nf)] if with_ctx else []),
        out_shape=[jax.ShapeDtypeStruct((rows, dff), BF16)]
                  + ([jax.ShapeDtypeStruct((rc, dff), BF16)] if with_ctx else []),
        scratch_shapes=[pltpu.VMEM((tm, d), BF16)] + ([pltpu.VMEM((rc, d), BF16)] if with_ctx else []),
        compiler_params=_cparams(("arbitrary", "arbitrary")),
        name="ffn_up",
    )(x, *ctx_in, mods, mods, g, wg, wu)
    h, hc = (res[0], res[1]) if with_ctx else (res[0], None)
    res = pl.pallas_call(
        functools.partial(_ffn_down_kernel, with_ctx=with_ctx),
        grid=(rows // tm, nn),
        in_specs=[pl.BlockSpec((tm, dff), lambda i, n: (i, 0))]
                 + ([pl.BlockSpec((rc, dff), lambda i, n: (0, 0))] if with_ctx else [])
                 + [pl.BlockSpec((None, None, dff, tn), lambda i, n: (l, j, 0, n)),
                    pl.BlockSpec((tm, tn), lambda i, n: (i, n))]
                 + ([_rider_spec(rc, tn, nn)] if with_ctx else [])
                 + [_mod_spec(l, k, 2, d, tn)],
        out_specs=[pl.BlockSpec((tm, tn), lambda i, n: (i, n))]
                  + ([_rider_spec(rc, tn, nn)] if with_ctx else []),
        out_shape=[jax.ShapeDtypeStruct((rows, d), F32)]
                  + ([jax.ShapeDtypeStruct((rc, d), F32)] if with_ctx else []),
        compiler_params=_cparams(("arbitrary", "arbitrary")),
        name="ffn_down",
    )(h, *([hc] if with_ctx else []), wd, x, *ctx_in, mods)
    return (res[0], res[1]) if with_ctx else (res[0], None)


def _win_kernel(x_ref, xc_ref, shift_ref, scale_ref, g_ref, w_ref, p_ref, pc_ref, u_scr, uc_scr, *, tm):
    i, n = pl.program_id(0), pl.program_id(1)
    w = w_ref[...]

    @pl.when(n == 0)
    def _():
        for r in range(0, tm, NORM_CHUNK):
            rs = slice(r, min(r + NORM_CHUNK, tm))
            u = _modulated(x_ref[rs, :], shift_ref, scale_ref, g_ref, LAT).astype(BF16)
            u_scr[rs, :] = u
            p_ref[rs, :] = jnp.dot(u, w, preferred_element_type=F32)

    @pl.when(n > 0)
    def _():
        p_ref[...] = jnp.dot(u_scr[...], w, preferred_element_type=F32)

    @pl.when(i == 0)
    def _():
        @pl.when(n == 0)
        def _():
            uc_scr[...] = _modulated(xc_ref[...], shift_ref, scale_ref, g_ref, CTX).astype(BF16)

        pc_ref[...] = jnp.dot(uc_scr[...], w, preferred_element_type=F32)


def _win(x, xc, mods, l, g, w_in_bf16):
    rows, d = x.shape
    rc = xc.shape[0]
    ncols = w_in_bf16.shape[-1]
    tm, tn = 1024, 1024
    nn = ncols // tn
    return pl.pallas_call(
        functools.partial(_win_kernel, tm=tm),
        grid=(rows // tm, nn),
        in_specs=[pl.BlockSpec((tm, d), lambda i, n: (i, 0)),
                  pl.BlockSpec((rc, d), lambda i, n: (0, 0)),
                  _mod_spec(l, 1, 0, d), _mod_spec(l, 1, 1, d),
                  pl.BlockSpec((1, d), lambda i, n: (0, 0)),
                  pl.BlockSpec((None, d, tn), lambda i, n: (l, 0, n))],
        out_specs=[pl.BlockSpec((tm, tn), lambda i, n: (i, n)), _rider_spec(rc, tn, nn)],
        out_shape=[jax.ShapeDtypeStruct((rows, ncols), F32), jax.ShapeDtypeStruct((rc, ncols), F32)],
        scratch_shapes=[pltpu.VMEM((tm, d), BF16), pltpu.VMEM((rc, d), BF16)],
        compiler_params=_cparams(("arbitrary", "arbitrary")),
        name="mixer_in_proj",
    )(x, xc, mods, mods, g, w_in_bf16)


def _to_col_major(x):
    rows, d = x.shape
    return x.reshape(rows // GRID_W, GRID_W, d).transpose(1, 0, 2).reshape(rows, d)


def _from_col_major(x):
    rows, d = x.shape
    return x.reshape(GRID_W, rows // GRID_W, d).transpose(1, 0, 2).reshape(rows, d)


def _lru_kernel(cv_ref, wax_ref, ba_ref, bx_ref, lam_ref, h0_ref, *rest, t_blk, heads, reverse, add_prev):
    if add_prev:
        prev_ref, o_ref, hf_ref, a_scr, b_scr, carry = rest
    else:
        o_ref, hf_ref, a_scr, b_scr, carry = rest
    c = cv_ref.shape[1]
    hd = c // heads

    @pl.when(pl.program_id(0) == 0)
    def _():
        carry[...] = h0_ref[...]

    cv = cv_ref[...]
    z = -lam_ref[...]
    c_sp = LRU_C * (jnp.maximum(z, 0.0) + jnp.log(1.0 + jnp.exp(-jnp.abs(z))))
    cvb = cv.astype(BF16)
    for h in range(heads):
        sl = slice(h * hd, (h + 1) * hd)
        gx = jnp.dot(cvb[:, sl], wax_ref[h], preferred_element_type=F32)
        r = 0.5 + 0.5 * jnp.tanh(0.5 * (gx[:, :hd] + ba_ref[:, sl]))
        ig = 0.5 + 0.5 * jnp.tanh(0.5 * (gx[:, hd:] + bx_ref[:, sl]))
        a = jnp.exp(-c_sp[:, sl] * r)
        a_scr[:, sl] = a
        b_scr[:, sl] = jnp.sqrt(1.0 - a * a) * (ig * cv[:, sl])

    row = lax.broadcasted_iota(jnp.int32, (SUBLANES, c), 0)
    ngrp = t_blk // SUBLANES

    def group(gi, hc):
        g = (ngrp - 1 - gi) if reverse else gi
        r0 = pl.multiple_of(g * SUBLANES, SUBLANES)
        av = a_scr[pl.ds(r0, SUBLANES), :]
        bv = b_scr[pl.ds(r0, SUBLANES), :]
        for s in (1, 2, 4):
            sh = (SUBLANES - s) if reverse else s
            m = (row < SUBLANES - s) if reverse else (row >= s)
            a_s = pltpu.roll(av, sh, axis=0)
            b_s = pltpu.roll(bv, sh, axis=0)
            bv = jnp.where(m, av * b_s + bv, bv)
            av = jnp.where(m, av * a_s, av)
        hv = av * hc + bv
        if add_prev:
            o_ref[pl.ds(r0, SUBLANES), :] = hv + prev_ref[pl.ds(r0, SUBLANES), :]
        else:
            o_ref[pl.ds(r0, SUBLANES), :] = hv
        return hv[0:1, :] if reverse else hv[SUBLANES - 1:SUBLANES, :]

    hc = lax.fori_loop(0, ngrp, group, carry[...])
    carry[...] = hc
    hf_ref[...] = hc


def _lru_scan(cv, wax, ba, bx, lam, h0, prev, reverse):
    seq, c = cv.shape
    heads = wax.shape[0]
    t_blk = min(seq, 1024)
    nblk = seq // t_blk

    def bi(j):
        return (nblk - 1 - j) if reverse else j

    row = pl.BlockSpec((1, c), lambda j: (0, 0))
    in_specs = [pl.BlockSpec((t_blk, c), lambda j: (bi(j), 0)),
                pl.BlockSpec(wax.shape, lambda j: (0, 0, 0)), row, row, row, row]
    args = [cv, wax, ba, bx, lam, h0]
    if prev is not None:
        in_specs.append(pl.BlockSpec((t_blk, c), lambda j: (bi(j), 0)))
        args.append(prev)
    return pl.pallas_call(
        functools.partial(_lru_kernel, t_blk=t_blk, heads=heads, reverse=reverse, add_prev=prev is not None),
        grid=(nblk,),
        in_specs=in_specs,
        out_specs=[pl.BlockSpec((t_blk, c), lambda j: (bi(j), 0)), row],
        out_shape=[jax.ShapeDtypeStruct((seq, c), F32), jax.ShapeDtypeStruct((1, c), F32)],
        scratch_shapes=[pltpu.VMEM((t_blk, c), F32), pltpu.VMEM((t_blk, c), F32), pltpu.VMEM((1, c), F32)],
        compiler_params=_cparams(("arbitrary",)),
        name="rglru_scan",
    )(*args)


def _short_conv_kernel(*refs, t_blk, nblk, c, with_hyena):
    n_in = 4 if with_hyena else 1
    ins = [refs[3 * k:3 * k + 3] for k in range(n_in)]
    lw_ref, lb_ref, hw_ref, hb_ref = refs[3 * n_in:3 * n_in + 4]
    outs = refs[3 * n_in + 4:-1]
    xs = refs[-1]
    blk = pl.program_id(0)
    zero8 = jnp.zeros((SUBLANES, c), F32)

    def conv(src, w_ref, b_ref, part):
        cur, prv, nxt = src
        xs[0:SUBLANES, :] = jnp.where(blk == 0, zero8, prv[...])
        xs[SUBLANES:SUBLANES + t_blk, :] = cur[...]
        xs[SUBLANES + t_blk:2 * SUBLANES + t_blk, :] = jnp.where(blk == nblk - 1, zero8, nxt[...])
        taps = w_ref.shape[0]
        left = (taps - 1) // 2
        sl = slice(part * c, (part + 1) * c)
        out = b_ref[:, sl]
        for k in range(taps):
            r0 = SUBLANES - left + k
            out = out + xs[r0:r0 + t_blk, :] * w_ref[k:k + 1, sl]
        return out

    outs[0][...] = conv(ins[0], lw_ref, lb_ref, 0)
    if with_hyena:
        outs[2][...] = conv(ins[1], hw_ref, hb_ref, 0)
        x1c = conv(ins[2], hw_ref, hb_ref, 1)
        outs[1][...] = conv(ins[3], hw_ref, hb_ref, 2) * x1c


def _short_convs(p, lru_w, lru_b, hy_w, hy_b, with_hyena):
    seq = p.shape[0]
    c = lru_w.shape[1]
    t_blk = min(seq, 512)
    nblk = seq // t_blk
    per = t_blk // SUBLANES
    nb8 = seq // SUBLANES
    in_specs, args = [], []
    for col in ([0, 2, 3, 4] if with_hyena else [0]):
        in_specs += [pl.BlockSpec((t_blk, c), lambda j, col=col: (j, col)),
                     pl.BlockSpec((SUBLANES, c), lambda j, col=col: (jnp.maximum(j * per - 1, 0), col)),
                     pl.BlockSpec((SUBLANES, c), lambda j, col=col: (jnp.minimum((j + 1) * per, nb8 - 1), col))]
        args += [p, p, p]
    full = lambda a: pl.BlockSpec(a.shape, lambda j: (0, 0))
    in_specs += [full(lru_w), full(lru_b), full(hy_w), full(hy_b)]
    args += [lru_w, lru_b, hy_w, hy_b]
    n_out = 3 if with_hyena else 1
    return pl.pallas_call(
        functools.partial(_short_conv_kernel, t_blk=t_blk, nblk=nblk, c=c, with_hyena=with_hyena),
        grid=(nblk,),
        in_specs=in_specs,
        out_specs=[pl.BlockSpec((t_blk, c), lambda j: (j, 0))] * n_out,
        out_shape=[jax.ShapeDtypeStruct((seq, c), F32)] * n_out,
        scratch_shapes=[pltpu.VMEM((t_blk + 2 * SUBLANES, c), F32)],
        compiler_params=_cparams(("arbitrary",)),
        name="mixer_short_convs",
    )(*args)


FILT_SLOT = 64


def _filt_kernel(w1_ref, b1_ref, w2_ref, b2_ref, w3_ref, b3_ref, w4a_ref, w4b_ref, b4_ref, fr_ref,
                 k_ref, nrm_ref, *, seq, t_blk, c):
    i = pl.program_id(0)
    hi = lax.Precision.HIGHEST
    lanes = 2 * FILT_SLOT
    th = t_blk // 2
    lane = lax.broadcasted_iota(jnp.int32, (1, lanes), 1)
    slot = lane & (FILT_SLOT - 1)
    pos_a = i * t_blk + lax.broadcasted_iota(jnp.int32, (th, 1), 0)
    pos = jnp.where(lane < FILT_SLOT, pos_a, pos_a + th)
    tf = jnp.where(pos < seq, pos, 2 * seq - pos).astype(F32)
    t = tf * (1.0 / (seq - 1))
    w = (2.0 * math.pi / seq) * tf
    is_cos = (slot >= 1) & (slot <= HY_BANDS)
    is_sin = (slot > HY_BANDS) & (slot <= 2 * HY_BANDS)
    band = jnp.where(is_cos, slot - 1, slot - 1 - HY_BANDS).astype(F32)
    f = 1e-4 + band * ((HY_BANDS - 1 - 1e-4) / (HY_BANDS - 1))
    phase = jnp.where(is_cos, 0.5 * math.pi, 0.0)
    sign = jnp.where(is_cos, 1.0, jnp.where(is_sin, -1.0, 0.0))
    z = jnp.where(slot == 0, t, sign * jnp.sin(w * f + phase))
    fr = fr_ref[...]
    h = jnp.sin(fr * (jnp.dot(z, w1_ref[...], precision=hi, preferred_element_type=F32) + b1_ref[...]))
    h = jnp.sin(fr * (jnp.dot(h, w2_ref[...], precision=hi, preferred_element_type=F32) + b2_ref[...]))
    h = jnp.sin(fr * (jnp.dot(h, w3_ref[...], precision=hi, preferred_element_type=F32) + b3_ref[...]))
    ch = lax.broadcasted_iota(jnp.int32, (1, c), 1).astype(F32)
    delta = jnp.abs(HY_MIN_DECAY + ch * ((HY_MAX_DECAY - HY_MIN_DECAY) / (c - 1)))
    part = jnp.zeros((1, c), F32)
    h_hi = h.astype(BF16)
    h_lo = (h - h_hi.astype(F32)).astype(BF16)
    for half, w4_ref in enumerate((w4a_ref, w4b_ref)):
        pos_h = pos_a + half * th
        t_h = jnp.where(pos_h < seq, pos_h, 2 * seq - pos_h).astype(F32) * (1.0 / (seq - 1))
        w_hi, w_lo = w4_ref[0], w4_ref[1]
        k = (jnp.dot(h_hi, w_hi, preferred_element_type=F32) + jnp.dot(h_lo, w_hi, preferred_element_type=F32)
             + jnp.dot(h_hi, w_lo, preferred_element_type=F32)) + b4_ref[...]
        k = jnp.where(pos_h != seq, k * jnp.exp(-t_h * delta), 0.0)
        k_ref[half * th:(half + 1) * th, :] = k
        part = part + jnp.sum(jnp.abs(k), axis=0, keepdims=True)

    @pl.when(i == 0)
    def _():
        nrm_ref[...] = part

    @pl.when(i > 0)
    def _():
        nrm_ref[...] = nrm_ref[...] + part


def _slots(a, row_used):
    r, cdim = a.shape
    blk = jnp.pad(a, ((0, (FILT_SLOT if row_used else 1) - r), (0, FILT_SLOT - cdim)))
    if not row_used:
        return jnp.concatenate([blk, blk], axis=1)
    z = jnp.zeros_like(blk)
    return jnp.concatenate([jnp.concatenate([blk, z], axis=1), jnp.concatenate([z, blk], axis=1)], axis=0)


def _hyena_filter(seq, w1, b1, w2, b2, w3, b3, w4, b4, freq):
    c = w4.shape[1] // 2
    od = w2.shape[0]
    assert od <= FILT_SLOT and w1.shape[0] <= FILT_SLOT
    lanes = 2 * FILT_SLOT
    t_blk = min(seq, 1024)
    half_blocks = seq // t_blk
    full = lambda shape: pl.BlockSpec(shape, lambda i: (0,) * len(shape))
    sq, row = (lanes, lanes), (1, lanes)
    w4p = jnp.pad(w4, ((0, FILT_SLOT - od), (0, 0)))
    w4_hi = w4p.astype(BF16)
    w4_lo = (w4p - w4_hi.astype(F32)).astype(BF16)
    w4s = jnp.stack([w4_hi, w4_lo])
    w4a = jnp.concatenate([w4s, jnp.zeros_like(w4s)], axis=1)
    w4b = jnp.concatenate([jnp.zeros_like(w4s), w4s], axis=1)
    w4_spec = pl.BlockSpec((2, lanes, c), lambda i: (0, 0, i // half_blocks))
    return pl.pallas_call(
        functools.partial(_filt_kernel, seq=seq, t_blk=t_blk, c=c),
        grid=(2 * seq // t_blk,),
        in_specs=[full(sq), full(row), full(sq), full(row), full(sq), full(row), w4_spec, w4_spec,
                  pl.BlockSpec((1, c), lambda i: (0, i // half_blocks)),
                  full(row)],
        out_specs=[pl.BlockSpec((t_blk, c), lambda i: (i, 0)), pl.BlockSpec((1, c), lambda i: (0, 0))],
        out_shape=[jax.ShapeDtypeStruct((2 * seq, c), F32), jax.ShapeDtypeStruct((1, c), F32)],
        compiler_params=_cparams(("arbitrary",)),
        name="hyena_filter",
    )(_slots(w1, True), _slots(b1[None, :], False), _slots(w2, True), _slots(b2[None, :], False),
      _slots(w3, True), _slots(b3[None, :], False), w4a, w4b, b4[None, :], _slots(freq[None, :], False))


def _bitrev(p, bits):
    r = 0
    for k in range(bits):
        r = (r << 1) | ((p >> k) & 1)
    return r


def _real_block(z):
    return np.block([[z.real, -z.imag], [z.imag, z.real]])


def _fft_split(s2):
    n_lo = max(min(8, s2 // 2), 1)
    return s2 // n_lo, n_lo


@functools.lru_cache(maxsize=None)
def _fft_tables(s):
    s2 = s // 2
    n = s * FFT_B
    n_hi, n_lo = _fft_split(s2)
    bits_hi, bits_lo = n_hi.bit_length() - 1, n_lo.bit_length() - 1
    b = np.arange(FFT_B)
    dft = np.exp(-2j * np.pi * np.outer(b, b) / FFT_B)
    e_lo = [2 * n_hi * _bitrev(p, bits_lo) for p in range(n_lo)]
    g = [dft * np.exp(-2j * np.pi * b * e / n)[None, :] for e in e_lo]
    chunks = np.arange(FFT_B).reshape(-1, FFT_ROWS)
    order = np.concatenate([np.concatenate([c, c + FFT_B]) for c in chunks])
    lhs_f = np.stack([_real_block(x)[order] for x in g])
    lhs_i = np.stack([_real_block(np.conj(x).T)[order] for x in g])
    e_hi = np.array([[2 * _bitrev(p, bits_hi) + hf for p in range(n_hi)] for hf in range(2)])
    ang = 2.0 * np.pi * e_hi[:, :, None, None] * b[None, None, :, None] / n * np.ones((1, 1, 1, FFT_B))
    lvl = 2.0 * np.pi * np.arange(s2) / s2
    first = np.concatenate([np.zeros(s2), 2.0 * np.pi * np.arange(s2) / s])
    f32 = lambda x: np.asarray(x, np.float32)
    return dict(lhs_f=f32(lhs_f), lhs_i=f32(lhs_i), col_c=f32(np.cos(ang)), col_s=f32(np.sin(ang)),
                lvl_c=f32(np.cos(lvl)), lvl_s=f32(np.sin(lvl)), first_c=f32(np.cos(first)),
                first_s=f32(np.sin(first)))


FFT_ROWS = 32


def _cmul(xr, xi, c, s):
    return xr * c + xi * s, xi * c - xr * s


def _cmul_conj(xr, xi, c, s):
    return xr * c - xi * s, xi * c + xr * s


def _pass_plan(s2):
    levels = s2.bit_length() - 1
    plan, h = [], s2 // 2
    if levels % 2 == 1:
        plan.append((h, 2))
        h //= 2
    while h >= 1:
        plan.append((h, 4))
        h //= 4
    return plan


def _butterfly_pass(load, store, lc_ref, ls_ref, s2, h, radix, inverse):
    h2 = h // 2 if radix == 4 else h
    trivial = h2 == 1
    shift = h2.bit_length() - 1

    def group(idx, carry):
        g = idx >> shift
        i = idx & (h2 - 1)
        base = g * 2 * h + i
        blocks = [base + k * h2 for k in range(radix)]
        c1 = s1 = c2 = s2_ = c3 = s3 = None
        if not trivial:
            t1 = i * (s2 // (2 * h))
            c1, s1 = lc_ref[t1], ls_ref[t1]
            if radix == 4:
                c2, s2_, c3, s3 = lc_ref[2 * t1], ls_ref[2 * t1], lc_ref[3 * t1], ls_ref[3 * t1]
        fwd_mul = (lambda xr, xi, c, s: (xr, xi)) if trivial else _cmul
        inv_mul = (lambda xr, xi, c, s: (xr, xi)) if trivial else _cmul_conj
        for r in range(0, FFT_B, FFT_ROWS):
            x = [load(bk, r) for bk in blocks]
            if radix == 2:
                (ar, ai), (br, bi) = x
                if inverse:
                    br, bi = inv_mul(br, bi, c1, s1)
                    out = [(ar + br, ai + bi), (ar - br, ai - bi)]
                else:
                    out = [(ar + br, ai + bi), fwd_mul(ar - br, ai - bi, c1, s1)]
            elif inverse:
                (x0r, x0i), (x1r, x1i), (x2r, x2i), (x3r, x3i) = x
                x1r, x1i = inv_mul(x1r, x1i, c2, s2_)
                x2r, x2i = inv_mul(x2r, x2i, c1, s1)
                x3r, x3i = inv_mul(x3r, x3i, c3, s3)
                t0r, t0i, t1r, t1i = x0r + x1r, x0i + x1i, x0r - x1r, x0i - x1i
                t2r, t2i, t3r, t3i = x2r + x3r, x2i + x3i, x2r - x3r, x2i - x3i
                out = [(t0r + t2r, t0i + t2i), (t1r - t3i, t1i + t3r),
                       (t0r - t2r, t0i - t2i), (t1r + t3i, t1i - t3r)]
            else:
                (x0r, x0i), (x1r, x1i), (x2r, x2i), (x3r, x3i) = x
                t0r, t0i, t1r, t1i = x0r + x2r, x0i + x2i, x1r + x3r, x1i + x3i
                t2r, t2i, t3r, t3i = x0r - x2r, x0i - x2i, x1r - x3r, x1i - x3i
                out = [(t0r + t1r, t0i + t1i), fwd_mul(t0r - t1r, t0i - t1i, c2, s2_),
                       fwd_mul(t2r + t3i, t2i - t3r, c1, s1), fwd_mul(t2r - t3i, t2i + t3r, c3, s3)]
            for bk, (o_r, o_i) in zip(blocks, out):
                store(bk, r, o_r, o_i)
        return carry

    lax.fori_loop(0, s2 // radix, group, 0)


def _work_access(wr, wi):
    def rows(bk, r):
        return pl.ds(pl.multiple_of(bk * FFT_B + r, FFT_ROWS), FFT_ROWS)

    def load(bk, r):
        return wr[rows(bk, r), :], wi[rows(bk, r), :]

    def store(bk, r, re, im):
        wr[rows(bk, r), :] = re
        wi[rows(bk, r), :] = im

    return load, store


def _run_passes(plan, first_load, last_store, work, lc_ref, ls_ref, s2, inverse):
    w_load, w_store = work
    for k, (h, radix) in enumerate(plan):
        load = first_load if k == 0 else w_load
        store = last_store if k == len(plan) - 1 else w_store
        _butterfly_pass(load, store, lc_ref, ls_ref, s2, h, radix, inverse)


MID_UNROLL = 16


def _pair_blocks(pp, s2):
    n_hi, n_lo = _fft_split(s2)
    half = n_hi // 2
    p_lo = pp >> (half.bit_length() - 1)
    j = pp & (half - 1)
    return p_lo, [(2 * j + q, pl.multiple_of(((2 * j + q) * n_lo + p_lo) * FFT_B, FFT_B)) for q in range(2)]


def _for_pairs(s2, body):
    def step(pp, carry):
        body(pp)
        return carry
    lax.fori_loop(0, s2 // 2, step, 0, unroll=min(MID_UNROLL, s2 // 2))


def _pack_store(panel, cc_ref, cs_ref, s2):
    n_hi, n_lo = _fft_split(s2)
    lo_bits = n_lo.bit_length() - 1

    def store(bk, r, re, im):
        p_hi = bk >> lo_bits
        p_lo = bk & (n_lo - 1)
        pp = p_lo * (n_hi // 2) + (p_hi >> 1)
        q = p_hi & 1
        tr, ti = _cmul(re, im, cc_ref[p_hi, r:r + FFT_ROWS, :], cs_ref[p_hi, r:r + FFT_ROWS, :])
        panel[pp, q, r:r + FFT_ROWS, :] = tr.astype(BF16)
        panel[pp, q, FFT_B + r:FFT_B + r + FFT_ROWS, :] = ti.astype(BF16)

    return store


def _dft_pair(m_ref, panel, pp, p_lo):
    blk = FFT_B
    rhs = jnp.concatenate([panel[pp, 0], panel[pp, 1]], axis=1)
    out = jnp.dot(m_ref[p_lo], rhs, preferred_element_type=F32)
    for r in range(0, blk, FFT_ROWS):
        for q in range(2):
            yield (q, r, out[2 * r:2 * r + FFT_ROWS, q * blk:(q + 1) * blk],
                   out[2 * r + FFT_ROWS:2 * r + 2 * FFT_ROWS, q * blk:(q + 1) * blk])


def _spec_kernel(lo_ref, hi_ref, nrm_ref, fc_ref, fs_ref, lc_ref, ls_ref, cc_ref, cs_ref, mf_ref,
                 kr_ref, ki_ref, panel, *, s2):
    blk = FFT_B
    hf = pl.program_id(1)
    n = 2 * s2 * blk
    sign = (1 - 2 * hf).astype(F32)

    def first_load(bk, r):
        rows = pl.ds(pl.multiple_of(bk * blk + r, FFT_ROWS), FFT_ROWS)
        d = lo_ref[rows, :] + sign * hi_ref[rows, :]
        return d * fc_ref[hf * s2 + bk], -d * fs_ref[hf * s2 + bk]

    _run_passes(_pass_plan(s2), first_load, _pack_store(panel, cc_ref, cs_ref, s2),
                _work_access(kr_ref, ki_ref), lc_ref, ls_ref, s2, inverse=False)
    scale = 1.0 / (nrm_ref[...] * n)

    def dft(pp):
        p_lo, blocks = _pair_blocks(pp, s2)
        for q, r, yr, yi in _dft_pair(mf_ref, panel, pp, p_lo):
            rows = pl.ds(blocks[q][1] + r, FFT_ROWS)
            kr_ref[rows, :] = yr * scale
            ki_ref[rows, :] = yi * scale

    _for_pairs(s2, dft)


def _conv_kernel(u_ref, bias_ref, kr_ref, ki_ref, fc_ref, fs_ref, lc_ref, ls_ref, cc_ref, cs_ref,
                 mf_ref, mi_ref, y_ref, wr, wi, panel, *, s2):
    blk = FFT_B
    hf = pl.program_id(1)
    plan = _pass_plan(s2)
    work = _work_access(wr, wi)

    def first_load(bk, r):
        ub = u_ref[pl.ds(pl.multiple_of(bk * blk + r, FFT_ROWS), FFT_ROWS), :]
        return ub * fc_ref[hf * s2 + bk], -ub * fs_ref[hf * s2 + bk]

    _run_passes(plan, first_load, _pack_store(panel, cc_ref, cs_ref, s2), work, lc_ref, ls_ref, s2,
                inverse=False)

    def spectrum_product(pp):
        p_lo, blocks = _pair_blocks(pp, s2)
        for q, r, yr, yi in _dft_pair(mf_ref, panel, pp, p_lo):
            rows = pl.ds(blocks[q][1] + r, FFT_ROWS)
            kr, ki = kr_ref[rows, :], ki_ref[rows, :]
            panel[pp, q, r:r + FFT_ROWS, :] = (yr * kr - yi * ki).astype(BF16)
            panel[pp, q, blk + r:blk + r + FFT_ROWS, :] = (yr * ki + yi * kr).astype(BF16)

    _for_pairs(s2, spectrum_product)

    def inverse_dft(pp):
        p_lo, blocks = _pair_blocks(pp, s2)
        for q, r, er, ei in _dft_pair(mi_ref, panel, pp, p_lo):
            p_hi, r0 = blocks[q]
            rows = pl.ds(r0 + r, FFT_ROWS)
            wr[rows, :], wi[rows, :] = _cmul_conj(er, ei, cc_ref[p_hi, r:r + FFT_ROWS, :],
                                                  cs_ref[p_hi, r:r + FFT_ROWS, :])

    _for_pairs(s2, inverse_dft)

    def rows_of(bk, r):
        return pl.ds(pl.multiple_of(bk * blk + r, FFT_ROWS), FFT_ROWS)

    def store_even(bk, r, er, ei):
        y_ref[rows_of(bk, r), :] = er

    def store_odd(bk, r, er, ei):
        rows = rows_of(bk, r)
        val = er * fc_ref[s2 + bk] - ei * fs_ref[s2 + bk]
        y_ref[rows, :] = y_ref[rows, :] + u_ref[rows, :] * bias_ref[...] + val

    inv_plan = plan[::-1]
    if len(inv_plan) > 1:
        _run_passes(inv_plan[:-1], work[0], work[1], work, lc_ref, ls_ref, s2, inverse=True)
    h_last, radix_last = inv_plan[-1]
    for half, store in ((0, store_even), (1, store_odd)):
        @pl.when(hf == half)
        def _(store=store):
            _butterfly_pass(work[0], store, lc_ref, ls_ref, s2, h_last, radix_last, True)


def _smem_spec():
    return pl.BlockSpec(memory_space=pltpu.SMEM)


def _fft_table_specs(s2):
    n_hi, n_lo = _fft_split(s2)
    col = pl.BlockSpec((None, n_hi, FFT_B, FFT_B), lambda ct, hf: (hf, 0, 0, 0))
    lhs = pl.BlockSpec((n_lo, 2 * FFT_B, 2 * FFT_B), lambda ct, hf: (0, 0, 0))
    return [_smem_spec(), _smem_spec(), _smem_spec(), _smem_spec(), col, col], lhs


def _hyena_spectrum(kfull, nrm):
    n, c = kfull.shape
    seq = n // 2
    s2 = seq // FFT_B
    tb = _fft_tables(2 * s2)
    lanes = 128
    tab_specs, lhs_spec = _fft_table_specs(s2)
    return pl.pallas_call(
        functools.partial(_spec_kernel, s2=s2),
        grid=(c // lanes, 2),
        in_specs=[pl.BlockSpec((seq, lanes), lambda ct, hf: (0, ct)),
                  pl.BlockSpec((seq, lanes), lambda ct, hf: (1, ct)),
                  pl.BlockSpec((1, lanes), lambda ct, hf: (0, ct))] + tab_specs + [lhs_spec],
        out_specs=[pl.BlockSpec((None, seq, lanes), lambda ct, hf: (hf, 0, ct)),
                   pl.BlockSpec((None, seq, lanes), lambda ct, hf: (hf, 0, ct))],
        out_shape=[jax.ShapeDtypeStruct((2, seq, c), F32), jax.ShapeDtypeStruct((2, seq, c), F32)],
        scratch_shapes=[pltpu.VMEM((s2 // 2, 2, 2 * FFT_B, FFT_B), BF16)],
        compiler_params=_cparams(("arbitrary", "arbitrary")),
        name="hyena_spectrum",
    )(kfull, kfull, nrm, tb["first_c"], tb["first_s"], tb["lvl_c"], tb["lvl_s"],
      tb["col_c"], tb["col_s"], jnp.asarray(tb["lhs_f"], BF16))


def _hyena_conv(u, bias, kr, ki):
    seq, c = u.shape
    s2 = seq // FFT_B
    tb = _fft_tables(2 * s2)
    lanes = 128
    tab_specs, lhs_spec = _fft_table_specs(s2)
    return pl.pallas_call(
        functools.partial(_conv_kernel, s2=s2),
        grid=(c // lanes, 2),
        in_specs=[pl.BlockSpec((seq, lanes), lambda ct, hf: (0, ct)),
                  pl.BlockSpec((1, lanes), lambda ct, hf: (0, ct)),
                  pl.BlockSpec((None, seq, lanes), lambda ct, hf: (hf, 0, ct)),
                  pl.BlockSpec((None, seq, lanes), lambda ct, hf: (hf, 0, ct))]
                 + tab_specs + [lhs_spec, lhs_spec],
        out_specs=pl.BlockSpec((seq, lanes), lambda ct, hf: (0, ct)),
        out_shape=jax.ShapeDtypeStruct((seq, c), F32),
        scratch_shapes=[pltpu.VMEM((seq, lanes), F32), pltpu.VMEM((seq, lanes), F32),
                        pltpu.VMEM((s2 // 2, 2, 2 * FFT_B, FFT_B), BF16)],
        compiler_params=_cparams(("arbitrary", "arbitrary")),
        name="hyena_fft_conv",
    )(u, bias, kr, ki, tb["first_c"], tb["first_s"], tb["lvl_c"], tb["lvl_s"],
      tb["col_c"], tb["col_s"], jnp.asarray(tb["lhs_f"], BF16), jnp.asarray(tb["lhs_i"], BF16))


MIXOUT_CHUNK = 256


def _mixout_kernel(h_ref, yr_ref, hy_ref, x0_ref, og_ref, w_ref, x_ref, gate_ref, o_ref, *, stream, lw, tm):
    gate = gate_ref[stream:stream + 1, :]
    w = w_ref[...]
    for r in range(0, tm, MIXOUT_CHUNK):
        rs = slice(r, r + min(MIXOUT_CHUNK, tm))
        yr = yr_ref[rs, :]
        gelu = 0.5 * yr * (1.0 + jnp.tanh(math.sqrt(2.0 / math.pi) * (yr + 0.044715 * (yr * yr * yr))))
        lru = (_rms(h_ref[rs, :] * gelu) * og_ref[:, :lw]).astype(BF16)
        hy = (_rms(hy_ref[rs, :] * x0_ref[rs, :]) * og_ref[:, lw:]).astype(BF16)
        y = (jnp.dot(lru, w[:lw, :], preferred_element_type=F32)
             + jnp.dot(hy, w[lw:, :], preferred_element_type=F32))
        o_ref[rs, :] = x_ref[rs, :] + gate * y


def _mixout(h_lru, p, y_hy, x0c, og, w_out_bf16, x, mods, l, stream):
    rows, d = x.shape
    lw = h_lru.shape[1]
    dm = w_out_bf16.shape[1]
    tm = min(rows, 512)
    row_blk = lambda cols, j: pl.BlockSpec((tm, cols), lambda i: (i, j))
    return pl.pallas_call(
        functools.partial(_mixout_kernel, stream=stream, lw=lw, tm=tm),
        grid=(rows // tm,),
        in_specs=[row_blk(lw, 0), row_blk(lw, 1), row_blk(dm - lw, 0), row_blk(dm - lw, 0),
                  pl.BlockSpec((1, dm), lambda i: (0, 0)),
                  pl.BlockSpec((None, dm, d), lambda i: (l, 0, 0)),
                  row_blk(d, 0),
                  _mod_spec(l, 1, 2, d)],
        out_specs=row_blk(d, 0),
        out_shape=jax.ShapeDtypeStruct((rows, d), F32),
        compiler_params=_cparams(("arbitrary",)),
        name="mixer_out_proj",
    )(h_lru, p, y_hy, x0c, og, w_out_bf16, x, mods)


def _final_kernel(x_ref, g_ref, o_ref):
    o_ref[...] = _rms(x_ref[...]) * g_ref[...]


def _final_norm(x, g):
    rows, d = x.shape
    tm = 1024
    return pl.pallas_call(
        _final_kernel,
        grid=(rows // tm,),
        in_specs=[pl.BlockSpec((tm, d), lambda i: (i, 0)), pl.BlockSpec((1, d), lambda i: (0, 0))],
        out_specs=pl.BlockSpec((tm, d), lambda i: (i, 0)),
        out_shape=jax.ShapeDtypeStruct((rows, d), F32),
        compiler_params=_cparams(("arbitrary",)),
        name="final_norm",
    )(x, g)


def kernel(x, c, ctx, c_ctx, ada_w, ada_b, norm_g, ffn_wg, ffn_wu, ffn_wd, w_in, w_out, out_g, lru_conv_w, lru_conv_b, lru_wa, lru_ba, lru_wx, lru_bx, lru_lam, hy_conv_w, hy_conv_b, hy_bias, filt_w1, filt_b1, filt_w2, filt_b2, filt_w3, filt_b3, filt_w4, filt_b4, filt_freq, final_g):
    assert x.shape[0] == 1 and ctx.shape[0] == 1
    depth = ada_w.shape[0]
    d = x.shape[-1]
    lw = lru_conv_w.shape[-1]
    n_lat, n_ctx = x.shape[1], ctx.shape[1]
    xl, xc = x.reshape(n_lat, d), ctx.reshape(n_ctx, d)
    mods = _ada_mods(c, c_ctx, ada_w, ada_b)
    w_in_bf16, w_out_bf16 = w_in.astype(BF16), w_out.astype(BF16)
    zero_state = jnp.zeros((1, lw), F32)
    lat_is_col = False
    for l in range(depth):
        last = l == depth - 1
        g = norm_g[l][:, None, :]

        xl, xc = _ffn(xl, xc, mods, 0, g[0], ffn_wg, ffn_wu, ffn_wd, l, 0)

        want_col = l % 2 == 1
        if want_col != lat_is_col:
            xl = _to_col_major(xl) if want_col else _from_col_major(xl)
            lat_is_col = want_col
        pl_, pc_ = _win(xl, xc, mods, l, g[1], w_in_bf16)

        convs = (lru_conv_w[l], lru_conv_b[l][None, :], hy_conv_w[l], hy_conv_b[l][None, :])
        cv_l, u_l, x0_l = _short_convs(pl_, *convs, True)
        cv_c, *hy_c = _short_convs(pc_, *convs, not last)

        h_lat = h_ctx = None
        for dr in range(2):
            wax = jnp.concatenate([lru_wa[l, dr], lru_wx[l, dr]], axis=-1).astype(BF16)
            ba, bx, lam = lru_ba[l, dr][None, :], lru_bx[l, dr][None, :], lru_lam[l, dr][None, :]
            h_ctx, state = _lru_scan(cv_c, wax, ba, bx, lam, zero_state, h_ctx, dr == 1)
            h_lat, _ = _lru_scan(cv_l, wax, ba, bx, lam, state, h_lat, dr == 1)

        filt = (filt_w1[l], filt_b1[l], filt_w2[l], filt_b2[l], filt_w3[l], filt_b3[l], filt_w4[l],
                filt_b4[l], filt_freq[l])
        hbias = hy_bias[l][None, :]
        og = out_g[l][None, :]

        kr, ki = _hyena_spectrum(*_hyena_filter(n_lat, *filt))
        y_hy = _hyena_conv(u_l, hbias, kr, ki)
        xl = _mixout(h_lat, pl_, y_hy, x0_l, og, w_out_bf16, xl, mods, l, LAT)

        if last:
            xc = None
        else:
            u_c, x0_c = hy_c
            kr, ki = _hyena_spectrum(*_hyena_filter(n_ctx, *filt))
            y_hy = _hyena_conv(u_c, hbias, kr, ki)
            xc = _mixout(h_ctx, pc_, y_hy, x0_c, og, w_out_bf16, xc, mods, l, CTX)
        xl, xc = _ffn(xl, xc, mods, 2, g[2], ffn_wg, ffn_wu, ffn_wd, l, 1)

    if lat_is_col:
        xl = _from_col_major(xl)
    return _final_norm(xl, final_g[None, :]).reshape(x.shape)
```

```python
import functools
import math

import numpy as np
import jax
import jax.numpy as jnp
from jax import lax
from jax.experimental import pallas as pl
from jax.experimental.pallas import tpu as pltpu

F32 = jnp.float32
BF16 = jnp.bfloat16
EPS = 1e-6
LRU_C = 8.0
GRID_W = 64
HY_BANDS = 16
HY_MAX_DECAY = math.log(1e-2) / 0.3
HY_MIN_DECAY = math.log(1e-2) / 1.5

V7X_VMEM_LIMIT_BYTES = 58 * 1024 * 1024
FFT_B = 128
SUBLANES = 8


def _cparams(sem):
    return pltpu.CompilerParams(dimension_semantics=sem, vmem_limit_bytes=V7X_VMEM_LIMIT_BYTES)


def _rms(x):
    return x * lax.rsqrt(jnp.mean(x * x, axis=-1, keepdims=True) + EPS)


def _modulated(x, shift_ref, scale_ref, g_ref, stream):
    shift = shift_ref[stream:stream + 1, :]
    scale = scale_ref[stream:stream + 1, :]
    return _rms(x) * g_ref[...] * (1.0 + scale) + shift


LANES = 128
N_STREAMS = 2


def _ada_kernel(c_ref, w_ref, b_ref, o_ref):
    o_ref[...] = jnp.zeros_like(o_ref)
    for k in range(N_STREAMS):
        c = c_ref[k]
        s = c * jax.nn.sigmoid(c)
        for j in range(0, w_ref.shape[1], LANES):
            cols = slice(j, j + LANES)
            o_ref[k:k + 1, cols] = jnp.sum(w_ref[:, cols] * s, axis=0, keepdims=True) + b_ref[:, cols]


def _ada_mods(c, c_ctx, ada_w, ada_b):
    depth, d, nm = ada_w.shape
    tn = d
    cc = jnp.broadcast_to(jnp.stack([c[0], c_ctx])[:, :, None], (N_STREAMS, d, LANES))
    return pl.pallas_call(
        _ada_kernel,
        grid=(depth, nm // tn),
        in_specs=[pl.BlockSpec((N_STREAMS, d, LANES), lambda l, j: (0, 0, 0)),
                  pl.BlockSpec((None, d, tn), lambda l, j: (l, 0, j)),
                  pl.BlockSpec((None, 1, tn), lambda l, j: (l, 0, j))],
        out_specs=pl.BlockSpec((None, SUBLANES, tn), lambda l, j: (l, 0, j)),
        out_shape=jax.ShapeDtypeStruct((depth, SUBLANES, nm), F32),
        compiler_params=_cparams(("arbitrary", "arbitrary")),
        name="ada_mods",
    )(cc, ada_w, ada_b[:, None, :])


def _mod_spec(l, k, which, d, tn=None):
    chunk = 3 * k + which
    if tn is None:
        return pl.BlockSpec((None, SUBLANES, d), lambda *g: (l, 0, chunk))
    return pl.BlockSpec((None, SUBLANES, tn), lambda *g: (l, 0, chunk * (d // tn) + g[-1]))


NORM_CHUNK = 256
LAT, CTX = 0, 1


def _rider_spec(rows, tcol, ncol_steps):
    return pl.BlockSpec((rows, tcol), lambda i, c: (0, jnp.where(i == 0, c, ncol_steps - 1)))


def _ffn_up_kernel(*refs, tm, with_ctx):
    if with_ctx:
        x_ref, xc_ref, shift_ref, scale_ref, g_ref, wg_ref, wu_ref, h_ref, hc_ref, u_scr, uc_scr = refs
    else:
        x_ref, shift_ref, scale_ref, g_ref, wg_ref, wu_ref, h_ref, u_scr = refs
    i, f = pl.program_id(0), pl.program_id(1)
    wg = wg_ref[...].astype(BF16)
    wu = wu_ref[...].astype(BF16)

    def swiglu_rows(u):
        g = jnp.dot(u, wg, preferred_element_type=F32)
        up = jnp.dot(u, wu, preferred_element_type=F32)
        return (g * jax.nn.sigmoid(g) * up).astype(BF16)

    @pl.when(f == 0)
    def _():
        for r in range(0, tm, NORM_CHUNK):
            rs = slice(r, min(r + NORM_CHUNK, tm))
            u = _modulated(x_ref[rs, :], shift_ref, scale_ref, g_ref, LAT).astype(BF16)
            u_scr[rs, :] = u
            h_ref[rs, :] = swiglu_rows(u)

    @pl.when(f > 0)
    def _():
        h_ref[...] = swiglu_rows(u_scr[...])

    if with_ctx:
        @pl.when(i == 0)
        def _():
            @pl.when(f == 0)
            def _():
                uc_scr[...] = _modulated(xc_ref[...], shift_ref, scale_ref, g_ref, CTX).astype(BF16)

            hc_ref[...] = swiglu_rows(uc_scr[...])


def _ffn_down_kernel(*refs, with_ctx):
    if with_ctx:
        h_ref, hc_ref, wd_ref, x_ref, xc_ref, gate_ref, o_ref, oc_ref = refs
    else:
        h_ref, wd_ref, x_ref, gate_ref, o_ref = refs
    wd = wd_ref[...].astype(BF16)
    o_ref[...] = x_ref[...] + (0.5 * gate_ref[LAT:LAT + 1, :]) * jnp.dot(h_ref[...], wd,
                                                                          preferred_element_type=F32)
    if with_ctx:
        @pl.when(pl.program_id(0) == 0)
        def _():
            oc_ref[...] = xc_ref[...] + (0.5 * gate_ref[CTX:CTX + 1, :]) * jnp.dot(
                hc_ref[...], wd, preferred_element_type=F32)


def _ffn(x, xc, mods, k, g, wg, wu, wd, l, j):
    rows, d = x.shape
    dff = wg.shape[-1]
    tm, tf, tn = 1024, 512, 256
    nf, nn = dff // tf, d // tn
    with_ctx = xc is not None
    rc = xc.shape[0] if with_ctx else 0
    ctx_in = [xc] if with_ctx else []
    res = pl.pallas_call(
        functools.partial(_ffn_up_kernel, tm=tm, with_ctx=with_ctx),
        grid=(rows // tm, nf),
        in_specs=[pl.BlockSpec((tm, d), lambda i, f: (i, 0))]
                 + ([pl.BlockSpec((rc, d), lambda i, f: (0, 0))] if with_ctx else [])
                 + [_mod_spec(l, k, 0, d), _mod_spec(l, k, 1, d),
                    pl.BlockSpec((1, d), lambda i, f: (0, 0)),
                    pl.BlockSpec((None, None, d, tf), lambda i, f: (l, j, 0, f)),
                    pl.BlockSpec((None, None, d, tf), lambda i, f: (l, j, 0, f))],
        out_specs=[pl.BlockSpec((tm, tf), lambda i, f: (i, f))]
                  + ([_rider_spec(rc, tf, nf)] if with_ctx else []),
        out_shape=[jax.ShapeDtypeStruct((rows, dff), BF16)]
                  + ([jax.ShapeDtypeStruct((rc, dff), BF16)] if with_ctx else []),
        scratch_shapes=[pltpu.VMEM((tm, d), BF16)] + ([pltpu.VMEM((rc, d), BF16)] if with_ctx else []),
        compiler_params=_cparams(("arbitrary", "arbitrary")),
        name="ffn_up",
    )(x, *ctx_in, mods, mods, g, wg, wu)
    h, hc = (res[0], res[1]) if with_ctx else (res[0], None)
    res = pl.pallas_call(
        functools.partial(_ffn_down_kernel, with_ctx=with_ctx),
        grid=(rows // tm, nn),
        in_specs=[pl.BlockSpec((tm, dff), lambda i, n: (i, 0))]
                 + ([pl.BlockSpec((rc, dff), lambda i, n: (0, 0))] if with_ctx else [])
                 + [pl.BlockSpec((None, None, dff, tn), lambda i, n: (l, j, 0, n)),
                    pl.BlockSpec((tm, tn), lambda i, n: (i, n))]
                 + ([_rider_spec(rc, tn, nn)] if with_ctx else [])
                 + [_mod_spec(l, k, 2, d, tn)],
        out_specs=[pl.BlockSpec((tm, tn), lambda i, n: (i, n))]
                  + ([_rider_spec(rc, tn, nn)] if with_ctx else []),
        out_shape=[jax.ShapeDtypeStruct((rows, d), F32)]
                  + ([jax.ShapeDtypeStruct((rc, d), F32)] if with_ctx else []),
        compiler_params=_cparams(("arbitrary", "arbitrary")),
        name="ffn_down",
    )(h, *([hc] if with_ctx else []), wd, x, *ctx_in, mods)
    return (res[0], res[1]) if with_ctx else (res[0], None)


def _win_kernel(x_ref, xc_ref, shift_ref, scale_ref, g_ref, w_ref, p_ref, pc_ref, u_scr, uc_scr, *, tm):
    i, n = pl.program_id(0), pl.program_id(1)
    w = w_ref[...]

    @pl.when(n == 0)
    def _():
        for r in range(0, tm, NORM_CHUNK):
            rs = slice(r, min(r + NORM_CHUNK, tm))
            u = _modulated(x_ref[rs, :], shift_ref, scale_ref, g_ref, LAT).astype(BF16)
            u_scr[rs, :] = u
            p_ref[rs, :] = jnp.dot(u, w, preferred_element_type=F32)

    @pl.when(n > 0)
    def _():
        p_ref[...] = jnp.dot(u_scr[...], w, preferred_element_type=F32)

    @pl.when(i == 0)
    def _():
        @pl.when(n == 0)
        def _():
            uc_scr[...] = _modulated(xc_ref[...], shift_ref, scale_ref, g_ref, CTX).astype(BF16)

        pc_ref[...] = jnp.dot(uc_scr[...], w, preferred_element_type=F32)


def _win(x, xc, mods, l, g, w_in_bf16):
    rows, d = x.shape
    rc = xc.shape[0]
    ncols = w_in_bf16.shape[-1]
    tm, tn = 1024, 1024
    nn = ncols // tn
    return pl.pallas_call(
        functools.partial(_win_kernel, tm=tm),
        grid=(rows // tm, nn),
        in_specs=[pl.BlockSpec((tm, d), lambda i, n: (i, 0)),
                  pl.BlockSpec((rc, d), lambda i, n: (0, 0)),
                  _mod_spec(l, 1, 0, d), _mod_spec(l, 1, 1, d),
                  pl.BlockSpec((1, d), lambda i, n: (0, 0)),
                  pl.BlockSpec((None, d, tn), lambda i, n: (l, 0, n))],
        out_specs=[pl.BlockSpec((tm, tn), lambda i, n: (i, n)), _rider_spec(rc, tn, nn)],
        out_shape=[jax.ShapeDtypeStruct((rows, ncols), F32), jax.ShapeDtypeStruct((rc, ncols), F32)],
        scratch_shapes=[pltpu.VMEM((tm, d), BF16), pltpu.VMEM((rc, d), BF16)],
        compiler_params=_cparams(("arbitrary", "arbitrary")),
        name="mixer_in_proj",
    )(x, xc, mods, mods, g, w_in_bf16)


def _to_col_major(x):
    rows, d = x.shape
    return x.reshape(rows // GRID_W, GRID_W, d).transpose(1, 0, 2).reshape(rows, d)


def _from_col_major(x):
    rows, d = x.shape
    return x.reshape(GRID_W, rows // GRID_W, d).transpose(1, 0, 2).reshape(rows, d)


def _lru_kernel(cv_ref, wax_ref, ba_ref, bx_ref, lam_ref, h0_ref, *rest, t_blk, heads, reverse, add_prev):
    if add_prev:
        prev_ref, o_ref, hf_ref, a_scr, b_scr, carry = rest
    else:
        o_ref, hf_ref, a_scr, b_scr, carry = rest
    c = cv_ref.shape[1]
    hd = c // heads

    @pl.when(pl.program_id(0) == 0)
    def _():
        carry[...] = h0_ref[...]

    cv = cv_ref[...]
    z = -lam_ref[...]
    c_sp = LRU_C * (jnp.maximum(z, 0.0) + jnp.log(1.0 + jnp.exp(-jnp.abs(z))))
    cvb = cv.astype(BF16)
    for h in range(heads):
        sl = slice(h * hd, (h + 1) * hd)
        gx = jnp.dot(cvb[:, sl], wax_ref[h], preferred_element_type=F32)
        r = 0.5 + 0.5 * jnp.tanh(0.5 * (gx[:, :hd] + ba_ref[:, sl]))
        ig = 0.5 + 0.5 * jnp.tanh(0.5 * (gx[:, hd:] + bx_ref[:, sl]))
        a = jnp.exp(-c_sp[:, sl] * r)
        a_scr[:, sl] = a
        b_scr[:, sl] = jnp.sqrt(1.0 - a * a) * (ig * cv[:, sl])

    row = lax.broadcasted_iota(jnp.int32, (SUBLANES, c), 0)
    ngrp = t_blk // SUBLANES

    def group(gi, hc):
        g = (ngrp - 1 - gi) if reverse else gi
        r0 = pl.multiple_of(g * SUBLANES, SUBLANES)
        av = a_scr[pl.ds(r0, SUBLANES), :]
        bv = b_scr[pl.ds(r0, SUBLANES), :]
        for s in (1, 2, 4):
            sh = (SUBLANES - s) if reverse else s
            m = (row < SUBLANES - s) if reverse else (row >= s)
            a_s = pltpu.roll(av, sh, axis=0)
            b_s = pltpu.roll(bv, sh, axis=0)
            bv = jnp.where(m, av * b_s + bv, bv)
            av = jnp.where(m, av * a_s, av)
        hv = av * hc + bv
        if add_prev:
            o_ref[pl.ds(r0, SUBLANES), :] = hv + prev_ref[pl.ds(r0, SUBLANES), :]
        else:
            o_ref[pl.ds(r0, SUBLANES), :] = hv
        return hv[0:1, :] if reverse else hv[SUBLANES - 1:SUBLANES, :]

    hc = lax.fori_loop(0, ngrp, group, carry[...])
    carry[...] = hc
    hf_ref[...] = hc


def _lru_scan(cv, wax, ba, bx, lam, h0, prev, reverse):
    seq, c = cv.shape
    heads = wax.shape[0]
    t_blk = min(seq, 1024)
    nblk = seq // t_blk

    def bi(j):
        return (nblk - 1 - j) if reverse else j

    row = pl.BlockSpec((1, c), lambda j: (0, 0))
    in_specs = [pl.BlockSpec((t_blk, c), lambda j: (bi(j), 0)),
                pl.BlockSpec(wax.shape, lambda j: (0, 0, 0)), row, row, row, row]
    args = [cv, wax, ba, bx, lam, h0]
    if prev is not None:
        in_specs.append(pl.BlockSpec((t_blk, c), lambda j: (bi(j), 0)))
        args.append(prev)
    return pl.pallas_call(
        functools.partial(_lru_kernel, t_blk=t_blk, heads=heads, reverse=reverse, add_prev=prev is not None),
        grid=(nblk,),
        in_specs=in_specs,
        out_specs=[pl.BlockSpec((t_blk, c), lambda j: (bi(j), 0)), row],
        out_shape=[jax.ShapeDtypeStruct((seq, c), F32), jax.ShapeDtypeStruct((1, c), F32)],
        scratch_shapes=[pltpu.VMEM((t_blk, c), F32), pltpu.VMEM((t_blk, c), F32), pltpu.VMEM((1, c), F32)],
        compiler_params=_cparams(("arbitrary",)),
        name="rglru_scan",
    )(*args)


def _short_conv_kernel(*refs, t_blk, nblk, c, with_hyena):
    n_in = 4 if with_hyena else 1
    ins = [refs[3 * k:3 * k + 3] for k in range(n_in)]
    lw_ref, lb_ref, hw_ref, hb_ref = refs[3 * n_in:3 * n_in + 4]
    outs = refs[3 * n_in + 4:-1]
    xs = refs[-1]
    blk = pl.program_id(0)
    zero8 = jnp.zeros((SUBLANES, c), F32)

    def conv(src, w_ref, b_ref, part):
        cur, prv, nxt = src
        xs[0:SUBLANES, :] = jnp.where(blk == 0, zero8, prv[...])
        xs[SUBLANES:SUBLANES + t_blk, :] = cur[...]
        xs[SUBLANES + t_blk:2 * SUBLANES + t_blk, :] = jnp.where(blk == nblk - 1, zero8, nxt[...])
        taps = w_ref.shape[0]
        left = (taps - 1) // 2
        sl = slice(part * c, (part + 1) * c)
        out = b_ref[:, sl]
        for k in range(taps):
            r0 = SUBLANES - left + k
            out = out + xs[r0:r0 + t_blk, :] * w_ref[k:k + 1, sl]
        return out

    outs[0][...] = conv(ins[0], lw_ref, lb_ref, 0)
    if with_hyena:
        outs[2][...] = conv(ins[1], hw_ref, hb_ref, 0)
        x1c = conv(ins[2], hw_ref, hb_ref, 1)
        outs[1][...] = conv(ins[3], hw_ref, hb_ref, 2) * x1c


def _short_convs(p, lru_w, lru_b, hy_w, hy_b, with_hyena):
    seq = p.shape[0]
    c = lru_w.shape[1]
    t_blk = min(seq, 512)
    nblk = seq // t_blk
    per = t_blk // SUBLANES
    nb8 = seq // SUBLANES
    in_specs, args = [], []
    for col in ([0, 2, 3, 4] if with_hyena else [0]):
        in_specs += [pl.BlockSpec((t_blk, c), lambda j, col=col: (j, col)),
                     pl.BlockSpec((SUBLANES, c), lambda j, col=col: (jnp.maximum(j * per - 1, 0), col)),
                     pl.BlockSpec((SUBLANES, c), lambda j, col=col: (jnp.minimum((j + 1) * per, nb8 - 1), col))]
        args += [p, p, p]
    full = lambda a: pl.BlockSpec(a.shape, lambda j: (0, 0))
    in_specs += [full(lru_w), full(lru_b), full(hy_w), full(hy_b)]
    args += [lru_w, lru_b, hy_w, hy_b]
    n_out = 3 if with_hyena else 1
    return pl.pallas_call(
        functools.partial(_short_conv_kernel, t_blk=t_blk, nblk=nblk, c=c, with_hyena=with_hyena),
        grid=(nblk,),
        in_specs=in_specs,
        out_specs=[pl.BlockSpec((t_blk, c), lambda j: (j, 0))] * n_out,
        out_shape=[jax.ShapeDtypeStruct((seq, c), F32)] * n_out,
        scratch_shapes=[pltpu.VMEM((t_blk + 2 * SUBLANES, c), F32)],
        compiler_params=_cparams(("arbitrary",)),
        name="mixer_short_convs",
    )(*args)


FILT_SLOT = 64


def _filt_kernel(w1_ref, b1_ref, w2_ref, b2_ref, w3_ref, b3_ref, w4a_ref, w4b_ref, b4_ref, fr_ref,
                 k_ref, nrm_ref, *, seq, t_blk, c):
    i = pl.program_id(0)
    hi = lax.Precision.HIGHEST
    lanes = 2 * FILT_SLOT
    th = t_blk // 2
    lane = lax.broadcasted_iota(jnp.int32, (1, lanes), 1)
    slot = lane & (FILT_SLOT - 1)
    pos_a = i * t_blk + lax.broadcasted_iota(jnp.int32, (th, 1), 0)
    pos = jnp.where(lane < FILT_SLOT, pos_a, pos_a + th)
    tf = jnp.where(pos < seq, pos, 2 * seq - pos).astype(F32)
    t = tf * (1.0 / (seq - 1))
    w = (2.0 * math.pi / seq) * tf
    is_cos = (slot >= 1) & (slot <= HY_BANDS)
    is_sin = (slot > HY_BANDS) & (slot <= 2 * HY_BANDS)
    band = jnp.where(is_cos, slot - 1, slot - 1 - HY_BANDS).astype(F32)
    f = 1e-4 + band * ((HY_BANDS - 1 - 1e-4) / (HY_BANDS - 1))
    phase = jnp.where(is_cos, 0.5 * math.pi, 0.0)
    sign = jnp.where(is_cos, 1.0, jnp.where(is_sin, -1.0, 0.0))
    z = jnp.where(slot == 0, t, sign * jnp.sin(w * f + phase))
    fr = fr_ref[...]
    h = jnp.sin(fr * (jnp.dot(z, w1_ref[...], precision=hi, preferred_element_type=F32) + b1_ref[...]))
    h = jnp.sin(fr * (jnp.dot(h, w2_ref[...], precision=hi, preferred_element_type=F32) + b2_ref[...]))
    h = jnp.sin(fr * (jnp.dot(h, w3_ref[...], precision=hi, preferred_element_type=F32) + b3_ref[...]))
    ch = lax.broadcasted_iota(jnp.int32, (1, c), 1).astype(F32)
    delta = jnp.abs(HY_MIN_DECAY + ch * ((HY_MAX_DECAY - HY_MIN_DECAY) / (c - 1)))
    part = jnp.zeros((1, c), F32)
    h_hi = h.astype(BF16)
    h_lo = (h - h_hi.astype(F32)).astype(BF16)
    for half, w4_ref in enumerate((w4a_ref, w4b_ref)):
        pos_h = pos_a + half * th
        t_h = jnp.where(pos_h < seq, pos_h, 2 * seq - pos_h).astype(F32) * (1.0 / (seq - 1))
        w_hi, w_lo = w4_ref[0], w4_ref[1]
        k = (jnp.dot(h_hi, w_hi, preferred_element_type=F32) + jnp.dot(h_lo, w_hi, preferred_element_type=F32)
             + jnp.dot(h_hi, w_lo, preferred_element_type=F32)) + b4_ref[...]
        k = jnp.where(pos_h != seq, k * jnp.exp(-t_h * delta), 0.0)
        k_ref[half * th:(half + 1) * th, :] = k
        part = part + jnp.sum(jnp.abs(k), axis=0, keepdims=True)

    @pl.when(i == 0)
    def _():
        nrm_ref[...] = part

    @pl.when(i > 0)
    def _():
        nrm_ref[...] = nrm_ref[...] + part


def _slots(a, row_used):
    r, cdim = a.shape
    blk = jnp.pad(a, ((0, (FILT_SLOT if row_used else 1) - r), (0, FILT_SLOT - cdim)))
    if not row_used:
        return jnp.concatenate([blk, blk], axis=1)
    z = jnp.zeros_like(blk)
    return jnp.concatenate([jnp.concatenate([blk, z], axis=1), jnp.concatenate([z, blk], axis=1)], axis=0)


def _hyena_filter(seq, w1, b1, w2, b2, w3, b3, w4, b4, freq):
    c = w4.shape[1] // 2
    od = w2.shape[0]
    assert od <= FILT_SLOT and w1.shape[0] <= FILT_SLOT
    lanes = 2 * FILT_SLOT
    t_blk = min(seq, 1024)
    half_blocks = seq // t_blk
    full = lambda shape: pl.BlockSpec(shape, lambda i: (0,) * len(shape))
    sq, row = (lanes, lanes), (1, lanes)
    w4p = jnp.pad(w4, ((0, FILT_SLOT - od), (0, 0)))
    w4_hi = w4p.astype(BF16)
    w4_lo = (w4p - w4_hi.astype(F32)).astype(BF16)
    w4s = jnp.stack([w4_hi, w4_lo])
    w4a = jnp.concatenate([w4s, jnp.zeros_like(w4s)], axis=1)
    w4b = jnp.concatenate([jnp.zeros_like(w4s), w4s], axis=1)
    w4_spec = pl.BlockSpec((2, lanes, c), lambda i: (0, 0, i // half_blocks))
    return pl.pallas_call(
        functools.partial(_filt_kernel, seq=seq, t_blk=t_blk, c=c),
        grid=(2 * seq // t_blk,),
        in_specs=[full(sq), full(row), full(sq), full(row), full(sq), full(row), w4_spec, w4_spec,
                  pl.BlockSpec((1, c), lambda i: (0, i // half_blocks)),
                  full(row)],
        out_specs=[pl.BlockSpec((t_blk, c), lambda i: (i, 0)), pl.BlockSpec((1, c), lambda i: (0, 0))],
        out_shape=[jax.ShapeDtypeStruct((2 * seq, c), F32), jax.ShapeDtypeStruct((1, c), F32)],
        compiler_params=_cparams(("arbitrary",)),
        name="hyena_filter",
    )(_slots(w1, True), _slots(b1[None, :], False), _slots(w2, True), _slots(b2[None, :], False),
      _slots(w3, True), _slots(b3[None, :], False), w4a, w4b, b4[None, :], _slots(freq[None, :], False))


def _bitrev(p, bits):
    r = 0
    for k in range(bits):
        r = (r << 1) | ((p >> k) & 1)
    return r


def _real_block(z):
    return np.block([[z.real, -z.imag], [z.imag, z.real]])


def _fft_split(s2):
    n_lo = max(min(8, s2 // 2), 1)
    return s2 // n_lo, n_lo


@functools.lru_cache(maxsize=None)
def _fft_tables(s):
    s2 = s // 2
    n = s * FFT_B
    n_hi, n_lo = _fft_split(s2)
    bits_hi, bits_lo = n_hi.bit_length() - 1, n_lo.bit_length() - 1
    b = np.arange(FFT_B)
    dft = np.exp(-2j * np.pi * np.outer(b, b) / FFT_B)
    e_lo = [2 * n_hi * _bitrev(p, bits_lo) for p in range(n_lo)]
    g = [dft * np.exp(-2j * np.pi * b * e / n)[None, :] for e in e_lo]
    chunks = np.arange(FFT_B).reshape(-1, FFT_ROWS)
    order = np.concatenate([np.concatenate([c, c + FFT_B]) for c in chunks])
    lhs_f = np.stack([_real_block(x)[order] for x in g])
    lhs_i = np.stack([_real_block(np.conj(x).T)[order] for x in g])
    e_hi = np.array([[2 * _bitrev(p, bits_hi) + hf for p in range(n_hi)] for hf in range(2)])
    ang = 2.0 * np.pi * e_hi[:, :, None, None] * b[None, None, :, None] / n * np.ones((1, 1, 1, FFT_B))
    lvl = 2.0 * np.pi * np.arange(s2) / s2
    first = np.concatenate([np.zeros(s2), 2.0 * np.pi * np.arange(s2) / s])
    f32 = lambda x: np.asarray(x, np.float32)
    return dict(lhs_f=f32(lhs_f), lhs_i=f32(lhs_i), col_c=f32(np.cos(ang)), col_s=f32(np.sin(ang)),
                lvl_c=f32(np.cos(lvl)), lvl_s=f32(np.sin(lvl)), first_c=f32(np.cos(first)),
                first_s=f32(np.sin(first)))


FFT_ROWS = 32


def _cmul(xr, xi, c, s):
    return xr * c + xi * s, xi * c - xr * s


def _cmul_conj(xr, xi, c, s):
    return xr * c - xi * s, xi * c + xr * s


def _pass_plan(s2):
    levels = s2.bit_length() - 1
    plan, h = [], s2 // 2
    if levels % 2 == 1:
        plan.append((h, 2))
        h //= 2
    while h >= 1:
        plan.append((h, 4))
        h //= 4
    return plan


def _butterfly_pass(load, store, lc_ref, ls_ref, s2, h, radix, inverse):
    h2 = h // 2 if radix == 4 else h
    trivial = h2 == 1
    shift = h2.bit_length() - 1

    def group(idx, carry):
        g = idx >> shift
        i = idx & (h2 - 1)
        base = g * 2 * h + i
        blocks = [base + k * h2 for k in range(radix)]
        c1 = s1 = c2 = s2_ = c3 = s3 = None
        if not trivial:
            t1 = i * (s2 // (2 * h))
            c1, s1 = lc_ref[t1], ls_ref[t1]
            if radix == 4:
                c2, s2_, c3, s3 = lc_ref[2 * t1], ls_ref[2 * t1], lc_ref[3 * t1], ls_ref[3 * t1]
        fwd_mul = (lambda xr, xi, c, s: (xr, xi)) if trivial else _cmul
        inv_mul = (lambda xr, xi, c, s: (xr, xi)) if trivial else _cmul_conj
        for r in range(0, FFT_B, FFT_ROWS):
            x = [load(bk, r) for bk in blocks]
            if radix == 2:
                (ar, ai), (br, bi) = x
                if inverse:
                    br, bi = inv_mul(br, bi, c1, s1)
                    out = [(ar + br, ai + bi), (ar - br, ai - bi)]
                else:
                    out = [(ar + br, ai + bi), fwd_mul(ar - br, ai - bi, c1, s1)]
            elif inverse:
                (x0r, x0i), (x1r, x1i), (x2r, x2i), (x3r, x3i) = x
                x1r, x1i = inv_mul(x1r, x1i, c2, s2_)
                x2r, x2i = inv_mul(x2r, x2i, c1, s1)
                x3r, x3i = inv_mul(x3r, x3i, c3, s3)
                t0r, t0i, t1r, t1i = x0r + x1r, x0i + x1i, x0r - x1r, x0i - x1i
                t2r, t2i, t3r, t3i = x2r + x3r, x2i + x3i, x2r - x3r, x2i - x3i
                out = [(t0r + t2r, t0i + t2i), (t1r - t3i, t1i + t3r),
                       (t0r - t2r, t0i - t2i), (t1r + t3i, t1i - t3r)]
            else:
                (x0r, x0i), (x1r, x1i), (x2r, x2i), (x3r, x3i) = x
                t0r, t0i, t1r, t1i = x0r + x2r, x0i + x2i, x1r + x3r, x1i + x3i
                t2r, t2i, t3r, t3i = x0r - x2r, x0i - x2i, x1r - x3r, x1i - x3i
                out = [(t0r + t1r, t0i + t1i), fwd_mul(t0r - t1r, t0i - t1i, c2, s2_),
                       fwd_mul(t2r + t3i, t2i - t3r, c1, s1), fwd_mul(t2r - t3i, t2i + t3r, c3, s3)]
            for bk, (o_r, o_i) in zip(blocks, out):
                store(bk, r, o_r, o_i)
        return carry

    lax.fori_loop(0, s2 // radix, group, 0)


def _work_access(wr, wi):
    def rows(bk, r):
        return pl.ds(pl.multiple_of(bk * FFT_B + r, FFT_ROWS), FFT_ROWS)

    def load(bk, r):
        return wr[rows(bk, r), :], wi[rows(bk, r), :]

    def store(bk, r, re, im):
        wr[rows(bk, r), :] = re
        wi[rows(bk, r), :] = im

    return load, store


def _run_passes(plan, first_load, last_store, work, lc_ref, ls_ref, s2, inverse):
    w_load, w_store = work
    for k, (h, radix) in enumerate(plan):
        load = first_load if k == 0 else w_load
        store = last_store if k == len(plan) - 1 else w_store
        _butterfly_pass(load, store, lc_ref, ls_ref, s2, h, radix, inverse)


MID_UNROLL = 32


def _pair_blocks(pp, s2):
    n_hi, n_lo = _fft_split(s2)
    half = n_hi // 2
    p_lo = pp >> (half.bit_length() - 1)
    j = pp & (half - 1)
    return p_lo, [(2 * j + q, pl.multiple_of(((2 * j + q) * n_lo + p_lo) * FFT_B, FFT_B)) for q in range(2)]


def _for_pairs(s2, body):
    def step(pp, carry):
        body(pp)
        return carry
    lax.fori_loop(0, s2 // 2, step, 0, unroll=min(MID_UNROLL, s2 // 2))


def _pack_store(panel, cc_ref, cs_ref, s2):
    n_hi, n_lo = _fft_split(s2)
    lo_bits = n_lo.bit_length() - 1

    def store(bk, r, re, im):
        p_hi = bk >> lo_bits
        p_lo = bk & (n_lo - 1)
        pp = p_lo * (n_hi // 2) + (p_hi >> 1)
        q = p_hi & 1
        tr, ti = _cmul(re, im, cc_ref[p_hi, r:r + FFT_ROWS, :], cs_ref[p_hi, r:r + FFT_ROWS, :])
        panel[pp, q, r:r + FFT_ROWS, :] = tr.astype(BF16)
        panel[pp, q, FFT_B + r:FFT_B + r + FFT_ROWS, :] = ti.astype(BF16)

    return store


def _dft_pair(m_ref, panel, pp, p_lo):
    blk = FFT_B
    rhs = jnp.concatenate([panel[pp, 0], panel[pp, 1]], axis=1)
    out = jnp.dot(m_ref[p_lo], rhs, preferred_element_type=F32)
    for r in range(0, blk, FFT_ROWS):
        for q in range(2):
            yield (q, r, out[2 * r:2 * r + FFT_ROWS, q * blk:(q + 1) * blk],
                   out[2 * r + FFT_ROWS:2 * r + 2 * FFT_ROWS, q * blk:(q + 1) * blk])


def _spec_kernel(lo_ref, hi_ref, nrm_ref, fc_ref, fs_ref, lc_ref, ls_ref, cc_ref, cs_ref, mf_ref,
                 kr_ref, ki_ref, panel, *, s2):
    blk = FFT_B
    hf = pl.program_id(1)
    n = 2 * s2 * blk
    sign = (1 - 2 * hf).astype(F32)

    def first_load(bk, r):
        rows = pl.ds(pl.multiple_of(bk * blk + r, FFT_ROWS), FFT_ROWS)
        d = lo_ref[rows, :] + sign * hi_ref[rows, :]
        return d * fc_ref[hf * s2 + bk], -d * fs_ref[hf * s2 + bk]

    _run_passes(_pass_plan(s2), first_load, _pack_store(panel, cc_ref, cs_ref, s2),
                _work_access(kr_ref, ki_ref), lc_ref, ls_ref, s2, inverse=False)
    scale = 1.0 / (nrm_ref[...] * n)

    def dft(pp):
        p_lo, blocks = _pair_blocks(pp, s2)
        for q, r, yr, yi in _dft_pair(mf_ref, panel, pp, p_lo):
            rows = pl.ds(blocks[q][1] + r, FFT_ROWS)
            kr_ref[rows, :] = yr * scale
            ki_ref[rows, :] = yi * scale

    _for_pairs(s2, dft)


def _conv_kernel(u_ref, bias_ref, kr_ref, ki_ref, fc_ref, fs_ref, lc_ref, ls_ref, cc_ref, cs_ref,
                 mf_ref, mi_ref, y_ref, wr, wi, panel, *, s2):
    blk = FFT_B
    hf = pl.program_id(1)
    plan = _pass_plan(s2)
    work = _work_access(wr, wi)

    def first_load(bk, r):
        ub = u_ref[pl.ds(pl.multiple_of(bk * blk + r, FFT_ROWS), FFT_ROWS), :]
        return ub * fc_ref[hf * s2 + bk], -ub * fs_ref[hf * s2 + bk]

    _run_passes(plan, first_load, _pack_store(panel, cc_ref, cs_ref, s2), work, lc_ref, ls_ref, s2,
                inverse=False)

    def spectrum_product(pp):
        p_lo, blocks = _pair_blocks(pp, s2)
        for q, r, yr, yi in _dft_pair(mf_ref, panel, pp, p_lo):
            rows = pl.ds(blocks[q][1] + r, FFT_ROWS)
            kr, ki = kr_ref[rows, :], ki_ref[rows, :]
            panel[pp, q, r:r + FFT_ROWS, :] = (yr * kr - yi * ki).astype(BF16)
            panel[pp, q, blk + r:blk + r + FFT_ROWS, :] = (yr * ki + yi * kr).astype(BF16)

    _for_pairs(s2, spectrum_product)

    def inverse_dft(pp):
        p_lo, blocks = _pair_blocks(pp, s2)
        for q, r, er, ei in _dft_pair(mi_ref, panel, pp, p_lo):
            p_hi, r0 = blocks[q]
            rows = pl.ds(r0 + r, FFT_ROWS)
            wr[rows, :], wi[rows, :] = _cmul_conj(er, ei, cc_ref[p_hi, r:r + FFT_ROWS, :],
                                                  cs_ref[p_hi, r:r + FFT_ROWS, :])

    _for_pairs(s2, inverse_dft)

    def rows_of(bk, r):
        return pl.ds(pl.multiple_of(bk * blk + r, FFT_ROWS), FFT_ROWS)

    def store_even(bk, r, er, ei):
        y_ref[rows_of(bk, r), :] = er

    def store_odd(bk, r, er, ei):
        rows = rows_of(bk, r)
        val = er * fc_ref[s2 + bk] - ei * fs_ref[s2 + bk]
        y_ref[rows, :] = y_ref[rows, :] + u_ref[rows, :] * bias_ref[...] + val

    inv_plan = plan[::-1]
    if len(inv_plan) > 1:
        _run_passes(inv_plan[:-1], work[0], work[1], work, lc_ref, ls_ref, s2, inverse=True)
    h_last, radix_last = inv_plan[-1]
    for half, store in ((0, store_even), (1, store_odd)):
        @pl.when(hf == half)
        def _(store=store):
            _butterfly_pass(work[0], store, lc_ref, ls_ref, s2, h_last, radix_last, True)


def _smem_spec():
    return pl.BlockSpec(memory_space=pltpu.SMEM)


def _fft_table_specs(s2):
    n_hi, n_lo = _fft_split(s2)
    col = pl.BlockSpec((None, n_hi, FFT_B, FFT_B), lambda ct, hf: (hf, 0, 0, 0))
    lhs = pl.BlockSpec((n_lo, 2 * FFT_B, 2 * FFT_B), lambda ct, hf: (0, 0, 0))
    return [_smem_spec(), _smem_spec(), _smem_spec(), _smem_spec(), col, col], lhs


def _hyena_spectrum(kfull, nrm):
    n, c = kfull.shape
    seq = n // 2
    s2 = seq // FFT_B
    tb = _fft_tables(2 * s2)
    lanes = 128
    tab_specs, lhs_spec = _fft_table_specs(s2)
    return pl.pallas_call(
        functools.partial(_spec_kernel, s2=s2),
        grid=(c // lanes, 2),
        in_specs=[pl.BlockSpec((seq, lanes), lambda ct, hf: (0, ct)),
                  pl.BlockSpec((seq, lanes), lambda ct, hf: (1, ct)),
                  pl.BlockSpec((1, lanes), lambda ct, hf: (0, ct))] + tab_specs + [lhs_spec],
        out_specs=[pl.BlockSpec((None, seq, lanes), lambda ct, hf: (hf, 0, ct)),
                   pl.BlockSpec((None, seq, lanes), lambda ct, hf: (hf, 0, ct))],
        out_shape=[jax.ShapeDtypeStruct((2, seq, c), F32), jax.ShapeDtypeStruct((2, seq, c), F32)],
        scratch_shapes=[pltpu.VMEM((s2 // 2, 2, 2 * FFT_B, FFT_B), BF16)],
        compiler_params=_cparams(("arbitrary", "arbitrary")),
        name="hyena_spectrum",
    )(kfull, kfull, nrm, tb["first_c"], tb["first_s"], tb["lvl_c"], tb["lvl_s"],
      tb["col_c"], tb["col_s"], jnp.asarray(tb["lhs_f"], BF16))


def _hyena_conv(u, bias, kr, ki):
    seq, c = u.shape
    s2 = seq // FFT_B
    tb = _fft_tables(2 * s2)
    lanes = 128
    tab_specs, lhs_spec = _fft_table_specs(s2)
    return pl.pallas_call(
        functools.partial(_conv_kernel, s2=s2),
        grid=(c // lanes, 2),
        in_specs=[pl.BlockSpec((seq, lanes), lambda ct, hf: (0, ct)),
                  pl.BlockSpec((1, lanes), lambda ct, hf: (0, ct)),
                  pl.BlockSpec((None, seq, lanes), lambda ct, hf: (hf, 0, ct)),
                  pl.BlockSpec((None, seq, lanes), lambda ct, hf: (hf, 0, ct))]
                 + tab_specs + [lhs_spec, lhs_spec],
        out_specs=pl.BlockSpec((seq, lanes), lambda ct, hf: (0, ct)),
        out_shape=jax.ShapeDtypeStruct((seq, c), F32),
        scratch_shapes=[pltpu.VMEM((seq, lanes), F32), pltpu.VMEM((seq, lanes), F32),
                        pltpu.VMEM((s2 // 2, 2, 2 * FFT_B, FFT_B), BF16)],
        compiler_params=_cparams(("arbitrary", "arbitrary")),
        name="hyena_fft_conv",
    )(u, bias, kr, ki, tb["first_c"], tb["first_s"], tb["lvl_c"], tb["lvl_s"],
      tb["col_c"], tb["col_s"], jnp.asarray(tb["lhs_f"], BF16), jnp.asarray(tb["lhs_i"], BF16))


MIXOUT_CHUNK = 256


def _mixout_kernel(h_ref, yr_ref, hy_ref, x0_ref, og_ref, w_ref, x_ref, gate_ref, o_ref, *, stream, lw, tm):
    gate = gate_ref[stream:stream + 1, :]
    w = w_ref[...]
    for r in range(0, tm, MIXOUT_CHUNK):
        rs = slice(r, r + min(MIXOUT_CHUNK, tm))
        yr = yr_ref[rs, :]
        gelu = 0.5 * yr * (1.0 + jnp.tanh(math.sqrt(2.0 / math.pi) * (yr + 0.044715 * (yr * yr * yr))))
        lru = (_rms(h_ref[rs, :] * gelu) * og_ref[:, :lw]).astype(BF16)
        hy = (_rms(hy_ref[rs, :] * x0_ref[rs, :]) * og_ref[:, lw:]).astype(BF16)
        y = (jnp.dot(lru, w[:lw, :], preferred_element_type=F32)
             + jnp.dot(hy, w[lw:, :], preferred_element_type=F32))
        o_ref[rs, :] = x_ref[rs, :] + gate * y


def _mixout(h_lru, p, y_hy, x0c, og, w_out_bf16, x, mods, l, stream):
    rows, d = x.shape
    lw = h_lru.shape[1]
    dm = w_out_bf16.shape[1]
    tm = min(rows, 512)
    row_blk = lambda cols, j: pl.BlockSpec((tm, cols), lambda i: (i, j))
    return pl.pallas_call(
        functools.partial(_mixout_kernel, stream=stream, lw=lw, tm=tm),
        grid=(rows // tm,),
        in_specs=[row_blk(lw, 0), row_blk(lw, 1), row_blk(dm - lw, 0), row_blk(dm - lw, 0),
                  pl.BlockSpec((1, dm), lambda i: (0, 0)),
                  pl.BlockSpec((None, dm, d), lambda i: (l, 0, 0)),
                  row_blk(d, 0),
                  _mod_spec(l, 1, 2, d)],
        out_specs=row_blk(d, 0),
        out_shape=jax.ShapeDtypeStruct((rows, d), F32),
        compiler_params=_cparams(("arbitrary",)),
        name="mixer_out_proj",
    )(h_lru, p, y_hy, x0c, og, w_out_bf16, x, mods)


def _final_kernel(x_ref, g_ref, o_ref):
    o_ref[...] = _rms(x_ref[...]) * g_ref[...]


def _final_norm(x, g):
    rows, d = x.shape
    tm = 1024
    return pl.pallas_call(
        _final_kernel,
        grid=(rows // tm,),
        in_specs=[pl.BlockSpec((tm, d), lambda i: (i, 0)), pl.BlockSpec((1, d), lambda i: (0, 0))],
        out_specs=pl.BlockSpec((tm, d), lambda i: (i, 0)),
        out_shape=jax.ShapeDtypeStruct((rows, d), F32),
        compiler_params=_cparams(("arbitrary",)),
        name="final_norm",
    )(x, g)


def kernel(x, c, ctx, c_ctx, ada_w, ada_b, norm_g, ffn_wg, ffn_wu, ffn_wd, w_in, w_out, out_g, lru_conv_w, lru_conv_b, lru_wa, lru_ba, lru_wx, lru_bx, lru_lam, hy_conv_w, hy_conv_b, hy_bias, filt_w1, filt_b1, filt_w2, filt_b2, filt_w3, filt_b3, filt_w4, filt_b4, filt_freq, final_g):
    assert x.shape[0] == 1 and ctx.shape[0] == 1
    depth = ada_w.shape[0]
    d = x.shape[-1]
    lw = lru_conv_w.shape[-1]
    n_lat, n_ctx = x.shape[1], ctx.shape[1]
    xl, xc = x.reshape(n_lat, d), ctx.reshape(n_ctx, d)
    mods = _ada_mods(c, c_ctx, ada_w, ada_b)
    w_in_bf16, w_out_bf16 = w_in.astype(BF16), w_out.astype(BF16)
    zero_state = jnp.zeros((1, lw), F32)
    lat_is_col = False
    for l in range(depth):
        last = l == depth - 1
        g = norm_g[l][:, None, :]

        xl, xc = _ffn(xl, xc, mods, 0, g[0], ffn_wg, ffn_wu, ffn_wd, l, 0)

        want_col = l % 2 == 1
        if want_col != lat_is_col:
            xl = _to_col_major(xl) if want_col else _from_col_major(xl)
            lat_is_col = want_col
        pl_, pc_ = _win(xl, xc, mods, l, g[1], w_in_bf16)

        convs = (lru_conv_w[l], lru_conv_b[l][None, :], hy_conv_w[l], hy_conv_b[l][None, :])
        cv_l, u_l, x0_l = _short_convs(pl_, *convs, True)
        cv_c, *hy_c = _short_convs(pc_, *convs, not last)

        h_lat = h_ctx = None
        for dr in range(2):
            wax = jnp.concatenate([lru_wa[l, dr], lru_wx[l, dr]], axis=-1).astype(BF16)
            ba, bx, lam = lru_ba[l, dr][None, :], lru_bx[l, dr][None, :], lru_lam[l, dr][None, :]
            h_ctx, state = _lru_scan(cv_c, wax, ba, bx, lam, zero_state, h_ctx, dr == 1)
            h_lat, _ = _lru_scan(cv_l, wax, ba, bx, lam, state, h_lat, dr == 1)

        filt = (filt_w1[l], filt_b1[l], filt_w2[l], filt_b2[l], filt_w3[l], filt_b3[l], filt_w4[l],
                filt_b4[l], filt_freq[l])
        hbias = hy_bias[l][None, :]
        og = out_g[l][None, :]

        kr, ki = _hyena_spectrum(*_hyena_filter(n_lat, *filt))
        y_hy = _hyena_conv(u_l, hbias, kr, ki)
        xl = _mixout(h_lat, pl_, y_hy, x0_l, og, w_out_bf16, xl, mods, l, LAT)

        if last:
            xc = None
        else:
            u_c, x0_c = hy_c
            kr, ki = _hyena_spectrum(*_hyena_filter(n_ctx, *filt))
            y_hy = _hyena_conv(u_c, hbias, kr, ki)
            xc = _mixout(h_ctx, pc_, y_hy, x0_c, og, w_out_bf16, xc, mods, l, CTX)
        xl, xc = _ffn(xl, xc, mods, 2, g[2], ffn_wg, ffn_wu, ffn_wd, l, 1)

    if lat_is_col:
        xl = _from_col_major(xl)
    return _final_norm(xl, final_g[None, :]).reshape(x.shape)
```

```python
import functools
import math

import numpy as np
import jax
import jax.numpy as jnp
from jax import lax
from jax.experimental import pallas as pl
from jax.experimental.pallas import tpu as pltpu

F32 = jnp.float32
BF16 = jnp.bfloat16
EPS = 1e-6
LRU_C = 8.0
GRID_W = 64
HY_BANDS = 16
HY_MAX_DECAY = math.log(1e-2) / 0.3
HY_MIN_DECAY = math.log(1e-2) / 1.5

V7X_VMEM_LIMIT_BYTES = 58 * 1024 * 1024
FFT_B = 128
SUBLANES = 8


def _cparams(sem):
    return pltpu.CompilerParams(dimension_semantics=sem, vmem_limit_bytes=V7X_VMEM_LIMIT_BYTES)


def _rms(x):
    return x * lax.rsqrt(jnp.mean(x * x, axis=-1, keepdims=True) + EPS)


def _modulated(x, shift_ref, scale_ref, g_ref, stream):
    shift = shift_ref[stream:stream + 1, :]
    scale = scale_ref[stream:stream + 1, :]
    return _rms(x) * g_ref[...] * (1.0 + scale) + shift


LANES = 128
N_STREAMS = 2


def _ada_kernel(c_ref, w_ref, b_ref, o_ref):
    o_ref[...] = jnp.zeros_like(o_ref)
    for k in range(N_STREAMS):
        c = c_ref[k]
        s = c * jax.nn.sigmoid(c)
        for j in range(0, w_ref.shape[1], LANES):
            cols = slice(j, j + LANES)
            o_ref[k:k + 1, cols] = jnp.sum(w_ref[:, cols] * s, axis=0, keepdims=True) + b_ref[:, cols]


def _ada_mods(c, c_ctx, ada_w, ada_b):
    depth, d, nm = ada_w.shape
    tn = d
    cc = jnp.broadcast_to(jnp.stack([c[0], c_ctx])[:, :, None], (N_STREAMS, d, LANES))
    return pl.pallas_call(
        _ada_kernel,
        grid=(depth, nm // tn),
        in_specs=[pl.BlockSpec((N_STREAMS, d, LANES), lambda l, j: (0, 0, 0)),
                  pl.BlockSpec((None, d, tn), lambda l, j: (l, 0, j)),
                  pl.BlockSpec((None, 1, tn), lambda l, j: (l, 0, j))],
        out_specs=pl.BlockSpec((None, SUBLANES, tn), lambda l, j: (l, 0, j)),
        out_shape=jax.ShapeDtypeStruct((depth, SUBLANES, nm), F32),
        compiler_params=_cparams(("arbitrary", "arbitrary")),
        name="ada_mods",
    )(cc, ada_w, ada_b[:, None, :])


def _mod_spec(l, k, which, d, tn=None):
    chunk = 3 * k + which
    if tn is None:
        return pl.BlockSpec((None, SUBLANES, d), lambda *g: (l, 0, chunk))
    return pl.BlockSpec((None, SUBLANES, tn), lambda *g: (l, 0, chunk * (d // tn) + g[-1]))


NORM_CHUNK = 256
LAT, CTX = 0, 1


def _rider_spec(rows, tcol, ncol_steps):
    return pl.BlockSpec((rows, tcol), lambda i, c: (0, jnp.where(i == 0, c, ncol_steps - 1)))


def _ffn_up_kernel(*refs, tm, with_ctx):
    if with_ctx:
        x_ref, xc_ref, shift_ref, scale_ref, g_ref, wg_ref, wu_ref, h_ref, hc_ref, u_scr, uc_scr = refs
    else:
        x_ref, shift_ref, scale_ref, g_ref, wg_ref, wu_ref, h_ref, u_scr = refs
    i, f = pl.program_id(0), pl.program_id(1)
    wg = wg_ref[...].astype(BF16)
    wu = wu_ref[...].astype(BF16)

    def swiglu_rows(u):
        g = jnp.dot(u, wg, preferred_element_type=F32)
        up = jnp.dot(u, wu, preferred_element_type=F32)
        return (g * jax.nn.sigmoid(g) * up).astype(BF16)

    @pl.when(f == 0)
    def _():
        for r in range(0, tm, NORM_CHUNK):
            rs = slice(r, min(r + NORM_CHUNK, tm))
            u = _modulated(x_ref[rs, :], shift_ref, scale_ref, g_ref, LAT).astype(BF16)
            u_scr[rs, :] = u
            h_ref[rs, :] = swiglu_rows(u)

    @pl.when(f > 0)
    def _():
        h_ref[...] = swiglu_rows(u_scr[...])

    if with_ctx:
        @pl.when(i == 0)
        def _():
            @pl.when(f == 0)
            def _():
                uc_scr[...] = _modulated(xc_ref[...], shift_ref, scale_ref, g_ref, CTX).astype(BF16)

            hc_ref[...] = swiglu_rows(uc_scr[...])


def _ffn_down_kernel(*refs, with_ctx):
    if with_ctx:
        h_ref, hc_ref, wd_ref, x_ref, xc_ref, gate_ref, o_ref, oc_ref = refs
    else:
        h_ref, wd_ref, x_ref, gate_ref, o_ref = refs
    wd = wd_ref[...].astype(BF16)
    o_ref[...] = x_ref[...] + (0.5 * gate_ref[LAT:LAT + 1, :]) * jnp.dot(h_ref[...], wd,
                                                                          preferred_element_type=F32)
    if with_ctx:
        @pl.when(pl.program_id(0) == 0)
        def _():
            oc_ref[...] = xc_ref[...] + (0.5 * gate_ref[CTX:CTX + 1, :]) * jnp.dot(
                hc_ref[...], wd, preferred_element_type=F32)


def _ffn(x, xc, mods, k, g, wg, wu, wd, l, j):
    rows, d = x.shape
    dff = wg.shape[-1]
    tm, tf, tn = 1024, 512, 256
    nf, nn = dff // tf, d // tn
    with_ctx = xc is not None
    rc = xc.shape[0] if with_ctx else 0
    ctx_in = [xc] if with_ctx else []
    res = pl.pallas_call(
        functools.partial(_ffn_up_kernel, tm=tm, with_ctx=with_ctx),
        grid=(rows // tm, nf),
        in_specs=[pl.BlockSpec((tm, d), lambda i, f: (i, 0))]
                 + ([pl.BlockSpec((rc, d), lambda i, f: (0, 0))] if with_ctx else [])
                 + [_mod_spec(l, k, 0, d), _mod_spec(l, k, 1, d),
                    pl.BlockSpec((1, d), lambda i, f: (0, 0)),
                    pl.BlockSpec((None, None, d, tf), lambda i, f: (l, j, 0, f)),
                    pl.BlockSpec((None, None, d, tf), lambda i, f: (l, j, 0, f))],
        out_specs=[pl.BlockSpec((tm, tf), lambda i, f: (i, f))]
                  + ([_rider_spec(rc, tf, nf)] if with_ctx else []),
        out_shape=[jax.ShapeDtypeStruct((rows, dff), BF16)]
                  + ([jax.ShapeDtypeStruct((rc, dff), BF16)] if with_ctx else []),
        scratch_shapes=[pltpu.VMEM((tm, d), BF16)] + ([pltpu.VMEM((rc, d), BF16)] if with_ctx else []),
        compiler_params=_cparams(("arbitrary", "arbitrary")),
        name="ffn_up",
    )(x, *ctx_in, mods, mods, g, wg, wu)
    h, hc = (res[0], res[1]) if with_ctx else (res[0], None)
    res = pl.pallas_call(
        functools.partial(_ffn_down_kernel, with_ctx=with_ctx),
        grid=(rows // tm, nn),
        in_specs=[pl.BlockSpec((tm, dff), lambda i, n: (i, 0))]
                 + ([pl.BlockSpec((rc, dff), lambda i, n: (0, 0))] if with_ctx else [])
                 + [pl.BlockSpec((None, None, dff, tn), lambda i, n: (l, j, 0, n)),
                    pl.BlockSpec((tm, tn), lambda i, n: (i, n))]
                 + ([_rider_spec(rc, tn, nn)] if with_ctx else [])
                 + [_mod_spec(l, k, 2, d, tn)],
        out_specs=[pl.BlockSpec((tm, tn), lambda i, n: (i, n))]
                  + ([_rider_spec(rc, tn, nn)] if with_ctx else []),
        out_shape=[jax.ShapeDtypeStruct((rows, d), F32)]
                  + ([jax.ShapeDtypeStruct((rc, d), F32)] if with_ctx else []),
        compiler_params=_cparams(("arbitrary", "arbitrary")),
        name="ffn_down",
    )(h, *([hc] if with_ctx else []), wd, x, *ctx_in, mods)
    return (res[0], res[1]) if with_ctx else (res[0], None)


def _win_kernel(x_ref, xc_ref, shift_ref, scale_ref, g_ref, w_ref, p_ref, pc_ref, u_scr, uc_scr, *, tm):
    i, n = pl.program_id(0), pl.program_id(1)
    w = w_ref[...]

    @pl.when(n == 0)
    def _():
        for r in range(0, tm, NORM_CHUNK):
            rs = slice(r, min(r + NORM_CHUNK, tm))
            u = _modulated(x_ref[rs, :], shift_ref, scale_ref, g_ref, LAT).astype(BF16)
            u_scr[rs, :] = u
            p_ref[rs, :] = jnp.dot(u, w, preferred_element_type=F32)

    @pl.when(n > 0)
    def _():
        p_ref[...] = jnp.dot(u_scr[...], w, preferred_element_type=F32)

    @pl.when(i == 0)
    def _():
        @pl.when(n == 0)
        def _():
            uc_scr[...] = _modulated(xc_ref[...], shift_ref, scale_ref, g_ref, CTX).astype(BF16)

        pc_ref[...] = jnp.dot(uc_scr[...], w, preferred_element_type=F32)


def _win(x, xc, mods, l, g, w_in_bf16):
    rows, d = x.shape
    rc = xc.shape[0]
    ncols = w_in_bf16.shape[-1]
    tm, tn = 1024, 1024
    nn = ncols // tn
    return pl.pallas_call(
        functools.partial(_win_kernel, tm=tm),
        grid=(rows // tm, nn),
        in_specs=[pl.BlockSpec((tm, d), lambda i, n: (i, 0)),
                  pl.BlockSpec((rc, d), lambda i, n: (0, 0)),
                  _mod_spec(l, 1, 0, d), _mod_spec(l, 1, 1, d),
                  pl.BlockSpec((1, d), lambda i, n: (0, 0)),
                  pl.BlockSpec((None, d, tn), lambda i, n: (l, 0, n))],
        out_specs=[pl.BlockSpec((tm, tn), lambda i, n: (i, n)), _rider_spec(rc, tn, nn)],
        out_shape=[jax.ShapeDtypeStruct((rows, ncols), F32), jax.ShapeDtypeStruct((rc, ncols), F32)],
        scratch_shapes=[pltpu.VMEM((tm, d), BF16), pltpu.VMEM((rc, d), BF16)],
        compiler_params=_cparams(("arbitrary", "arbitrary")),
        name="mixer_in_proj",
    )(x, xc, mods, mods, g, w_in_bf16)


def _to_col_major(x):
    rows, d = x.shape
    return x.reshape(rows // GRID_W, GRID_W, d).transpose(1, 0, 2).reshape(rows, d)


def _from_col_major(x):
    rows, d = x.shape
    return x.reshape(GRID_W, rows // GRID_W, d).transpose(1, 0, 2).reshape(rows, d)


def _lru_kernel(cv_ref, wax_ref, ba_ref, bx_ref, lam_ref, h0_ref, *rest, t_blk, heads, reverse, add_prev):
    if add_prev:
        prev_ref, o_ref, hf_ref, a_scr, b_scr, carry = rest
    else:
        o_ref, hf_ref, a_scr, b_scr, carry = rest
    c = cv_ref.shape[1]
    hd = c // heads

    @pl.when(pl.program_id(0) == 0)
    def _():
        carry[...] = h0_ref[...]

    cv = cv_ref[...]
    z = -lam_ref[...]
    c_sp = LRU_C * (jnp.maximum(z, 0.0) + jnp.log(1.0 + jnp.exp(-jnp.abs(z))))
    cvb = cv.astype(BF16)
    for h in range(heads):
        sl = slice(h * hd, (h + 1) * hd)
        gx = jnp.dot(cvb[:, sl], wax_ref[h], preferred_element_type=F32)
        r = 0.5 + 0.5 * jnp.tanh(0.5 * (gx[:, :hd] + ba_ref[:, sl]))
        ig = 0.5 + 0.5 * jnp.tanh(0.5 * (gx[:, hd:] + bx_ref[:, sl]))
        a = jnp.exp(-c_sp[:, sl] * r)
        a_scr[:, sl] = a
        b_scr[:, sl] = jnp.sqrt(1.0 - a * a) * (ig * cv[:, sl])

    row = lax.broadcasted_iota(jnp.int32, (SUBLANES, c), 0)
    ngrp = t_blk // SUBLANES

    def group(gi, hc):
        g = (ngrp - 1 - gi) if reverse else gi
        r0 = pl.multiple_of(g * SUBLANES, SUBLANES)
        av = a_scr[pl.ds(r0, SUBLANES), :]
        bv = b_scr[pl.ds(r0, SUBLANES), :]
        for s in (1, 2, 4):
            sh = (SUBLANES - s) if reverse else s
            m = (row < SUBLANES - s) if reverse else (row >= s)
            a_s = pltpu.roll(av, sh, axis=0)
            b_s = pltpu.roll(bv, sh, axis=0)
            bv = jnp.where(m, av * b_s + bv, bv)
            av = jnp.where(m, av * a_s, av)
        hv = av * hc + bv
        if add_prev:
            o_ref[pl.ds(r0, SUBLANES), :] = hv + prev_ref[pl.ds(r0, SUBLANES), :]
        else:
            o_ref[pl.ds(r0, SUBLANES), :] = hv
        return hv[0:1, :] if reverse else hv[SUBLANES - 1:SUBLANES, :]

    hc = lax.fori_loop(0, ngrp, group, carry[...])
    carry[...] = hc
    hf_ref[...] = hc


def _lru_scan(cv, wax, ba, bx, lam, h0, prev, reverse):
    seq, c = cv.shape
    heads = wax.shape[0]
    t_blk = min(seq, 1024)
    nblk = seq // t_blk

    def bi(j):
        return (nblk - 1 - j) if reverse else j

    row = pl.BlockSpec((1, c), lambda j: (0, 0))
    in_specs = [pl.BlockSpec((t_blk, c), lambda j: (bi(j), 0)),
                pl.BlockSpec(wax.shape, lambda j: (0, 0, 0)), row, row, row, row]
    args = [cv, wax, ba, bx, lam, h0]
    if prev is not None:
        in_specs.append(pl.BlockSpec((t_blk, c), lambda j: (bi(j), 0)))
        args.append(prev)
    return pl.pallas_call(
        functools.partial(_lru_kernel, t_blk=t_blk, heads=heads, reverse=reverse, add_prev=prev is not None),
        grid=(nblk,),
        in_specs=in_specs,
        out_specs=[pl.BlockSpec((t_blk, c), lambda j: (bi(j), 0)), row],
        out_shape=[jax.ShapeDtypeStruct((seq, c), F32), jax.ShapeDtypeStruct((1, c), F32)],
        scratch_shapes=[pltpu.VMEM((t_blk, c), F32), pltpu.VMEM((t_blk, c), F32), pltpu.VMEM((1, c), F32)],
        compiler_params=_cparams(("arbitrary",)),
        name="rglru_scan",
    )(*args)


def _short_conv_kernel(*refs, t_blk, nblk, c, with_hyena):
    n_in = 4 if with_hyena else 1
    ins = [refs[3 * k:3 * k + 3] for k in range(n_in)]
    lw_ref, lb_ref, hw_ref, hb_ref = refs[3 * n_in:3 * n_in + 4]
    outs = refs[3 * n_in + 4:-1]
    xs = refs[-1]
    blk = pl.program_id(0)
    zero8 = jnp.zeros((SUBLANES, c), F32)

    def conv(src, w_ref, b_ref, part):
        cur, prv, nxt = src
        xs[0:SUBLANES, :] = jnp.where(blk == 0, zero8, prv[...])
        xs[SUBLANES:SUBLANES + t_blk, :] = cur[...]
        xs[SUBLANES + t_blk:2 * SUBLANES + t_blk, :] = jnp.where(blk == nblk - 1, zero8, nxt[...])
        taps = w_ref.shape[0]
        left = (taps - 1) // 2
        sl = slice(part * c, (part + 1) * c)
        out = b_ref[:, sl]
        for k in range(taps):
            r0 = SUBLANES - left + k
            out = out + xs[r0:r0 + t_blk, :] * w_ref[k:k + 1, sl]
        return out

    outs[0][...] = conv(ins[0], lw_ref, lb_ref, 0)
    if with_hyena:
        outs[2][...] = conv(ins[1], hw_ref, hb_ref, 0)
        x1c = conv(ins[2], hw_ref, hb_ref, 1)
        outs[1][...] = conv(ins[3], hw_ref, hb_ref, 2) * x1c


def _short_convs(p, lru_w, lru_b, hy_w, hy_b, with_hyena):
    seq = p.shape[0]
    c = lru_w.shape[1]
    t_blk = min(seq, 512)
    nblk = seq // t_blk
    per = t_blk // SUBLANES
    nb8 = seq // SUBLANES
    in_specs, args = [], []
    for col in ([0, 2, 3, 4] if with_hyena else [0]):
        in_specs += [pl.BlockSpec((t_blk, c), lambda j, col=col: (j, col)),
                     pl.BlockSpec((SUBLANES, c), lambda j, col=col: (jnp.maximum(j * per - 1, 0), col)),
                     pl.BlockSpec((SUBLANES, c), lambda j, col=col: (jnp.minimum((j + 1) * per, nb8 - 1), col))]
        args += [p, p, p]
    full = lambda a: pl.BlockSpec(a.shape, lambda j: (0, 0))
    in_specs += [full(lru_w), full(lru_b), full(hy_w), full(hy_b)]
    args += [lru_w, lru_b, hy_w, hy_b]
    n_out = 3 if with_hyena else 1
    return pl.pallas_call(
        functools.partial(_short_conv_kernel, t_blk=t_blk, nblk=nblk, c=c, with_hyena=with_hyena),
        grid=(nblk,),
        in_specs=in_specs,
        out_specs=[pl.BlockSpec((t_blk, c), lambda j: (j, 0))] * n_out,
        out_shape=[jax.ShapeDtypeStruct((seq, c), F32)] * n_out,
        scratch_shapes=[pltpu.VMEM((t_blk + 2 * SUBLANES, c), F32)],
        compiler_params=_cparams(("arbitrary",)),
        name="mixer_short_convs",
    )(*args)


FILT_SLOT = 64


def _filt_kernel(w1_ref, b1_ref, w2_ref, b2_ref, w3_ref, b3_ref, w4a_ref, w4b_ref, b4_ref, fr_ref,
                 k_ref, nrm_ref, ec_scr, es_scr, *, seq, t_blk, c):
    i = pl.program_id(0)
    hi = lax.Precision.HIGHEST
    lanes = 2 * FILT_SLOT
    th = t_blk // 2
    lane = lax.broadcasted_iota(jnp.int32, (1, lanes), 1)
    slot = lane & (FILT_SLOT - 1)
    pos_a = i * t_blk + lax.broadcasted_iota(jnp.int32, (th, 1), 0)
    pos = jnp.where(lane < FILT_SLOT, pos_a, pos_a + th)
    tf = jnp.where(pos < seq, pos, 2 * seq - pos).astype(F32)
    t = tf * (1.0 / (seq - 1))
    is_cos = (slot >= 1) & (slot <= HY_BANDS)
    is_sin = (slot > HY_BANDS) & (slot <= 2 * HY_BANDS)
    band = jnp.where(is_cos, slot - 1, slot - 1 - HY_BANDS).astype(F32)
    alpha = (2.0 * math.pi / seq) * (1e-4 + band * ((HY_BANDS - 1 - 1e-4) / (HY_BANDS - 1)))

    @pl.when(i == 0)
    def _():
        off = (pos - i * t_blk).astype(F32)
        ec_scr[...] = jnp.sin(alpha * off + 0.5 * math.pi)
        es_scr[...] = jnp.sin(alpha * off)

    forward = i * t_blk < seq
    base = jnp.where(forward, i * t_blk, 2 * seq - i * t_blk).astype(F32)
    sigma = jnp.where(forward, 1.0, -1.0)
    cb = jnp.sin(alpha * base + 0.5 * math.pi)
    sb = jnp.sin(alpha * base)
    a_row = jnp.where(is_cos, cb, jnp.where(is_sin, -sb, 0.0))
    b_row = -sigma * jnp.where(is_cos, sb, jnp.where(is_sin, cb, 0.0))
    z = jnp.where(slot == 0, t, a_row * ec_scr[...] + b_row * es_scr[...])
    fr = fr_ref[...]
    h = jnp.sin(fr * (jnp.dot(z, w1_ref[...], precision=hi, preferred_element_type=F32) + b1_ref[...]))
    h = jnp.sin(fr * (jnp.dot(h, w2_ref[...], precision=hi, preferred_element_type=F32) + b2_ref[...]))
    h = jnp.sin(fr * (jnp.dot(h, w3_ref[...], precision=hi, preferred_element_type=F32) + b3_ref[...]))
    ch = lax.broadcasted_iota(jnp.int32, (1, c), 1).astype(F32)
    delta = jnp.abs(HY_MIN_DECAY + ch * ((HY_MAX_DECAY - HY_MIN_DECAY) / (c - 1)))
    part = jnp.zeros((1, c), F32)
    h_hi = h.astype(BF16)
    h_lo = (h - h_hi.astype(F32)).astype(BF16)
    for half, w4_ref in enumerate((w4a_ref, w4b_ref)):
        pos_h = pos_a + half * th
        t_h = jnp.where(pos_h < seq, pos_h, 2 * seq - pos_h).astype(F32) * (1.0 / (seq - 1))
        w_hi, w_lo = w4_ref[0], w4_ref[1]
        k = (jnp.dot(h_hi, w_hi, preferred_element_type=F32) + jnp.dot(h_lo, w_hi, preferred_element_type=F32)
             + jnp.dot(h_hi, w_lo, preferred_element_type=F32)) + b4_ref[...]
        k = jnp.where(pos_h != seq, k * jnp.exp(-t_h * delta), 0.0)
        k_ref[half * th:(half + 1) * th, :] = k
        part = part + jnp.sum(jnp.abs(k), axis=0, keepdims=True)

    @pl.when(i == 0)
    def _():
        nrm_ref[...] = part

    @pl.when(i > 0)
    def _():
        nrm_ref[...] = nrm_ref[...] + part


def _slots(a, row_used):
    r, cdim = a.shape
    blk = jnp.pad(a, ((0, (FILT_SLOT if row_used else 1) - r), (0, FILT_SLOT - cdim)))
    if not row_used:
        return jnp.concatenate([blk, blk], axis=1)
    z = jnp.zeros_like(blk)
    return jnp.concatenate([jnp.concatenate([blk, z], axis=1), jnp.concatenate([z, blk], axis=1)], axis=0)


def _hyena_filter(seq, w1, b1, w2, b2, w3, b3, w4, b4, freq):
    c = w4.shape[1] // 2
    od = w2.shape[0]
    assert od <= FILT_SLOT and w1.shape[0] <= FILT_SLOT
    lanes = 2 * FILT_SLOT
    t_blk = min(seq, 1024)
    half_blocks = seq // t_blk
    full = lambda shape: pl.BlockSpec(shape, lambda i: (0,) * len(shape))
    sq, row = (lanes, lanes), (1, lanes)
    w4p = jnp.pad(w4, ((0, FILT_SLOT - od), (0, 0)))
    w4_hi = w4p.astype(BF16)
    w4_lo = (w4p - w4_hi.astype(F32)).astype(BF16)
    w4s = jnp.stack([w4_hi, w4_lo])
    w4a = jnp.concatenate([w4s, jnp.zeros_like(w4s)], axis=1)
    w4b = jnp.concatenate([jnp.zeros_like(w4s), w4s], axis=1)
    w4_spec = pl.BlockSpec((2, lanes, c), lambda i: (0, 0, i // half_blocks))
    return pl.pallas_call(
        functools.partial(_filt_kernel, seq=seq, t_blk=t_blk, c=c),
        grid=(2 * seq // t_blk,),
        in_specs=[full(sq), full(row), full(sq), full(row), full(sq), full(row), w4_spec, w4_spec,
                  pl.BlockSpec((1, c), lambda i: (0, i // half_blocks)),
                  full(row)],
        out_specs=[pl.BlockSpec((t_blk, c), lambda i: (i, 0)), pl.BlockSpec((1, c), lambda i: (0, 0))],
        out_shape=[jax.ShapeDtypeStruct((2 * seq, c), F32), jax.ShapeDtypeStruct((1, c), F32)],
        scratch_shapes=[pltpu.VMEM((t_blk // 2, lanes), F32), pltpu.VMEM((t_blk // 2, lanes), F32)],
        compiler_params=_cparams(("arbitrary",)),
        name="hyena_filter",
    )(_slots(w1, True), _slots(b1[None, :], False), _slots(w2, True), _slots(b2[None, :], False),
      _slots(w3, True), _slots(b3[None, :], False), w4a, w4b, b4[None, :], _slots(freq[None, :], False))


def _bitrev(p, bits):
    r = 0
    for k in range(bits):
        r = (r << 1) | ((p >> k) & 1)
    return r


def _real_block(z):
    return np.block([[z.real, -z.imag], [z.imag, z.real]])


def _fft_split(s2):
    n_lo = max(min(8, s2 // 2), 1)
    return s2 // n_lo, n_lo


@functools.lru_cache(maxsize=None)
def _fft_tables(s):
    s2 = s // 2
    n = s * FFT_B
    n_hi, n_lo = _fft_split(s2)
    bits_hi, bits_lo = n_hi.bit_length() - 1, n_lo.bit_length() - 1
    b = np.arange(FFT_B)
    dft = np.exp(-2j * np.pi * np.outer(b, b) / FFT_B)
    e_lo = [2 * n_hi * _bitrev(p, bits_lo) for p in range(n_lo)]
    g = [dft * np.exp(-2j * np.pi * b * e / n)[None, :] for e in e_lo]
    chunks = np.arange(FFT_B).reshape(-1, FFT_ROWS)
    order = np.concatenate([np.concatenate([c, c + FFT_B]) for c in chunks])
    lhs_f = np.stack([_real_block(x)[order] for x in g])
    lhs_i = np.stack([_real_block(np.conj(x).T)[order] for x in g])
    e_hi = np.array([[2 * _bitrev(p, bits_hi) + hf for p in range(n_hi)] for hf in range(2)])
    ang = 2.0 * np.pi * e_hi[:, :, None, None] * b[None, None, :, None] / n * np.ones((1, 1, 1, FFT_B))
    lvl = 2.0 * np.pi * np.arange(s2) / s2
    first = np.concatenate([np.zeros(s2), 2.0 * np.pi * np.arange(s2) / s])
    f32 = lambda x: np.asarray(x, np.float32)
    return dict(lhs_f=f32(lhs_f), lhs_i=f32(lhs_i), col_c=f32(np.cos(ang)), col_s=f32(np.sin(ang)),
                lvl_c=f32(np.cos(lvl)), lvl_s=f32(np.sin(lvl)), first_c=f32(np.cos(first)),
                first_s=f32(np.sin(first)))


FFT_ROWS = 32


def _cmul(xr, xi, c, s):
    return xr * c + xi * s, xi * c - xr * s


def _cmul_conj(xr, xi, c, s):
    return xr * c - xi * s, xi * c + xr * s


def _pass_plan(s2):
    levels = s2.bit_length() - 1
    plan, h = [], s2 // 2
    if levels % 2 == 1:
        plan.append((h, 2))
        h //= 2
    while h >= 1:
        plan.append((h, 4))
        h //= 4
    return plan


def _butterfly_pass(load, store, lc_ref, ls_ref, s2, h, radix, inverse):
    h2 = h // 2 if radix == 4 else h
    trivial = h2 == 1
    shift = h2.bit_length() - 1

    def group(idx, carry):
        g = idx >> shift
        i = idx & (h2 - 1)
        base = g * 2 * h + i
        blocks = [base + k * h2 for k in range(radix)]
        c1 = s1 = c2 = s2_ = c3 = s3 = None
        if not trivial:
            t1 = i * (s2 // (2 * h))
            c1, s1 = lc_ref[t1], ls_ref[t1]
            if radix == 4:
                c2, s2_, c3, s3 = lc_ref[2 * t1], ls_ref[2 * t1], lc_ref[3 * t1], ls_ref[3 * t1]
        fwd_mul = (lambda xr, xi, c, s: (xr, xi)) if trivial else _cmul
        inv_mul = (lambda xr, xi, c, s: (xr, xi)) if trivial else _cmul_conj
        for r in range(0, FFT_B, FFT_ROWS):
            x = [load(bk, r) for bk in blocks]
            if radix == 2:
                (ar, ai), (br, bi) = x
                if inverse:
                    br, bi = inv_mul(br, bi, c1, s1)
                    out = [(ar + br, ai + bi), (ar - br, ai - bi)]
                else:
                    out = [(ar + br, ai + bi), fwd_mul(ar - br, ai - bi, c1, s1)]
            elif inverse:
                (x0r, x0i), (x1r, x1i), (x2r, x2i), (x3r, x3i) = x
                x1r, x1i = inv_mul(x1r, x1i, c2, s2_)
                x2r, x2i = inv_mul(x2r, x2i, c1, s1)
                x3r, x3i = inv_mul(x3r, x3i, c3, s3)
                t0r, t0i, t1r, t1i = x0r + x1r, x0i + x1i, x0r - x1r, x0i - x1i
                t2r, t2i, t3r, t3i = x2r + x3r, x2i + x3i, x2r - x3r, x2i - x3i
                out = [(t0r + t2r, t0i + t2i), (t1r - t3i, t1i + t3r),
                       (t0r - t2r, t0i - t2i), (t1r + t3i, t1i - t3r)]
            else:
                (x0r, x0i), (x1r, x1i), (x2r, x2i), (x3r, x3i) = x
                t0r, t0i, t1r, t1i = x0r + x2r, x0i + x2i, x1r + x3r, x1i + x3i
                t2r, t2i, t3r, t3i = x0r - x2r, x0i - x2i, x1r - x3r, x1i - x3i
                out = [(t0r + t1r, t0i + t1i), fwd_mul(t0r - t1r, t0i - t1i, c2, s2_),
                       fwd_mul(t2r + t3i, t2i - t3r, c1, s1), fwd_mul(t2r - t3i, t2i + t3r, c3, s3)]
            for bk, (o_r, o_i) in zip(blocks, out):
                store(bk, r, o_r, o_i)
        return carry

    lax.fori_loop(0, s2 // radix, group, 0)


def _work_access(wr, wi):
    def rows(bk, r):
        return pl.ds(pl.multiple_of(bk * FFT_B + r, FFT_ROWS), FFT_ROWS)

    def load(bk, r):
        return wr[rows(bk, r), :], wi[rows(bk, r), :]

    def store(bk, r, re, im):
        wr[rows(bk, r), :] = re
        wi[rows(bk, r), :] = im

    return load, store


def _run_passes(plan, first_load, last_store, work, lc_ref, ls_ref, s2, inverse):
    w_load, w_store = work
    for k, (h, radix) in enumerate(plan):
        load = first_load if k == 0 else w_load
        store = last_store if k == len(plan) - 1 else w_store
        _butterfly_pass(load, store, lc_ref, ls_ref, s2, h, radix, inverse)


MID_UNROLL = 32


def _pair_blocks(pp, s2):
    n_hi, n_lo = _fft_split(s2)
    half = n_hi // 2
    p_lo = pp >> (half.bit_length() - 1)
    j = pp & (half - 1)
    return p_lo, [(2 * j + q, pl.multiple_of(((2 * j + q) * n_lo + p_lo) * FFT_B, FFT_B)) for q in range(2)]


def _for_pairs(s2, body):
    def step(pp, carry):
        body(pp)
        return carry
    lax.fori_loop(0, s2 // 2, step, 0, unroll=min(MID_UNROLL, s2 // 2))


def _pack_store(panel, cc_ref, cs_ref, s2):
    n_hi, n_lo = _fft_split(s2)
    lo_bits = n_lo.bit_length() - 1

    def store(bk, r, re, im):
        p_hi = bk >> lo_bits
        p_lo = bk & (n_lo - 1)
        pp = p_lo * (n_hi // 2) + (p_hi >> 1)
        q = p_hi & 1
        tr, ti = _cmul(re, im, cc_ref[p_hi, r:r + FFT_ROWS, :], cs_ref[p_hi, r:r + FFT_ROWS, :])
        panel[pp, q, r:r + FFT_ROWS, :] = tr.astype(BF16)
        panel[pp, q, FFT_B + r:FFT_B + r + FFT_ROWS, :] = ti.astype(BF16)

    return store


def _dft_pair(m_ref, panel, pp, p_lo):
    blk = FFT_B
    rhs = jnp.concatenate([panel[pp, 0], panel[pp, 1]], axis=1)
    out = jnp.dot(m_ref[p_lo], rhs, preferred_element_type=F32)
    for r in range(0, blk, FFT_ROWS):
        for q in range(2):
            yield (q, r, out[2 * r:2 * r + FFT_ROWS, q * blk:(q + 1) * blk],
                   out[2 * r + FFT_ROWS:2 * r + 2 * FFT_ROWS, q * blk:(q + 1) * blk])


def _spec_kernel(lo_ref, hi_ref, nrm_ref, fc_ref, fs_ref, lc_ref, ls_ref, cc_ref, cs_ref, mf_ref,
                 kr_ref, ki_ref, panel, *, s2):
    blk = FFT_B
    hf = pl.program_id(1)
    n = 2 * s2 * blk
    sign = (1 - 2 * hf).astype(F32)

    def first_load(bk, r):
        rows = pl.ds(pl.multiple_of(bk * blk + r, FFT_ROWS), FFT_ROWS)
        d = lo_ref[rows, :] + sign * hi_ref[rows, :]
        return d * fc_ref[hf * s2 + bk], -d * fs_ref[hf * s2 + bk]

    _run_passes(_pass_plan(s2), first_load, _pack_store(panel, cc_ref, cs_ref, s2),
                _work_access(kr_ref, ki_ref), lc_ref, ls_ref, s2, inverse=False)
    scale = 1.0 / (nrm_ref[...] * n)

    def dft(pp):
        p_lo, blocks = _pair_blocks(pp, s2)
        for q, r, yr, yi in _dft_pair(mf_ref, panel, pp, p_lo):
            rows = pl.ds(blocks[q][1] + r, FFT_ROWS)
            kr_ref[rows, :] = yr * scale
            ki_ref[rows, :] = yi * scale

    _for_pairs(s2, dft)


def _conv_kernel(u_ref, bias_ref, kr_ref, ki_ref, fc_ref, fs_ref, lc_ref, ls_ref, cc_ref, cs_ref,
                 mf_ref, mi_ref, y_ref, wr, wi, panel, *, s2):
    blk = FFT_B
    hf = pl.program_id(1)
    plan = _pass_plan(s2)
    work = _work_access(wr, wi)

    def first_load(bk, r):
        ub = u_ref[pl.ds(pl.multiple_of(bk * blk + r, FFT_ROWS), FFT_ROWS), :]
        return ub * fc_ref[hf * s2 + bk], -ub * fs_ref[hf * s2 + bk]

    _run_passes(plan, first_load, _pack_store(panel, cc_ref, cs_ref, s2), work, lc_ref, ls_ref, s2,
                inverse=False)

    def spectrum_product(pp):
        p_lo, blocks = _pair_blocks(pp, s2)
        for q, r, yr, yi in _dft_pair(mf_ref, panel, pp, p_lo):
            rows = pl.ds(blocks[q][1] + r, FFT_ROWS)
            kr, ki = kr_ref[rows, :], ki_ref[rows, :]
            panel[pp, q, r:r + FFT_ROWS, :] = (yr * kr - yi * ki).astype(BF16)
            panel[pp, q, blk + r:blk + r + FFT_ROWS, :] = (yr * ki + yi * kr).astype(BF16)

    _for_pairs(s2, spectrum_product)

    def inverse_dft(pp):
        p_lo, blocks = _pair_blocks(pp, s2)
        for q, r, er, ei in _dft_pair(mi_ref, panel, pp, p_lo):
            p_hi, r0 = blocks[q]
            rows = pl.ds(r0 + r, FFT_ROWS)
            wr[rows, :], wi[rows, :] = _cmul_conj(er, ei, cc_ref[p_hi, r:r + FFT_ROWS, :],
                                                  cs_ref[p_hi, r:r + FFT_ROWS, :])

    _for_pairs(s2, inverse_dft)

    def rows_of(bk, r):
        return pl.ds(pl.multiple_of(bk * blk + r, FFT_ROWS), FFT_ROWS)

    def store_even(bk, r, er, ei):
        y_ref[rows_of(bk, r), :] = er

    def store_odd(bk, r, er, ei):
        rows = rows_of(bk, r)
        val = er * fc_ref[s2 + bk] - ei * fs_ref[s2 + bk]
        y_ref[rows, :] = y_ref[rows, :] + u_ref[rows, :] * bias_ref[...] + val

    inv_plan = plan[::-1]
    if len(inv_plan) > 1:
        _run_passes(inv_plan[:-1], work[0], work[1], work, lc_ref, ls_ref, s2, inverse=True)
    h_last, radix_last = inv_plan[-1]
    for half, store in ((0, store_even), (1, store_odd)):
        @pl.when(hf == half)
        def _(store=store):
            _butterfly_pass(work[0], store, lc_ref, ls_ref, s2, h_last, radix_last, True)


def _smem_spec():
    return pl.BlockSpec(memory_space=pltpu.SMEM)


def _fft_table_specs(s2):
    n_hi, n_lo = _fft_split(s2)
    col = pl.BlockSpec((None, n_hi, FFT_B, FFT_B), lambda ct, hf: (hf, 0, 0, 0))
    lhs = pl.BlockSpec((n_lo, 2 * FFT_B, 2 * FFT_B), lambda ct, hf: (0, 0, 0))
    return [_smem_spec(), _smem_spec(), _smem_spec(), _smem_spec(), col, col], lhs


def _hyena_spectrum(kfull, nrm):
    n, c = kfull.shape
    seq = n // 2
    s2 = seq // FFT_B
    tb = _fft_tables(2 * s2)
    lanes = 128
    tab_specs, lhs_spec = _fft_table_specs(s2)
    return pl.pallas_call(
        functools.partial(_spec_kernel, s2=s2),
        grid=(c // lanes, 2),
        in_specs=[pl.BlockSpec((seq, lanes), lambda ct, hf: (0, ct)),
                  pl.BlockSpec((seq, lanes), lambda ct, hf: (1, ct)),
                  pl.BlockSpec((1, lanes), lambda ct, hf: (0, ct))] + tab_specs + [lhs_spec],
        out_specs=[pl.BlockSpec((None, seq, lanes), lambda ct, hf: (hf, 0, ct)),
                   pl.BlockSpec((None, seq, lanes), lambda ct, hf: (hf, 0, ct))],
        out_shape=[jax.ShapeDtypeStruct((2, seq, c), F32), jax.ShapeDtypeStruct((2, seq, c), F32)],
        scratch_shapes=[pltpu.VMEM((s2 // 2, 2, 2 * FFT_B, FFT_B), BF16)],
        compiler_params=_cparams(("arbitrary", "arbitrary")),
        name="hyena_spectrum",
    )(kfull, kfull, nrm, tb["first_c"], tb["first_s"], tb["lvl_c"], tb["lvl_s"],
      tb["col_c"], tb["col_s"], jnp.asarray(tb["lhs_f"], BF16))


def _hyena_conv(u, bias, kr, ki):
    seq, c = u.shape
    s2 = seq // FFT_B
    tb = _fft_tables(2 * s2)
    lanes = 128
    tab_specs, lhs_spec = _fft_table_specs(s2)
    return pl.pallas_call(
        functools.partial(_conv_kernel, s2=s2),
        grid=(c // lanes, 2),
        in_specs=[pl.BlockSpec((seq, lanes), lambda ct, hf: (0, ct)),
                  pl.BlockSpec((1, lanes), lambda ct, hf: (0, ct)),
                  pl.BlockSpec((None, seq, lanes), lambda ct, hf: (hf, 0, ct)),
                  pl.BlockSpec((None, seq, lanes), lambda ct, hf: (hf, 0, ct))]
                 + tab_specs + [lhs_spec, lhs_spec],
        out_specs=pl.BlockSpec((seq, lanes), lambda ct, hf: (0, ct)),
        out_shape=jax.ShapeDtypeStruct((seq, c), F32),
        scratch_shapes=[pltpu.VMEM((seq, lanes), F32), pltpu.VMEM((seq, lanes), F32),
                        pltpu.VMEM((s2 // 2, 2, 2 * FFT_B, FFT_B), BF16)],
        compiler_params=_cparams(("arbitrary", "arbitrary")),
        name="hyena_fft_conv",
    )(u, bias, kr, ki, tb["first_c"], tb["first_s"], tb["lvl_c"], tb["lvl_s"],
      tb["col_c"], tb["col_s"], jnp.asarray(tb["lhs_f"], BF16), jnp.asarray(tb["lhs_i"], BF16))


MIXOUT_CHUNK = 256


def _mixout_kernel(h_ref, yr_ref, hy_ref, x0_ref, og_ref, w_ref, x_ref, gate_ref, o_ref, *, stream, lw, tm):
    gate = gate_ref[stream:stream + 1, :]
    w = w_ref[...]
    for r in range(0, tm, MIXOUT_CHUNK):
        rs = slice(r, r + min(MIXOUT_CHUNK, tm))
        yr = yr_ref[rs, :]
        gelu = 0.5 * yr * (1.0 + jnp.tanh(math.sqrt(2.0 / math.pi) * (yr + 0.044715 * (yr * yr * yr))))
        lru = (_rms(h_ref[rs, :] * gelu) * og_ref[:, :lw]).astype(BF16)
        hy = (_rms(hy_ref[rs, :] * x0_ref[rs, :]) * og_ref[:, lw:]).astype(BF16)
        y = (jnp.dot(lru, w[:lw, :], preferred_element_type=F32)
             + jnp.dot(hy, w[lw:, :], preferred_element_type=F32))
        o_ref[rs, :] = x_ref[rs, :] + gate * y


def _mixout(h_lru, p, y_hy, x0c, og, w_out_bf16, x, mods, l, stream):
    rows, d = x.shape
    lw = h_lru.shape[1]
    dm = w_out_bf16.shape[1]
    tm = min(rows, 512)
    row_blk = lambda cols, j: pl.BlockSpec((tm, cols), lambda i: (i, j))
    return pl.pallas_call(
        functools.partial(_mixout_kernel, stream=stream, lw=lw, tm=tm),
        grid=(rows // tm,),
        in_specs=[row_blk(lw, 0), row_blk(lw, 1), row_blk(dm - lw, 0), row_blk(dm - lw, 0),
                  pl.BlockSpec((1, dm), lambda i: (0, 0)),
                  pl.BlockSpec((None, dm, d), lambda i: (l, 0, 0)),
                  row_blk(d, 0),
                  _mod_spec(l, 1, 2, d)],
        out_specs=row_blk(d, 0),
        out_shape=jax.ShapeDtypeStruct((rows, d), F32),
        compiler_params=_cparams(("arbitrary",)),
        name="mixer_out_proj",
    )(h_lru, p, y_hy, x0c, og, w_out_bf16, x, mods)


def _final_kernel(x_ref, g_ref, o_ref):
    o_ref[...] = _rms(x_ref[...]) * g_ref[...]


def _final_norm(x, g):
    rows, d = x.shape
    tm = 1024
    return pl.pallas_call(
        _final_kernel,
        grid=(rows // tm,),
        in_specs=[pl.BlockSpec((tm, d), lambda i: (i, 0)), pl.BlockSpec((1, d), lambda i: (0, 0))],
        out_specs=pl.BlockSpec((tm, d), lambda i: (i, 0)),
        out_shape=jax.ShapeDtypeStruct((rows, d), F32),
        compiler_params=_cparams(("arbitrary",)),
        name="final_norm",
    )(x, g)


def kernel(x, c, ctx, c_ctx, ada_w, ada_b, norm_g, ffn_wg, ffn_wu, ffn_wd, w_in, w_out, out_g, lru_conv_w, lru_conv_b, lru_wa, lru_ba, lru_wx, lru_bx, lru_lam, hy_conv_w, hy_conv_b, hy_bias, filt_w1, filt_b1, filt_w2, filt_b2, filt_w3, filt_b3, filt_w4, filt_b4, filt_freq, final_g):
    assert x.shape[0] == 1 and ctx.shape[0] == 1
    depth = ada_w.shape[0]
    d = x.shape[-1]
    lw = lru_conv_w.shape[-1]
    n_lat, n_ctx = x.shape[1], ctx.shape[1]
    xl, xc = x.reshape(n_lat, d), ctx.reshape(n_ctx, d)
    mods = _ada_mods(c, c_ctx, ada_w, ada_b)
    w_in_bf16, w_out_bf16 = w_in.astype(BF16), w_out.astype(BF16)
    zero_state = jnp.zeros((1, lw), F32)
    lat_is_col = False
    for l in range(depth):
        last = l == depth - 1
        g = norm_g[l][:, None, :]

        xl, xc = _ffn(xl, xc, mods, 0, g[0], ffn_wg, ffn_wu, ffn_wd, l, 0)

        want_col = l % 2 == 1
        if want_col != lat_is_col:
            xl = _to_col_major(xl) if want_col else _from_col_major(xl)
            lat_is_col = want_col
        pl_, pc_ = _win(xl, xc, mods, l, g[1], w_in_bf16)

        convs = (lru_conv_w[l], lru_conv_b[l][None, :], hy_conv_w[l], hy_conv_b[l][None, :])
        cv_l, u_l, x0_l = _short_convs(pl_, *convs, True)
        cv_c, *hy_c = _short_convs(pc_, *convs, not last)

        h_lat = h_ctx = None
        for dr in range(2):
            wax = jnp.concatenate([lru_wa[l, dr], lru_wx[l, dr]], axis=-1).astype(BF16)
            ba, bx, lam = lru_ba[l, dr][None, :], lru_bx[l, dr][None, :], lru_lam[l, dr][None, :]
            h_ctx, state = _lru_scan(cv_c, wax, ba, bx, lam, zero_state, h_ctx, dr == 1)
            h_lat, _ = _lru_scan(cv_l, wax, ba, bx, lam, state, h_lat, dr == 1)

        filt = (filt_w1[l], filt_b1[l], filt_w2[l], filt_b2[l], filt_w3[l], filt_b3[l], filt_w4[l],
                filt_b4[l], filt_freq[l])
        hbias = hy_bias[l][None, :]
        og = out_g[l][None, :]

        kr, ki = _hyena_spectrum(*_hyena_filter(n_lat, *filt))
        y_hy = _hyena_conv(u_l, hbias, kr, ki)
        xl = _mixout(h_lat, pl_, y_hy, x0_l, og, w_out_bf16, xl, mods, l, LAT)

        if last:
            xc = None
        else:
            u_c, x0_c = hy_c
            kr, ki = _hyena_spectrum(*_hyena_filter(n_ctx, *filt))
            y_hy = _hyena_conv(u_c, hbias, kr, ki)
            xc = _mixout(h_ctx, pc_, y_hy, x0_c, og, w_out_bf16, xc, mods, l, CTX)
        xl, xc = _ffn(xl, xc, mods, 2, g[2], ffn_wg, ffn_wu, ffn_wd, l, 1)

    if lat_is_col:
        xl = _from_col_major(xl)
    return _final_norm(xl, final_g[None, :]).reshape(x.shape)
```

```python
import functools
import math

import numpy as np
import jax
import jax.numpy as jnp
from jax import lax
from jax.experimental import pallas as pl
from jax.experimental.pallas import tpu as pltpu

F32 = jnp.float32
BF16 = jnp.bfloat16
EPS = 1e-6
LRU_C = 8.0
GRID_W = 64
HY_BANDS = 16
HY_MAX_DECAY = math.log(1e-2) / 0.3
HY_MIN_DECAY = math.log(1e-2) / 1.5

V7X_VMEM_LIMIT_BYTES = 58 * 1024 * 1024
FFT_B = 128
SUBLANES = 8


def _cparams(sem):
    return pltpu.CompilerParams(dimension_semantics=sem, vmem_limit_bytes=V7X_VMEM_LIMIT_BYTES)


def _rms(x):
    return x * lax.rsqrt(jnp.mean(x * x, axis=-1, keepdims=True) + EPS)


def _modulated(x, shift_ref, scale_ref, g_ref, stream):
    shift = shift_ref[stream:stream + 1, :]
    scale = scale_ref[stream:stream + 1, :]
    return _rms(x) * g_ref[...] * (1.0 + scale) + shift


LANES = 128
N_STREAMS = 2


def _ada_kernel(c_ref, w_ref, b_ref, o_ref):
    o_ref[...] = jnp.zeros_like(o_ref)
    for k in range(N_STREAMS):
        c = c_ref[k]
        s = c * jax.nn.sigmoid(c)
        for j in range(0, w_ref.shape[1], LANES):
            cols = slice(j, j + LANES)
            o_ref[k:k + 1, cols] = jnp.sum(w_ref[:, cols] * s, axis=0, keepdims=True) + b_ref[:, cols]


def _ada_mods(c, c_ctx, ada_w, ada_b):
    depth, d, nm = ada_w.shape
    tn = d
    cc = jnp.broadcast_to(jnp.stack([c[0], c_ctx])[:, :, None], (N_STREAMS, d, LANES))
    return pl.pallas_call(
        _ada_kernel,
        grid=(depth, nm // tn),
        in_specs=[pl.BlockSpec((N_STREAMS, d, LANES), lambda l, j: (0, 0, 0)),
                  pl.BlockSpec((None, d, tn), lambda l, j: (l, 0, j)),
                  pl.BlockSpec((None, 1, tn), lambda l, j: (l, 0, j))],
        out_specs=pl.BlockSpec((None, SUBLANES, tn), lambda l, j: (l, 0, j)),
        out_shape=jax.ShapeDtypeStruct((depth, SUBLANES, nm), F32),
        compiler_params=_cparams(("arbitrary", "arbitrary")),
        name="ada_mods",
    )(cc, ada_w, ada_b[:, None, :])


def _mod_spec(l, k, which, d, tn=None):
    chunk = 3 * k + which
    if tn is None:
        return pl.BlockSpec((None, SUBLANES, d), lambda *g: (l, 0, chunk))
    return pl.BlockSpec((None, SUBLANES, tn), lambda *g: (l, 0, chunk * (d // tn) + g[-1]))


NORM_CHUNK = 256
LAT, CTX = 0, 1


def _rider_spec(rows, tcol, ncol_steps):
    return pl.BlockSpec((rows, tcol), lambda i, c: (0, jnp.where(i == 0, c, ncol_steps - 1)))


def _ffn_up_kernel(*refs, tm, with_ctx):
    if with_ctx:
        x_ref, xc_ref, shift_ref, scale_ref, g_ref, wg_ref, wu_ref, h_ref, hc_ref, u_scr, uc_scr = refs
    else:
        x_ref, shift_ref, scale_ref, g_ref, wg_ref, wu_ref, h_ref, u_scr = refs
    i, f = pl.program_id(0), pl.program_id(1)
    wg = wg_ref[...].astype(BF16)
    wu = wu_ref[...].astype(BF16)

    def swiglu_rows(u):
        g = jnp.dot(u, wg, preferred_element_type=F32)
        up = jnp.dot(u, wu, preferred_element_type=F32)
        return (g * jax.nn.sigmoid(g) * up).astype(BF16)

    @pl.when(f == 0)
    def _():
        for r in range(0, tm, NORM_CHUNK):
            rs = slice(r, min(r + NORM_CHUNK, tm))
            u = _modulated(x_ref[rs, :], shift_ref, scale_ref, g_ref, LAT).astype(BF16)
            u_scr[rs, :] = u
            h_ref[rs, :] = swiglu_rows(u)

    @pl.when(f > 0)
    def _():
        h_ref[...] = swiglu_rows(u_scr[...])

    if with_ctx:
        @pl.when(i == 0)
        def _():
            @pl.when(f == 0)
            def _():
                uc_scr[...] = _modulated(xc_ref[...], shift_ref, scale_ref, g_ref, CTX).astype(BF16)

            hc_ref[...] = swiglu_rows(uc_scr[...])


def _ffn_down_kernel(*refs, with_ctx):
    if with_ctx:
        h_ref, hc_ref, wd_ref, x_ref, xc_ref, gate_ref, o_ref, oc_ref = refs
    else:
        h_ref, wd_ref, x_ref, gate_ref, o_ref = refs
    wd = wd_ref[...].astype(BF16)
    o_ref[...] = x_ref[...] + (0.5 * gate_ref[LAT:LAT + 1, :]) * jnp.dot(h_ref[...], wd,
                                                                          preferred_element_type=F32)
    if with_ctx:
        @pl.when(pl.program_id(0) == 0)
        def _():
            oc_ref[...] = xc_ref[...] + (0.5 * gate_ref[CTX:CTX + 1, :]) * jnp.dot(
                hc_ref[...], wd, preferred_element_type=F32)


def _ffn(x, xc, mods, k, g, wg, wu, wd, l, j):
    rows, d = x.shape
    dff = wg.shape[-1]
    tm, tf, tn = 1024, 512, 256
    nf, nn = dff // tf, d // tn
    with_ctx = xc is not None
    rc = xc.shape[0] if with_ctx else 0
    ctx_in = [xc] if with_ctx else []
    res = pl.pallas_call(
        functools.partial(_ffn_up_kernel, tm=tm, with_ctx=with_ctx),
        grid=(rows // tm, nf),
        in_specs=[pl.BlockSpec((tm, d), lambda i, f: (i, 0))]
                 + ([pl.BlockSpec((rc, d), lambda i, f: (0, 0))] if with_ctx else [])
                 + [_mod_spec(l, k, 0, d), _mod_spec(l, k, 1, d),
                    pl.BlockSpec((1, d), lambda i, f: (0, 0)),
                    pl.BlockSpec((None, None, d, tf), lambda i, f: (l, j, 0, f)),
                    pl.BlockSpec((None, None, d, tf), lambda i, f: (l, j, 0, f))],
        out_specs=[pl.BlockSpec((tm, tf), lambda i, f: (i, f))]
                  + ([_rider_spec(rc, tf, nf)] if with_ctx else []),
        out_shape=[jax.ShapeDtypeStruct((rows, dff), BF16)]
                  + ([jax.ShapeDtypeStruct((rc, dff), BF16)] if with_ctx else []),
        scratch_shapes=[pltpu.VMEM((tm, d), BF16)] + ([pltpu.VMEM((rc, d), BF16)] if with_ctx else []),
        compiler_params=_cparams(("arbitrary", "arbitrary")),
        name="ffn_up",
    )(x, *ctx_in, mods, mods, g, wg, wu)
    h, hc = (res[0], res[1]) if with_ctx else (res[0], None)
    res = pl.pallas_call(
        functools.partial(_ffn_down_kernel, with_ctx=with_ctx),
        grid=(rows // tm, nn),
        in_specs=[pl.BlockSpec((tm, dff), lambda i, n: (i, 0))]
                 + ([pl.BlockSpec((rc, dff), lambda i, n: (0, 0))] if with_ctx else [])
                 + [pl.BlockSpec((None, None, dff, tn), lambda i, n: (l, j, 0, n)),
                    pl.BlockSpec((tm, tn), lambda i, n: (i, n))]
                 + ([_rider_spec(rc, tn, nn)] if with_ctx else [])
                 + [_mod_spec(l, k, 2, d, tn)],
        out_specs=[pl.BlockSpec((tm, tn), lambda i, n: (i, n))]
                  + ([_rider_spec(rc, tn, nn)] if with_ctx else []),
        out_shape=[jax.ShapeDtypeStruct((rows, d), F32)]
                  + ([jax.ShapeDtypeStruct((rc, d), F32)] if with_ctx else []),
        compiler_params=_cparams(("arbitrary", "arbitrary")),
        name="ffn_down",
    )(h, *([hc] if with_ctx else []), wd, x, *ctx_in, mods)
    return (res[0], res[1]) if with_ctx else (res[0], None)


def _win_kernel(x_ref, xc_ref, shift_ref, scale_ref, g_ref, w_ref, p_ref, pc_ref, u_scr, uc_scr, *, tm):
    i, n = pl.program_id(0), pl.program_id(1)
    w = w_ref[...]

    @pl.when(n == 0)
    def _():
        for r in range(0, tm, NORM_CHUNK):
            rs = slice(r, min(r + NORM_CHUNK, tm))
            u = _modulated(x_ref[rs, :], shift_ref, scale_ref, g_ref, LAT).astype(BF16)
            u_scr[rs, :] = u
            p_ref[rs, :] = jnp.dot(u, w, preferred_element_type=F32)

    @pl.when(n > 0)
    def _():
        p_ref[...] = jnp.dot(u_scr[...], w, preferred_element_type=F32)

    @pl.when(i == 0)
    def _():
        @pl.when(n == 0)
        def _():
            uc_scr[...] = _modulated(xc_ref[...], shift_ref, scale_ref, g_ref, CTX).astype(BF16)

        pc_ref[...] = jnp.dot(uc_scr[...], w, preferred_element_type=F32)


def _win(x, xc, mods, l, g, w_in_bf16):
    rows, d = x.shape
    rc = xc.shape[0]
    ncols = w_in_bf16.shape[-1]
    tm, tn = 1024, 1024
    nn = ncols // tn
    return pl.pallas_call(
        functools.partial(_win_kernel, tm=tm),
        grid=(rows // tm, nn),
        in_specs=[pl.BlockSpec((tm, d), lambda i, n: (i, 0)),
                  pl.BlockSpec((rc, d), lambda i, n: (0, 0)),
                  _mod_spec(l, 1, 0, d), _mod_spec(l, 1, 1, d),
                  pl.BlockSpec((1, d), lambda i, n: (0, 0)),
                  pl.BlockSpec((None, d, tn), lambda i, n: (l, 0, n))],
        out_specs=[pl.BlockSpec((tm, tn), lambda i, n: (i, n)), _rider_spec(rc, tn, nn)],
        out_shape=[jax.ShapeDtypeStruct((rows, ncols), F32), jax.ShapeDtypeStruct((rc, ncols), F32)],
        scratch_shapes=[pltpu.VMEM((tm, d), BF16), pltpu.VMEM((rc, d), BF16)],
        compiler_params=_cparams(("arbitrary", "arbitrary")),
        name="mixer_in_proj",
    )(x, xc, mods, mods, g, w_in_bf16)


def _to_col_major(x):
    rows, d = x.shape
    return x.reshape(rows // GRID_W, GRID_W, d).transpose(1, 0, 2).reshape(rows, d)


def _from_col_major(x):
    rows, d = x.shape
    return x.reshape(GRID_W, rows // GRID_W, d).transpose(1, 0, 2).reshape(rows, d)


def _lru_kernel(cv_ref, wax_ref, ba_ref, bx_ref, lam_ref, h0_ref, *rest, t_blk, heads, reverse, add_prev):
    if add_prev:
        prev_ref, o_ref, hf_ref, a_scr, b_scr, carry = rest
    else:
        o_ref, hf_ref, a_scr, b_scr, carry = rest
    c = cv_ref.shape[1]
    hd = c // heads

    @pl.when(pl.program_id(0) == 0)
    def _():
        carry[...] = h0_ref[...]

    cv = cv_ref[...]
    z = -lam_ref[...]
    c_sp = LRU_C * (jnp.maximum(z, 0.0) + jnp.log(1.0 + jnp.exp(-jnp.abs(z))))
    cvb = cv.astype(BF16)
    for h in range(heads):
        sl = slice(h * hd, (h + 1) * hd)
        gx = jnp.dot(cvb[:, sl], wax_ref[h], preferred_element_type=F32)
        r = 0.5 + 0.5 * jnp.tanh(0.5 * (gx[:, :hd] + ba_ref[:, sl]))
        ig = 0.5 + 0.5 * jnp.tanh(0.5 * (gx[:, hd:] + bx_ref[:, sl]))
        a = jnp.exp(-c_sp[:, sl] * r)
        a_scr[:, sl] = a
        b_scr[:, sl] = jnp.sqrt(1.0 - a * a) * (ig * cv[:, sl])

    row = lax.broadcasted_iota(jnp.int32, (SUBLANES, c), 0)
    ngrp = t_blk // SUBLANES

    def group(gi, hc):
        g = (ngrp - 1 - gi) if reverse else gi
        r0 = pl.multiple_of(g * SUBLANES, SUBLANES)
        av = a_scr[pl.ds(r0, SUBLANES), :]
        bv = b_scr[pl.ds(r0, SUBLANES), :]
        for s in (1, 2, 4):
            sh = (SUBLANES - s) if reverse else s
            m = (row < SUBLANES - s) if reverse else (row >= s)
            a_s = pltpu.roll(av, sh, axis=0)
            b_s = pltpu.roll(bv, sh, axis=0)
            bv = jnp.where(m, av * b_s + bv, bv)
            av = jnp.where(m, av * a_s, av)
        hv = av * hc + bv
        if add_prev:
            o_ref[pl.ds(r0, SUBLANES), :] = hv + prev_ref[pl.ds(r0, SUBLANES), :]
        else:
            o_ref[pl.ds(r0, SUBLANES), :] = hv
        return hv[0:1, :] if reverse else hv[SUBLANES - 1:SUBLANES, :]

    hc = lax.fori_loop(0, ngrp, group, carry[...])
    carry[...] = hc
    hf_ref[...] = hc


def _lru_scan(cv, wax, ba, bx, lam, h0, prev, reverse):
    seq, c = cv.shape
    heads = wax.shape[0]
    t_blk = min(seq, 1024)
    nblk = seq // t_blk

    def bi(j):
        return (nblk - 1 - j) if reverse else j

    row = pl.BlockSpec((1, c), lambda j: (0, 0))
    in_specs = [pl.BlockSpec((t_blk, c), lambda j: (bi(j), 0)),
                pl.BlockSpec(wax.shape, lambda j: (0, 0, 0)), row, row, row, row]
    args = [cv, wax, ba, bx, lam, h0]
    if prev is not None:
        in_specs.append(pl.BlockSpec((t_blk, c), lambda j: (bi(j), 0)))
        args.append(prev)
    return pl.pallas_call(
        functools.partial(_lru_kernel, t_blk=t_blk, heads=heads, reverse=reverse, add_prev=prev is not None),
        grid=(nblk,),
        in_specs=in_specs,
        out_specs=[pl.BlockSpec((t_blk, c), lambda j: (bi(j), 0)), row],
        out_shape=[jax.ShapeDtypeStruct((seq, c), F32), jax.ShapeDtypeStruct((1, c), F32)],
        scratch_shapes=[pltpu.VMEM((t_blk, c), F32), pltpu.VMEM((t_blk, c), F32), pltpu.VMEM((1, c), F32)],
        compiler_params=_cparams(("arbitrary",)),
        name="rglru_scan",
    )(*args)


def _short_conv_kernel(*refs, t_blk, nblk, c, with_hyena):
    n_in = 4 if with_hyena else 1
    ins = [refs[3 * k:3 * k + 3] for k in range(n_in)]
    lw_ref, lb_ref, hw_ref, hb_ref = refs[3 * n_in:3 * n_in + 4]
    outs = refs[3 * n_in + 4:-1]
    xs = refs[-1]
    blk = pl.program_id(0)
    zero8 = jnp.zeros((SUBLANES, c), F32)

    def conv(src, w_ref, b_ref, part):
        cur, prv, nxt = src
        xs[0:SUBLANES, :] = jnp.where(blk == 0, zero8, prv[...])
        xs[SUBLANES:SUBLANES + t_blk, :] = cur[...]
        xs[SUBLANES + t_blk:2 * SUBLANES + t_blk, :] = jnp.where(blk == nblk - 1, zero8, nxt[...])
        taps = w_ref.shape[0]
        left = (taps - 1) // 2
        sl = slice(part * c, (part + 1) * c)
        out = b_ref[:, sl]
        for k in range(taps):
            r0 = SUBLANES - left + k
            out = out + xs[r0:r0 + t_blk, :] * w_ref[k:k + 1, sl]
        return out

    outs[0][...] = conv(ins[0], lw_ref, lb_ref, 0)
    if with_hyena:
        outs[2][...] = conv(ins[1], hw_ref, hb_ref, 0)
        x1c = conv(ins[2], hw_ref, hb_ref, 1)
        outs[1][...] = conv(ins[3], hw_ref, hb_ref, 2) * x1c


def _short_convs(p, lru_w, lru_b, hy_w, hy_b, with_hyena):
    seq = p.shape[0]
    c = lru_w.shape[1]
    t_blk = min(seq, 512)
    nblk = seq // t_blk
    per = t_blk // SUBLANES
    nb8 = seq // SUBLANES
    in_specs, args = [], []
    for col in ([0, 2, 3, 4] if with_hyena else [0]):
        in_specs += [pl.BlockSpec((t_blk, c), lambda j, col=col: (j, col)),
                     pl.BlockSpec((SUBLANES, c), lambda j, col=col: (jnp.maximum(j * per - 1, 0), col)),
                     pl.BlockSpec((SUBLANES, c), lambda j, col=col: (jnp.minimum((j + 1) * per, nb8 - 1), col))]
        args += [p, p, p]
    full = lambda a: pl.BlockSpec(a.shape, lambda j: (0, 0))
    in_specs += [full(lru_w), full(lru_b), full(hy_w), full(hy_b)]
    args += [lru_w, lru_b, hy_w, hy_b]
    n_out = 3 if with_hyena else 1
    return pl.pallas_call(
        functools.partial(_short_conv_kernel, t_blk=t_blk, nblk=nblk, c=c, with_hyena=with_hyena),
        grid=(nblk,),
        in_specs=in_specs,
        out_specs=[pl.BlockSpec((t_blk, c), lambda j: (j, 0))] * n_out,
        out_shape=[jax.ShapeDtypeStruct((seq, c), F32)] * n_out,
        scratch_shapes=[pltpu.VMEM((t_blk + 2 * SUBLANES, c), F32)],
        compiler_params=_cparams(("arbitrary",)),
        name="mixer_short_convs",
    )(*args)


FILT_SLOT = 64


def _filt_kernel(w1_ref, b1_ref, w2_ref, b2_ref, w3_ref, b3_ref, w4a_ref, w4b_ref, b4_ref, fr_ref,
                 k_ref, nrm_ref, ec_scr, es_scr, *, seq, t_blk, c):
    i = pl.program_id(0)
    hi = lax.Precision.HIGHEST
    lanes = 2 * FILT_SLOT
    th = t_blk // 2
    lane = lax.broadcasted_iota(jnp.int32, (1, lanes), 1)
    slot = lane & (FILT_SLOT - 1)
    pos_a = i * t_blk + lax.broadcasted_iota(jnp.int32, (th, 1), 0)
    pos = jnp.where(lane < FILT_SLOT, pos_a, pos_a + th)
    tf = jnp.where(pos < seq, pos, 2 * seq - pos).astype(F32)
    t = tf * (1.0 / (seq - 1))
    is_cos = (slot >= 1) & (slot <= HY_BANDS)
    is_sin = (slot > HY_BANDS) & (slot <= 2 * HY_BANDS)
    band = jnp.where(is_cos, slot - 1, slot - 1 - HY_BANDS).astype(F32)
    alpha = (2.0 * math.pi / seq) * (1e-4 + band * ((HY_BANDS - 1 - 1e-4) / (HY_BANDS - 1)))

    @pl.when(i == 0)
    def _():
        off = (pos - i * t_blk).astype(F32)
        ec_scr[...] = jnp.sin(alpha * off + 0.5 * math.pi)
        es_scr[...] = jnp.sin(alpha * off)

    forward = i * t_blk < seq
    base = jnp.where(forward, i * t_blk, 2 * seq - i * t_blk).astype(F32)
    sigma = jnp.where(forward, 1.0, -1.0)
    cb = jnp.sin(alpha * base + 0.5 * math.pi)
    sb = jnp.sin(alpha * base)
    a_row = jnp.where(is_cos, cb, jnp.where(is_sin, -sb, 0.0))
    b_row = -sigma * jnp.where(is_cos, sb, jnp.where(is_sin, cb, 0.0))
    z = jnp.where(slot == 0, t, a_row * ec_scr[...] + b_row * es_scr[...])
    fr = fr_ref[...]
    h = jnp.sin(fr * (jnp.dot(z, w1_ref[...], precision=hi, preferred_element_type=F32) + b1_ref[...]))
    h = jnp.sin(fr * (jnp.dot(h, w2_ref[...], precision=hi, preferred_element_type=F32) + b2_ref[...]))
    h = jnp.sin(fr * (jnp.dot(h, w3_ref[...], precision=hi, preferred_element_type=F32) + b3_ref[...]))
    ch = lax.broadcasted_iota(jnp.int32, (1, c), 1).astype(F32)
    delta = jnp.abs(HY_MIN_DECAY + ch * ((HY_MAX_DECAY - HY_MIN_DECAY) / (c - 1)))
    part = jnp.zeros((1, c), F32)
    h_hi = h.astype(BF16)
    h_lo = (h - h_hi.astype(F32)).astype(BF16)
    for half, w4_ref in enumerate((w4a_ref, w4b_ref)):
        pos_h = pos_a + half * th
        t_h = jnp.where(pos_h < seq, pos_h, 2 * seq - pos_h).astype(F32) * (1.0 / (seq - 1))
        w_hi, w_lo = w4_ref[0], w4_ref[1]
        k = (jnp.dot(h_hi, w_hi, preferred_element_type=F32) + jnp.dot(h_lo, w_hi, preferred_element_type=F32)
             + jnp.dot(h_hi, w_lo, preferred_element_type=F32)) + b4_ref[...]
        k = jnp.where(pos_h != seq, k * jnp.exp(-t_h * delta), 0.0)
        k_ref[half * th:(half + 1) * th, :] = k
        part = part + jnp.sum(jnp.abs(k), axis=0, keepdims=True)

    @pl.when(i == 0)
    def _():
        nrm_ref[...] = part

    @pl.when(i > 0)
    def _():
        nrm_ref[...] = nrm_ref[...] + part


def _slots(a, row_used):
    r, cdim = a.shape
    blk = jnp.pad(a, ((0, (FILT_SLOT if row_used else 1) - r), (0, FILT_SLOT - cdim)))
    if not row_used:
        return jnp.concatenate([blk, blk], axis=1)
    z = jnp.zeros_like(blk)
    return jnp.concatenate([jnp.concatenate([blk, z], axis=1), jnp.concatenate([z, blk], axis=1)], axis=0)


def _hyena_filter(seq, w1, b1, w2, b2, w3, b3, w4, b4, freq):
    c = w4.shape[1] // 2
    od = w2.shape[0]
    assert od <= FILT_SLOT and w1.shape[0] <= FILT_SLOT
    lanes = 2 * FILT_SLOT
    t_blk = min(seq, 1024)
    half_blocks = seq // t_blk
    full = lambda shape: pl.BlockSpec(shape, lambda i: (0,) * len(shape))
    sq, row = (lanes, lanes), (1, lanes)
    w4p = jnp.pad(w4, ((0, FILT_SLOT - od), (0, 0)))
    w4_hi = w4p.astype(BF16)
    w4_lo = (w4p - w4_hi.astype(F32)).astype(BF16)
    w4s = jnp.stack([w4_hi, w4_lo])
    w4a = jnp.concatenate([w4s, jnp.zeros_like(w4s)], axis=1)
    w4b = jnp.concatenate([jnp.zeros_like(w4s), w4s], axis=1)
    w4_spec = pl.BlockSpec((2, lanes, c), lambda i: (0, 0, i // half_blocks))
    return pl.pallas_call(
        functools.partial(_filt_kernel, seq=seq, t_blk=t_blk, c=c),
        grid=(2 * seq // t_blk,),
        in_specs=[full(sq), full(row), full(sq), full(row), full(sq), full(row), w4_spec, w4_spec,
                  pl.BlockSpec((1, c), lambda i: (0, i // half_blocks)),
                  full(row)],
        out_specs=[pl.BlockSpec((t_blk, c), lambda i: (i, 0)), pl.BlockSpec((1, c), lambda i: (0, 0))],
        out_shape=[jax.ShapeDtypeStruct((2 * seq, c), F32), jax.ShapeDtypeStruct((1, c), F32)],
        scratch_shapes=[pltpu.VMEM((t_blk // 2, lanes), F32), pltpu.VMEM((t_blk // 2, lanes), F32)],
        compiler_params=_cparams(("arbitrary",)),
        name="hyena_filter",
    )(_slots(w1, True), _slots(b1[None, :], False), _slots(w2, True), _slots(b2[None, :], False),
      _slots(w3, True), _slots(b3[None, :], False), w4a, w4b, b4[None, :], _slots(freq[None, :], False))


def _bitrev(p, bits):
    r = 0
    for k in range(bits):
        r = (r << 1) | ((p >> k) & 1)
    return r


def _real_block(z):
    return np.block([[z.real, -z.imag], [z.imag, z.real]])


def _fft_split(s2):
    n_lo = max(min(8, s2 // 2), 1)
    return s2 // n_lo, n_lo


@functools.lru_cache(maxsize=None)
def _fft_tables(s):
    s2 = s // 2
    n = s * FFT_B
    n_hi, n_lo = _fft_split(s2)
    bits_hi, bits_lo = n_hi.bit_length() - 1, n_lo.bit_length() - 1
    b = np.arange(FFT_B)
    dft = np.exp(-2j * np.pi * np.outer(b, b) / FFT_B)
    e_lo = [2 * n_hi * _bitrev(p, bits_lo) for p in range(n_lo)]
    g = [dft * np.exp(-2j * np.pi * b * e / n)[None, :] for e in e_lo]
    chunks = np.arange(FFT_B).reshape(-1, FFT_ROWS)
    order = np.concatenate([np.concatenate([c, c + FFT_B]) for c in chunks])
    lhs_f = np.stack([_real_block(x)[order] for x in g])
    lhs_i = np.stack([_real_block(np.conj(x).T)[order] for x in g])
    e_hi = np.array([[2 * _bitrev(p, bits_hi) + hf for p in range(n_hi)] for hf in range(2)])
    ang = 2.0 * np.pi * e_hi[:, :, None, None] * b[None, None, :, None] / n * np.ones((1, 1, 1, FFT_B))
    lvl = 2.0 * np.pi * np.arange(s2) / s2
    first = 2.0 * np.pi * np.arange(s2) / s
    f32 = lambda x: np.asarray(x, np.float32)
    return dict(lhs_f=f32(lhs_f), lhs_i=f32(lhs_i), col_c=f32(np.cos(ang)), col_s=f32(np.sin(ang)),
                lvl_c=f32(np.cos(lvl)), lvl_s=f32(np.sin(lvl)), first_c=f32(np.cos(first)),
                first_s=f32(np.sin(first)))


FFT_ROWS = 32


def _cmul(xr, xi, c, s):
    return xr * c + xi * s, xi * c - xr * s


def _cmul_conj(xr, xi, c, s):
    return xr * c - xi * s, xi * c + xr * s


def _pass_plan(s2):
    levels = s2.bit_length() - 1
    plan, h = [], s2 // 2
    if levels % 2 == 1:
        plan.append((h, 2))
        h //= 2
    while h >= 1:
        plan.append((h, 4))
        h //= 4
    return plan


def _butterfly_pass(load, store, lc_ref, ls_ref, s2, h, radix, inverse, real_input=False):
    h2 = h // 2 if radix == 4 else h
    trivial = h2 == 1
    shift = h2.bit_length() - 1
    assert not real_input or (radix == 4 and not inverse and not trivial)

    def group(idx, carry):
        g = idx >> shift
        i = idx & (h2 - 1)
        base = g * 2 * h + i
        blocks = [base + k * h2 for k in range(radix)]
        c1 = s1 = c2 = s2_ = c3 = s3 = None
        if not trivial:
            t1 = i * (s2 // (2 * h))
            c1, s1 = lc_ref[t1], ls_ref[t1]
            if radix == 4:
                c2, s2_, c3, s3 = lc_ref[2 * t1], ls_ref[2 * t1], lc_ref[3 * t1], ls_ref[3 * t1]
        fwd_mul = (lambda xr, xi, c, s: (xr, xi)) if trivial else _cmul
        inv_mul = (lambda xr, xi, c, s: (xr, xi)) if trivial else _cmul_conj
        for r in range(0, FFT_B, FFT_ROWS):
            x = [load(bk, r) for bk in blocks]
            if radix == 2:
                (ar, ai), (br, bi) = x
                if inverse:
                    br, bi = inv_mul(br, bi, c1, s1)
                    out = [(ar + br, ai + bi), (ar - br, ai - bi)]
                else:
                    out = [(ar + br, ai + bi), fwd_mul(ar - br, ai - bi, c1, s1)]
            elif real_input:
                x0, x1, x2, x3 = (v[0] for v in x)
                t0, t1, t2, t3 = x0 + x2, x1 + x3, x0 - x2, x1 - x3
                dif = t0 - t1
                out = [(t0 + t1, jnp.zeros_like(t0)), (dif * c2, -(dif * s2_)),
                       (t2 * c1 - t3 * s1, -(t3 * c1 + t2 * s1)), (t2 * c3 + t3 * s3, t3 * c3 - t2 * s3)]
            elif inverse:
                (x0r, x0i), (x1r, x1i), (x2r, x2i), (x3r, x3i) = x
                x1r, x1i = inv_mul(x1r, x1i, c2, s2_)
                x2r, x2i = inv_mul(x2r, x2i, c1, s1)
                x3r, x3i = inv_mul(x3r, x3i, c3, s3)
                t0r, t0i, t1r, t1i = x0r + x1r, x0i + x1i, x0r - x1r, x0i - x1i
                t2r, t2i, t3r, t3i = x2r + x3r, x2i + x3i, x2r - x3r, x2i - x3i
                out = [(t0r + t2r, t0i + t2i), (t1r - t3i, t1i + t3r),
                       (t0r - t2r, t0i - t2i), (t1r + t3i, t1i - t3r)]
            else:
                (x0r, x0i), (x1r, x1i), (x2r, x2i), (x3r, x3i) = x
                t0r, t0i, t1r, t1i = x0r + x2r, x0i + x2i, x1r + x3r, x1i + x3i
                t2r, t2i, t3r, t3i = x0r - x2r, x0i - x2i, x1r - x3r, x1i - x3i
                out = [(t0r + t1r, t0i + t1i), fwd_mul(t0r - t1r, t0i - t1i, c2, s2_),
                       fwd_mul(t2r + t3i, t2i - t3r, c1, s1), fwd_mul(t2r - t3i, t2i + t3r, c3, s3)]
            for bk, (o_r, o_i) in zip(blocks, out):
                store(bk, r, o_r, o_i)
        return carry

    lax.fori_loop(0, s2 // radix, group, 0)


def _work_access(wr, wi):
    def rows(bk, r):
        return pl.ds(pl.multiple_of(bk * FFT_B + r, FFT_ROWS), FFT_ROWS)

    def load(bk, r):
        return wr[rows(bk, r), :], wi[rows(bk, r), :]

    def store(bk, r, re, im):
        wr[rows(bk, r), :] = re
        wi[rows(bk, r), :] = im

    return load, store


def _forward_passes(hf, even_load, odd_load, pack_store, work, lc_ref, ls_ref, s2):
    plan = _pass_plan(s2)
    (h, radix), rest = plan[0], plan[1:]
    store0 = work[1] if rest else pack_store
    if radix == 4 and h > 2:
        @pl.when(hf == 0)
        def _():
            _butterfly_pass(lambda bk, r: (even_load(bk, r), None), store0, lc_ref, ls_ref, s2, h, radix,
                            False, real_input=True)

        @pl.when(hf == 1)
        def _():
            _butterfly_pass(odd_load, store0, lc_ref, ls_ref, s2, h, radix, False)
    else:
        def load(bk, r):
            ev = even_load(bk, r)
            od_r, od_i = odd_load(bk, r)
            return jnp.where(hf == 0, ev, od_r), jnp.where(hf == 0, 0.0, od_i)
        _butterfly_pass(load, store0, lc_ref, ls_ref, s2, h, radix, False)
    if rest:
        _run_passes(rest, work[0], pack_store, work, lc_ref, ls_ref, s2, inverse=False)


def _run_passes(plan, first_load, last_store, work, lc_ref, ls_ref, s2, inverse):
    w_load, w_store = work
    for k, (h, radix) in enumerate(plan):
        load = first_load if k == 0 else w_load
        store = last_store if k == len(plan) - 1 else w_store
        _butterfly_pass(load, store, lc_ref, ls_ref, s2, h, radix, inverse)


MID_UNROLL = 32


def _pair_blocks(pp, s2):
    n_hi, n_lo = _fft_split(s2)
    half = n_hi // 2
    p_lo = pp >> (half.bit_length() - 1)
    j = pp & (half - 1)
    return p_lo, [(2 * j + q, pl.multiple_of(((2 * j + q) * n_lo + p_lo) * FFT_B, FFT_B)) for q in range(2)]


def _for_pairs(s2, body):
    def step(pp, carry):
        body(pp)
        return carry
    lax.fori_loop(0, s2 // 2, step, 0, unroll=min(MID_UNROLL, s2 // 2))


def _pack_store(panel, cc_ref, cs_ref, s2):
    n_hi, n_lo = _fft_split(s2)
    lo_bits = n_lo.bit_length() - 1

    def store(bk, r, re, im):
        p_hi = bk >> lo_bits
        p_lo = bk & (n_lo - 1)
        pp = p_lo * (n_hi // 2) + (p_hi >> 1)
        q = p_hi & 1
        tr, ti = _cmul(re, im, cc_ref[p_hi, r:r + FFT_ROWS, :], cs_ref[p_hi, r:r + FFT_ROWS, :])
        panel[pp, q, r:r + FFT_ROWS, :] = tr.astype(BF16)
        panel[pp, q, FFT_B + r:FFT_B + r + FFT_ROWS, :] = ti.astype(BF16)

    return store


def _dft_pair(m_ref, panel, pp, p_lo):
    blk = FFT_B
    rhs = jnp.concatenate([panel[pp, 0], panel[pp, 1]], axis=1)
    out = jnp.dot(m_ref[p_lo], rhs, preferred_element_type=F32)
    for r in range(0, blk, FFT_ROWS):
        for q in range(2):
            yield (q, r, out[2 * r:2 * r + FFT_ROWS, q * blk:(q + 1) * blk],
                   out[2 * r + FFT_ROWS:2 * r + 2 * FFT_ROWS, q * blk:(q + 1) * blk])


def _spec_kernel(lo_ref, hi_ref, nrm_ref, fc_ref, fs_ref, lc_ref, ls_ref, cc_ref, cs_ref, mf_ref,
                 kr_ref, ki_ref, panel, *, s2):
    blk = FFT_B
    hf = pl.program_id(1)
    n = 2 * s2 * blk

    def rows_of(bk, r):
        return pl.ds(pl.multiple_of(bk * blk + r, FFT_ROWS), FFT_ROWS)

    def even_load(bk, r):
        return lo_ref[rows_of(bk, r), :] + hi_ref[rows_of(bk, r), :]

    def odd_load(bk, r):
        d = lo_ref[rows_of(bk, r), :] - hi_ref[rows_of(bk, r), :]
        return d * fc_ref[bk], -d * fs_ref[bk]

    _forward_passes(hf, even_load, odd_load, _pack_store(panel, cc_ref, cs_ref, s2),
                    _work_access(kr_ref, ki_ref), lc_ref, ls_ref, s2)
    scale = 1.0 / (nrm_ref[...] * n)

    def dft(pp):
        p_lo, blocks = _pair_blocks(pp, s2)
        for q, r, yr, yi in _dft_pair(mf_ref, panel, pp, p_lo):
            rows = pl.ds(blocks[q][1] + r, FFT_ROWS)
            kr_ref[rows, :] = yr * scale
            ki_ref[rows, :] = yi * scale

    _for_pairs(s2, dft)


def _conv_kernel(u_ref, bias_ref, kr_ref, ki_ref, fc_ref, fs_ref, lc_ref, ls_ref, cc_ref, cs_ref,
                 mf_ref, mi_ref, y_ref, wr, wi, panel, *, s2):
    blk = FFT_B
    hf = pl.program_id(1)
    plan = _pass_plan(s2)
    work = _work_access(wr, wi)

    def even_load(bk, r):
        return u_ref[pl.ds(pl.multiple_of(bk * blk + r, FFT_ROWS), FFT_ROWS), :]

    def odd_load(bk, r):
        ub = even_load(bk, r)
        return ub * fc_ref[bk], -ub * fs_ref[bk]

    _forward_passes(hf, even_load, odd_load, _pack_store(panel, cc_ref, cs_ref, s2), work, lc_ref, ls_ref, s2)

    def spectrum_product(pp):
        p_lo, blocks = _pair_blocks(pp, s2)
        for q, r, yr, yi in _dft_pair(mf_ref, panel, pp, p_lo):
            rows = pl.ds(blocks[q][1] + r, FFT_ROWS)
            kr, ki = kr_ref[rows, :], ki_ref[rows, :]
            panel[pp, q, r:r + FFT_ROWS, :] = (yr * kr - yi * ki).astype(BF16)
            panel[pp, q, blk + r:blk + r + FFT_ROWS, :] = (yr * ki + yi * kr).astype(BF16)

    _for_pairs(s2, spectrum_product)

    def inverse_dft(pp):
        p_lo, blocks = _pair_blocks(pp, s2)
        for q, r, er, ei in _dft_pair(mi_ref, panel, pp, p_lo):
            p_hi, r0 = blocks[q]
            rows = pl.ds(r0 + r, FFT_ROWS)
            wr[rows, :], wi[rows, :] = _cmul_conj(er, ei, cc_ref[p_hi, r:r + FFT_ROWS, :],
                                                  cs_ref[p_hi, r:r + FFT_ROWS, :])

    _for_pairs(s2, inverse_dft)

    def rows_of(bk, r):
        return pl.ds(pl.multiple_of(bk * blk + r, FFT_ROWS), FFT_ROWS)

    def store_even(bk, r, er, ei):
        y_ref[rows_of(bk, r), :] = er

    def store_odd(bk, r, er, ei):
        rows = rows_of(bk, r)
        val = er * fc_ref[bk] - ei * fs_ref[bk]
        y_ref[rows, :] = y_ref[rows, :] + u_ref[rows, :] * bias_ref[...] + val

    inv_plan = plan[::-1]
    if len(inv_plan) > 1:
        _run_passes(inv_plan[:-1], work[0], work[1], work, lc_ref, ls_ref, s2, inverse=True)
    h_last, radix_last = inv_plan[-1]
    for half, store in ((0, store_even), (1, store_odd)):
        @pl.when(hf == half)
        def _(store=store):
            _butterfly_pass(work[0], store, lc_ref, ls_ref, s2, h_last, radix_last, True)


def _smem_spec():
    return pl.BlockSpec(memory_space=pltpu.SMEM)


def _fft_table_specs(s2):
    n_hi, n_lo = _fft_split(s2)
    col = pl.BlockSpec((None, n_hi, FFT_B, FFT_B), lambda ct, hf: (hf, 0, 0, 0))
    lhs = pl.BlockSpec((n_lo, 2 * FFT_B, 2 * FFT_B), lambda ct, hf: (0, 0, 0))
    return [_smem_spec(), _smem_spec(), _smem_spec(), _smem_spec(), col, col], lhs


def _hyena_spectrum(kfull, nrm):
    n, c = kfull.shape
    seq = n // 2
    s2 = seq // FFT_B
    tb = _fft_tables(2 * s2)
    lanes = 128
    tab_specs, lhs_spec = _fft_table_specs(s2)
    return pl.pallas_call(
        functools.partial(_spec_kernel, s2=s2),
        grid=(c // lanes, 2),
        in_specs=[pl.BlockSpec((seq, lanes), lambda ct, hf: (0, ct)),
                  pl.BlockSpec((seq, lanes), lambda ct, hf: (1, ct)),
                  pl.BlockSpec((1, lanes), lambda ct, hf: (0, ct))] + tab_specs + [lhs_spec],
        out_specs=[pl.BlockSpec((None, seq, lanes), lambda ct, hf: (hf, 0, ct)),
                   pl.BlockSpec((None, seq, lanes), lambda ct, hf: (hf, 0, ct))],
        out_shape=[jax.ShapeDtypeStruct((2, seq, c), F32), jax.ShapeDtypeStruct((2, seq, c), F32)],
        scratch_shapes=[pltpu.VMEM((s2 // 2, 2, 2 * FFT_B, FFT_B), BF16)],
        compiler_params=_cparams(("arbitrary", "arbitrary")),
        name="hyena_spectrum",
    )(kfull, kfull, nrm, tb["first_c"], tb["first_s"], tb["lvl_c"], tb["lvl_s"],
      tb["col_c"], tb["col_s"], jnp.asarray(tb["lhs_f"], BF16))


def _hyena_conv(u, bias, kr, ki):
    seq, c = u.shape
    s2 = seq // FFT_B
    tb = _fft_tables(2 * s2)
    lanes = 128
    tab_specs, lhs_spec = _fft_table_specs(s2)
    return pl.pallas_call(
        functools.partial(_conv_kernel, s2=s2),
        grid=(c // lanes, 2),
        in_specs=[pl.BlockSpec((seq, lanes), lambda ct, hf: (0, ct)),
                  pl.BlockSpec((1, lanes), lambda ct, hf: (0, ct)),
                  pl.BlockSpec((None, seq, lanes), lambda ct, hf: (hf, 0, ct)),
                  pl.BlockSpec((None, seq, lanes), lambda ct, hf: (hf, 0, ct))]
                 + tab_specs + [lhs_spec, lhs_spec],
        out_specs=pl.BlockSpec((seq, lanes), lambda ct, hf: (0, ct)),
        out_shape=jax.ShapeDtypeStruct((seq, c), F32),
        scratch_shapes=[pltpu.VMEM((seq, lanes), F32), pltpu.VMEM((seq, lanes), F32),
                        pltpu.VMEM((s2 // 2, 2, 2 * FFT_B, FFT_B), BF16)],
        compiler_params=_cparams(("arbitrary", "arbitrary")),
        name="hyena_fft_conv",
    )(u, bias, kr, ki, tb["first_c"], tb["first_s"], tb["lvl_c"], tb["lvl_s"],
      tb["col_c"], tb["col_s"], jnp.asarray(tb["lhs_f"], BF16), jnp.asarray(tb["lhs_i"], BF16))


MIXOUT_CHUNK = 256


def _mixout_kernel(h_ref, yr_ref, hy_ref, x0_ref, og_ref, w_ref, x_ref, gate_ref, o_ref, *, stream, lw, tm):
    gate = gate_ref[stream:stream + 1, :]
    w = w_ref[...]
    for r in range(0, tm, MIXOUT_CHUNK):
        rs = slice(r, r + min(MIXOUT_CHUNK, tm))
        yr = yr_ref[rs, :]
        gelu = 0.5 * yr * (1.0 + jnp.tanh(math.sqrt(2.0 / math.pi) * (yr + 0.044715 * (yr * yr * yr))))
        lru = (_rms(h_ref[rs, :] * gelu) * og_ref[:, :lw]).astype(BF16)
        hy = (_rms(hy_ref[rs, :] * x0_ref[rs, :]) * og_ref[:, lw:]).astype(BF16)
        y = (jnp.dot(lru, w[:lw, :], preferred_element_type=F32)
             + jnp.dot(hy, w[lw:, :], preferred_element_type=F32))
        o_ref[rs, :] = x_ref[rs, :] + gate * y


def _mixout(h_lru, p, y_hy, x0c, og, w_out_bf16, x, mods, l, stream):
    rows, d = x.shape
    lw = h_lru.shape[1]
    dm = w_out_bf16.shape[1]
    tm = min(rows, 512)
    row_blk = lambda cols, j: pl.BlockSpec((tm, cols), lambda i: (i, j))
    return pl.pallas_call(
        functools.partial(_mixout_kernel, stream=stream, lw=lw, tm=tm),
        grid=(rows // tm,),
        in_specs=[row_blk(lw, 0), row_blk(lw, 1), row_blk(dm - lw, 0), row_blk(dm - lw, 0),
                  pl.BlockSpec((1, dm), lambda i: (0, 0)),
                  pl.BlockSpec((None, dm, d), lambda i: (l, 0, 0)),
                  row_blk(d, 0),
                  _mod_spec(l, 1, 2, d)],
        out_specs=row_blk(d, 0),
        out_shape=jax.ShapeDtypeStruct((rows, d), F32),
        compiler_params=_cparams(("arbitrary",)),
        name="mixer_out_proj",
    )(h_lru, p, y_hy, x0c, og, w_out_bf16, x, mods)


def _final_kernel(x_ref, g_ref, o_ref):
    o_ref[...] = _rms(x_ref[...]) * g_ref[...]


def _final_norm(x, g):
    rows, d = x.shape
    tm = 1024
    return pl.pallas_call(
        _final_kernel,
        grid=(rows // tm,),
        in_specs=[pl.BlockSpec((tm, d), lambda i: (i, 0)), pl.BlockSpec((1, d), lambda i: (0, 0))],
        out_specs=pl.BlockSpec((tm, d), lambda i: (i, 0)),
        out_shape=jax.ShapeDtypeStruct((rows, d), F32),
        compiler_params=_cparams(("arbitrary",)),
        name="final_norm",
    )(x, g)


def kernel(x, c, ctx, c_ctx, ada_w, ada_b, norm_g, ffn_wg, ffn_wu, ffn_wd, w_in, w_out, out_g, lru_conv_w, lru_conv_b, lru_wa, lru_ba, lru_wx, lru_bx, lru_lam, hy_conv_w, hy_conv_b, hy_bias, filt_w1, filt_b1, filt_w2, filt_b2, filt_w3, filt_b3, filt_w4, filt_b4, filt_freq, final_g):
    assert x.shape[0] == 1 and ctx.shape[0] == 1
    depth = ada_w.shape[0]
    d = x.shape[-1]
    lw = lru_conv_w.shape[-1]
    n_lat, n_ctx = x.shape[1], ctx.shape[1]
    xl, xc = x.reshape(n_lat, d), ctx.reshape(n_ctx, d)
    mods = _ada_mods(c, c_ctx, ada_w, ada_b)
    w_in_bf16, w_out_bf16 = w_in.astype(BF16), w_out.astype(BF16)
    zero_state = jnp.zeros((1, lw), F32)
    lat_is_col = False
    for l in range(depth):
        last = l == depth - 1
        g = norm_g[l][:, None, :]

        xl, xc = _ffn(xl, xc, mods, 0, g[0], ffn_wg, ffn_wu, ffn_wd, l, 0)

        want_col = l % 2 == 1
        if want_col != lat_is_col:
            xl = _to_col_major(xl) if want_col else _from_col_major(xl)
            lat_is_col = want_col
        pl_, pc_ = _win(xl, xc, mods, l, g[1], w_in_bf16)

        convs = (lru_conv_w[l], lru_conv_b[l][None, :], hy_conv_w[l], hy_conv_b[l][None, :])
        cv_l, u_l, x0_l = _short_convs(pl_, *convs, True)
        cv_c, *hy_c = _short_convs(pc_, *convs, not last)

        h_lat = h_ctx = None
        for dr in range(2):
            wax = jnp.concatenate([lru_wa[l, dr], lru_wx[l, dr]], axis=-1).astype(BF16)
            ba, bx, lam = lru_ba[l, dr][None, :], lru_bx[l, dr][None, :], lru_lam[l, dr][None, :]
            h_ctx, state = _lru_scan(cv_c, wax, ba, bx, lam, zero_state, h_ctx, dr == 1)
            h_lat, _ = _lru_scan(cv_l, wax, ba, bx, lam, state, h_lat, dr == 1)

        filt = (filt_w1[l], filt_b1[l], filt_w2[l], filt_b2[l], filt_w3[l], filt_b3[l], filt_w4[l],
                filt_b4[l], filt_freq[l])
        hbias = hy_bias[l][None, :]
        og = out_g[l][None, :]

        kr, ki = _hyena_spectrum(*_hyena_filter(n_lat, *filt))
        y_hy = _hyena_conv(u_l, hbias, kr, ki)
        xl = _mixout(h_lat, pl_, y_hy, x0_l, og, w_out_bf16, xl, mods, l, LAT)

        if last:
            xc = None
        else:
            u_c, x0_c = hy_c
            kr, ki = _hyena_spectrum(*_hyena_filter(n_ctx, *filt))
            y_hy = _hyena_conv(u_c, hbias, kr, ki)
            xc = _mixout(h_ctx, pc_, y_hy, x0_c, og, w_out_bf16, xc, mods, l, CTX)
        xl, xc = _ffn(xl, xc, mods, 2, g[2], ffn_wg, ffn_wu, ffn_wd, l, 1)

    if lat_is_col:
        xl = _from_col_major(xl)
    return _final_norm(xl, final_g[None, :]).reshape(x.shape)
```

```python
import functools
import math

import numpy as np
import jax
import jax.numpy as jnp
from jax import lax
from jax.experimental import pallas as pl
from jax.experimental.pallas import tpu as pltpu

F32 = jnp.float32
BF16 = jnp.bfloat16
EPS = 1e-6
LRU_C = 8.0
GRID_W = 64
HY_BANDS = 16
HY_MAX_DECAY = math.log(1e-2) / 0.3
HY_MIN_DECAY = math.log(1e-2) / 1.5

V7X_VMEM_LIMIT_BYTES = 58 * 1024 * 1024
FFT_B = 128
SUBLANES = 8


def _cparams(sem):
    return pltpu.CompilerParams(dimension_semantics=sem, vmem_limit_bytes=V7X_VMEM_LIMIT_BYTES)


def _rms(x):
    return x * lax.rsqrt(jnp.mean(x * x, axis=-1, keepdims=True) + EPS)


def _modulated(x, shift_ref, scale_ref, g_ref, stream):
    shift = shift_ref[stream:stream + 1, :]
    scale = scale_ref[stream:stream + 1, :]
    return _rms(x) * g_ref[...] * (1.0 + scale) + shift


LANES = 128
N_STREAMS = 2


def _ada_kernel(c_ref, w_ref, b_ref, o_ref):
    o_ref[...] = jnp.zeros_like(o_ref)
    for k in range(N_STREAMS):
        c = c_ref[k]
        s = c * jax.nn.sigmoid(c)
        for j in range(0, w_ref.shape[1], LANES):
            cols = slice(j, j + LANES)
            o_ref[k:k + 1, cols] = jnp.sum(w_ref[:, cols] * s, axis=0, keepdims=True) + b_ref[:, cols]


def _ada_mods(c, c_ctx, ada_w, ada_b):
    depth, d, nm = ada_w.shape
    tn = d
    cc = jnp.broadcast_to(jnp.stack([c[0], c_ctx])[:, :, None], (N_STREAMS, d, LANES))
    return pl.pallas_call(
        _ada_kernel,
        grid=(depth, nm // tn),
        in_specs=[pl.BlockSpec((N_STREAMS, d, LANES), lambda l, j: (0, 0, 0)),
                  pl.BlockSpec((None, d, tn), lambda l, j: (l, 0, j)),
                  pl.BlockSpec((None, 1, tn), lambda l, j: (l, 0, j))],
        out_specs=pl.BlockSpec((None, SUBLANES, tn), lambda l, j: (l, 0, j)),
        out_shape=jax.ShapeDtypeStruct((depth, SUBLANES, nm), F32),
        compiler_params=_cparams(("arbitrary", "arbitrary")),
        name="ada_mods",
    )(cc, ada_w, ada_b[:, None, :])


def _mod_spec(l, k, which, d, tn=None):
    chunk = 3 * k + which
    if tn is None:
        return pl.BlockSpec((None, SUBLANES, d), lambda *g: (l, 0, chunk))
    return pl.BlockSpec((None, SUBLANES, tn), lambda *g: (l, 0, chunk * (d // tn) + g[-1]))


NORM_CHUNK = 256
LAT, CTX = 0, 1


def _rider_spec(rows, tcol, ncol_steps):
    return pl.BlockSpec((rows, tcol), lambda i, c: (0, jnp.where(i == 0, c, ncol_steps - 1)))


def _ffn_up_kernel(*refs, tm, with_ctx):
    if with_ctx:
        x_ref, xc_ref, shift_ref, scale_ref, g_ref, wg_ref, wu_ref, h_ref, hc_ref, u_scr, uc_scr = refs
    else:
        x_ref, shift_ref, scale_ref, g_ref, wg_ref, wu_ref, h_ref, u_scr = refs
    i, f = pl.program_id(0), pl.program_id(1)
    wg = wg_ref[...].astype(BF16)
    wu = wu_ref[...].astype(BF16)

    def swiglu_rows(u):
        g = jnp.dot(u, wg, preferred_element_type=F32)
        up = jnp.dot(u, wu, preferred_element_type=F32)
        return (g * jax.nn.sigmoid(g) * up).astype(BF16)

    @pl.when(f == 0)
    def _():
        for r in range(0, tm, NORM_CHUNK):
            rs = slice(r, min(r + NORM_CHUNK, tm))
            u = _modulated(x_ref[rs, :], shift_ref, scale_ref, g_ref, LAT).astype(BF16)
            u_scr[rs, :] = u
            h_ref[rs, :] = swiglu_rows(u)

    @pl.when(f > 0)
    def _():
        h_ref[...] = swiglu_rows(u_scr[...])

    if with_ctx:
        @pl.when(i == 0)
        def _():
            @pl.when(f == 0)
            def _():
                uc_scr[...] = _modulated(xc_ref[...], shift_ref, scale_ref, g_ref, CTX).astype(BF16)

            hc_ref[...] = swiglu_rows(uc_scr[...])


def _ffn_down_kernel(*refs, with_ctx):
    if with_ctx:
        h_ref, hc_ref, wd_ref, x_ref, xc_ref, gate_ref, o_ref, oc_ref = refs
    else:
        h_ref, wd_ref, x_ref, gate_ref, o_ref = refs
    wd = wd_ref[...].astype(BF16)
    o_ref[...] = x_ref[...] + (0.5 * gate_ref[LAT:LAT + 1, :]) * jnp.dot(h_ref[...], wd,
                                                                          preferred_element_type=F32)
    if with_ctx:
        @pl.when(pl.program_id(0) == 0)
        def _():
            oc_ref[...] = xc_ref[...] + (0.5 * gate_ref[CTX:CTX + 1, :]) * jnp.dot(
                hc_ref[...], wd, preferred_element_type=F32)


def _ffn(x, xc, mods, k, g, wg, wu, wd, l, j):
    rows, d = x.shape
    dff = wg.shape[-1]
    tm, tf, tn = 1024, 512, 256
    nf, nn = dff // tf, d // tn
    with_ctx = xc is not None
    rc = xc.shape[0] if with_ctx else 0
    ctx_in = [xc] if with_ctx else []
    res = pl.pallas_call(
        functools.partial(_ffn_up_kernel, tm=tm, with_ctx=with_ctx),
        grid=(rows // tm, nf),
        in_specs=[pl.BlockSpec((tm, d), lambda i, f: (i, 0))]
                 + ([pl.BlockSpec((rc, d), lambda i, f: (0, 0))] if with_ctx else [])
                 + [_mod_spec(l, k, 0, d), _mod_spec(l, k, 1, d),
                    pl.BlockSpec((1, d), lambda i, f: (0, 0)),
                    pl.BlockSpec((None, None, d, tf), lambda i, f: (l, j, 0, f)),
                    pl.BlockSpec((None, None, d, tf), lambda i, f: (l, j, 0, f))],
        out_specs=[pl.BlockSpec((tm, tf), lambda i, f: (i, f))]
                  + ([_rider_spec(rc, tf, nf)] if with_ctx else []),
        out_shape=[jax.ShapeDtypeStruct((rows, dff), BF16)]
                  + ([jax.ShapeDtypeStruct((rc, dff), BF16)] if with_ctx else []),
        scratch_shapes=[pltpu.VMEM((tm, d), BF16)] + ([pltpu.VMEM((rc, d), BF16)] if with_ctx else []),
        compiler_params=_cparams(("arbitrary", "arbitrary")),
        name="ffn_up",
    )(x, *ctx_in, mods, mods, g, wg, wu)
    h, hc = (res[0], res[1]) if with_ctx else (res[0], None)
    res = pl.pallas_call(
        functools.partial(_ffn_down_kernel, with_ctx=with_ctx),
        grid=(rows // tm, nn),
        in_specs=[pl.BlockSpec((tm, dff), lambda i, n: (i, 0))]
                 + ([pl.BlockSpec((rc, dff), lambda i, n: (0, 0))] if with_ctx else [])
                 + [pl.BlockSpec((None, None, dff, tn), lambda i, n: (l, j, 0, n)),
                    pl.BlockSpec((tm, tn), lambda i, n: (i, n))]
                 + ([_rider_spec(rc, tn, nn)] if with_ctx else [])
                 + [_mod_spec(l, k, 2, d, tn)],
        out_specs=[pl.BlockSpec((tm, tn), lambda i, n: (i, n))]
                  + ([_rider_spec(rc, tn, nn)] if with_ctx else []),
        out_shape=[jax.ShapeDtypeStruct((rows, d), F32)]
                  + ([jax.ShapeDtypeStruct((rc, d), F32)] if with_ctx else []),
        compiler_params=_cparams(("arbitrary", "arbitrary")),
        name="ffn_down",
    )(h, *([hc] if with_ctx else []), wd, x, *ctx_in, mods)
    return (res[0], res[1]) if with_ctx else (res[0], None)


def _win_kernel(x_ref, xc_ref, shift_ref, scale_ref, g_ref, w_ref, p_ref, pc_ref, u_scr, uc_scr, *, tm):
    i, n = pl.program_id(0), pl.program_id(1)

    @pl.when(n == 0)
    def _():
        for r in range(0, tm, NORM_CHUNK):
            rs = slice(r, min(r + NORM_CHUNK, tm))
            u = _modulated(x_ref[rs, :], shift_ref, scale_ref, g_ref, LAT).astype(BF16)
            u_scr[rs, :] = u
            p_ref[rs, :] = jnp.dot(u, w_ref[...], preferred_element_type=F32)

    @pl.when(n > 0)
    def _():
        p_ref[...] = jnp.dot(u_scr[...], w_ref[...], preferred_element_type=F32)

    @pl.when(i == 0)
    def _():
        @pl.when(n == 0)
        def _():
            uc_scr[...] = _modulated(xc_ref[...], shift_ref, scale_ref, g_ref, CTX).astype(BF16)

        pc_ref[...] = jnp.dot(uc_scr[...], w_ref[...], preferred_element_type=F32)


def _win(x, xc, mods, l, g, w_in_bf16):
    rows, d = x.shape
    rc = xc.shape[0]
    ncols = w_in_bf16.shape[-1]
    tm, tn = 1024, 1024
    nn = ncols // tn
    return pl.pallas_call(
        functools.partial(_win_kernel, tm=tm),
        grid=(rows // tm, nn),
        in_specs=[pl.BlockSpec((tm, d), lambda i, n: (i, 0)),
                  pl.BlockSpec((rc, d), lambda i, n: (0, 0)),
                  _mod_spec(l, 1, 0, d), _mod_spec(l, 1, 1, d),
                  pl.BlockSpec((1, d), lambda i, n: (0, 0)),
                  pl.BlockSpec((None, d, tn), lambda i, n: (l, 0, n))],
        out_specs=[pl.BlockSpec((tm, tn), lambda i, n: (i, n)), _rider_spec(rc, tn, nn)],
        out_shape=[jax.ShapeDtypeStruct((rows, ncols), F32), jax.ShapeDtypeStruct((rc, ncols), F32)],
        scratch_shapes=[pltpu.VMEM((tm, d), BF16), pltpu.VMEM((rc, d), BF16)],
        compiler_params=_cparams(("arbitrary", "arbitrary")),
        name="mixer_in_proj",
    )(x, xc, mods, mods, g, w_in_bf16)


def _to_col_major(x):
    rows, d = x.shape
    return x.reshape(rows // GRID_W, GRID_W, d).transpose(1, 0, 2).reshape(rows, d)


def _from_col_major(x):
    rows, d = x.shape
    return x.reshape(GRID_W, rows // GRID_W, d).transpose(1, 0, 2).reshape(rows, d)


def _lru_kernel(cv_ref, wax_ref, ba_ref, bx_ref, lam_ref, h0_ref, *rest, t_blk, heads, reverse, add_prev):
    if add_prev:
        prev_ref, o_ref, hf_ref, a_scr, b_scr, carry = rest
    else:
        o_ref, hf_ref, a_scr, b_scr, carry = rest
    c = cv_ref.shape[1]
    hd = c // heads

    @pl.when(pl.program_id(0) == 0)
    def _():
        carry[...] = h0_ref[...]

    cv = cv_ref[...]
    z = -lam_ref[...]
    c_sp = LRU_C * (jnp.maximum(z, 0.0) + jnp.log(1.0 + jnp.exp(-jnp.abs(z))))
    cvb = cv.astype(BF16)
    for h in range(heads):
        sl = slice(h * hd, (h + 1) * hd)
        gx = jnp.dot(cvb[:, sl], wax_ref[h], preferred_element_type=F32)
        r = 0.5 + 0.5 * jnp.tanh(0.5 * (gx[:, :hd] + ba_ref[:, sl]))
        ig = 0.5 + 0.5 * jnp.tanh(0.5 * (gx[:, hd:] + bx_ref[:, sl]))
        a = jnp.exp(-c_sp[:, sl] * r)
        a_scr[:, sl] = a
        b_scr[:, sl] = jnp.sqrt(1.0 - a * a) * (ig * cv[:, sl])

    row = lax.broadcasted_iota(jnp.int32, (SUBLANES, c), 0)
    ngrp = t_blk // SUBLANES

    def group(gi, hc):
        g = (ngrp - 1 - gi) if reverse else gi
        r0 = pl.multiple_of(g * SUBLANES, SUBLANES)
        av = a_scr[pl.ds(r0, SUBLANES), :]
        bv = b_scr[pl.ds(r0, SUBLANES), :]
        for s in (1, 2, 4):
            sh = (SUBLANES - s) if reverse else s
            m = (row < SUBLANES - s) if reverse else (row >= s)
            a_s = pltpu.roll(av, sh, axis=0)
            b_s = pltpu.roll(bv, sh, axis=0)
            bv = jnp.where(m, av * b_s + bv, bv)
            av = jnp.where(m, av * a_s, av)
        hv = av * hc + bv
        if add_prev:
            o_ref[pl.ds(r0, SUBLANES), :] = hv + prev_ref[pl.ds(r0, SUBLANES), :]
        else:
            o_ref[pl.ds(r0, SUBLANES), :] = hv
        return hv[0:1, :] if reverse else hv[SUBLANES - 1:SUBLANES, :]

    hc = lax.fori_loop(0, ngrp, group, carry[...])
    carry[...] = hc
    hf_ref[...] = hc


def _lru_scan(cv, wax, ba, bx, lam, h0, prev, reverse):
    seq, c = cv.shape
    heads = wax.shape[0]
    t_blk = min(seq, 1024)
    nblk = seq // t_blk

    def bi(j):
        return (nblk - 1 - j) if reverse else j

    row = pl.BlockSpec((1, c), lambda j: (0, 0))
    in_specs = [pl.BlockSpec((t_blk, c), lambda j: (bi(j), 0)),
                pl.BlockSpec(wax.shape, lambda j: (0, 0, 0)), row, row, row, row]
    args = [cv, wax, ba, bx, lam, h0]
    if prev is not None:
        in_specs.append(pl.BlockSpec((t_blk, c), lambda j: (bi(j), 0)))
        args.append(prev)
    return pl.pallas_call(
        functools.partial(_lru_kernel, t_blk=t_blk, heads=heads, reverse=reverse, add_prev=prev is not None),
        grid=(nblk,),
        in_specs=in_specs,
        out_specs=[pl.BlockSpec((t_blk, c), lambda j: (bi(j), 0)), row],
        out_shape=[jax.ShapeDtypeStruct((seq, c), F32), jax.ShapeDtypeStruct((1, c), F32)],
        scratch_shapes=[pltpu.VMEM((t_blk, c), F32), pltpu.VMEM((t_blk, c), F32), pltpu.VMEM((1, c), F32)],
        compiler_params=_cparams(("arbitrary",)),
        name="rglru_scan",
    )(*args)


def _short_conv_kernel(*refs, t_blk, nblk, c, with_hyena):
    n_in = 4 if with_hyena else 1
    ins = [refs[3 * k:3 * k + 3] for k in range(n_in)]
    lw_ref, lb_ref, hw_ref, hb_ref = refs[3 * n_in:3 * n_in + 4]
    outs = refs[3 * n_in + 4:-1]
    xs = refs[-1]
    blk = pl.program_id(0)
    zero8 = jnp.zeros((SUBLANES, c), F32)

    def conv(src, w_ref, b_ref, part):
        cur, prv, nxt = src
        xs[0:SUBLANES, :] = jnp.where(blk == 0, zero8, prv[...])
        xs[SUBLANES:SUBLANES + t_blk, :] = cur[...]
        xs[SUBLANES + t_blk:2 * SUBLANES + t_blk, :] = jnp.where(blk == nblk - 1, zero8, nxt[...])
        taps = w_ref.shape[0]
        left = (taps - 1) // 2
        sl = slice(part * c, (part + 1) * c)
        out = b_ref[:, sl]
        for k in range(taps):
            r0 = SUBLANES - left + k
            out = out + xs[r0:r0 + t_blk, :] * w_ref[k:k + 1, sl]
        return out

    outs[0][...] = conv(ins[0], lw_ref, lb_ref, 0)
    if with_hyena:
        outs[2][...] = conv(ins[1], hw_ref, hb_ref, 0)
        x1c = conv(ins[2], hw_ref, hb_ref, 1)
        outs[1][...] = conv(ins[3], hw_ref, hb_ref, 2) * x1c


def _short_convs(p, lru_w, lru_b, hy_w, hy_b, with_hyena):
    seq = p.shape[0]
    c = lru_w.shape[1]
    t_blk = min(seq, 512)
    nblk = seq // t_blk
    per = t_blk // SUBLANES
    nb8 = seq // SUBLANES
    in_specs, args = [], []
    for col in ([0, 2, 3, 4] if with_hyena else [0]):
        in_specs += [pl.BlockSpec((t_blk, c), lambda j, col=col: (j, col)),
                     pl.BlockSpec((SUBLANES, c), lambda j, col=col: (jnp.maximum(j * per - 1, 0), col)),
                     pl.BlockSpec((SUBLANES, c), lambda j, col=col: (jnp.minimum((j + 1) * per, nb8 - 1), col))]
        args += [p, p, p]
    full = lambda a: pl.BlockSpec(a.shape, lambda j: (0, 0))
    in_specs += [full(lru_w), full(lru_b), full(hy_w), full(hy_b)]
    args += [lru_w, lru_b, hy_w, hy_b]
    n_out = 3 if with_hyena else 1
    return pl.pallas_call(
        functools.partial(_short_conv_kernel, t_blk=t_blk, nblk=nblk, c=c, with_hyena=with_hyena),
        grid=(nblk,),
        in_specs=in_specs,
        out_specs=[pl.BlockSpec((t_blk, c), lambda j: (j, 0))] * n_out,
        out_shape=[jax.ShapeDtypeStruct((seq, c), F32)] * n_out,
        scratch_shapes=[pltpu.VMEM((t_blk + 2 * SUBLANES, c), F32)],
        compiler_params=_cparams(("arbitrary",)),
        name="mixer_short_convs",
    )(*args)


FILT_SLOT = 64


def _filt_kernel(w1_ref, b1_ref, w2_ref, b2_ref, w3_ref, b3_ref, w4a_ref, w4b_ref, b4_ref, fr_ref,
                 k_ref, nrm_ref, ec_scr, es_scr, *, seq, t_blk, c):
    i = pl.program_id(0)
    hi = lax.Precision.HIGHEST
    lanes = 2 * FILT_SLOT
    th = t_blk // 2
    lane = lax.broadcasted_iota(jnp.int32, (1, lanes), 1)
    slot = lane & (FILT_SLOT - 1)
    pos_a = i * t_blk + lax.broadcasted_iota(jnp.int32, (th, 1), 0)
    pos = jnp.where(lane < FILT_SLOT, pos_a, pos_a + th)
    tf = jnp.where(pos < seq, pos, 2 * seq - pos).astype(F32)
    t = tf * (1.0 / (seq - 1))
    is_cos = (slot >= 1) & (slot <= HY_BANDS)
    is_sin = (slot > HY_BANDS) & (slot <= 2 * HY_BANDS)
    band = jnp.where(is_cos, slot - 1, slot - 1 - HY_BANDS).astype(F32)
    alpha = (2.0 * math.pi / seq) * (1e-4 + band * ((HY_BANDS - 1 - 1e-4) / (HY_BANDS - 1)))

    @pl.when(i == 0)
    def _():
        off = (pos - i * t_blk).astype(F32)
        ec_scr[...] = jnp.sin(alpha * off + 0.5 * math.pi)
        es_scr[...] = jnp.sin(alpha * off)

    forward = i * t_blk < seq
    base = jnp.where(forward, i * t_blk, 2 * seq - i * t_blk).astype(F32)
    sigma = jnp.where(forward, 1.0, -1.0)
    cb = jnp.sin(alpha * base + 0.5 * math.pi)
    sb = jnp.sin(alpha * base)
    a_row = jnp.where(is_cos, cb, jnp.where(is_sin, -sb, 0.0))
    b_row = -sigma * jnp.where(is_cos, sb, jnp.where(is_sin, cb, 0.0))
    z = jnp.where(slot == 0, t, a_row * ec_scr[...] + b_row * es_scr[...])
    fr = fr_ref[...]
    h = jnp.sin(fr * (jnp.dot(z, w1_ref[...], precision=hi, preferred_element_type=F32) + b1_ref[...]))
    h = jnp.sin(fr * (jnp.dot(h, w2_ref[...], precision=hi, preferred_element_type=F32) + b2_ref[...]))
    h = jnp.sin(fr * (jnp.dot(h, w3_ref[...], precision=hi, preferred_element_type=F32) + b3_ref[...]))
    ch = lax.broadcasted_iota(jnp.int32, (1, c), 1).astype(F32)
    delta = jnp.abs(HY_MIN_DECAY + ch * ((HY_MAX_DECAY - HY_MIN_DECAY) / (c - 1)))
    part = jnp.zeros((1, c), F32)
    h_hi = h.astype(BF16)
    h_lo = (h - h_hi.astype(F32)).astype(BF16)
    for half, w4_ref in enumerate((w4a_ref, w4b_ref)):
        pos_h = pos_a + half * th
        t_h = jnp.where(pos_h < seq, pos_h, 2 * seq - pos_h).astype(F32) * (1.0 / (seq - 1))
        w_hi, w_lo = w4_ref[0], w4_ref[1]
        k = (jnp.dot(h_hi, w_hi, preferred_element_type=F32) + jnp.dot(h_lo, w_hi, preferred_element_type=F32)
             + jnp.dot(h_hi, w_lo, preferred_element_type=F32)) + b4_ref[...]
        k = jnp.where(pos_h != seq, k * jnp.exp(-t_h * delta), 0.0)
        k_ref[half * th:(half + 1) * th, :] = k
        part = part + jnp.sum(jnp.abs(k), axis=0, keepdims=True)

    @pl.when(i == 0)
    def _():
        nrm_ref[...] = part

    @pl.when(i > 0)
    def _():
        nrm_ref[...] = nrm_ref[...] + part


def _slots(a, row_used):
    r, cdim = a.shape
    blk = jnp.pad(a, ((0, (FILT_SLOT if row_used else 1) - r), (0, FILT_SLOT - cdim)))
    if not row_used:
        return jnp.concatenate([blk, blk], axis=1)
    z = jnp.zeros_like(blk)
    return jnp.concatenate([jnp.concatenate([blk, z], axis=1), jnp.concatenate([z, blk], axis=1)], axis=0)


def _hyena_filter(seq, w1, b1, w2, b2, w3, b3, w4, b4, freq):
    c = w4.shape[1] // 2
    od = w2.shape[0]
    assert od <= FILT_SLOT and w1.shape[0] <= FILT_SLOT
    lanes = 2 * FILT_SLOT
    t_blk = min(seq, 1024)
    half_blocks = seq // t_blk
    full = lambda shape: pl.BlockSpec(shape, lambda i: (0,) * len(shape))
    sq, row = (lanes, lanes), (1, lanes)
    w4p = jnp.pad(w4, ((0, FILT_SLOT - od), (0, 0)))
    w4_hi = w4p.astype(BF16)
    w4_lo = (w4p - w4_hi.astype(F32)).astype(BF16)
    w4s = jnp.stack([w4_hi, w4_lo])
    w4a = jnp.concatenate([w4s, jnp.zeros_like(w4s)], axis=1)
    w4b = jnp.concatenate([jnp.zeros_like(w4s), w4s], axis=1)
    w4_spec = pl.BlockSpec((2, lanes, c), lambda i: (0, 0, i // half_blocks))
    return pl.pallas_call(
        functools.partial(_filt_kernel, seq=seq, t_blk=t_blk, c=c),
        grid=(2 * seq // t_blk,),
        in_specs=[full(sq), full(row), full(sq), full(row), full(sq), full(row), w4_spec, w4_spec,
                  pl.BlockSpec((1, c), lambda i: (0, i // half_blocks)),
                  full(row)],
        out_specs=[pl.BlockSpec((t_blk, c), lambda i: (i, 0)), pl.BlockSpec((1, c), lambda i: (0, 0))],
        out_shape=[jax.ShapeDtypeStruct((2 * seq, c), F32), jax.ShapeDtypeStruct((1, c), F32)],
        scratch_shapes=[pltpu.VMEM((t_blk // 2, lanes), F32), pltpu.VMEM((t_blk // 2, lanes), F32)],
        compiler_params=_cparams(("arbitrary",)),
        name="hyena_filter",
    )(_slots(w1, True), _slots(b1[None, :], False), _slots(w2, True), _slots(b2[None, :], False),
      _slots(w3, True), _slots(b3[None, :], False), w4a, w4b, b4[None, :], _slots(freq[None, :], False))


def _bitrev(p, bits):
    r = 0
    for k in range(bits):
        r = (r << 1) | ((p >> k) & 1)
    return r


def _real_block(z):
    return np.block([[z.real, -z.imag], [z.imag, z.real]])


def _fft_split(s2):
    n_lo = max(min(8, s2 // 2), 1)
    return s2 // n_lo, n_lo


@functools.lru_cache(maxsize=None)
def _fft_tables(s):
    s2 = s // 2
    n = s * FFT_B
    n_hi, n_lo = _fft_split(s2)
    bits_hi, bits_lo = n_hi.bit_length() - 1, n_lo.bit_length() - 1
    b = np.arange(FFT_B)
    dft = np.exp(-2j * np.pi * np.outer(b, b) / FFT_B)
    e_lo = [2 * n_hi * _bitrev(p, bits_lo) for p in range(n_lo)]
    g = [dft * np.exp(-2j * np.pi * b * e / n)[None, :] for e in e_lo]
    chunks = np.arange(FFT_B).reshape(-1, FFT_ROWS)
    order = np.concatenate([np.concatenate([c, c + FFT_B]) for c in chunks])
    lhs_f = np.stack([_real_block(x)[order] for x in g])
    lhs_i = np.stack([_real_block(np.conj(x).T)[order] for x in g])
    e_hi = np.array([[2 * _bitrev(p, bits_hi) + hf for p in range(n_hi)] for hf in range(2)])
    ang = 2.0 * np.pi * e_hi[:, :, None, None] * b[None, None, :, None] / n * np.ones((1, 1, 1, FFT_B))
    lvl = 2.0 * np.pi * np.arange(s2) / s2
    first = 2.0 * np.pi * np.arange(s2) / s
    f32 = lambda x: np.asarray(x, np.float32)
    return dict(lhs_f=f32(lhs_f), lhs_i=f32(lhs_i), col_c=f32(np.cos(ang)), col_s=f32(np.sin(ang)),
                lvl_c=f32(np.cos(lvl)), lvl_s=f32(np.sin(lvl)), first_c=f32(np.cos(first)),
                first_s=f32(np.sin(first)))


FFT_ROWS = 16


def _cmul(xr, xi, c, s):
    return xr * c + xi * s, xi * c - xr * s


def _cmul_conj(xr, xi, c, s):
    return xr * c - xi * s, xi * c + xr * s


def _pass_plan(s2):
    levels = s2.bit_length() - 1
    plan, h = [], s2 // 2
    if levels % 2 == 1:
        plan.append((h, 2))
        h //= 2
    while h >= 1:
        plan.append((h, 4))
        h //= 4
    return plan


def _butterfly_pass(load, store, lc_ref, ls_ref, s2, h, radix, inverse, real_input=False):
    h2 = h // 2 if radix == 4 else h
    trivial = h2 == 1
    shift = h2.bit_length() - 1
    assert not real_input or (radix == 4 and not inverse and not trivial)

    def group(idx, carry):
        g = idx >> shift
        i = idx & (h2 - 1)
        base = g * 2 * h + i
        blocks = [base + k * h2 for k in range(radix)]
        c1 = s1 = c2 = s2_ = c3 = s3 = None
        if not trivial:
            t1 = i * (s2 // (2 * h))
            c1, s1 = lc_ref[t1], ls_ref[t1]
            if radix == 4:
                c2, s2_, c3, s3 = lc_ref[2 * t1], ls_ref[2 * t1], lc_ref[3 * t1], ls_ref[3 * t1]
        fwd_mul = (lambda xr, xi, c, s: (xr, xi)) if trivial else _cmul
        inv_mul = (lambda xr, xi, c, s: (xr, xi)) if trivial else _cmul_conj
        for r in range(0, FFT_B, FFT_ROWS):
            x = [load(bk, r) for bk in blocks]
            if radix == 2:
                (ar, ai), (br, bi) = x
                if inverse:
                    br, bi = inv_mul(br, bi, c1, s1)
                    out = [(ar + br, ai + bi), (ar - br, ai - bi)]
                else:
                    out = [(ar + br, ai + bi), fwd_mul(ar - br, ai - bi, c1, s1)]
            elif real_input:
                x0, x1, x2, x3 = (v[0] for v in x)
                t0, t1, t2, t3 = x0 + x2, x1 + x3, x0 - x2, x1 - x3
                dif = t0 - t1
                out = [(t0 + t1, jnp.zeros_like(t0)), (dif * c2, -(dif * s2_)),
                       (t2 * c1 - t3 * s1, -(t3 * c1 + t2 * s1)), (t2 * c3 + t3 * s3, t3 * c3 - t2 * s3)]
            elif inverse:
                (x0r, x0i), (x1r, x1i), (x2r, x2i), (x3r, x3i) = x
                x1r, x1i = inv_mul(x1r, x1i, c2, s2_)
                x2r, x2i = inv_mul(x2r, x2i, c1, s1)
                x3r, x3i = inv_mul(x3r, x3i, c3, s3)
                t0r, t0i, t1r, t1i = x0r + x1r, x0i + x1i, x0r - x1r, x0i - x1i
                t2r, t2i, t3r, t3i = x2r + x3r, x2i + x3i, x2r - x3r, x2i - x3i
                out = [(t0r + t2r, t0i + t2i), (t1r - t3i, t1i + t3r),
                       (t0r - t2r, t0i - t2i), (t1r + t3i, t1i - t3r)]
            else:
                (x0r, x0i), (x1r, x1i), (x2r, x2i), (x3r, x3i) = x
                t0r, t0i, t1r, t1i = x0r + x2r, x0i + x2i, x1r + x3r, x1i + x3i
                t2r, t2i, t3r, t3i = x0r - x2r, x0i - x2i, x1r - x3r, x1i - x3i
                out = [(t0r + t1r, t0i + t1i), fwd_mul(t0r - t1r, t0i - t1i, c2, s2_),
                       fwd_mul(t2r + t3i, t2i - t3r, c1, s1), fwd_mul(t2r - t3i, t2i + t3r, c3, s3)]
            for bk, (o_r, o_i) in zip(blocks, out):
                store(bk, r, o_r, o_i)
        return carry

    lax.fori_loop(0, s2 // radix, group, 0)


def _work_access(wr, wi):
    def rows(bk, r):
        return pl.ds(pl.multiple_of(bk * FFT_B + r, FFT_ROWS), FFT_ROWS)

    def load(bk, r):
        return wr[rows(bk, r), :], wi[rows(bk, r), :]

    def store(bk, r, re, im):
        wr[rows(bk, r), :] = re
        wi[rows(bk, r), :] = im

    return load, store


def _forward_passes(hf, even_load, odd_load, pack_store, work, lc_ref, ls_ref, s2):
    plan = _pass_plan(s2)
    (h, radix), rest = plan[0], plan[1:]
    store0 = work[1] if rest else pack_store
    if radix == 4 and h > 2:
        @pl.when(hf == 0)
        def _():
            _butterfly_pass(lambda bk, r: (even_load(bk, r), None), store0, lc_ref, ls_ref, s2, h, radix,
                            False, real_input=True)

        @pl.when(hf == 1)
        def _():
            _butterfly_pass(odd_load, store0, lc_ref, ls_ref, s2, h, radix, False)
    else:
        def load(bk, r):
            ev = even_load(bk, r)
            od_r, od_i = odd_load(bk, r)
            return jnp.where(hf == 0, ev, od_r), jnp.where(hf == 0, 0.0, od_i)
        _butterfly_pass(load, store0, lc_ref, ls_ref, s2, h, radix, False)
    if rest:
        _run_passes(rest, work[0], pack_store, work, lc_ref, ls_ref, s2, inverse=False)


def _run_passes(plan, first_load, last_store, work, lc_ref, ls_ref, s2, inverse):
    w_load, w_store = work
    for k, (h, radix) in enumerate(plan):
        load = first_load if k == 0 else w_load
        store = last_store if k == len(plan) - 1 else w_store
        _butterfly_pass(load, store, lc_ref, ls_ref, s2, h, radix, inverse)


MID_UNROLL = 32


def _pair_blocks(pp, s2):
    n_hi, n_lo = _fft_split(s2)
    half = n_hi // 2
    p_lo = pp >> (half.bit_length() - 1)
    j = pp & (half - 1)
    return p_lo, [(2 * j + q, pl.multiple_of(((2 * j + q) * n_lo + p_lo) * FFT_B, FFT_B)) for q in range(2)]


def _for_pairs(s2, body):
    def step(pp, carry):
        body(pp)
        return carry
    lax.fori_loop(0, s2 // 2, step, 0, unroll=min(MID_UNROLL, s2 // 2))


def _pack_store(panel, cc_ref, cs_ref, s2):
    n_hi, n_lo = _fft_split(s2)
    lo_bits = n_lo.bit_length() - 1

    def store(bk, r, re, im):
        p_hi = bk >> lo_bits
        p_lo = bk & (n_lo - 1)
        pp = p_lo * (n_hi // 2) + (p_hi >> 1)
        q = p_hi & 1
        tr, ti = _cmul(re, im, cc_ref[p_hi, r:r + FFT_ROWS, :], cs_ref[p_hi, r:r + FFT_ROWS, :])
        panel[pp, q, r:r + FFT_ROWS, :] = tr.astype(BF16)
        panel[pp, q, FFT_B + r:FFT_B + r + FFT_ROWS, :] = ti.astype(BF16)

    return store


def _dft_pair(m_ref, panel, pp, p_lo):
    blk = FFT_B
    rhs = jnp.concatenate([panel[pp, 0], panel[pp, 1]], axis=1)
    out = jnp.dot(m_ref[p_lo], rhs, preferred_element_type=F32)
    for r in range(0, blk, FFT_ROWS):
        for q in range(2):
            yield (q, r, out[2 * r:2 * r + FFT_ROWS, q * blk:(q + 1) * blk],
                   out[2 * r + FFT_ROWS:2 * r + 2 * FFT_ROWS, q * blk:(q + 1) * blk])


def _spec_kernel(lo_ref, hi_ref, nrm_ref, fc_ref, fs_ref, lc_ref, ls_ref, cc_ref, cs_ref, mf_ref,
                 kr_ref, ki_ref, panel, *, s2):
    blk = FFT_B
    hf = pl.program_id(1)
    n = 2 * s2 * blk

    def rows_of(bk, r):
        return pl.ds(pl.multiple_of(bk * blk + r, FFT_ROWS), FFT_ROWS)

    def even_load(bk, r):
        return lo_ref[rows_of(bk, r), :] + hi_ref[rows_of(bk, r), :]

    def odd_load(bk, r):
        d = lo_ref[rows_of(bk, r), :] - hi_ref[rows_of(bk, r), :]
        return d * fc_ref[bk], -d * fs_ref[bk]

    _forward_passes(hf, even_load, odd_load, _pack_store(panel, cc_ref, cs_ref, s2),
                    _work_access(kr_ref, ki_ref), lc_ref, ls_ref, s2)
    scale = 1.0 / (nrm_ref[...] * n)

    def dft(pp):
        p_lo, blocks = _pair_blocks(pp, s2)
        for q, r, yr, yi in _dft_pair(mf_ref, panel, pp, p_lo):
            rows = pl.ds(blocks[q][1] + r, FFT_ROWS)
            kr_ref[rows, :] = yr * scale
            ki_ref[rows, :] = yi * scale

    _for_pairs(s2, dft)


def _conv_kernel(u_ref, bias_ref, kr_ref, ki_ref, fc_ref, fs_ref, lc_ref, ls_ref, cc_ref, cs_ref,
                 mf_ref, mi_ref, y_ref, wr, wi, panel, *, s2):
    blk = FFT_B
    hf = pl.program_id(1)
    plan = _pass_plan(s2)
    work = _work_access(wr, wi)

    def even_load(bk, r):
        return u_ref[pl.ds(pl.multiple_of(bk * blk + r, FFT_ROWS), FFT_ROWS), :]

    def odd_load(bk, r):
        ub = even_load(bk, r)
        return ub * fc_ref[bk], -ub * fs_ref[bk]

    _forward_passes(hf, even_load, odd_load, _pack_store(panel, cc_ref, cs_ref, s2), work, lc_ref, ls_ref, s2)

    def spectrum_product(pp):
        p_lo, blocks = _pair_blocks(pp, s2)
        for q, r, yr, yi in _dft_pair(mf_ref, panel, pp, p_lo):
            rows = pl.ds(blocks[q][1] + r, FFT_ROWS)
            kr, ki = kr_ref[rows, :], ki_ref[rows, :]
            panel[pp, q, r:r + FFT_ROWS, :] = (yr * kr - yi * ki).astype(BF16)
            panel[pp, q, blk + r:blk + r + FFT_ROWS, :] = (yr * ki + yi * kr).astype(BF16)

    _for_pairs(s2, spectrum_product)

    def inverse_dft(pp):
        p_lo, blocks = _pair_blocks(pp, s2)
        for q, r, er, ei in _dft_pair(mi_ref, panel, pp, p_lo):
            p_hi, r0 = blocks[q]
            rows = pl.ds(r0 + r, FFT_ROWS)
            wr[rows, :], wi[rows, :] = _cmul_conj(er, ei, cc_ref[p_hi, r:r + FFT_ROWS, :],
                                                  cs_ref[p_hi, r:r + FFT_ROWS, :])

    _for_pairs(s2, inverse_dft)

    def rows_of(bk, r):
        return pl.ds(pl.multiple_of(bk * blk + r, FFT_ROWS), FFT_ROWS)

    def store_even(bk, r, er, ei):
        y_ref[rows_of(bk, r), :] = er

    def store_odd(bk, r, er, ei):
        rows = rows_of(bk, r)
        val = er * fc_ref[bk] - ei * fs_ref[bk]
        y_ref[rows, :] = y_ref[rows, :] + u_ref[rows, :] * bias_ref[...] + val

    inv_plan = plan[::-1]
    if len(inv_plan) > 1:
        _run_passes(inv_plan[:-1], work[0], work[1], work, lc_ref, ls_ref, s2, inverse=True)
    h_last, radix_last = inv_plan[-1]
    for half, store in ((0, store_even), (1, store_odd)):
        @pl.when(hf == half)
        def _(store=store):
            _butterfly_pass(work[0], store, lc_ref, ls_ref, s2, h_last, radix_last, True)


def _smem_spec():
    return pl.BlockSpec(memory_space=pltpu.SMEM)


def _fft_table_specs(s2):
    n_hi, n_lo = _fft_split(s2)
    col = pl.BlockSpec((None, n_hi, FFT_B, FFT_B), lambda ct, hf: (hf, 0, 0, 0))
    lhs = pl.BlockSpec((n_lo, 2 * FFT_B, 2 * FFT_B), lambda ct, hf: (0, 0, 0))
    return [_smem_spec(), _smem_spec(), _smem_spec(), _smem_spec(), col, col], lhs


def _hyena_spectrum(kfull, nrm):
    n, c = kfull.shape
    seq = n // 2
    s2 = seq // FFT_B
    tb = _fft_tables(2 * s2)
    lanes = 128
    tab_specs, lhs_spec = _fft_table_specs(s2)
    return pl.pallas_call(
        functools.partial(_spec_kernel, s2=s2),
        grid=(c // lanes, 2),
        in_specs=[pl.BlockSpec((seq, lanes), lambda ct, hf: (0, ct)),
                  pl.BlockSpec((seq, lanes), lambda ct, hf: (1, ct)),
                  pl.BlockSpec((1, lanes), lambda ct, hf: (0, ct))] + tab_specs + [lhs_spec],
        out_specs=[pl.BlockSpec((None, seq, lanes), lambda ct, hf: (hf, 0, ct)),
                   pl.BlockSpec((None, seq, lanes), lambda ct, hf: (hf, 0, ct))],
        out_shape=[jax.ShapeDtypeStruct((2, seq, c), F32), jax.ShapeDtypeStruct((2, seq, c), F32)],
        scratch_shapes=[pltpu.VMEM((s2 // 2, 2, 2 * FFT_B, FFT_B), BF16)],
        compiler_params=_cparams(("arbitrary", "arbitrary")),
        name="hyena_spectrum",
    )(kfull, kfull, nrm, tb["first_c"], tb["first_s"], tb["lvl_c"], tb["lvl_s"],
      tb["col_c"], tb["col_s"], jnp.asarray(tb["lhs_f"], BF16))


def _hyena_conv(u, bias, kr, ki):
    seq, c = u.shape
    s2 = seq // FFT_B
    tb = _fft_tables(2 * s2)
    lanes = 128
    tab_specs, lhs_spec = _fft_table_specs(s2)
    return pl.pallas_call(
        functools.partial(_conv_kernel, s2=s2),
        grid=(c // lanes, 2),
        in_specs=[pl.BlockSpec((seq, lanes), lambda ct, hf: (0, ct)),
                  pl.BlockSpec((1, lanes), lambda ct, hf: (0, ct)),
                  pl.BlockSpec((None, seq, lanes), lambda ct, hf: (hf, 0, ct)),
                  pl.BlockSpec((None, seq, lanes), lambda ct, hf: (hf, 0, ct))]
                 + tab_specs + [lhs_spec, lhs_spec],
        out_specs=pl.BlockSpec((seq, lanes), lambda ct, hf: (0, ct)),
        out_shape=jax.ShapeDtypeStruct((seq, c), F32),
        scratch_shapes=[pltpu.VMEM((seq, lanes), F32), pltpu.VMEM((seq, lanes), F32),
                        pltpu.VMEM((s2 // 2, 2, 2 * FFT_B, FFT_B), BF16)],
        compiler_params=_cparams(("arbitrary", "arbitrary")),
        name="hyena_fft_conv",
    )(u, bias, kr, ki, tb["first_c"], tb["first_s"], tb["lvl_c"], tb["lvl_s"],
      tb["col_c"], tb["col_s"], jnp.asarray(tb["lhs_f"], BF16), jnp.asarray(tb["lhs_i"], BF16))


MIXOUT_CHUNK = 256


def _mixout_kernel(h_ref, yr_ref, hy_ref, x0_ref, og_ref, w_ref, x_ref, gate_ref, o_ref, *, stream, lw, tm):
    gate = gate_ref[stream:stream + 1, :]
    for r in range(0, tm, MIXOUT_CHUNK):
        rs = slice(r, r + min(MIXOUT_CHUNK, tm))
        yr = yr_ref[rs, :]
        gelu = 0.5 * yr * (1.0 + jnp.tanh(math.sqrt(2.0 / math.pi) * (yr + 0.044715 * (yr * yr * yr))))
        lru = (_rms(h_ref[rs, :] * gelu) * og_ref[:, :lw]).astype(BF16)
        hy = (_rms(hy_ref[rs, :] * x0_ref[rs, :]) * og_ref[:, lw:]).astype(BF16)
        y = (jnp.dot(lru, w_ref[:lw, :], preferred_element_type=F32)
             + jnp.dot(hy, w_ref[lw:, :], preferred_element_type=F32))
        o_ref[rs, :] = x_ref[rs, :] + gate * y


def _mixout(h_lru, p, y_hy, x0c, og, w_out_bf16, x, mods, l, stream):
    rows, d = x.shape
    lw = h_lru.shape[1]
    dm = w_out_bf16.shape[1]
    tm = min(rows, 512)
    row_blk = lambda cols, j: pl.BlockSpec((tm, cols), lambda i: (i, j))
    return pl.pallas_call(
        functools.partial(_mixout_kernel, stream=stream, lw=lw, tm=tm),
        grid=(rows // tm,),
        in_specs=[row_blk(lw, 0), row_blk(lw, 1), row_blk(dm - lw, 0), row_blk(dm - lw, 0),
                  pl.BlockSpec((1, dm), lambda i: (0, 0)),
                  pl.BlockSpec((None, dm, d), lambda i: (l, 0, 0)),
                  row_blk(d, 0),
                  _mod_spec(l, 1, 2, d)],
        out_specs=row_blk(d, 0),
        out_shape=jax.ShapeDtypeStruct((rows, d), F32),
        compiler_params=_cparams(("arbitrary",)),
        name="mixer_out_proj",
    )(h_lru, p, y_hy, x0c, og, w_out_bf16, x, mods)


def _final_kernel(x_ref, g_ref, o_ref):
    o_ref[...] = _rms(x_ref[...]) * g_ref[...]


def _final_norm(x, g):
    rows, d = x.shape
    tm = 1024
    return pl.pallas_call(
        _final_kernel,
        grid=(rows // tm,),
        in_specs=[pl.BlockSpec((tm, d), lambda i: (i, 0)), pl.BlockSpec((1, d), lambda i: (0, 0))],
        out_specs=pl.BlockSpec((tm, d), lambda i: (i, 0)),
        out_shape=jax.ShapeDtypeStruct((rows, d), F32),
        compiler_params=_cparams(("arbitrary",)),
        name="final_norm",
    )(x, g)


def kernel(x, c, ctx, c_ctx, ada_w, ada_b, norm_g, ffn_wg, ffn_wu, ffn_wd, w_in, w_out, out_g, lru_conv_w, lru_conv_b, lru_wa, lru_ba, lru_wx, lru_bx, lru_lam, hy_conv_w, hy_conv_b, hy_bias, filt_w1, filt_b1, filt_w2, filt_b2, filt_w3, filt_b3, filt_w4, filt_b4, filt_freq, final_g):
    assert x.shape[0] == 1 and ctx.shape[0] == 1
    depth = ada_w.shape[0]
    d = x.shape[-1]
    lw = lru_conv_w.shape[-1]
    n_lat, n_ctx = x.shape[1], ctx.shape[1]
    xl, xc = x.reshape(n_lat, d), ctx.reshape(n_ctx, d)
    mods = _ada_mods(c, c_ctx, ada_w, ada_b)
    w_in_bf16, w_out_bf16 = w_in.astype(BF16), w_out.astype(BF16)
    zero_state = jnp.zeros((1, lw), F32)
    lat_is_col = False
    for l in range(depth):
        last = l == depth - 1
        g = norm_g[l][:, None, :]

        xl, xc = _ffn(xl, xc, mods, 0, g[0], ffn_wg, ffn_wu, ffn_wd, l, 0)

        want_col = l % 2 == 1
        if want_col != lat_is_col:
            xl = _to_col_major(xl) if want_col else _from_col_major(xl)
            lat_is_col = want_col
        pl_, pc_ = _win(xl, xc, mods, l, g[1], w_in_bf16)

        convs = (lru_conv_w[l], lru_conv_b[l][None, :], hy_conv_w[l], hy_conv_b[l][None, :])
        cv_l, u_l, x0_l = _short_convs(pl_, *convs, True)
        cv_c, *hy_c = _short_convs(pc_, *convs, not last)

        h_lat = h_ctx = None
        for dr in range(2):
            wax = jnp.concatenate([lru_wa[l, dr], lru_wx[l, dr]], axis=-1).astype(BF16)
            ba, bx, lam = lru_ba[l, dr][None, :], lru_bx[l, dr][None, :], lru_lam[l, dr][None, :]
            h_ctx, state = _lru_scan(cv_c, wax, ba, bx, lam, zero_state, h_ctx, dr == 1)
            h_lat, _ = _lru_scan(cv_l, wax, ba, bx, lam, state, h_lat, dr == 1)

        filt = (filt_w1[l], filt_b1[l], filt_w2[l], filt_b2[l], filt_w3[l], filt_b3[l], filt_w4[l],
                filt_b4[l], filt_freq[l])
        hbias = hy_bias[l][None, :]
        og = out_g[l][None, :]

        kr, ki = _hyena_spectrum(*_hyena_filter(n_lat, *filt))
        y_hy = _hyena_conv(u_l, hbias, kr, ki)
        xl = _mixout(h_lat, pl_, y_hy, x0_l, og, w_out_bf16, xl, mods, l, LAT)

        if last:
            xc = None
        else:
            u_c, x0_c = hy_c
            kr, ki = _hyena_spectrum(*_hyena_filter(n_ctx, *filt))
            y_hy = _hyena_conv(u_c, hbias, kr, ki)
            xc = _mixout(h_ctx, pc_, y_hy, x0_c, og, w_out_bf16, xc, mods, l, CTX)
        xl, xc = _ffn(xl, xc, mods, 2, g[2], ffn_wg, ffn_wu, ffn_wd, l, 1)

    if lat_is_col:
        xl = _from_col_major(xl)
    return _final_norm(xl, final_g[None, :]).reshape(x.shape)
```
